```python
import jax, jax.numpy as jnp
from jax import lax
import numpy as np

D_MODEL = 1024
BATCH = 8
SEQ = 2048
DEPTH = 2

D_FF = 2816
NORM_EPS = 1e-6
LRU_WIDTH = 512
LRU_HEADS = 8
LRU_HEAD_DIM = LRU_WIDTH // LRU_HEADS
LRU_CONV_WIDTH = 4
LRU_C = 8.0
MLA_HEADS = 8
QK_NOPE_DIM = 64
QK_ROPE_DIM = 32
V_HEAD_DIM = 64
Q_LORA_RANK = 384
KV_LORA_RANK = 256
ROPE_THETA = 10000.0
Q_BLOCK = 128
CONV_CH = 512
CONV_WIDTH = 31
N_BRANCH = 3
IN_A = 2 * LRU_WIDTH
IN_B = Q_LORA_RANK + KV_LORA_RANK + QK_ROPE_DIM
IN_C = 2 * CONV_CH
IN_G = N_BRANCH * D_MODEL
D_IN = IN_A + IN_B + IN_C + IN_G
MAX_POS_OFFSET = 4096

kernel_name = 'hybrid_rglru_mla_conformer_macaron'


def rms_norm(x, g):
    xf = x.astype(jnp.float32)
    y = xf * lax.rsqrt(jnp.mean(xf * xf, axis=-1, keepdims=True) + NORM_EPS)
    return (y * g.astype(jnp.float32)).astype(x.dtype)


def layer_norm(x, g, b):
    xf = x.astype(jnp.float32)
    mu = jnp.mean(xf, axis=-1, keepdims=True)
    var = jnp.mean(jnp.square(xf - mu), axis=-1, keepdims=True)
    y = (xf - mu) * lax.rsqrt(var + NORM_EPS)
    return (y * g.astype(jnp.float32) + b.astype(jnp.float32)).astype(x.dtype)


def swiglu(x, w1, w2):
    gu = x @ w1
    g, u = gu[..., :D_FF], gu[..., D_FF:]
    return (jax.nn.silu(g) * u) @ w2


def causal_depthwise_conv(x, w, b):
    k = w.shape[0]
    y = lax.conv_general_dilated(
        x, w[:, None, :].astype(x.dtype), window_strides=(1,), padding=[(k - 1, 0)],
        dimension_numbers=('NWC', 'WIO', 'NWC'), feature_group_count=x.shape[-1])
    return y + b.astype(x.dtype)


def rg_lru(x, w_gate, b_gate, lam):
    b, s, w = x.shape
    xf = x.astype(jnp.float32)
    xh = xf.reshape(b, s, LRU_HEADS, LRU_HEAD_DIM)
    gates = jnp.einsum('bshd,hde->bshe', xh, w_gate.astype(jnp.float32)) + b_gate.astype(jnp.float32)
    r = jax.nn.sigmoid(gates[..., :LRU_HEAD_DIM]).reshape(b, s, w)
    i = jax.nn.sigmoid(gates[..., LRU_HEAD_DIM:]).reshape(b, s, w)
    log_a = -LRU_C * r * jax.nn.softplus(-lam.astype(jnp.float32))
    a = jnp.exp(log_a)
    u = jnp.sqrt(-jnp.expm1(2.0 * log_a)) * (i * xf)

    def combine(left, right):
        a_l, h_l = left
        a_r, h_r = right
        return a_l * a_r, a_r * h_l + h_r

    _, h = lax.associative_scan(combine, (a, u), axis=1)
    return h.astype(x.dtype)


def rope_tables(positions):
    inv_freq = ROPE_THETA ** (-jnp.arange(0, QK_ROPE_DIM, 2, dtype=jnp.float32) / QK_ROPE_DIM)
    ang = positions.astype(jnp.float32)[..., None] * inv_freq
    return jnp.cos(ang), jnp.sin(ang)


def apply_rope(x, cos, sin):
    half = x.shape[-1] // 2
    x1, x2 = x[..., :half], x[..., half:]
    cos = cos.astype(x.dtype)
    sin = sin.astype(x.dtype)
    return jnp.concatenate([x1 * cos - x2 * sin, x2 * cos + x1 * sin], axis=-1)


def mla_branch(cq, ckv, kpe, positions, q_norm, w_uq, kv_norm, w_ukv, w_o):
    b, s, _ = cq.shape
    q = (rms_norm(cq, q_norm) @ w_uq).reshape(b, s, MLA_HEADS, QK_NOPE_DIM + QK_ROPE_DIM)
    q_nope, q_pe = q[..., :QK_NOPE_DIM], q[..., QK_NOPE_DIM:]
    kv = (rms_norm(ckv, kv_norm) @ w_ukv).reshape(b, s, MLA_HEADS, QK_NOPE_DIM + V_HEAD_DIM)
    k_nope, v = kv[..., :QK_NOPE_DIM], kv[..., QK_NOPE_DIM:]
    cos, sin = rope_tables(positions)
    q_pe = apply_rope(q_pe, cos[:, :, None, :], sin[:, :, None, :])
    k_pe = apply_rope(kpe, cos, sin)
    n_blk = s // Q_BLOCK
    qn_blocks = q_nope.reshape(b, n_blk, Q_BLOCK, MLA_HEADS, QK_NOPE_DIM).swapaxes(0, 1)
    qp_blocks = q_pe.reshape(b, n_blk, Q_BLOCK, MLA_HEADS, QK_ROPE_DIM).swapaxes(0, 1)
    scale = (QK_NOPE_DIM + QK_ROPE_DIM) ** -0.5
    key_idx = jnp.arange(s)

    def attend(args):
        qn, qp, blk = args
        sc = (jnp.einsum('bqhd,bkhd->bhqk', qn, k_nope, preferred_element_type=jnp.float32)
              + jnp.einsum('bqhr,bkr->bhqk', qp, k_pe, preferred_element_type=jnp.float32)) * scale
        q_idx = blk * Q_BLOCK + jnp.arange(Q_BLOCK)
        mask = key_idx[None, :] <= q_idx[:, None]
        sc = jnp.where(mask, sc, jnp.finfo(jnp.float32).min)
        p = jax.nn.softmax(sc, axis=-1).astype(v.dtype)
        return jnp.einsum('bhqk,bkhd->bqhd', p, v)

    o = lax.map(attend, (qn_blocks, qp_blocks, jnp.arange(n_blk)))
    o = o.swapaxes(0, 1).reshape(b, s, MLA_HEADS * V_HEAD_DIM)
    return o @ w_o


def conformer_conv_branch(pc, dw_w, dw_b, ln_g, ln_b, w_pw, b_pw):
    c = pc[..., :CONV_CH] * jax.nn.sigmoid(pc[..., CONV_CH:])
    c = causal_depthwise_conv(c, dw_w, dw_b)
    c = jax.nn.silu(layer_norm(c, ln_g, ln_b))
    return c @ w_pw + b_pw


def hybrid_mixer(h, positions, w_in, b_in, lru_conv_w, lru_conv_b, lru_w_gate, lru_b_gate,
                 lru_lambda, lru_w_out, q_norm, w_uq, kv_norm, w_ukv, mla_w_o,
                 conv_dw_w, conv_dw_b, conv_ln_g, conv_ln_b, conv_w_out, conv_b_out, w_out):
    proj = h @ w_in + b_in
    o1, o2, o3 = IN_A, IN_A + IN_B, IN_A + IN_B + IN_C
    pa, pb, pc, pg = proj[..., :o1], proj[..., o1:o2], proj[..., o2:o3], proj[..., o3:]
    xa = causal_depthwise_conv(pa[..., :LRU_WIDTH], lru_conv_w, lru_conv_b)
    y_a = (rg_lru(xa, lru_w_gate, lru_b_gate, lru_lambda) * jax.nn.gelu(pa[..., LRU_WIDTH:])) @ lru_w_out
    cq = pb[..., :Q_LORA_RANK]
    ckv = pb[..., Q_LORA_RANK:Q_LORA_RANK + KV_LORA_RANK]
    kpe = pb[..., Q_LORA_RANK + KV_LORA_RANK:]
    y_b = mla_branch(cq, ckv, kpe, positions, q_norm, w_uq, kv_norm, w_ukv, mla_w_o)
    y_c = conformer_conv_branch(pc, conv_dw_w, conv_dw_b, conv_ln_g, conv_ln_b, conv_w_out, conv_b_out)
    gates = jax.nn.sigmoid(pg.astype(jnp.float32)).astype(h.dtype)
    gates = gates.reshape(*pg.shape[:-1], N_BRANCH, D_MODEL)
    merged = gates[..., 0, :] * y_a + gates[..., 1, :] * y_b + gates[..., 2, :] * y_c
    return merged @ w_out


def _fwd_setup_inputs(seed: int = 0) -> dict:
    key = jax.random.key(seed)
    ks = iter(jax.random.split(key, 48))
    L = DEPTH

    def w(shape, fan_in):
        return jax.random.normal(next(ks), shape, jnp.float32) * fan_in ** -0.5

    def gain(shape):
        return 1.0 + 0.02 * jax.random.normal(next(ks), shape, jnp.float32)

    def bias(shape):
        return 0.02 * jax.random.normal(next(ks), shape, jnp.float32)

    x = jax.random.normal(next(ks), (BATCH, SEQ, D_MODEL), jnp.float32)
    offsets = jax.random.randint(next(ks), (BATCH, 1), 0, MAX_POS_OFFSET, dtype=jnp.int32)
    positions = offsets + jnp.arange(SEQ, dtype=jnp.int32)[None, :]
    u = jax.random.uniform(next(ks), (L, LRU_WIDTH), jnp.float32, 0.9, 0.999)
    a0 = u ** (1.0 / LRU_C)
    lru_lambda = jnp.log(a0) - jnp.log1p(-a0)
    return {
        'x': x,
        'positions': positions,
        'ffn1_norm': gain((L, D_MODEL)),
        'ffn1_w1': w((L, D_MODEL, 2 * D_FF), D_MODEL),
        'ffn1_w2': w((L, D_FF, D_MODEL), D_FF),
        'mix_norm': gain((L, D_MODEL)),
        'w_in': w((L, D_MODEL, D_IN), D_MODEL),
        'b_in': bias((L, D_IN)),
        'lru_conv_w': w((L, LRU_CONV_WIDTH, LRU_WIDTH), LRU_CONV_WIDTH),
        'lru_conv_b': bias((L, LRU_WIDTH)),
        'lru_w_gate': w((L, LRU_HEADS, LRU_HEAD_DIM, 2 * LRU_HEAD_DIM), LRU_HEAD_DIM),
        'lru_b_gate': bias((L, LRU_HEADS, 2 * LRU_HEAD_DIM)),
        'lru_lambda': lru_lambda,
        'lru_w_out': w((L, LRU_WIDTH, D_MODEL), LRU_WIDTH),
        'q_norm': gain((L, Q_LORA_RANK)),
        'w_uq': w((L, Q_LORA_RANK, MLA_HEADS * (QK_NOPE_DIM + QK_ROPE_DIM)), Q_LORA_RANK),
        'kv_norm': gain((L, KV_LORA_RANK)),
        'w_ukv': w((L, KV_LORA_RANK, MLA_HEADS * (QK_NOPE_DIM + V_HEAD_DIM)), KV_LORA_RANK),
        'mla_w_o': w((L, MLA_HEADS * V_HEAD_DIM, D_MODEL), MLA_HEADS * V_HEAD_DIM),
        'conv_dw_w': w((L, CONV_WIDTH, CONV_CH), CONV_WIDTH),
        'conv_dw_b': bias((L, CONV_CH)),
        'conv_ln_g': gain((L, CONV_CH)),
        'conv_ln_b': bias((L, CONV_CH)),
        'conv_w_out': w((L, CONV_CH, D_MODEL), CONV_CH),
        'conv_b_out': bias((L, D_MODEL)),
        'w_out': w((L, D_MODEL, D_MODEL), D_MODEL),
        'ffn2_norm': gain((L, D_MODEL)),
        'ffn2_w1': w((L, D_MODEL, 2 * D_FF), D_MODEL),
        'ffn2_w2': w((L, D_FF, D_MODEL), D_FF),
        'final_norm': gain((D_MODEL,)),
    }


def _fwd_reference(x, positions, ffn1_norm, ffn1_w1, ffn1_w2, mix_norm, w_in, b_in,
              lru_conv_w, lru_conv_b, lru_w_gate, lru_b_gate, lru_lambda, lru_w_out,
              q_norm, w_uq, kv_norm, w_ukv, mla_w_o,
              conv_dw_w, conv_dw_b, conv_ln_g, conv_ln_b, conv_w_out, conv_b_out,
              w_out, ffn2_norm, ffn2_w1, ffn2_w2, final_norm):
    for l in range(DEPTH):
        x = x + 0.5 * swiglu(rms_norm(x, ffn1_norm[l]), ffn1_w1[l], ffn1_w2[l])
        x = x + hybrid_mixer(
            rms_norm(x, mix_norm[l]), positions, w_in[l], b_in[l],
            lru_conv_w[l], lru_conv_b[l], lru_w_gate[l], lru_b_gate[l], lru_lambda[l], lru_w_out[l],
            q_norm[l], w_uq[l], kv_norm[l], w_ukv[l], mla_w_o[l],
            conv_dw_w[l], conv_dw_b[l], conv_ln_g[l], conv_ln_b[l], conv_w_out[l], conv_b_out[l],
            w_out[l])
        x = x + 0.5 * swiglu(rms_norm(x, ffn2_norm[l]), ffn2_w1[l], ffn2_w2[l])
    return rms_norm(x, final_norm)


import jax as _jax
import jax.numpy as _jnp

TWIN_FORMAT = 'train_step'
FWD_PARAMS = ['x', 'positions', 'ffn1_norm', 'ffn1_w1', 'ffn1_w2', 'mix_norm', 'w_in', 'b_in', 'lru_conv_w', 'lru_conv_b', 'lru_w_gate', 'lru_b_gate', 'lru_lambda', 'lru_w_out', 'q_norm', 'w_uq', 'kv_norm', 'w_ukv', 'mla_w_o', 'conv_dw_w', 'conv_dw_b', 'conv_ln_g', 'conv_ln_b', 'conv_w_out', 'conv_b_out', 'w_out', 'ffn2_norm', 'ffn2_w1', 'ffn2_w2', 'final_norm']
TWIN_WEIGHTS = ['ffn1_norm', 'ffn1_w1', 'ffn1_w2', 'mix_norm', 'w_in', 'b_in', 'lru_conv_w', 'lru_conv_b', 'lru_w_gate', 'lru_b_gate', 'lru_lambda', 'lru_w_out', 'q_norm', 'w_uq', 'kv_norm', 'w_ukv', 'mla_w_o', 'conv_dw_w', 'conv_dw_b', 'conv_ln_g', 'conv_ln_b', 'conv_w_out', 'conv_b_out', 'w_out', 'ffn2_norm', 'ffn2_w1', 'ffn2_w2', 'final_norm']
TWIN_DIFF_INPUT = 'x'
TWIN_INPUTS = ['x', 'positions', 'ffn1_norm', 'ffn1_w1', 'ffn1_w2', 'mix_norm', 'w_in', 'b_in', 'lru_conv_w', 'lru_conv_b', 'lru_w_gate', 'lru_b_gate', 'lru_lambda', 'lru_w_out', 'q_norm', 'w_uq', 'kv_norm', 'w_ukv', 'mla_w_o', 'conv_dw_w', 'conv_dw_b', 'conv_ln_g', 'conv_ln_b', 'conv_w_out', 'conv_b_out', 'w_out', 'ffn2_norm', 'ffn2_w1', 'ffn2_w2', 'final_norm', 'loss_target', 'm_ffn1_norm', 'm_ffn1_w1', 'm_ffn1_w2', 'm_mix_norm', 'm_w_in', 'm_b_in', 'm_lru_conv_w', 'm_lru_conv_b', 'm_lru_w_gate', 'm_lru_b_gate', 'm_lru_lambda', 'm_lru_w_out', 'm_q_norm', 'm_w_uq', 'm_kv_norm', 'm_w_ukv', 'm_mla_w_o', 'm_conv_dw_w', 'm_conv_dw_b', 'm_conv_ln_g', 'm_conv_ln_b', 'm_conv_w_out', 'm_conv_b_out', 'm_w_out', 'm_ffn2_norm', 'm_ffn2_w1', 'm_ffn2_w2', 'm_final_norm', 'v_ffn1_norm', 'v_ffn1_w1', 'v_ffn1_w2', 'v_mix_norm', 'v_w_in', 'v_b_in', 'v_lru_conv_w', 'v_lru_conv_b', 'v_lru_w_gate', 'v_lru_b_gate', 'v_lru_lambda', 'v_lru_w_out', 'v_q_norm', 'v_w_uq', 'v_kv_norm', 'v_w_ukv', 'v_mla_w_o', 'v_conv_dw_w', 'v_conv_dw_b', 'v_conv_ln_g', 'v_conv_ln_b', 'v_conv_w_out', 'v_conv_b_out', 'v_w_out', 'v_ffn2_norm', 'v_ffn2_w1', 'v_ffn2_w2', 'v_final_norm']
TWIN_OUTPUTS = ['loss', 'grad_x', 'grad_ffn1_norm', 'grad_ffn1_w1', 'grad_ffn1_w2', 'grad_mix_norm', 'grad_w_in', 'grad_b_in', 'grad_lru_conv_w', 'grad_lru_conv_b', 'grad_lru_w_gate', 'grad_lru_b_gate', 'grad_lru_lambda', 'grad_lru_w_out', 'grad_q_norm', 'grad_w_uq', 'grad_kv_norm', 'grad_w_ukv', 'grad_mla_w_o', 'grad_conv_dw_w', 'grad_conv_dw_b', 'grad_conv_ln_g', 'grad_conv_ln_b', 'grad_conv_w_out', 'grad_conv_b_out', 'grad_w_out', 'grad_ffn2_norm', 'grad_ffn2_w1', 'grad_ffn2_w2', 'grad_final_norm', 'delta_ffn1_norm', 'delta_ffn1_w1', 'delta_ffn1_w2', 'delta_mix_norm', 'delta_w_in', 'delta_b_in', 'delta_lru_conv_w', 'delta_lru_conv_b', 'delta_lru_w_gate', 'delta_lru_b_gate', 'delta_lru_lambda', 'delta_lru_w_out', 'delta_q_norm', 'delta_w_uq', 'delta_kv_norm', 'delta_w_ukv', 'delta_mla_w_o', 'delta_conv_dw_w', 'delta_conv_dw_b', 'delta_conv_ln_g', 'delta_conv_ln_b', 'delta_conv_w_out', 'delta_conv_b_out', 'delta_w_out', 'delta_ffn2_norm', 'delta_ffn2_w1', 'delta_ffn2_w2', 'delta_final_norm', 'new_m_ffn1_norm', 'new_m_ffn1_w1', 'new_m_ffn1_w2', 'new_m_mix_norm', 'new_m_w_in', 'new_m_b_in', 'new_m_lru_conv_w', 'new_m_lru_conv_b', 'new_m_lru_w_gate', 'new_m_lru_b_gate', 'new_m_lru_lambda', 'new_m_lru_w_out', 'new_m_q_norm', 'new_m_w_uq', 'new_m_kv_norm', 'new_m_w_ukv', 'new_m_mla_w_o', 'new_m_conv_dw_w', 'new_m_conv_dw_b', 'new_m_conv_ln_g', 'new_m_conv_ln_b', 'new_m_conv_w_out', 'new_m_conv_b_out', 'new_m_w_out', 'new_m_ffn2_norm', 'new_m_ffn2_w1', 'new_m_ffn2_w2', 'new_m_final_norm', 'new_v_ffn1_norm', 'new_v_ffn1_w1', 'new_v_ffn1_w2', 'new_v_mix_norm', 'new_v_w_in', 'new_v_b_in', 'new_v_lru_conv_w', 'new_v_lru_conv_b', 'new_v_lru_w_gate', 'new_v_lru_b_gate', 'new_v_lru_lambda', 'new_v_lru_w_out', 'new_v_q_norm', 'new_v_w_uq', 'new_v_kv_norm', 'new_v_w_ukv', 'new_v_mla_w_o', 'new_v_conv_dw_w', 'new_v_conv_dw_b', 'new_v_conv_ln_g', 'new_v_conv_ln_b', 'new_v_conv_w_out', 'new_v_conv_b_out', 'new_v_w_out', 'new_v_ffn2_norm', 'new_v_ffn2_w1', 'new_v_ffn2_w2', 'new_v_final_norm']
TWIN_LEAF_KINDS = {'loss': 'loss', 'grad_x': 'grad_x', 'grad_ffn1_norm': 'grad_w', 'grad_ffn1_w1': 'grad_w', 'grad_ffn1_w2': 'grad_w', 'grad_mix_norm': 'grad_w', 'grad_w_in': 'grad_w', 'grad_b_in': 'grad_w', 'grad_lru_conv_w': 'grad_w', 'grad_lru_conv_b': 'grad_w', 'grad_lru_w_gate': 'grad_w', 'grad_lru_b_gate': 'grad_w', 'grad_lru_lambda': 'grad_w', 'grad_lru_w_out': 'grad_w', 'grad_q_norm': 'grad_w', 'grad_w_uq': 'grad_w', 'grad_kv_norm': 'grad_w', 'grad_w_ukv': 'grad_w', 'grad_mla_w_o': 'grad_w', 'grad_conv_dw_w': 'grad_w', 'grad_conv_dw_b': 'grad_w', 'grad_conv_ln_g': 'grad_w', 'grad_conv_ln_b': 'grad_w', 'grad_conv_w_out': 'grad_w', 'grad_conv_b_out': 'grad_w', 'grad_w_out': 'grad_w', 'grad_ffn2_norm': 'grad_w', 'grad_ffn2_w1': 'grad_w', 'grad_ffn2_w2': 'grad_w', 'grad_final_norm': 'grad_w', 'delta_ffn1_norm': 'delta_w', 'delta_ffn1_w1': 'delta_w', 'delta_ffn1_w2': 'delta_w', 'delta_mix_norm': 'delta_w', 'delta_w_in': 'delta_w', 'delta_b_in': 'delta_w', 'delta_lru_conv_w': 'delta_w', 'delta_lru_conv_b': 'delta_w', 'delta_lru_w_gate': 'delta_w', 'delta_lru_b_gate': 'delta_w', 'delta_lru_lambda': 'delta_w', 'delta_lru_w_out': 'delta_w', 'delta_q_norm': 'delta_w', 'delta_w_uq': 'delta_w', 'delta_kv_norm': 'delta_w', 'delta_w_ukv': 'delta_w', 'delta_mla_w_o': 'delta_w', 'delta_conv_dw_w': 'delta_w', 'delta_conv_dw_b': 'delta_w', 'delta_conv_ln_g': 'delta_w', 'delta_conv_ln_b': 'delta_w', 'delta_conv_w_out': 'delta_w', 'delta_conv_b_out': 'delta_w', 'delta_w_out': 'delta_w', 'delta_ffn2_norm': 'delta_w', 'delta_ffn2_w1': 'delta_w', 'delta_ffn2_w2': 'delta_w', 'delta_final_norm': 'delta_w', 'new_m_ffn1_norm': 'new_m', 'new_m_ffn1_w1': 'new_m', 'new_m_ffn1_w2': 'new_m', 'new_m_mix_norm': 'new_m', 'new_m_w_in': 'new_m', 'new_m_b_in': 'new_m', 'new_m_lru_conv_w': 'new_m', 'new_m_lru_conv_b': 'new_m', 'new_m_lru_w_gate': 'new_m', 'new_m_lru_b_gate': 'new_m', 'new_m_lru_lambda': 'new_m', 'new_m_lru_w_out': 'new_m', 'new_m_q_norm': 'new_m', 'new_m_w_uq': 'new_m', 'new_m_kv_norm': 'new_m', 'new_m_w_ukv': 'new_m', 'new_m_mla_w_o': 'new_m', 'new_m_conv_dw_w': 'new_m', 'new_m_conv_dw_b': 'new_m', 'new_m_conv_ln_g': 'new_m', 'new_m_conv_ln_b': 'new_m', 'new_m_conv_w_out': 'new_m', 'new_m_conv_b_out': 'new_m', 'new_m_w_out': 'new_m', 'new_m_ffn2_norm': 'new_m', 'new_m_ffn2_w1': 'new_m', 'new_m_ffn2_w2': 'new_m', 'new_m_final_norm': 'new_m', 'new_v_ffn1_norm': 'new_v', 'new_v_ffn1_w1': 'new_v', 'new_v_ffn1_w2': 'new_v', 'new_v_mix_norm': 'new_v', 'new_v_w_in': 'new_v', 'new_v_b_in': 'new_v', 'new_v_lru_conv_w': 'new_v', 'new_v_lru_conv_b': 'new_v', 'new_v_lru_w_gate': 'new_v', 'new_v_lru_b_gate': 'new_v', 'new_v_lru_lambda': 'new_v', 'new_v_lru_w_out': 'new_v', 'new_v_q_norm': 'new_v', 'new_v_w_uq': 'new_v', 'new_v_kv_norm': 'new_v', 'new_v_w_ukv': 'new_v', 'new_v_mla_w_o': 'new_v', 'new_v_conv_dw_w': 'new_v', 'new_v_conv_dw_b': 'new_v', 'new_v_conv_ln_g': 'new_v', 'new_v_conv_ln_b': 'new_v', 'new_v_conv_w_out': 'new_v', 'new_v_conv_b_out': 'new_v', 'new_v_w_out': 'new_v', 'new_v_ffn2_norm': 'new_v', 'new_v_ffn2_w1': 'new_v', 'new_v_ffn2_w2': 'new_v', 'new_v_final_norm': 'new_v'}


def _forward(args):
    return _fwd_reference(*[args[k] for k in FWD_PARAMS])


def _output_shape():
    out = _jax.eval_shape(lambda: _forward(_fwd_setup_inputs(0)))
    return out.shape, out.dtype

N_MICROBATCH = 1
ADAM_LR = 0.001
ADAM_B1 = 0.9
ADAM_B2 = 0.999
ADAM_EPS = 1e-08
ADAM_WD = 0.01
ADAM_STEP = 10
PER_EXAMPLE_BATCH_AXIS = {'x': 0, 'positions': 0, 'loss_target': 0}
SHARED_INPUTS = []
_WEIGHT_DTYPES = {'ffn1_norm': _jnp.float32, 'ffn1_w1': _jnp.float32, 'ffn1_w2': _jnp.float32, 'mix_norm': _jnp.float32, 'w_in': _jnp.float32, 'b_in': _jnp.float32, 'lru_conv_w': _jnp.float32, 'lru_conv_b': _jnp.float32, 'lru_w_gate': _jnp.float32, 'lru_b_gate': _jnp.float32, 'lru_lambda': _jnp.float32, 'lru_w_out': _jnp.float32, 'q_norm': _jnp.float32, 'w_uq': _jnp.float32, 'kv_norm': _jnp.float32, 'w_ukv': _jnp.float32, 'mla_w_o': _jnp.float32, 'conv_dw_w': _jnp.float32, 'conv_dw_b': _jnp.float32, 'conv_ln_g': _jnp.float32, 'conv_ln_b': _jnp.float32, 'conv_w_out': _jnp.float32, 'conv_b_out': _jnp.float32, 'w_out': _jnp.float32, 'ffn2_norm': _jnp.float32, 'ffn2_w1': _jnp.float32, 'ffn2_w2': _jnp.float32, 'final_norm': _jnp.float32}
MOMENT_SCALE = {'ffn1_norm': 5.512947e-02, 'ffn1_w1': 2.340590e-02, 'ffn1_w2': 3.817093e-02, 'mix_norm': 6.290318e-02, 'w_in': 2.727019e-02, 'b_in': 1.330001e-01, 'lru_conv_w': 4.950741e-02, 'lru_conv_b': 4.382652e-01, 'lru_w_gate': 2.350866e-02, 'lru_b_gate': 1.515631e-02, 'lru_lambda': 2.654290e-02, 'lru_w_out': 3.839435e-02, 'q_norm': 1.868462e-02, 'w_uq': 1.318157e-02, 'kv_norm': 3.660601e-02, 'w_ukv': 1.826103e-02, 'mla_w_o': 1.565326e-02, 'conv_dw_w': 5.480251e-02, 'conv_dw_b': 1.365503e-01, 'conv_ln_g': 6.962641e-02, 'conv_ln_b': 7.068558e-02, 'conv_w_out': 3.934974e-02, 'conv_b_out': 9.065058e-02, 'w_out': 5.317417e-02, 'ffn2_norm': 4.787364e-02, 'ffn2_w1': 1.977096e-02, 'ffn2_w2': 3.221555e-02, 'final_norm': 1.599699e+01}


def _to_microbatches(a, axis):
    t = _jnp.moveaxis(a, axis, 0)
    t = t.reshape((N_MICROBATCH, t.shape[0] // N_MICROBATCH) + t.shape[1:])
    return _jnp.moveaxis(t, 1, axis + 1)


def setup_inputs(seed: int = 0) -> dict:
    inp = _fwd_setup_inputs(seed)
    key = _jax.random.fold_in(_jax.random.key(seed), 7919)
    shape, _ = _output_shape()
    out = dict(inp)
    out["loss_target"] = _jax.random.normal(_jax.random.fold_in(key, 0), shape, _jnp.float32)
    for i, name in enumerate(TWIN_WEIGHTS):
        w = inp[name].astype(_jnp.float32)
        if MOMENT_SCALE is None:
            s = _jnp.sqrt(_jnp.mean(_jnp.square(w)) + 1e-30)
        else:
            s = MOMENT_SCALE[name]
        km, kv = _jax.random.split(_jax.random.fold_in(key, i + 1))
        out[name] = w
        out["m_" + name] = s * _jax.random.normal(km, w.shape, _jnp.float32)
        out["v_" + name] = (s * s) * _jax.random.uniform(kv, w.shape, _jnp.float32, 0.5, 1.5)
    if N_MICROBATCH > 1:
        for name, axis in PER_EXAMPLE_BATCH_AXIS.items():
            out[name] = _to_microbatches(out[name], axis)
    return {'x': out['x'], 'positions': out['positions'], 'ffn1_norm': out['ffn1_norm'], 'ffn1_w1': out['ffn1_w1'], 'ffn1_w2': out['ffn1_w2'], 'mix_norm': out['mix_norm'], 'w_in': out['w_in'], 'b_in': out['b_in'], 'lru_conv_w': out['lru_conv_w'], 'lru_conv_b': out['lru_conv_b'], 'lru_w_gate': out['lru_w_gate'], 'lru_b_gate': out['lru_b_gate'], 'lru_lambda': out['lru_lambda'], 'lru_w_out': out['lru_w_out'], 'q_norm': out['q_norm'], 'w_uq': out['w_uq'], 'kv_norm': out['kv_norm'], 'w_ukv': out['w_ukv'], 'mla_w_o': out['mla_w_o'], 'conv_dw_w': out['conv_dw_w'], 'conv_dw_b': out['conv_dw_b'], 'conv_ln_g': out['conv_ln_g'], 'conv_ln_b': out['conv_ln_b'], 'conv_w_out': out['conv_w_out'], 'conv_b_out': out['conv_b_out'], 'w_out': out['w_out'], 'ffn2_norm': out['ffn2_norm'], 'ffn2_w1': out['ffn2_w1'], 'ffn2_w2': out['ffn2_w2'], 'final_norm': out['final_norm'], 'loss_target': out['loss_target'], 'm_ffn1_norm': out['m_ffn1_norm'], 'm_ffn1_w1': out['m_ffn1_w1'], 'm_ffn1_w2': out['m_ffn1_w2'], 'm_mix_norm': out['m_mix_norm'], 'm_w_in': out['m_w_in'], 'm_b_in': out['m_b_in'], 'm_lru_conv_w': out['m_lru_conv_w'], 'm_lru_conv_b': out['m_lru_conv_b'], 'm_lru_w_gate': out['m_lru_w_gate'], 'm_lru_b_gate': out['m_lru_b_gate'], 'm_lru_lambda': out['m_lru_lambda'], 'm_lru_w_out': out['m_lru_w_out'], 'm_q_norm': out['m_q_norm'], 'm_w_uq': out['m_w_uq'], 'm_kv_norm': out['m_kv_norm'], 'm_w_ukv': out['m_w_ukv'], 'm_mla_w_o': out['m_mla_w_o'], 'm_conv_dw_w': out['m_conv_dw_w'], 'm_conv_dw_b': out['m_conv_dw_b'], 'm_conv_ln_g': out['m_conv_ln_g'], 'm_conv_ln_b': out['m_conv_ln_b'], 'm_conv_w_out': out['m_conv_w_out'], 'm_conv_b_out': out['m_conv_b_out'], 'm_w_out': out['m_w_out'], 'm_ffn2_norm': out['m_ffn2_norm'], 'm_ffn2_w1': out['m_ffn2_w1'], 'm_ffn2_w2': out['m_ffn2_w2'], 'm_final_norm': out['m_final_norm'], 'v_ffn1_norm': out['v_ffn1_norm'], 'v_ffn1_w1': out['v_ffn1_w1'], 'v_ffn1_w2': out['v_ffn1_w2'], 'v_mix_norm': out['v_mix_norm'], 'v_w_in': out['v_w_in'], 'v_b_in': out['v_b_in'], 'v_lru_conv_w': out['v_lru_conv_w'], 'v_lru_conv_b': out['v_lru_conv_b'], 'v_lru_w_gate': out['v_lru_w_gate'], 'v_lru_b_gate': out['v_lru_b_gate'], 'v_lru_lambda': out['v_lru_lambda'], 'v_lru_w_out': out['v_lru_w_out'], 'v_q_norm': out['v_q_norm'], 'v_w_uq': out['v_w_uq'], 'v_kv_norm': out['v_kv_norm'], 'v_w_ukv': out['v_w_ukv'], 'v_mla_w_o': out['v_mla_w_o'], 'v_conv_dw_w': out['v_conv_dw_w'], 'v_conv_dw_b': out['v_conv_dw_b'], 'v_conv_ln_g': out['v_conv_ln_g'], 'v_conv_ln_b': out['v_conv_ln_b'], 'v_conv_w_out': out['v_conv_w_out'], 'v_conv_b_out': out['v_conv_b_out'], 'v_w_out': out['v_w_out'], 'v_ffn2_norm': out['v_ffn2_norm'], 'v_ffn2_w1': out['v_ffn2_w1'], 'v_ffn2_w2': out['v_ffn2_w2'], 'v_final_norm': out['v_final_norm']}


def _loss(weights, diff, rest, loss_target):
    with _jax.named_scope("forward"):
        args = {**rest, TWIN_DIFF_INPUT: diff, **{k: w.astype(_WEIGHT_DTYPES[k]) for k, w in weights.items()}}
        y = _forward(args)
    with _jax.named_scope("loss_head"):
        err = _jnp.square(y.astype(_jnp.float32) - loss_target)
        return 0.5 * _jnp.sum(_jnp.mean(err, axis=-1)) if err.ndim else 0.5 * err


def _adamw(w, g, m, v):
    m = ADAM_B1 * m + (1.0 - ADAM_B1) * g
    v = ADAM_B2 * v + (1.0 - ADAM_B2) * _jnp.square(g)
    m_hat = m / (1.0 - ADAM_B1 ** ADAM_STEP)
    v_hat = v / (1.0 - ADAM_B2 ** ADAM_STEP)
    delta = -ADAM_LR * (m_hat / (_jnp.sqrt(v_hat) + ADAM_EPS) + ADAM_WD * w)
    return delta, m, v


def reference(x, positions, ffn1_norm, ffn1_w1, ffn1_w2, mix_norm, w_in, b_in, lru_conv_w, lru_conv_b, lru_w_gate, lru_b_gate, lru_lambda, lru_w_out, q_norm, w_uq, kv_norm, w_ukv, mla_w_o, conv_dw_w, conv_dw_b, conv_ln_g, conv_ln_b, conv_w_out, conv_b_out, w_out, ffn2_norm, ffn2_w1, ffn2_w2, final_norm, loss_target, m_ffn1_norm, m_ffn1_w1, m_ffn1_w2, m_mix_norm, m_w_in, m_b_in, m_lru_conv_w, m_lru_conv_b, m_lru_w_gate, m_lru_b_gate, m_lru_lambda, m_lru_w_out, m_q_norm, m_w_uq, m_kv_norm, m_w_ukv, m_mla_w_o, m_conv_dw_w, m_conv_dw_b, m_conv_ln_g, m_conv_ln_b, m_conv_w_out, m_conv_b_out, m_w_out, m_ffn2_norm, m_ffn2_w1, m_ffn2_w2, m_final_norm, v_ffn1_norm, v_ffn1_w1, v_ffn1_w2, v_mix_norm, v_w_in, v_b_in, v_lru_conv_w, v_lru_conv_b, v_lru_w_gate, v_lru_b_gate, v_lru_lambda, v_lru_w_out, v_q_norm, v_w_uq, v_kv_norm, v_w_ukv, v_mla_w_o, v_conv_dw_w, v_conv_dw_b, v_conv_ln_g, v_conv_ln_b, v_conv_w_out, v_conv_b_out, v_w_out, v_ffn2_norm, v_ffn2_w1, v_ffn2_w2, v_final_norm):
    given = dict(x=x, positions=positions, ffn1_norm=ffn1_norm, ffn1_w1=ffn1_w1, ffn1_w2=ffn1_w2, mix_norm=mix_norm, w_in=w_in, b_in=b_in, lru_conv_w=lru_conv_w, lru_conv_b=lru_conv_b, lru_w_gate=lru_w_gate, lru_b_gate=lru_b_gate, lru_lambda=lru_lambda, lru_w_out=lru_w_out, q_norm=q_norm, w_uq=w_uq, kv_norm=kv_norm, w_ukv=w_ukv, mla_w_o=mla_w_o, conv_dw_w=conv_dw_w, conv_dw_b=conv_dw_b, conv_ln_g=conv_ln_g, conv_ln_b=conv_ln_b, conv_w_out=conv_w_out, conv_b_out=conv_b_out, w_out=w_out, ffn2_norm=ffn2_norm, ffn2_w1=ffn2_w1, ffn2_w2=ffn2_w2, final_norm=final_norm, loss_target=loss_target, m_ffn1_norm=m_ffn1_norm, m_ffn1_w1=m_ffn1_w1, m_ffn1_w2=m_ffn1_w2, m_mix_norm=m_mix_norm, m_w_in=m_w_in, m_b_in=m_b_in, m_lru_conv_w=m_lru_conv_w, m_lru_conv_b=m_lru_conv_b, m_lru_w_gate=m_lru_w_gate, m_lru_b_gate=m_lru_b_gate, m_lru_lambda=m_lru_lambda, m_lru_w_out=m_lru_w_out, m_q_norm=m_q_norm, m_w_uq=m_w_uq, m_kv_norm=m_kv_norm, m_w_ukv=m_w_ukv, m_mla_w_o=m_mla_w_o, m_conv_dw_w=m_conv_dw_w, m_conv_dw_b=m_conv_dw_b, m_conv_ln_g=m_conv_ln_g, m_conv_ln_b=m_conv_ln_b, m_conv_w_out=m_conv_w_out, m_conv_b_out=m_conv_b_out, m_w_out=m_w_out, m_ffn2_norm=m_ffn2_norm, m_ffn2_w1=m_ffn2_w1, m_ffn2_w2=m_ffn2_w2, m_final_norm=m_final_norm, v_ffn1_norm=v_ffn1_norm, v_ffn1_w1=v_ffn1_w1, v_ffn1_w2=v_ffn1_w2, v_mix_norm=v_mix_norm, v_w_in=v_w_in, v_b_in=v_b_in, v_lru_conv_w=v_lru_conv_w, v_lru_conv_b=v_lru_conv_b, v_lru_w_gate=v_lru_w_gate, v_lru_b_gate=v_lru_b_gate, v_lru_lambda=v_lru_lambda, v_lru_w_out=v_lru_w_out, v_q_norm=v_q_norm, v_w_uq=v_w_uq, v_kv_norm=v_kv_norm, v_w_ukv=v_w_ukv, v_mla_w_o=v_mla_w_o, v_conv_dw_w=v_conv_dw_w, v_conv_dw_b=v_conv_dw_b, v_conv_ln_g=v_conv_ln_g, v_conv_ln_b=v_conv_ln_b, v_conv_w_out=v_conv_w_out, v_conv_b_out=v_conv_b_out, v_w_out=v_w_out, v_ffn2_norm=v_ffn2_norm, v_ffn2_w1=v_ffn2_w1, v_ffn2_w2=v_ffn2_w2, v_final_norm=v_final_norm)
    weights = {n: given[n] for n in TWIN_WEIGHTS}
    shared = {n: given[n] for n in SHARED_INPUTS}
    per_example = {n: given[n] for n in ['x', 'positions']}
    grad_fn = _jax.value_and_grad(_loss, argnums=(0, 1))

    def one_microbatch(ex, loss_target):
        ex = dict(ex)
        diff = ex.pop(TWIN_DIFF_INPUT)
        return grad_fn(weights, diff, {**shared, **ex}, loss_target)

    if N_MICROBATCH == 1:
        loss, (grad_w, grad_x) = one_microbatch(per_example, given["loss_target"])
    else:
        def body(carry, xs):
            loss_sum, grad_sum = carry
            l_k, (gw_k, gx_k) = one_microbatch(xs[0], xs[1])
            with _jax.named_scope("update"):
                return (loss_sum + l_k, _jax.tree.map(_jnp.add, grad_sum, gw_k)), gx_k

        init = (_jnp.zeros((), _jnp.float32), _jax.tree.map(_jnp.zeros_like, weights))
        (loss, grad_w), grad_x = _jax.lax.scan(body, init, (per_example, given["loss_target"]))
    with _jax.named_scope("update"):
        delta_w, new_m, new_v = {}, {}, {}
        for n in TWIN_WEIGHTS:
            delta_w[n], new_m[n], new_v[n] = _adamw(weights[n], grad_w[n], given["m_" + n], given["v_" + n])
    return (loss, grad_x, *[grad_w[n] for n in TWIN_WEIGHTS], *[delta_w[n] for n in TWIN_WEIGHTS],
            *[new_m[n] for n in TWIN_WEIGHTS], *[new_v[n] for n in TWIN_WEIGHTS])
```

```python
import functools

import jax
import jax.numpy as jnp
from jax import lax
from jax.experimental import pallas as pl
from jax.experimental.pallas import tpu as pltpu

F32, BF16 = jnp.float32, jnp.bfloat16
S = jax.ShapeDtypeStruct

LANES = 128
VMEM_LIMIT = 56 * 2**20
NORM_EPS = 1e-6
LRU_C = 8.0
MLA_HEADS = 8
QK_NOPE, QK_ROPE, V_HEAD = 64, 32, 64
HEAD_PAD = 128
ROPE_THETA = 10000.0
ADAM_LR, ADAM_B1, ADAM_B2, ADAM_EPS, ADAM_WD, ADAM_STEP = 0.001, 0.9, 0.999, 1e-08, 0.01, 10
MESH_AXES = ("x", "y", "c")
NT = (((1,), (1,)), ((), ()))
TN = (((0,), (0,)), ((), ()))
NN = (((1,), (0,)), ((), ()))


def _tile(n, cap, unit=LANES):
    best = None
    for d in range(unit, min(n, cap) + 1, unit):
        if n % d == 0:
            best = d
    return best if best is not None else n


def _params(sem):
    return pltpu.CompilerParams(dimension_semantics=sem, vmem_limit_bytes=VMEM_LIMIT)


def _mm(a, b, mode="nn", out_dtype=F32, bias=None, res=None, alpha=1.0, name="mm"):
    if mode == "nn":
        (m, k), (k2, n), dims = a.shape, b.shape, NN
    elif mode == "nt":
        (m, k), (n, k2), dims = a.shape, b.shape, NT
    else:
        (k, m), (k2, n), dims = a.shape, b.shape, TN
    assert k == k2, (name, a.shape, b.shape, mode)
    tm, tn = _tile(m, 512), _tile(n, 512)
    tk = k if k <= 3072 else _tile(k, 3072)
    nk = k // tk
    if mode == "tn":
        a_spec = pl.BlockSpec((tk, tm), lambda i, j, kk: (kk, i))
    else:
        a_spec = pl.BlockSpec((tm, tk), lambda i, j, kk: (i, kk))
    if mode == "nt":
        b_spec = pl.BlockSpec((tn, tk), lambda i, j, kk: (j, kk))
    else:
        b_spec = pl.BlockSpec((tk, tn), lambda i, j, kk: (kk, j))
    operands, in_specs = [a, b], [a_spec, b_spec]
    if bias is not None:
        operands.append(bias)
        in_specs.append(pl.BlockSpec((1, tn), lambda i, j, kk: (0, j)))
    if res is not None:
        operands.append(res)
        in_specs.append(pl.BlockSpec((tm, tn), lambda i, j, kk: (i, j)))

    def body(*refs):
        a_ref, b_ref = refs[0], refs[1]
        pos = 2
        bias_ref = res_ref = None
        if bias is not None:
            bias_ref, pos = refs[pos], pos + 1
        if res is not None:
            res_ref, pos = refs[pos], pos + 1
        o_ref = refs[pos]
        part = lax.dot_general(a_ref[...].astype(BF16), b_ref[...].astype(BF16), dims, preferred_element_type=F32)

        def finish(acc):
            out = acc if alpha == 1.0 else acc * alpha
            if bias_ref is not None:
                out = out + bias_ref[...]
            if res_ref is not None:
                out = out + res_ref[...]
            o_ref[...] = out.astype(o_ref.dtype)

        if nk == 1:
            finish(part)
        else:
            acc_ref = refs[pos + 1]
            kk = pl.program_id(2)

            @pl.when(kk == 0)
            def _():
                acc_ref[...] = part

            @pl.when(kk > 0)
            def _():
                acc_ref[...] += part

            @pl.when(kk == nk - 1)
            def _():
                finish(acc_ref[...])

    return pl.pallas_call(
        body, name=name, grid=(m // tm, n // tn, nk), in_specs=in_specs,
        out_specs=pl.BlockSpec((tm, tn), lambda i, j, kk: (i, j)),
        out_shape=S((m, n), out_dtype),
        scratch_shapes=[pltpu.VMEM((tm, tn), F32)] if nk > 1 else [],
        compiler_params=_params(("parallel", "parallel", "arbitrary")),
    )(*operands)


def _rowwise(fn, rows, params=(), outs=(), accs=(), tt=256, name="rowwise"):
    rows = [r if isinstance(r, tuple) else (r, r.shape[1], 0) for r in rows]
    t = rows[0][0].shape[0]
    tt = min(tt, t)
    n_rows, n_par, n_out = len(rows), len(params), len(outs)
    in_specs = [pl.BlockSpec((tt, w), functools.partial(lambda i, cb: (i, cb), cb=cb)) for (_, w, cb) in rows]
    in_specs += [pl.BlockSpec(p.shape, functools.partial(lambda i, nd: (0,) * nd, nd=p.ndim)) for p in params]
    out_shape = [S((t, w), dt) for (w, dt) in outs] + [S(shape, dt) for (shape, dt) in accs]
    out_specs = [pl.BlockSpec((tt, w), lambda i: (i, 0)) for (w, _) in outs]
    out_specs += [pl.BlockSpec(shape, functools.partial(lambda i, nd: (0,) * nd, nd=len(shape))) for (shape, _) in accs]

    def body(*refs):
        vals = [r[...] for r in refs[:n_rows + n_par]]
        o_vals, a_vals = fn(*vals)
        o_refs = refs[n_rows + n_par:n_rows + n_par + n_out]
        a_refs = refs[n_rows + n_par + n_out:]
        for ref, val in zip(o_refs, o_vals, strict=True):
            ref[...] = val.astype(ref.dtype)
        i = pl.program_id(0)
        for ref, val in zip(a_refs, a_vals, strict=True):
            @pl.when(i == 0)
            def _(ref=ref, val=val):
                ref[...] = val.astype(ref.dtype)

            @pl.when(i > 0)
            def _(ref=ref, val=val):
                ref[...] += val.astype(ref.dtype)

    res = pl.pallas_call(
        body, name=name, grid=(t // tt,), in_specs=in_specs, out_specs=out_specs, out_shape=out_shape,
        compiler_params=_params(("arbitrary",) if accs else ("parallel",)),
    )(*[r[0] for r in rows], *params)
    return res


def _rms(x, g):
    x = x.astype(F32)
    return x * lax.rsqrt(jnp.mean(x * x, axis=-1, keepdims=True) + NORM_EPS) * g


def _layer_norm_silu(x, g, b):
    mu = jnp.mean(x, axis=-1, keepdims=True)
    var = jnp.mean(jnp.square(x - mu), axis=-1, keepdims=True)
    return jax.nn.silu((x - mu) * lax.rsqrt(var + NORM_EPS) * g + b)


def _neg_expm1(z):
    series = -z * (1.0 + z * (0.5 + z * (1.0 / 6.0 + z * (1.0 / 24.0 + z * (1.0 / 120.0)))))
    return jnp.where(z > -0.05, series, 1.0 - jnp.exp(z))


def _shift_down(x, s, fill=0.0):
    if s == 0:
        return x
    row = lax.broadcasted_iota(jnp.int32, x.shape, 0)
    return jnp.where(row >= s, pltpu.roll(x, s, 0), fill)


def _shift_up(x, s, fill=0.0):
    if s == 0:
        return x
    t = x.shape[0]
    row = lax.broadcasted_iota(jnp.int32, x.shape, 0)
    return jnp.where(row < t - s, pltpu.roll(x, t - s, 0), fill)


def _scan(a, u, shift):
    t, d = a.shape[0], 1
    while d < t:
        u = u + a * shift(u, d, 0.0)
        if 2 * d < t:
            a = a * shift(a, d, 1.0)
        d *= 2
    return u


def _lru_gates(xa, wr, wi, br, bi, lam):
    xb = xa.astype(BF16)
    r = jax.nn.sigmoid(jnp.dot(xb, wr.astype(BF16), preferred_element_type=F32) + br)
    i = jax.nn.sigmoid(jnp.dot(xb, wi.astype(BF16), preferred_element_type=F32) + bi)
    log_a = -LRU_C * r * jax.nn.softplus(-lam)
    return jnp.exp(log_a), jnp.sqrt(_neg_expm1(2.0 * log_a)) * (i * xa)


def _conv_fwd(x, w_ref, b, width):
    y = b + w_ref[pl.ds(width - 1, 1), :] * x
    for j in range(width - 1):
        y = y + w_ref[pl.ds(j, 1), :] * _shift_down(x, width - 1 - j)
    return y


def _conv_bwd(x, dy, w_ref, dw_ref, width):
    dx = w_ref[pl.ds(width - 1, 1), :] * dy
    dw_ref[pl.ds(width - 1, 1), :] = jnp.sum(dy * x, axis=0, keepdims=True)
    for j in range(width - 1):
        s = width - 1 - j
        dx = dx + w_ref[pl.ds(j, 1), :] * _shift_up(dy, s)
        dw_ref[pl.ds(j, 1), :] = jnp.sum(dy * _shift_down(x, s), axis=0, keepdims=True)
    return dx


def _rope(z, c, s1, s2):
    return z * c + pltpu.roll(z, HEAD_PAD - QK_ROPE // 2, 1) * s1 + pltpu.roll(z, QK_ROPE // 2, 1) * s2


def _rope_t(d, c, s1, s2):
    return d * c + pltpu.roll(d * s1, QK_ROPE // 2, 1) + pltpu.roll(d * s2, HEAD_PAD - QK_ROPE // 2, 1)


def _heads(z):
    return [z[:, h * HEAD_PAD:(h + 1) * HEAD_PAD] for h in range(z.shape[1] // HEAD_PAD)]


def _chan_spec(t, c_off=0):
    return pl.BlockSpec((t, LANES), lambda c: (0, c_off + c))


def _lru_specs(t, n_tiles, width):
    vec = pl.BlockSpec((1, LANES), lambda c: (0, c))
    mat = pl.BlockSpec((1, LANES, LANES), lambda c: (c, 0, 0))
    return [_chan_spec(t), _chan_spec(t, n_tiles), pl.BlockSpec((width, LANES), lambda c: (0, c)), vec, mat, mat, vec, vec, vec]


def _lru_fwd(pa, cw, cb, wr, wi, br, bi, lam, name):
    t, w = pa.shape[0], pa.shape[1] // 2
    n_tiles, width = w // LANES, cw.shape[0]

    def body(x_ref, g_ref, cw_ref, cb_ref, wr_ref, wi_ref, br_ref, bi_ref, lam_ref, y_ref):
        xa = _conv_fwd(x_ref[...], cw_ref, cb_ref[...], width)
        a, u = _lru_gates(xa, wr_ref[0], wi_ref[0], br_ref[...], bi_ref[...], lam_ref[...])
        h = _scan(a, u, _shift_down)
        y_ref[...] = (h * jax.nn.gelu(g_ref[...])).astype(y_ref.dtype)

    return pl.pallas_call(
        body, name=name, grid=(n_tiles,), in_specs=_lru_specs(t, n_tiles, width), out_specs=_chan_spec(t),
        out_shape=S((t, w), BF16), compiler_params=_params(("parallel",)),
    )(pa, pa, cw, cb, wr, wi, br, bi, lam)


def _lru_bwd(pa, dy, cw, cb, wr, wi, br, bi, lam, name):
    t, w = pa.shape[0], pa.shape[1] // 2
    n_tiles, width = w // LANES, cw.shape[0]

    def body(x_ref, g_ref, cw_ref, cb_ref, wr_ref, wi_ref, br_ref, bi_ref, lam_ref, dy_ref,
             dx_ref, dg_ref, dcw_ref, dcb_ref, dwr_ref, dwi_ref, dbr_ref, dbi_ref, dlam_ref, sx_ref, sg_ref):
        x = x_ref[...]
        xa = _conv_fwd(x, cw_ref, cb_ref[...], width)
        (a, u), gates_vjp = jax.vjp(_lru_gates, xa, wr_ref[0], wi_ref[0], br_ref[...], bi_ref[...], lam_ref[...])
        h = _scan(a, u, _shift_down)
        _, out_vjp = jax.vjp(lambda h_, g_: h_ * jax.nn.gelu(g_), h, g_ref[...])
        dh, dgate = out_vjp(dy_ref[...])
        adj = _scan(_shift_up(a, 1), dh, _shift_up)
        dxa, dwr, dwi, dbr, dbi, dlam = gates_vjp((adj * _shift_down(h, 1), adj))
        dx = _conv_bwd(x, dxa, cw_ref, dcw_ref, width)
        dcb_ref[...] = jnp.sum(dxa, axis=0, keepdims=True)
        dx_ref[...] = dx.astype(dx_ref.dtype)
        dg_ref[...] = dgate.astype(dg_ref.dtype)
        sx_ref[...] = jnp.sum(dx, axis=0, keepdims=True)
        sg_ref[...] = jnp.sum(dgate, axis=0, keepdims=True)
        dwr_ref[0], dwi_ref[0] = dwr, dwi
        dbr_ref[...], dbi_ref[...], dlam_ref[...] = dbr, dbi, dlam

    vec = pl.BlockSpec((1, LANES), lambda c: (0, c))
    mat = pl.BlockSpec((1, LANES, LANES), lambda c: (c, 0, 0))
    vec_s, mat_s = S((1, w), F32), S((n_tiles, LANES, LANES), F32)
    return pl.pallas_call(
        body, name=name, grid=(n_tiles,), in_specs=_lru_specs(t, n_tiles, width) + [_chan_spec(t)],
        out_specs=[_chan_spec(t), _chan_spec(t), pl.BlockSpec((width, LANES), lambda c: (0, c)), vec, mat, mat, vec, vec, vec, vec, vec],
        out_shape=[S((t, w), BF16), S((t, w), BF16), S((width, w), F32), vec_s, mat_s, mat_s, vec_s, vec_s, vec_s, vec_s, vec_s],
        compiler_params=_params(("parallel",)),
    )(pa, pa, cw, cb, wr, wi, br, bi, lam, dy)


def _glu_conv_fwd(pc, cw, cb, name):
    t, c = pc.shape[0], pc.shape[1] // 2
    n_tiles, width = c // LANES, cw.shape[0]

    def body(v_ref, g_ref, cw_ref, cb_ref, y_ref):
        y_ref[...] = _conv_fwd(v_ref[...] * jax.nn.sigmoid(g_ref[...]), cw_ref, cb_ref[...], width)

    return pl.pallas_call(
        body, name=name, grid=(n_tiles,),
        in_specs=[_chan_spec(t), _chan_spec(t, n_tiles), pl.BlockSpec((width, LANES), lambda i: (0, i)), pl.BlockSpec((1, LANES), lambda i: (0, i))],
        out_specs=_chan_spec(t), out_shape=S((t, c), F32), compiler_params=_params(("parallel",)),
    )(pc, pc, cw, cb)


def _glu_conv_bwd(pc, dy, cw, name):
    t, c = pc.shape[0], pc.shape[1] // 2
    n_tiles, width = c // LANES, cw.shape[0]

    def body(v_ref, g_ref, cw_ref, dy_ref, dv_ref, dg_ref, dcw_ref, dcb_ref, sv_ref, sg_ref):
        glu = lambda v_, g_: v_ * jax.nn.sigmoid(g_)
        x, glu_vjp = jax.vjp(glu, v_ref[...], g_ref[...])
        dy_ = dy_ref[...]
        dv, dg = glu_vjp(_conv_bwd(x, dy_, cw_ref, dcw_ref, width))
        dcb_ref[...] = jnp.sum(dy_, axis=0, keepdims=True)
        dv_ref[...] = dv.astype(dv_ref.dtype)
        dg_ref[...] = dg.astype(dg_ref.dtype)
        sv_ref[...] = jnp.sum(dv, axis=0, keepdims=True)
        sg_ref[...] = jnp.sum(dg, axis=0, keepdims=True)

    vec = pl.BlockSpec((1, LANES), lambda i: (0, i))
    wspec = pl.BlockSpec((width, LANES), lambda i: (0, i))
    return pl.pallas_call(
        body, name=name, grid=(n_tiles,), in_specs=[_chan_spec(t), _chan_spec(t, n_tiles), wspec, _chan_spec(t)],
        out_specs=[_chan_spec(t), _chan_spec(t), wspec, vec, vec, vec],
        out_shape=[S((t, c), BF16), S((t, c), BF16), S((width, c), F32), S((1, c), F32), S((1, c), F32), S((1, c), F32)],
        compiler_params=_params(("parallel",)),
    )(pc, pc, cw, dy)


def _softmax_rows(q, k, causal, scale):
    s = lax.dot_general(q, k, NT, preferred_element_type=F32) * scale
    s = jnp.where(causal, s, jnp.finfo(F32).min)
    p = jnp.exp(s - jnp.max(s, axis=-1, keepdims=True))
    return p / jnp.sum(p, axis=-1, keepdims=True)


def _attn_specs(t, tq):
    return [pl.BlockSpec((tq, 2 * HEAD_PAD), lambda hp, i: (i, hp)), pl.BlockSpec((t, 2 * HEAD_PAD), lambda hp, i: (0, hp)),
            pl.BlockSpec((t, 2 * V_HEAD), lambda hp, i: (0, hp))]


def _attn_masks(t, tq):
    i = pl.program_id(1)
    row = lax.broadcasted_iota(jnp.int32, (tq, t), 0) + i * tq
    col = lax.broadcasted_iota(jnp.int32, (tq, t), 1)
    return col <= row


def _attn_fwd(qh, kh, v, name):
    t = qh.shape[0]
    tq = min(256, t)
    scale = (QK_NOPE + QK_ROPE) ** -0.5

    def body(q_ref, k_ref, v_ref, o_ref):
        causal = _attn_masks(t, tq)
        lane = lax.broadcasted_iota(jnp.int32, (t, 2 * V_HEAD), 1)
        vv = v_ref[...]
        acc = jnp.zeros((tq, 2 * V_HEAD), F32)
        for e in range(2):
            p = _softmax_rows(q_ref[:, e * HEAD_PAD:(e + 1) * HEAD_PAD], k_ref[:, e * HEAD_PAD:(e + 1) * HEAD_PAD], causal, scale)
            ve = jnp.where((lane >= V_HEAD * e) & (lane < V_HEAD * (e + 1)), vv, jnp.zeros_like(vv))
            acc = acc + jnp.dot(p.astype(BF16), ve, preferred_element_type=F32)
        o_ref[...] = acc.astype(o_ref.dtype)

    return pl.pallas_call(
        body, name=name, grid=(MLA_HEADS // 2, t // tq), in_specs=_attn_specs(t, tq),
        out_specs=pl.BlockSpec((tq, 2 * V_HEAD), lambda hp, i: (i, hp)), out_shape=S((t, MLA_HEADS * V_HEAD), BF16),
        compiler_params=_params(("parallel", "parallel")),
    )(qh, kh, v)


def _attn_bwd(qh, kh, v, do, name):
    t = qh.shape[0]
    tq = min(256, t)
    scale = (QK_NOPE + QK_ROPE) ** -0.5

    def body(q_ref, k_ref, v_ref, do_ref, dq_ref, dk_ref, dv_ref):
        causal = _attn_masks(t, tq)
        lane = lax.broadcasted_iota(jnp.int32, (tq, 2 * V_HEAD), 1)
        vv, dd = v_ref[...], do_ref[...]
        dqs, dks = [], []
        dv = jnp.zeros((t, 2 * V_HEAD), F32)
        for e in range(2):
            q, k = q_ref[:, e * HEAD_PAD:(e + 1) * HEAD_PAD], k_ref[:, e * HEAD_PAD:(e + 1) * HEAD_PAD]
            p = _softmax_rows(q, k, causal, scale)
            de = jnp.where((lane >= V_HEAD * e) & (lane < V_HEAD * (e + 1)), dd, jnp.zeros_like(dd))
            dp = lax.dot_general(de, vv, NT, preferred_element_type=F32)
            ds = (p * (dp - jnp.sum(p * dp, axis=-1, keepdims=True)) * scale).astype(BF16)
            dqs.append(jnp.dot(ds, k, preferred_element_type=F32))
            dks.append(lax.dot_general(ds, q, TN, preferred_element_type=F32))
            dv = dv + lax.dot_general(p.astype(BF16), de, TN, preferred_element_type=F32)
        dq_ref[...] = jnp.concatenate(dqs, axis=-1)
        dk = jnp.concatenate(dks, axis=-1)
        i = pl.program_id(1)

        @pl.when(i == 0)
        def _():
            dk_ref[...] = dk
            dv_ref[...] = dv

        @pl.when(i > 0)
        def _():
            dk_ref[...] += dk
            dv_ref[...] += dv

    return pl.pallas_call(
        body, name=name, grid=(MLA_HEADS // 2, t // tq),
        in_specs=_attn_specs(t, tq) + [pl.BlockSpec((tq, 2 * V_HEAD), lambda hp, i: (i, hp))],
        out_specs=[pl.BlockSpec((tq, 2 * HEAD_PAD), lambda hp, i: (i, hp)), pl.BlockSpec((t, 2 * HEAD_PAD), lambda hp, i: (0, hp)),
                   pl.BlockSpec((t, 2 * V_HEAD), lambda hp, i: (0, hp))],
        out_shape=[S((t, MLA_HEADS * HEAD_PAD), F32), S((t, MLA_HEADS * HEAD_PAD), F32), S((t, MLA_HEADS * V_HEAD), F32)],
        compiler_params=_params(("parallel", "arbitrary")),
    )(qh, kh, v, do)


def _ffn_fwd(x, g, w1, w2, tag):
    d, f = x.shape[1], w2.shape[0]
    h, = _rowwise(lambda x_, g_: ([_rms(x_, g_)], []), [x], [g], outs=[(d, BF16)], name=tag + "_rms")
    gu = _mm(h, w1, out_dtype=BF16, name=tag + "_up")
    a, = _rowwise(lambda g_, u_: ([jax.nn.silu(g_.astype(F32)) * u_.astype(F32)], []), [(gu, f, 0), (gu, f, 1)],
                  outs=[(f, BF16)], name=tag + "_act")
    return _mm(a, w2, res=x, alpha=0.5, name=tag + "_down"), (x, h, gu, a)


def _ffn_bwd(dy, saved, g, w1, w2, tag):
    x, h, gu, a = saved
    d, f = x.shape[1], w2.shape[0]
    da = _mm(dy, w2, "nt", out_dtype=BF16, alpha=0.5, name=tag + "_dact")
    dw2 = _mm(a, dy, "tn", alpha=0.5, name=tag + "_dw2")

    def act_bwd(g_, u_, da_):
        _, vjp = jax.vjp(lambda p, q: jax.nn.silu(p) * q, g_.astype(F32), u_.astype(F32))
        return [jnp.concatenate(vjp(da_.astype(F32)), axis=-1)], []

    dgu, = _rowwise(act_bwd, [(gu, f, 0), (gu, f, 1), da], outs=[(2 * f, BF16)], tt=128, name=tag + "_dgu")
    dw1 = _mm(h, dgu, "tn", name=tag + "_dw1")
    dh = _mm(dgu, w1, "nt", name=tag + "_dh")
    dx, dg = _rms_bwd(x, dh, dy, g, tag + "_drms")
    return dx, (dg, dw1, dw2)


def _rms_bwd(x, dh, dres, g, name):
    def fn(x_, dh_, dres_, g_):
        _, vjp = jax.vjp(_rms, x_, g_)
        dx, dg = vjp(dh_)
        return [dx + dres_], [dg]

    return _rowwise(fn, [x, dh, dres], [g], outs=[(x.shape[1], F32)], accs=[((1, x.shape[1]), F32)], name=name)


def _rope_tables(positions, name):
    half = QK_ROPE // 2
    inv = ROPE_THETA ** (-jnp.arange(0, QK_ROPE, 2, dtype=F32) / QK_ROPE)
    inv_lanes = jnp.zeros((1, HEAD_PAD), F32).at[0, QK_NOPE:QK_NOPE + QK_ROPE].set(jnp.tile(inv, 2))

    def fn(pos, inv_):
        ang = pos.astype(F32) * inv_
        lane = lax.broadcasted_iota(jnp.int32, ang.shape, 1)
        cos, sin = jnp.cos(ang), jnp.sin(ang)
        c = jnp.where(lane < QK_NOPE, 1.0, jnp.where(lane < QK_NOPE + QK_ROPE, cos, 0.0))
        s1 = jnp.where((lane >= QK_NOPE) & (lane < QK_NOPE + half), -sin, 0.0)
        s2 = jnp.where((lane >= QK_NOPE + half) & (lane < QK_NOPE + QK_ROPE), sin, 0.0)
        return [c, s1, s2], []

    return _rowwise(fn, [positions.reshape(-1, 1)], [inv_lanes], outs=[(HEAD_PAD, F32)] * 3, name=name)


def _mix_weights(p, l):
    wl, ql, kvl = p["lru_lambda"].shape[-1], p["q_norm"].shape[-1], p["kv_norm"].shape[-1]
    cc = p["conv_ln_g"].shape[-1]
    o1, o2 = 2 * wl, 2 * wl + ql + kvl + QK_ROPE
    o3 = o2 + 2 * cc
    w_in, b_in = p["w_in"][l], p["b_in"][l][None, :]

    def mla_cols(m):
        z = lambda n: jnp.zeros(m.shape[:-1] + (n,), m.dtype)
        return jnp.concatenate([m[..., o1 + ql:o1 + ql + kvl], z(QK_NOPE), m[..., o1 + ql + kvl:o2],
                                z(HEAD_PAD - QK_NOPE - QK_ROPE), m[..., o1:o1 + ql]], axis=-1)

    hd = p["lru_w_gate"].shape[-2]
    per = LANES // hd
    eye = jnp.eye(per, dtype=F32)
    wg = p["lru_w_gate"][l].reshape(-1, per, hd, 2 * hd)
    block_diag = lambda m: jnp.einsum("cedk,ef->cedfk", m, eye).reshape(-1, LANES, LANES)
    bg = p["lru_b_gate"][l]
    w_uq = p["w_uq"][l].reshape(ql, MLA_HEADS, QK_NOPE + QK_ROPE)
    w_ukv = p["w_ukv"][l].reshape(kvl, MLA_HEADS, QK_NOPE + V_HEAD)
    pad = lambda m, n: jnp.pad(m, ((0, 0), (0, 0), (0, n)))
    return dict(
        w_a=w_in[:, :o1], w_b=mla_cols(w_in), w_c=w_in[:, o2:o3], w_g=w_in[:, o3:],
        b_a=b_in[:, :o1], b_b=mla_cols(b_in), b_c=b_in[:, o2:o3], b_g=b_in[:, o3:],
        wr=block_diag(wg[..., :hd]), wi=block_diag(wg[..., hd:]),
        br=bg[:, :hd].reshape(1, -1), bi=bg[:, hd:].reshape(1, -1),
        w_uq=pad(w_uq, HEAD_PAD - QK_NOPE - QK_ROPE).reshape(ql, -1),
        w_k=pad(w_ukv[..., :QK_NOPE], HEAD_PAD - QK_NOPE).reshape(kvl, -1),
        w_v=w_ukv[..., QK_NOPE:].reshape(kvl, -1),
    )


def _mix_fwd(x, p, mw, l, rope, tag):
    d = x.shape[1]
    wl, ql, kvl = p["lru_lambda"].shape[-1], p["q_norm"].shape[-1], p["kv_norm"].shape[-1]
    row = lambda name: p[name][l][None, :]
    h, = _rowwise(lambda x_, g_: ([_rms(x_, g_)], []), [x], [row("mix_norm")], outs=[(d, BF16)], name=tag + "_rms")
    pa = _mm(h, mw["w_a"], bias=mw["b_a"], name=tag + "_pa")
    pb = _mm(h, mw["w_b"], bias=mw["b_b"], name=tag + "_pb")
    pc = _mm(h, mw["w_c"], bias=mw["b_c"], name=tag + "_pc")
    pg = _mm(h, mw["w_g"], bias=mw["b_g"], name=tag + "_pg")
    lru_args = (p["lru_conv_w"][l], row("lru_conv_b"), mw["wr"], mw["wi"], mw["br"], mw["bi"], row("lru_lambda"))
    ya_pre = _lru_fwd(pa, *lru_args, name=tag + "_lru")
    y_a = _mm(ya_pre, p["lru_w_out"][l], name=tag + "_ya")
    mla_rows = [(pb, kvl, 0), (pb, ql, (kvl + HEAD_PAD) // ql)]
    assert (kvl + HEAD_PAD) % ql == 0 and kvl % HEAD_PAD == 0
    ckvn, cqn = _rowwise(lambda kv_, q_, gk, gq: ([_rms(kv_, gk), _rms(q_, gq)], []), mla_rows, [row("kv_norm"), row("q_norm")],
                         outs=[(kvl, BF16), (ql, BF16)], name=tag + "_lat_rms")
    q0 = _mm(cqn, mw["w_uq"], name=tag + "_q")
    k0 = _mm(ckvn, mw["w_k"], name=tag + "_k")
    v = _mm(ckvn, mw["w_v"], out_dtype=BF16, name=tag + "_v")

    def rope_fwd(q_, k_, kpe, c, s1, s2):
        kr = _rope(kpe, c, s1, s2)
        return [jnp.concatenate([_rope(z, c, s1, s2) for z in _heads(q_)], axis=-1),
                jnp.concatenate([z + kr for z in _heads(k_)], axis=-1)], []

    qh, kh = _rowwise(rope_fwd, [q0, k0, (pb, HEAD_PAD, kvl // HEAD_PAD), *rope],
                      outs=[(q0.shape[1], BF16), (k0.shape[1], BF16)], name=tag + "_rope")
    o = _attn_fwd(qh, kh, v, tag + "_attn")
    y_b = _mm(o, p["mla_w_o"][l], name=tag + "_yb")
    c2 = _glu_conv_fwd(pc, p["conv_dw_w"][l], row("conv_dw_b"), tag + "_conv")
    c3, = _rowwise(lambda c_, g_, b_: ([_layer_norm_silu(c_, g_, b_)], []), [c2], [row("conv_ln_g"), row("conv_ln_b")],
                   outs=[(c2.shape[1], BF16)], name=tag + "_ln")
    y_c = _mm(c3, p["conv_w_out"][l], bias=row("conv_b_out"), name=tag + "_yc")
    merged, = _rowwise(_merge, [y_a, y_b, y_c, (pg, d, 0), (pg, d, 1), (pg, d, 2)], outs=[(d, BF16)], name=tag + "_merge")
    out = _mm(merged, p["w_out"][l], res=x, name=tag + "_out")
    saved = dict(x=x, h=h, pa=pa, pb=pb, pc=pc, pg=pg, ya_pre=ya_pre, y_a=y_a, y_b=y_b, y_c=y_c, ckvn=ckvn, cqn=cqn,
                 qh=qh, kh=kh, v=v, o=o, c2=c2, c3=c3, merged=merged, lru_args=lru_args)
    return out, saved


def _merge(ya, yb, yc, g0, g1, g2):
    return [jax.nn.sigmoid(g0) * ya + jax.nn.sigmoid(g1) * yb + jax.nn.sigmoid(g2) * yc], []


def _mix_bwd(dy, s, p, mw, l, rope, tag):
    x, h = s["x"], s["h"]
    d = x.shape[1]
    wl, ql, kvl = p["lru_lambda"].shape[-1], p["q_norm"].shape[-1], p["kv_norm"].shape[-1]
    row = lambda name: p[name][l][None, :]
    g = {}
    dmerged = _mm(dy, p["w_out"][l], "nt", name=tag + "_dmerged")
    g["w_out"] = _mm(s["merged"], dy, "tn", name=tag + "_dw_out")

    def merge_bwd(ya, yb, yc, g0, g1, g2, dm):
        _, vjp = jax.vjp(lambda *a: _merge(*a)[0][0], ya, yb, yc, g0, g1, g2)
        dya, dyb, dyc, d0, d1, d2 = vjp(dm)
        dpg = jnp.concatenate([d0, d1, d2], axis=-1)
        return [dya, dyb, dyc, dpg], [jnp.sum(dyc, axis=0, keepdims=True), jnp.sum(dpg, axis=0, keepdims=True)]

    pg = s["pg"]
    dya, dyb, dyc, dpg, g["conv_b_out"], db_g = _rowwise(
        merge_bwd, [s["y_a"], s["y_b"], s["y_c"], (pg, d, 0), (pg, d, 1), (pg, d, 2), dmerged],
        outs=[(d, BF16), (d, BF16), (d, BF16), (3 * d, BF16)], accs=[((1, d), F32), ((1, 3 * d), F32)], tt=128, name=tag + "_dmerge")
    g["lru_w_out"] = _mm(s["ya_pre"], dya, "tn", name=tag + "_dw_lru_out")
    dya_pre = _mm(dya, p["lru_w_out"][l], "nt", name=tag + "_dya_pre")
    (dpa_x, dpa_g, g["lru_conv_w"], g["lru_conv_b"], g["wr"], g["wi"], g["br"], g["bi"], g["lru_lambda"], sb_x, sb_g) = _lru_bwd(
        s["pa"], dya_pre, *s["lru_args"], name=tag + "_dlru")
    dpa = jnp.concatenate([dpa_x, dpa_g], axis=1)
    db_a = jnp.concatenate([sb_x, sb_g], axis=1)
    g["conv_w_out"] = _mm(s["c3"], dyc, "tn", name=tag + "_dw_conv_out")
    dc3 = _mm(dyc, p["conv_w_out"][l], "nt", name=tag + "_dc3")

    def ln_bwd(c_, dc_, g_, b_):
        _, vjp = jax.vjp(_layer_norm_silu, c_, g_, b_)
        dc, dg_, db_ = vjp(dc_)
        return [dc], [dg_, db_]

    cc = s["c2"].shape[1]
    dc2, g["conv_ln_g"], g["conv_ln_b"] = _rowwise(ln_bwd, [s["c2"], dc3], [row("conv_ln_g"), row("conv_ln_b")], outs=[(cc, F32)],
                                                    accs=[((1, cc), F32)] * 2, name=tag + "_dln")
    dpc_v, dpc_g, g["conv_dw_w"], g["conv_dw_b"], sc_v, sc_g = _glu_conv_bwd(s["pc"], dc2, p["conv_dw_w"][l], tag + "_dconv")
    dpc = jnp.concatenate([dpc_v, dpc_g], axis=1)
    db_c = jnp.concatenate([sc_v, sc_g], axis=1)
    g["mla_w_o"] = _mm(s["o"], dyb, "tn", name=tag + "_dw_o")
    do = _mm(dyb, p["mla_w_o"][l], "nt", out_dtype=BF16, name=tag + "_do")
    dqh, dkh, dv = _attn_bwd(s["qh"], s["kh"], s["v"], do, tag + "_dattn")

    def rope_bwd(dq_, dk_, c, s1, s2):
        lane = lax.broadcasted_iota(jnp.int32, c.shape, 1)
        dkr = functools.reduce(lambda a, b: a + b, _heads(dk_))
        dkpe = jnp.where((lane >= QK_NOPE) & (lane < QK_NOPE + QK_ROPE), _rope_t(dkr, c, s1, s2), 0.0)
        return [jnp.concatenate([_rope_t(z, c, s1, s2) for z in _heads(dq_)], axis=-1), dk_, dkpe], []

    dq0, dk0, dkpe = _rowwise(rope_bwd, [dqh, dkh, *rope], outs=[(dqh.shape[1], BF16), (dkh.shape[1], BF16), (HEAD_PAD, F32)],
                              name=tag + "_drope")
    dvb = dv.astype(BF16)
    g["w_uq"] = _mm(s["cqn"], dq0, "tn", name=tag + "_dw_uq")
    g["w_k"] = _mm(s["ckvn"], dk0, "tn", name=tag + "_dw_k")
    g["w_v"] = _mm(s["ckvn"], dvb, "tn", name=tag + "_dw_v")
    dcqn = _mm(dq0, mw["w_uq"], "nt", name=tag + "_dcqn")
    dckvn = _mm(dk0, mw["w_k"], "nt", name=tag + "_dckvn_k")
    dckvn = _mm(dvb, mw["w_v"], "nt", res=dckvn, name=tag + "_dckvn_v")

    def lat_bwd(kv_, q_, dkv_, dq_, dkpe_, gk, gq):
        _, vjp_k = jax.vjp(_rms, kv_, gk)
        _, vjp_q = jax.vjp(_rms, q_, gq)
        (dkv, dgk), (dq, dgq) = vjp_k(dkv_), vjp_q(dq_)
        dpb = jnp.concatenate([dkv, dkpe_, dq], axis=-1)
        return [dpb], [dgk, dgq, jnp.sum(dpb, axis=0, keepdims=True)]

    pb = s["pb"]
    dpb, g["kv_norm"], g["q_norm"], db_b = _rowwise(
        lat_bwd, [(pb, kvl, 0), (pb, ql, (kvl + HEAD_PAD) // ql), dckvn, dcqn, dkpe], [row("kv_norm"), row("q_norm")],
        outs=[(pb.shape[1], BF16)], accs=[((1, kvl), F32), ((1, ql), F32), ((1, pb.shape[1]), F32)], name=tag + "_dlat")
    dh = None
    for part, dpart in (("a", dpa), ("b", dpb), ("c", dpc), ("g", dpg)):
        g["w_" + part] = _mm(h, dpart, "tn", name=tag + "_dw_" + part)
        dh = _mm(dpart, mw["w_" + part], "nt", res=dh, name=tag + "_dh_" + part)
    g["b_a"], g["b_b"], g["b_c"], g["b_g"] = db_a, db_b, db_c, db_g
    dx, g["mix_norm"] = _rms_bwd(x, dh, dy, row("mix_norm"), tag + "_drms")
    return dx, g


def _mix_grads_to_params(g, p):
    wl, ql, kvl = p["lru_lambda"].shape[-1], p["q_norm"].shape[-1], p["kv_norm"].shape[-1]

    def mla_cols_t(m):
        return jnp.concatenate([m[..., kvl + HEAD_PAD:], m[..., :kvl], m[..., kvl + QK_NOPE:kvl + QK_NOPE + QK_ROPE]], axis=-1)

    hd = p["lru_w_gate"].shape[-2]
    per = LANES // hd
    eye = jnp.eye(per, dtype=F32)
    diag = lambda m: jnp.einsum("cedfk,ef->cedk", m.reshape(-1, per, hd, per, hd), eye).reshape(-1, hd, hd)
    out = {k: g[k] for k in ("w_out", "lru_w_out", "conv_w_out", "mla_w_o", "lru_conv_w", "conv_dw_w")}
    for k in ("mix_norm", "conv_b_out", "lru_conv_b", "lru_lambda", "conv_ln_g", "conv_ln_b", "conv_dw_b", "kv_norm", "q_norm"):
        out[k] = g[k][0]
    out["w_in"] = jnp.concatenate([g["w_a"], mla_cols_t(g["w_b"]), g["w_c"], g["w_g"]], axis=1)
    out["b_in"] = jnp.concatenate([g["b_a"], mla_cols_t(g["b_b"]), g["b_c"], g["b_g"]], axis=1)[0]
    out["lru_w_gate"] = jnp.concatenate([diag(g["wr"]), diag(g["wi"])], axis=-1)
    out["lru_b_gate"] = jnp.concatenate([g["br"].reshape(-1, hd), g["bi"].reshape(-1, hd)], axis=-1)
    out["w_uq"] = g["w_uq"].reshape(ql, MLA_HEADS, HEAD_PAD)[..., :QK_NOPE + QK_ROPE].reshape(ql, -1)
    out["w_ukv"] = jnp.concatenate([g["w_k"].reshape(kvl, MLA_HEADS, HEAD_PAD)[..., :QK_NOPE],
                                    g["w_v"].reshape(kvl, MLA_HEADS, V_HEAD)], axis=-1).reshape(kvl, -1)
    return out


def _loss_head(x, target, g, name):
    def fn(x_, t_, g_):
        y, vjp = jax.vjp(_rms, x_, g_)
        err = y - t_
        dx, dg = vjp(err * (1.0 / x_.shape[1]))
        loss = 0.5 * jnp.sum(jnp.mean(err * err, axis=-1, keepdims=True), axis=0, keepdims=True)
        return [dx], [dg, jnp.broadcast_to(loss, (1, LANES))]

    dx, dg, loss = _rowwise(fn, [x, target], [g], outs=[(x.shape[1], F32)], accs=[((1, x.shape[1]), F32), ((1, LANES), F32)], name=name)
    return loss[0, 0], dx, dg


def _local_step(x, positions, target, p):
    depth = p["ffn1_norm"].shape[0]
    rope = _rope_tables(positions, "rope_tables")
    saved, mws = [], []
    for l in range(depth):
        tag = f"l{l}"
        mw = _mix_weights(p, l)
        x, s1 = _ffn_fwd(x, p["ffn1_norm"][l][None, :], p["ffn1_w1"][l], p["ffn1_w2"][l], tag + "_ffn1")
        x, s2 = _mix_fwd(x, p, mw, l, rope, tag + "_mix")
        x, s3 = _ffn_fwd(x, p["ffn2_norm"][l][None, :], p["ffn2_w1"][l], p["ffn2_w2"][l], tag + "_ffn2")
        saved.append((s1, s2, s3))
        mws.append(mw)
    loss, dx, dfinal = _loss_head(x, target, p["final_norm"][None, :], "loss_head")
    layer_grads = [None] * depth
    for l in reversed(range(depth)):
        tag = f"l{l}"
        s1, s2, s3 = saved[l]
        dx, (dn2, dw1_2, dw2_2) = _ffn_bwd(dx, s3, p["ffn2_norm"][l][None, :], p["ffn2_w1"][l], p["ffn2_w2"][l], tag + "_ffn2")
        dx, gm = _mix_bwd(dx, s2, p, mws[l], l, rope, tag + "_mix")
        dx, (dn1, dw1_1, dw2_1) = _ffn_bwd(dx, s1, p["ffn1_norm"][l][None, :], p["ffn1_w1"][l], p["ffn1_w2"][l], tag + "_ffn1")
        g = _mix_grads_to_params(gm, p)
        g.update(ffn1_norm=dn1[0], ffn1_w1=dw1_1, ffn1_w2=dw2_1, ffn2_norm=dn2[0], ffn2_w1=dw1_2, ffn2_w2=dw2_2)
        layer_grads[l] = g
    grads = {k: jnp.stack([lg[k] for lg in layer_grads]) for k in layer_grads[0]}
    grads["final_norm"] = dfinal[0]
    return loss, dx, grads


ANY = pl.BlockSpec(memory_space=pl.ANY)
VMEM_WHOLE = pl.BlockSpec(memory_space=pltpu.VMEM)
PACK_COLS = 1024
N_CHIPS = 4


def _place():
    x, y, c = (lax.axis_index(a) for a in MESH_AXES)
    return x, y, c, [(1 - x, y), (x, 1 - y), (1 - x, 1 - y)]


def _remote(src, dst, send_sem, recv_sem, device):
    return pltpu.make_async_remote_copy(src_ref=src, dst_ref=dst, send_sem=send_sem, recv_sem=recv_sem, device_id=device,
                                        device_id_type=pl.DeviceIdType.MESH)


def _half_rows(c, half):
    return pl.ds(pl.multiple_of(c * half, 16), half)


def _gather_chips(shard, name):
    r, cols = shard.shape
    half = r // 2

    def body(in_ref, out_ref, send_sems, recv_sems, local_sem):
        x, y, c, chips = _place()
        me, sibling = 2 * x + y, (x, y, 1 - c)
        mine, other = _half_rows(c, half), _half_rows(1 - c, half)
        local = pltpu.make_async_copy(in_ref, out_ref.at[me], local_sem)
        local.start()
        sent = [_remote(in_ref.at[mine], out_ref.at[me, mine], send_sems.at[j], recv_sems.at[j], (cx, cy, c))
                for j, (cx, cy) in enumerate(chips)]
        for cp in sent:
            cp.start()
        for j, (cx, cy) in enumerate(chips):
            rows = out_ref.at[2 * cx + cy, mine]
            _remote(rows, rows, send_sems.at[j], recv_sems.at[j], (cx, cy, c)).wait_recv()
            passed = _remote(rows, rows, send_sems.at[3 + j], recv_sems.at[3 + j], sibling)
            passed.start()
            sent.append(passed)
        for j, (cx, cy) in enumerate(chips):
            rows = out_ref.at[2 * cx + cy, other]
            _remote(rows, rows, send_sems.at[3 + j], recv_sems.at[3 + j], sibling).wait_recv()
        for cp in sent:
            cp.wait_send()
        local.wait()

    return pl.pallas_call(
        body, name=name, in_specs=[ANY], out_specs=ANY, out_shape=S((N_CHIPS, r, cols), shard.dtype),
        scratch_shapes=[pltpu.SemaphoreType.DMA((6,)), pltpu.SemaphoreType.DMA((6,)), pltpu.SemaphoreType.DMA(())],
    )(shard)


def _allreduce_all(v, name):
    r, cols = v.shape

    def body(v_ref, out_ref, buf, send_sems, recv_sems):
        x, y, c, chips = _place()
        sibling = (x, y, 1 - c)
        slot = lambda px, py, pc: buf.at[4 * px + 2 * py + pc]
        buf[4 * x + 2 * y + c] = v_ref[...]
        sent = [_remote(v_ref, slot(x, y, c), send_sems.at[0], recv_sems.at[0], sibling)]
        sent += [_remote(v_ref, slot(x, y, c), send_sems.at[1 + j], recv_sems.at[1 + j], (cx, cy, c)) for j, (cx, cy) in enumerate(chips)]
        for cp in sent:
            cp.start()
        for j, (cx, cy) in enumerate(chips):
            blk = slot(cx, cy, c)
            _remote(blk, blk, send_sems.at[1 + j], recv_sems.at[1 + j], (cx, cy, c)).wait_recv()
            passed = _remote(blk, blk, send_sems.at[4 + j], recv_sems.at[4 + j], sibling)
            passed.start()
            sent.append(passed)
        blk = slot(x, y, 1 - c)
        _remote(blk, blk, send_sems.at[0], recv_sems.at[0], sibling).wait_recv()
        for j, (cx, cy) in enumerate(chips):
            blk = slot(cx, cy, 1 - c)
            _remote(blk, blk, send_sems.at[4 + j], recv_sems.at[4 + j], sibling).wait_recv()
        for cp in sent:
            cp.wait_send()
        acc = buf[0]
        for k in range(1, 2 * N_CHIPS):
            acc = acc + buf[k]
        out_ref[...] = acc

    return pl.pallas_call(
        body, name=name, in_specs=[VMEM_WHOLE], out_specs=VMEM_WHOLE, out_shape=S((r, cols), F32),
        scratch_shapes=[pltpu.VMEM((2 * N_CHIPS, r, cols), F32), pltpu.SemaphoreType.DMA((7,)), pltpu.SemaphoreType.DMA((7,))],
        compiler_params=pltpu.CompilerParams(vmem_limit_bytes=VMEM_LIMIT),
    )(v)


def _pair_exchange(g, name):
    n, r, cols = g.shape
    half = r // 2

    def body(g_ref, a_ref, send_sem, recv_sem):
        x, y, c, _ = _place()
        cp = _remote(g_ref.at[pl.ds(0, n), _half_rows(1 - c, half)], a_ref, send_sem, recv_sem, (x, y, 1 - c))
        cp.start()
        cp.wait()

    return pl.pallas_call(body, name=name, in_specs=[ANY], out_specs=ANY, out_shape=S((n, half, cols), g.dtype),
                          scratch_shapes=[pltpu.SemaphoreType.DMA(()), pltpu.SemaphoreType.DMA(())])(g)


def _pair_sum(g, a, name):
    n, r, cols = g.shape
    half = r // 2
    tr = _tile(half, 512, 16)
    n_blk = half // tr

    def body(c_ref, g_ref, a_ref, o_ref):
        o_ref[...] = (g_ref[...] + a_ref[...]).astype(o_ref.dtype)

    blk = pl.BlockSpec((1, tr, cols), lambda j, i, c_ref: (j, i, 0))
    return pl.pallas_call(
        body, name=name, out_shape=S((n, half, cols), BF16),
        grid_spec=pltpu.PrefetchScalarGridSpec(
            num_scalar_prefetch=1, grid=(n, n_blk),
            in_specs=[pl.BlockSpec((1, tr, cols), lambda j, i, c_ref: (j, c_ref[0] * n_blk + i, 0)), blk], out_specs=blk),
        compiler_params=_params(("parallel", "parallel")),
    )(lax.axis_index("c").reshape(1).astype(jnp.int32), g, a)


def _chip_exchange(pb, name):
    n, h, cols = pb.shape

    def body(p_ref, q_ref, send_sems, recv_sems, local_sem):
        x, y, c, chips = _place()
        me = 2 * x + y
        local = pltpu.make_async_copy(p_ref.at[me], q_ref.at[me], local_sem)
        local.start()
        sent = [_remote(p_ref.at[2 * cx + cy], q_ref.at[me], send_sems.at[j], recv_sems.at[j], (cx, cy, c)) for j, (cx, cy) in enumerate(chips)]
        for cp in sent:
            cp.start()
        for j, (cx, cy) in enumerate(chips):
            blk = q_ref.at[2 * cx + cy]
            _remote(blk, blk, send_sems.at[j], recv_sems.at[j], (cx, cy, c)).wait_recv()
        for cp in sent:
            cp.wait_send()
        local.wait()

    return pl.pallas_call(body, name=name, in_specs=[ANY], out_specs=ANY, out_shape=S((n, h, cols), pb.dtype),
                          scratch_shapes=[pltpu.SemaphoreType.DMA((3,)), pltpu.SemaphoreType.DMA((3,)), pltpu.SemaphoreType.DMA(())])(pb)


def _quad_sum(q, name):
    n, h, cols = q.shape
    tr = _tile(h, 512, 16)

    def body(*refs):
        acc = refs[0][0].astype(F32)
        for ref in refs[1:n]:
            acc = acc + ref[0].astype(F32)
        refs[n][...] = acc

    in_specs = [pl.BlockSpec((1, tr, cols), functools.partial(lambda i, k: (k, i, 0), k=k)) for k in range(n)]
    return pl.pallas_call(body, name=name, grid=(h // tr,), in_specs=in_specs, out_specs=pl.BlockSpec((tr, cols), lambda i: (i, 0)),
                          out_shape=S((h, cols), F32), compiler_params=_params(("parallel",)))(*([q] * n))


def _pair_gather(rh, name):
    h, cols = rh.shape

    def body(r_ref, o_ref, send_sem, recv_sem, local_sem):
        x, y, c, _ = _place()
        mine = _half_rows(c, h)
        local = pltpu.make_async_copy(r_ref, o_ref.at[mine], local_sem)
        local.start()
        cp = _remote(r_ref, o_ref.at[mine], send_sem, recv_sem, (x, y, 1 - c))
        cp.start()
        other = o_ref.at[_half_rows(1 - c, h)]
        _remote(other, other, send_sem, recv_sem, (x, y, 1 - c)).wait_recv()
        cp.wait_send()
        local.wait()

    return pl.pallas_call(body, name=name, in_specs=[ANY], out_specs=ANY, out_shape=S((2 * h, cols), rh.dtype),
                          scratch_shapes=[pltpu.SemaphoreType.DMA(()), pltpu.SemaphoreType.DMA(()), pltpu.SemaphoreType.DMA(())])(rh)


def _adamw(w, g, m, v, name):
    shape = w.shape
    w, g, m, v = (t.reshape(-1, shape[-1]) for t in (w, g, m, v))
    r, cols = w.shape
    tr = _tile(r, 256, 8)

    def body(w_ref, g_ref, m_ref, v_ref, d_ref, mo_ref, vo_ref):
        g_ = g_ref[...]
        m_ = ADAM_B1 * m_ref[...] + (1.0 - ADAM_B1) * g_
        v_ = ADAM_B2 * v_ref[...] + (1.0 - ADAM_B2) * jnp.square(g_)
        m_hat = m_ / (1.0 - ADAM_B1 ** ADAM_STEP)
        v_hat = v_ / (1.0 - ADAM_B2 ** ADAM_STEP)
        d_ref[...] = -ADAM_LR * (m_hat / (jnp.sqrt(v_hat) + ADAM_EPS) + ADAM_WD * w_ref[...])
        mo_ref[...] = m_
        vo_ref[...] = v_

    blk = pl.BlockSpec((tr, cols), lambda i: (i, 0))
    outs = pl.pallas_call(body, name=name, grid=(r // tr,), in_specs=[blk] * 4, out_specs=[blk] * 3, out_shape=[S((r, cols), F32)] * 3,
                          compiler_params=_params(("parallel",)))(w, g, m, v)
    return [o.reshape(shape) for o in outs]


WEIGHTS = ("ffn1_norm", "ffn1_w1", "ffn1_w2", "mix_norm", "w_in", "b_in", "lru_conv_w", "lru_conv_b", "lru_w_gate", "lru_b_gate",
           "lru_lambda", "lru_w_out", "q_norm", "w_uq", "kv_norm", "w_ukv", "mla_w_o", "conv_dw_w", "conv_dw_b", "conv_ln_g",
           "conv_ln_b", "conv_w_out", "conv_b_out", "w_out", "ffn2_norm", "ffn2_w1", "ffn2_w2", "final_norm")
ROW_SHARDED = ("ffn1_w2", "w_out", "ffn2_w2")
COL_SHARDED = ("ffn1_w1", "w_in", "lru_conv_w", "lru_w_out", "w_uq", "w_ukv", "mla_w_o", "conv_dw_w", "conv_w_out", "ffn2_w1")
F32_SHARDED = ("lru_conv_w", "conv_dw_w")
SHARDED = tuple(n for n in WEIGHTS if n in ROW_SHARDED + COL_SHARDED)
REPLICATED = tuple(n for n in WEIGHTS if n not in SHARDED)
INPUTS = ("x", "positions") + WEIGHTS + ("loss_target",) + tuple("m_" + n for n in WEIGHTS) + tuple("v_" + n for n in WEIGHTS)


def _pack(arrays, dtype, cols, row_unit):
    flat = jnp.concatenate([a.astype(dtype).reshape(-1) for a in arrays])
    unit = cols * row_unit
    return jnp.pad(flat, (0, -flat.shape[0] % unit)).reshape(-1, cols)


def _unpack(flat, shapes):
    out, off = [], 0
    for shp in shapes:
        n = 1
        for s_ in shp:
            n *= s_
        out.append(flat[..., off:off + n].reshape(flat.shape[:-1] + tuple(shp)))
        off += n
    return out


def _whole(gathered, name):
    n, l, r, c = gathered.shape
    if name in ROW_SHARDED:
        return gathered.transpose(1, 0, 2, 3).reshape(l, n * r, c)
    return gathered.transpose(1, 2, 0, 3).reshape(l, r, n * c)


def _by_chip(whole, name):
    l, r, c = whole.shape
    if name in ROW_SHARDED:
        return whole.reshape(l, N_CHIPS, r // N_CHIPS, c).transpose(1, 0, 2, 3).reshape(N_CHIPS, -1)
    return whole.reshape(l, r, N_CHIPS, c // N_CHIPS).transpose(2, 0, 1, 3).reshape(N_CHIPS, -1)


def _step(a):
    x, positions, target = a["x"][0], a["positions"][0], a["loss_target"][0]
    p = {n: a[n] for n in REPLICATED}
    for group, dtype in ((tuple(n for n in SHARDED if n not in F32_SHARDED), BF16), (F32_SHARDED, F32)):
        gathered = _gather_chips(_pack([a[n] for n in group], dtype, PACK_COLS, 32), "gather_" + jnp.dtype(dtype).name)
        parts = _unpack(gathered.reshape(N_CHIPS, -1), [a[n].shape for n in group])
        p.update({n: _whole(part, n) for n, part in zip(group, parts, strict=True)})
    loss, dx, grads = _local_step(x, positions, target, p)
    loss = lax.psum(loss, MESH_AXES)
    g_flat = jnp.concatenate([_by_chip(grads[n], n) for n in SHARDED], axis=1)
    g_flat = jnp.pad(g_flat, ((0, 0), (0, -g_flat.shape[1] % (PACK_COLS * 1024)))).reshape(N_CHIPS, -1, PACK_COLS)
    pair = _pair_sum(g_flat, _pair_exchange(g_flat, "grad_pair_exchange"), "grad_pair_sum")
    g_shard = _pair_gather(_quad_sum(_chip_exchange(pair, "grad_chip_exchange"), "grad_chip_sum"), "grad_pair_gather")
    g = dict(zip(SHARDED, _unpack(g_shard.reshape(-1), [a[n].shape for n in SHARDED]), strict=True))
    rep_shapes = [a[n].shape for n in REPLICATED]
    g_rep = _allreduce_all(_pack([grads[n] for n in REPLICATED], F32, LANES, 256), "grad_allreduce")
    g.update(zip(REPLICATED, _unpack(g_rep.reshape(-1), rep_shapes), strict=True))
    delta, new_m, new_v = {}, {}, {}
    for n in SHARDED:
        delta[n], new_m[n], new_v[n] = _adamw(a[n], g[n], a["m_" + n], a["v_" + n], "adamw_" + n)
    packed = [_pack([a[pre + n] for n in REPLICATED], F32, LANES, 256) for pre in ("", "m_", "v_")]
    for out, res in zip((delta, new_m, new_v), _adamw(packed[0], g_rep, packed[1], packed[2], "adamw_replicated"), strict=True):
        out.update(zip(REPLICATED, _unpack(res.reshape(-1), rep_shapes), strict=True))
    return (loss, dx[None], *[g[n] for n in WEIGHTS], *[delta[n] for n in WEIGHTS], *[new_m[n] for n in WEIGHTS], *[new_v[n] for n in WEIGHTS])


def kernel(x, positions, ffn1_norm, ffn1_w1, ffn1_w2, mix_norm, w_in, b_in, lru_conv_w, lru_conv_b, lru_w_gate, lru_b_gate, lru_lambda, lru_w_out, q_norm, w_uq, kv_norm, w_ukv, mla_w_o, conv_dw_w, conv_dw_b, conv_ln_g, conv_ln_b, conv_w_out, conv_b_out, w_out, ffn2_norm, ffn2_w1, ffn2_w2, final_norm, loss_target, m_ffn1_norm, m_ffn1_w1, m_ffn1_w2, m_mix_norm, m_w_in, m_b_in, m_lru_conv_w, m_lru_conv_b, m_lru_w_gate, m_lru_b_gate, m_lru_lambda, m_lru_w_out, m_q_norm, m_w_uq, m_kv_norm, m_w_ukv, m_mla_w_o, m_conv_dw_w, m_conv_dw_b, m_conv_ln_g, m_conv_ln_b, m_conv_w_out, m_conv_b_out, m_w_out, m_ffn2_norm, m_ffn2_w1, m_ffn2_w2, m_final_norm, v_ffn1_norm, v_ffn1_w1, v_ffn1_w2, v_mix_norm, v_w_in, v_b_in, v_lru_conv_w, v_lru_conv_b, v_lru_w_gate, v_lru_b_gate, v_lru_lambda, v_lru_w_out, v_q_norm, v_w_uq, v_kv_norm, v_w_ukv, v_mla_w_o, v_conv_dw_w, v_conv_dw_b, v_conv_ln_g, v_conv_ln_b, v_conv_w_out, v_conv_b_out, v_w_out, v_ffn2_norm, v_ffn2_w1, v_ffn2_w2, v_final_norm):
    return _step(dict(zip(INPUTS, (x, positions, ffn1_norm, ffn1_w1, ffn1_w2, mix_norm, w_in, b_in, lru_conv_w, lru_conv_b, lru_w_gate, lru_b_gate, lru_lambda, lru_w_out, q_norm, w_uq, kv_norm, w_ukv, mla_w_o, conv_dw_w, conv_dw_b, conv_ln_g, conv_ln_b, conv_w_out, conv_b_out, w_out, ffn2_norm, ffn2_w1, ffn2_w2, final_norm, loss_target, m_ffn1_norm, m_ffn1_w1, m_ffn1_w2, m_mix_norm, m_w_in, m_b_in, m_lru_conv_w, m_lru_conv_b, m_lru_w_gate, m_lru_b_gate, m_lru_lambda, m_lru_w_out, m_q_norm, m_w_uq, m_kv_norm, m_w_ukv, m_mla_w_o, m_conv_dw_w, m_conv_dw_b, m_conv_ln_g, m_conv_ln_b, m_conv_w_out, m_conv_b_out, m_w_out, m_ffn2_norm, m_ffn2_w1, m_ffn2_w2, m_final_norm, v_ffn1_norm, v_ffn1_w1, v_ffn1_w2, v_mix_norm, v_w_in, v_b_in, v_lru_conv_w, v_lru_conv_b, v_lru_w_gate, v_lru_b_gate, v_lru_lambda, v_lru_w_out, v_q_norm, v_w_uq, v_kv_norm, v_w_ukv, v_mla_w_o, v_conv_dw_w, v_conv_dw_b, v_conv_ln_g, v_conv_ln_b, v_conv_w_out, v_conv_b_out, v_w_out, v_ffn2_norm, v_ffn2_w1, v_ffn2_w2, v_final_norm), strict=True)))
```

```python
import functools

import jax
import jax.numpy as jnp
from jax import lax
from jax.experimental import pallas as pl
from jax.experimental.pallas import tpu as pltpu

F32, BF16 = jnp.float32, jnp.bfloat16
S = jax.ShapeDtypeStruct

LANES = 128
VMEM_LIMIT = 56 * 2**20
NORM_EPS = 1e-6
LRU_C = 8.0
MLA_HEADS = 8
QK_NOPE, QK_ROPE, V_HEAD = 64, 32, 64
HEAD_PAD = 128
ROPE_THETA = 10000.0
ADAM_LR, ADAM_B1, ADAM_B2, ADAM_EPS, ADAM_WD, ADAM_STEP = 0.001, 0.9, 0.999, 1e-08, 0.01, 10
MESH_AXES = ("x", "y", "c")
N_CHIPS = 4
NT =(((1,), (1,)), ((), ()))
TN = (((0,), (0,)), ((), ()))
NN = (((1,), (0,)), ((), ()))


def _tile(n, cap, unit=LANES):
    best = None
    for d in range(unit, min(n, cap) + 1, unit):
        if n % d == 0:
            best = d
    return best if best is not None else n


def _params(sem):
    return pltpu.CompilerParams(dimension_semantics=sem, vmem_limit_bytes=VMEM_LIMIT)


def _mm(a, b, mode="nn", out_dtype=F32, bias=None, res=None, alpha=1.0, b_chips=False, out_chips=False, name="mm"):
    n_unit = k_unit = None
    if b_chips:
        chips, rows, c = b.shape
        b_shape = (rows, chips * c)
        n_unit, k_unit = (c, None) if mode == "nn" else (None, c)
    else:
        b_shape = b.shape
    if mode == "nn":
        (m, k), (k2, n), dims = a.shape, b_shape, NN
    elif mode == "nt":
        (m, k), (n, k2), dims = a.shape, b_shape, NT
    else:
        (k, m), (k2, n), dims = a.shape, b_shape, TN
    assert k == k2 and not (b_chips and mode == "tn"), (name, a.shape, b.shape, mode)
    if out_chips:
        n_unit = n // N_CHIPS
    tm = _tile(m, 512)
    tn = _tile(n_unit or n, 1536)
    tk = k if (k <= 3072 and k_unit is None) else _tile(k_unit or k, 3072)
    nk = k // tk
    if mode == "tn":
        a_spec = pl.BlockSpec((tk, tm), lambda i, j, kk: (kk, i))
    else:
        a_spec = pl.BlockSpec((tm, tk), lambda i, j, kk: (i, kk))
    if b_chips and mode == "nn":
        b_spec = pl.BlockSpec((None, tk, tn), functools.partial(lambda i, j, kk, per: (j // per, kk, j % per), per=n_unit // tn))
    elif b_chips:
        b_spec = pl.BlockSpec((None, tn, tk), functools.partial(lambda i, j, kk, per: (kk // per, j, kk % per), per=k_unit // tk))
    elif mode == "nt":
        b_spec = pl.BlockSpec((tn, tk), lambda i, j, kk: (j, kk))
    else:
        b_spec = pl.BlockSpec((tk, tn), lambda i, j, kk: (kk, j))
    if out_chips:
        out_spec = pl.BlockSpec((None, tm, tn), functools.partial(lambda i, j, kk, per: (j // per, i, j % per), per=n_unit // tn))
        out_shape = S((N_CHIPS, m, n_unit), out_dtype)
    else:
        out_spec = pl.BlockSpec((tm, tn), lambda i, j, kk: (i, j))
        out_shape = S((m, n), out_dtype)
    operands, in_specs = [a, b], [a_spec, b_spec]
    if bias is not None:
        operands.append(bias)
        in_specs.append(pl.BlockSpec((1, tn), lambda i, j, kk: (0, j)))
    if res is not None:
        operands.append(res)
        in_specs.append(pl.BlockSpec((tm, tn), lambda i, j, kk: (i, j)))

    def body(*refs):
        a_ref, b_ref = refs[0], refs[1]
        pos = 2
        bias_ref = res_ref = None
        if bias is not None:
            bias_ref, pos = refs[pos], pos + 1
        if res is not None:
            res_ref, pos = refs[pos], pos + 1
        o_ref = refs[pos]
        part = lax.dot_general(a_ref[...].astype(BF16), b_ref[...].astype(BF16), dims, preferred_element_type=F32)

        def finish(acc):
            out = acc if alpha == 1.0 else acc * alpha
            if bias_ref is not None:
                out = out + bias_ref[...]
            if res_ref is not None:
                out = out + res_ref[...]
            o_ref[...] = out.astype(o_ref.dtype)

        if nk == 1:
            finish(part)
        else:
            acc_ref = refs[pos + 1]
            kk = pl.program_id(2)

            @pl.when(kk == 0)
            def _():
                acc_ref[...] = part

            @pl.when(kk > 0)
            def _():
                acc_ref[...] += part

            @pl.when(kk == nk - 1)
            def _():
                finish(acc_ref[...])

    return pl.pallas_call(
        body, name=name, grid=(m // tm, n // tn, nk), in_specs=in_specs, out_specs=out_spec, out_shape=out_shape,
        scratch_shapes=[pltpu.VMEM((tm, tn), F32)] if nk > 1 else [],
        compiler_params=_params(("parallel", "parallel", "arbitrary")),
    )(*operands)


def _rowwise(fn, rows, params=(), outs=(), accs=(), tt=256, name="rowwise"):
    rows = [r if isinstance(r, tuple) else (r, r.shape[1], 0) for r in rows]
    t = rows[0][0].shape[0]
    tt = min(tt, t)
    n_rows, n_par, n_out = len(rows), len(params), len(outs)
    in_specs = [pl.BlockSpec((tt, w), functools.partial(lambda i, cb: (i, cb), cb=cb)) for (_, w, cb) in rows]
    in_specs += [pl.BlockSpec(p.shape, functools.partial(lambda i, nd: (0,) * nd, nd=p.ndim)) for p in params]
    out_shape = [S((t, w), dt) for (w, dt) in outs] + [S(shape, dt) for (shape, dt) in accs]
    out_specs = [pl.BlockSpec((tt, w), lambda i: (i, 0)) for (w, _) in outs]
    out_specs += [pl.BlockSpec(shape, functools.partial(lambda i, nd: (0,) * nd, nd=len(shape))) for (shape, _) in accs]

    def body(*refs):
        vals = [r[...] for r in refs[:n_rows + n_par]]
        o_vals, a_vals = fn(*vals)
        o_refs = refs[n_rows + n_par:n_rows + n_par + n_out]
        a_refs = refs[n_rows + n_par + n_out:]
        for ref, val in zip(o_refs, o_vals, strict=True):
            ref[...] = val.astype(ref.dtype)
        i = pl.program_id(0)
        for ref, val in zip(a_refs, a_vals, strict=True):
            @pl.when(i == 0)
            def _(ref=ref, val=val):
                ref[...] = val.astype(ref.dtype)

            @pl.when(i > 0)
            def _(ref=ref, val=val):
                ref[...] += val.astype(ref.dtype)

    res = pl.pallas_call(
        body, name=name, grid=(t // tt,), in_specs=in_specs, out_specs=out_specs, out_shape=out_shape,
        compiler_params=_params(("arbitrary",) if accs else ("parallel",)),
    )(*[r[0] for r in rows], *params)
    return res


def _rms(x, g):
    x = x.astype(F32)
    return x * lax.rsqrt(jnp.mean(x * x, axis=-1, keepdims=True) + NORM_EPS) * g


def _layer_norm_silu(x, g, b):
    mu = jnp.mean(x, axis=-1, keepdims=True)
    var = jnp.mean(jnp.square(x - mu), axis=-1, keepdims=True)
    return jax.nn.silu((x - mu) * lax.rsqrt(var + NORM_EPS) * g + b)


def _neg_expm1(z):
    series = -z * (1.0 + z * (0.5 + z * (1.0 / 6.0 + z * (1.0 / 24.0 + z * (1.0 / 120.0)))))
    return jnp.where(z > -0.05, series, 1.0 - jnp.exp(z))


def _shift_down(x, s, fill=0.0):
    if s == 0:
        return x
    row = lax.broadcasted_iota(jnp.int32, x.shape, 0)
    return jnp.where(row >= s, pltpu.roll(x, s, 0), fill)


def _shift_up(x, s, fill=0.0):
    if s == 0:
        return x
    t = x.shape[0]
    row = lax.broadcasted_iota(jnp.int32, x.shape, 0)
    return jnp.where(row < t - s, pltpu.roll(x, t - s, 0), fill)


def _scan(a, u, shift):
    t, d = a.shape[0], 1
    while d < t:
        u = u + a * shift(u, d, 0.0)
        if 2 * d < t:
            a = a * shift(a, d, 1.0)
        d *= 2
    return u


def _lru_gates(xa, wr, wi, br, bi, lam):
    xb = xa.astype(BF16)
    r = jax.nn.sigmoid(jnp.dot(xb, wr.astype(BF16), preferred_element_type=F32) + br)
    i = jax.nn.sigmoid(jnp.dot(xb, wi.astype(BF16), preferred_element_type=F32) + bi)
    log_a = -LRU_C * r * jax.nn.softplus(-lam)
    return jnp.exp(log_a), jnp.sqrt(_neg_expm1(2.0 * log_a)) * (i * xa)


def _conv_fwd(x, w_ref, b, width):
    y = b + w_ref[pl.ds(width - 1, 1), :] * x
    for j in range(width - 1):
        y = y + w_ref[pl.ds(j, 1), :] * _shift_down(x, width - 1 - j)
    return y


def _conv_bwd(x, dy, w_ref, dw_ref, width):
    dx = w_ref[pl.ds(width - 1, 1), :] * dy
    dw_ref[pl.ds(width - 1, 1), :] = jnp.sum(dy * x, axis=0, keepdims=True)
    for j in range(width - 1):
        s = width - 1 - j
        dx = dx + w_ref[pl.ds(j, 1), :] * _shift_up(dy, s)
        dw_ref[pl.ds(j, 1), :] = jnp.sum(dy * _shift_down(x, s), axis=0, keepdims=True)
    return dx


def _rope(z, c, s1, s2):
    return z * c + pltpu.roll(z, HEAD_PAD - QK_ROPE // 2, 1) * s1 + pltpu.roll(z, QK_ROPE // 2, 1) * s2


def _rope_t(d, c, s1, s2):
    return d * c + pltpu.roll(d * s1, QK_ROPE // 2, 1) + pltpu.roll(d * s2, HEAD_PAD - QK_ROPE // 2, 1)


def _heads(z):
    return [z[:, h * HEAD_PAD:(h + 1) * HEAD_PAD] for h in range(z.shape[1] // HEAD_PAD)]


def _chan_spec(t, c_off=0):
    return pl.BlockSpec((t, LANES), lambda c: (0, c_off + c))


def _lru_specs(t, n_tiles, width):
    vec = pl.BlockSpec((1, LANES), lambda c: (0, c))
    mat = pl.BlockSpec((1, LANES, LANES), lambda c: (c, 0, 0))
    return [_chan_spec(t), _chan_spec(t, n_tiles), pl.BlockSpec((width, LANES), lambda c: (0, c)), vec, mat, mat, vec, vec, vec]


def _lru_fwd(pa, cw, cb, wr, wi, br, bi, lam, name):
    t, w = pa.shape[0], pa.shape[1] // 2
    n_tiles, width = w // LANES, cw.shape[0]

    def body(x_ref, g_ref, cw_ref, cb_ref, wr_ref, wi_ref, br_ref, bi_ref, lam_ref, y_ref):
        xa = _conv_fwd(x_ref[...], cw_ref, cb_ref[...], width)
        a, u = _lru_gates(xa, wr_ref[0], wi_ref[0], br_ref[...], bi_ref[...], lam_ref[...])
        h = _scan(a, u, _shift_down)
        y_ref[...] = (h * jax.nn.gelu(g_ref[...])).astype(y_ref.dtype)

    return pl.pallas_call(
        body, name=name, grid=(n_tiles,), in_specs=_lru_specs(t, n_tiles, width), out_specs=_chan_spec(t),
        out_shape=S((t, w), BF16), compiler_params=_params(("parallel",)),
    )(pa, pa, cw, cb, wr, wi, br, bi, lam)


def _lru_bwd(pa, dy, cw, cb, wr, wi, br, bi, lam, name):
    t, w = pa.shape[0], pa.shape[1] // 2
    n_tiles, width = w // LANES, cw.shape[0]

    def body(x_ref, g_ref, cw_ref, cb_ref, wr_ref, wi_ref, br_ref, bi_ref, lam_ref, dy_ref,
             dx_ref, dg_ref, dcw_ref, dcb_ref, dwr_ref, dwi_ref, dbr_ref, dbi_ref, dlam_ref, sx_ref, sg_ref):
        x = x_ref[...]
        xa = _conv_fwd(x, cw_ref, cb_ref[...], width)
        (a, u), gates_vjp = jax.vjp(_lru_gates, xa, wr_ref[0], wi_ref[0], br_ref[...], bi_ref[...], lam_ref[...])
        h = _scan(a, u, _shift_down)
        _, out_vjp = jax.vjp(lambda h_, g_: h_ * jax.nn.gelu(g_), h, g_ref[...])
        dh, dgate = out_vjp(dy_ref[...])
        adj = _scan(_shift_up(a, 1), dh, _shift_up)
        dxa, dwr, dwi, dbr, dbi, dlam = gates_vjp((adj * _shift_down(h, 1), adj))
        dx = _conv_bwd(x, dxa, cw_ref, dcw_ref, width)
        dcb_ref[...] = jnp.sum(dxa, axis=0, keepdims=True)
        dx_ref[...] = dx.astype(dx_ref.dtype)
        dg_ref[...] = dgate.astype(dg_ref.dtype)
        sx_ref[...] = jnp.sum(dx, axis=0, keepdims=True)
        sg_ref[...] = jnp.sum(dgate, axis=0, keepdims=True)
        dwr_ref[0], dwi_ref[0] = dwr, dwi
        dbr_ref[...], dbi_ref[...], dlam_ref[...] = dbr, dbi, dlam

    vec = pl.BlockSpec((1, LANES), lambda c: (0, c))
    mat = pl.BlockSpec((1, LANES, LANES), lambda c: (c, 0, 0))
    vec_s, mat_s = S((1, w), F32), S((n_tiles, LANES, LANES), F32)
    return pl.pallas_call(
        body, name=name, grid=(n_tiles,), in_specs=_lru_specs(t, n_tiles, width) + [_chan_spec(t)],
        out_specs=[_chan_spec(t), _chan_spec(t), pl.BlockSpec((width, LANES), lambda c: (0, c)), vec, mat, mat, vec, vec, vec, vec, vec],
        out_shape=[S((t, w), BF16), S((t, w), BF16), S((width, w), F32), vec_s, mat_s, mat_s, vec_s, vec_s, vec_s, vec_s, vec_s],
        compiler_params=_params(("parallel",)),
    )(pa, pa, cw, cb, wr, wi, br, bi, lam, dy)


def _glu_conv_fwd(pc, cw, cb, name):
    t, c = pc.shape[0], pc.shape[1] // 2
    n_tiles, width = c // LANES, cw.shape[0]

    def body(v_ref, g_ref, cw_ref, cb_ref, y_ref):
        y_ref[...] = _conv_fwd(v_ref[...] * jax.nn.sigmoid(g_ref[...]), cw_ref, cb_ref[...], width)

    return pl.pallas_call(
        body, name=name, grid=(n_tiles,),
        in_specs=[_chan_spec(t), _chan_spec(t, n_tiles), pl.BlockSpec((width, LANES), lambda i: (0, i)), pl.BlockSpec((1, LANES), lambda i: (0, i))],
        out_specs=_chan_spec(t), out_shape=S((t, c), F32), compiler_params=_params(("parallel",)),
    )(pc, pc, cw, cb)


def _glu_conv_bwd(pc, dy, cw, name):
    t, c = pc.shape[0], pc.shape[1] // 2
    n_tiles, width = c // LANES, cw.shape[0]

    def body(v_ref, g_ref, cw_ref, dy_ref, dv_ref, dg_ref, dcw_ref, dcb_ref, sv_ref, sg_ref):
        glu = lambda v_, g_: v_ * jax.nn.sigmoid(g_)
        x, glu_vjp = jax.vjp(glu, v_ref[...], g_ref[...])
        dy_ = dy_ref[...]
        dv, dg = glu_vjp(_conv_bwd(x, dy_, cw_ref, dcw_ref, width))
        dcb_ref[...] = jnp.sum(dy_, axis=0, keepdims=True)
        dv_ref[...] = dv.astype(dv_ref.dtype)
        dg_ref[...] = dg.astype(dg_ref.dtype)
        sv_ref[...] = jnp.sum(dv, axis=0, keepdims=True)
        sg_ref[...] = jnp.sum(dg, axis=0, keepdims=True)

    vec = pl.BlockSpec((1, LANES), lambda i: (0, i))
    wspec = pl.BlockSpec((width, LANES), lambda i: (0, i))
    return pl.pallas_call(
        body, name=name, grid=(n_tiles,), in_specs=[_chan_spec(t), _chan_spec(t, n_tiles), wspec, _chan_spec(t)],
        out_specs=[_chan_spec(t), _chan_spec(t), wspec, vec, vec, vec],
        out_shape=[S((t, c), BF16), S((t, c), BF16), S((width, c), F32), S((1, c), F32), S((1, c), F32), S((1, c), F32)],
        compiler_params=_params(("parallel",)),
    )(pc, pc, cw, dy)


def _softmax_rows(q, k, causal, scale):
    s = lax.dot_general(q, k, NT, preferred_element_type=F32) * scale
    s = jnp.where(causal, s, jnp.finfo(F32).min)
    p = jnp.exp(s - jnp.max(s, axis=-1, keepdims=True))
    return p / jnp.sum(p, axis=-1, keepdims=True)


def _attn_specs(t, tq):
    return [pl.BlockSpec((tq, 2 * HEAD_PAD), lambda hp, i: (i, hp)), pl.BlockSpec((t, 2 * HEAD_PAD), lambda hp, i: (0, hp)),
            pl.BlockSpec((t, 2 * V_HEAD), lambda hp, i: (0, hp))]


def _attn_masks(t, tq):
    i = pl.program_id(1)
    row = lax.broadcasted_iota(jnp.int32, (tq, t), 0) + i * tq
    col = lax.broadcasted_iota(jnp.int32, (tq, t), 1)
    return col <= row


def _attn_fwd(qh, kh, v, name):
    t = qh.shape[0]
    tq = min(256, t)
    scale = (QK_NOPE + QK_ROPE) ** -0.5

    def body(q_ref, k_ref, v_ref, o_ref):
        causal = _attn_masks(t, tq)
        lane = lax.broadcasted_iota(jnp.int32, (t, 2 * V_HEAD), 1)
        vv = v_ref[...]
        acc = jnp.zeros((tq, 2 * V_HEAD), F32)
        for e in range(2):
            p = _softmax_rows(q_ref[:, e * HEAD_PAD:(e + 1) * HEAD_PAD], k_ref[:, e * HEAD_PAD:(e + 1) * HEAD_PAD], causal, scale)
            ve = jnp.where((lane >= V_HEAD * e) & (lane < V_HEAD * (e + 1)), vv, jnp.zeros_like(vv))
            acc = acc + jnp.dot(p.astype(BF16), ve, preferred_element_type=F32)
        o_ref[...] = acc.astype(o_ref.dtype)

    return pl.pallas_call(
        body, name=name, grid=(MLA_HEADS // 2, t // tq), in_specs=_attn_specs(t, tq),
        out_specs=pl.BlockSpec((tq, 2 * V_HEAD), lambda hp, i: (i, hp)), out_shape=S((t, MLA_HEADS * V_HEAD), BF16),
        compiler_params=_params(("parallel", "parallel")),
    )(qh, kh, v)


def _attn_bwd(qh, kh, v, do, name):
    t = qh.shape[0]
    tq = min(256, t)
    scale = (QK_NOPE + QK_ROPE) ** -0.5

    def body(q_ref, k_ref, v_ref, do_ref, dq_ref, dk_ref, dv_ref):
        causal = _attn_masks(t, tq)
        lane = lax.broadcasted_iota(jnp.int32, (tq, 2 * V_HEAD), 1)
        vv, dd = v_ref[...], do_ref[...]
        dqs, dks = [], []
        dv = jnp.zeros((t, 2 * V_HEAD), F32)
        for e in range(2):
            q, k = q_ref[:, e * HEAD_PAD:(e + 1) * HEAD_PAD], k_ref[:, e * HEAD_PAD:(e + 1) * HEAD_PAD]
            p = _softmax_rows(q, k, causal, scale)
            de = jnp.where((lane >= V_HEAD * e) & (lane < V_HEAD * (e + 1)), dd, jnp.zeros_like(dd))
            dp = lax.dot_general(de, vv, NT, preferred_element_type=F32)
            ds = (p * (dp - jnp.sum(p * dp, axis=-1, keepdims=True)) * scale).astype(BF16)
            dqs.append(jnp.dot(ds, k, preferred_element_type=F32))
            dks.append(lax.dot_general(ds, q, TN, preferred_element_type=F32))
            dv = dv + lax.dot_general(p.astype(BF16), de, TN, preferred_element_type=F32)
        dq_ref[...] = jnp.concatenate(dqs, axis=-1)
        dk = jnp.concatenate(dks, axis=-1)
        i = pl.program_id(1)

        @pl.when(i == 0)
        def _():
            dk_ref[...] = dk
            dv_ref[...] = dv

        @pl.when(i > 0)
        def _():
            dk_ref[...] += dk
            dv_ref[...] += dv

    return pl.pallas_call(
        body, name=name, grid=(MLA_HEADS // 2, t // tq),
        in_specs=_attn_specs(t, tq) + [pl.BlockSpec((tq, 2 * V_HEAD), lambda hp, i: (i, hp))],
        out_specs=[pl.BlockSpec((tq, 2 * HEAD_PAD), lambda hp, i: (i, hp)), pl.BlockSpec((t, 2 * HEAD_PAD), lambda hp, i: (0, hp)),
                   pl.BlockSpec((t, 2 * V_HEAD), lambda hp, i: (0, hp))],
        out_shape=[S((t, MLA_HEADS * HEAD_PAD), F32), S((t, MLA_HEADS * HEAD_PAD), F32), S((t, MLA_HEADS * V_HEAD), F32)],
        compiler_params=_params(("parallel", "arbitrary")),
    )(qh, kh, v, do)


def _ffn_fwd(x, g, w1, w2, tag):
    d, f = x.shape[1], w2.shape[0]
    h, = _rowwise(lambda x_, g_: ([_rms(x_, g_)], []), [x], [g], outs=[(d, BF16)], name=tag + "_rms")
    gu = _mm(h, w1, out_dtype=BF16, b_chips=True, name=tag + "_up")
    a, = _rowwise(lambda g_, u_: ([jax.nn.silu(g_.astype(F32)) * u_.astype(F32)], []), [(gu, f, 0), (gu, f, 1)],
                  outs=[(f, BF16)], name=tag + "_act")
    return _mm(a, w2, res=x, alpha=0.5, name=tag + "_down"), (x, h, gu, a)


def _ffn_bwd(dy, saved, g, w1, w2, tag):
    x, h, gu, a = saved
    d, f = x.shape[1], w2.shape[0]
    da = _mm(dy, w2, "nt", out_dtype=BF16, alpha=0.5, name=tag + "_dact")
    dw2 = _mm(a, dy, "tn", alpha=0.5, name=tag + "_dw2")

    def act_bwd(g_, u_, da_):
        _, vjp = jax.vjp(lambda p, q: jax.nn.silu(p) * q, g_.astype(F32), u_.astype(F32))
        return [jnp.concatenate(vjp(da_.astype(F32)), axis=-1)], []

    dgu, = _rowwise(act_bwd, [(gu, f, 0), (gu, f, 1), da], outs=[(2 * f, BF16)], tt=128, name=tag + "_dgu")
    dw1 = _mm(h, dgu, "tn", out_chips=True, name=tag + "_dw1")
    dh = _mm(dgu, w1, "nt", b_chips=True, name=tag + "_dh")
    dx, dg = _rms_bwd(x, dh, dy, g, tag + "_drms")
    return dx, (dg, dw1, dw2)


def _rms_bwd(x, dh, dres, g, name):
    def fn(x_, dh_, dres_, g_):
        _, vjp = jax.vjp(_rms, x_, g_)
        dx, dg = vjp(dh_)
        return [dx + dres_], [dg]

    return _rowwise(fn, [x, dh, dres], [g], outs=[(x.shape[1], F32)], accs=[((1, x.shape[1]), F32)], name=name)


def _rope_tables(positions, name):
    half = QK_ROPE // 2
    inv = ROPE_THETA ** (-jnp.arange(0, QK_ROPE, 2, dtype=F32) / QK_ROPE)
    inv_lanes = jnp.zeros((1, HEAD_PAD), F32).at[0, QK_NOPE:QK_NOPE + QK_ROPE].set(jnp.tile(inv, 2))

    def fn(pos, inv_):
        ang = pos.astype(F32) * inv_
        lane = lax.broadcasted_iota(jnp.int32, ang.shape, 1)
        cos, sin = jnp.cos(ang), jnp.sin(ang)
        c = jnp.where(lane < QK_NOPE, 1.0, jnp.where(lane < QK_NOPE + QK_ROPE, cos, 0.0))
        s1 = jnp.where((lane >= QK_NOPE) & (lane < QK_NOPE + half), -sin, 0.0)
        s2 = jnp.where((lane >= QK_NOPE + half) & (lane < QK_NOPE + QK_ROPE), sin, 0.0)
        return [c, s1, s2], []

    return _rowwise(fn, [positions.reshape(-1, 1)], [inv_lanes], outs=[(HEAD_PAD, F32)] * 3, name=name)


def _chip_cols(g, lo, hi):
    c = g.shape[-1]
    parts = [g[j, :, max(lo, j * c) - j * c:min(hi, (j + 1) * c) - j * c] for j in range(g.shape[0]) if max(lo, j * c) < min(hi, (j + 1) * c)]
    return parts[0] if len(parts) == 1 else jnp.concatenate(parts, axis=-1)


def _cols_by_chip(segments, c):
    out, start = [[] for _ in range(N_CHIPS)], 0
    for arr, first, width in segments:
        for j in range(N_CHIPS):
            lo, hi = max(start, j * c), min(start + width, (j + 1) * c)
            if lo < hi:
                out[j].append(arr[:, first + lo - start:first + hi - start])
        start += width
    assert start == N_CHIPS * c
    return jnp.stack([jnp.concatenate(parts, axis=-1) for parts in out])


def _whole_cols(g):
    return g.transpose(1, 0, 2).reshape(g.shape[1], -1)


def _mix_offsets(lw):
    wl, ql, kvl = lw["lru_lambda"].shape[-1], lw["q_norm"].shape[-1], lw["kv_norm"].shape[-1]
    o1 = 2 * wl
    o2 = o1 + ql + kvl + QK_ROPE
    return wl, ql, kvl, o1, o2, o2 + 2 * lw["conv_ln_g"].shape[-1]


def _mix_weights(lw):
    wl, ql, kvl, o1, o2, o3 = _mix_offsets(lw)
    w_in, b_in = lw["w_in"], lw["b_in"][None, :]
    d_in = b_in.shape[1]
    z = lambda m, n: jnp.zeros(m.shape[:-1] + (n,), m.dtype)
    w_b = jnp.concatenate([_chip_cols(w_in, o1 + ql, o1 + ql + kvl), z(w_in[0], QK_NOPE), _chip_cols(w_in, o1 + ql + kvl, o2),
                           z(w_in[0], HEAD_PAD - QK_NOPE - QK_ROPE), _chip_cols(w_in, o1, o1 + ql)], axis=-1)
    b_b = jnp.concatenate([b_in[:, o1 + ql:o1 + ql + kvl], z(b_in, QK_NOPE), b_in[:, o1 + ql + kvl:o2],
                           z(b_in, HEAD_PAD - QK_NOPE - QK_ROPE), b_in[:, o1:o1 + ql]], axis=-1)
    hd = lw["lru_w_gate"].shape[-2]
    per = LANES // hd
    eye = jnp.eye(per, dtype=F32)
    wg = lw["lru_w_gate"].reshape(-1, per, hd, 2 * hd)
    block_diag = lambda m: jnp.einsum("cedk,ef->cedfk", m, eye).reshape(-1, LANES, LANES)
    bg = lw["lru_b_gate"]
    w_uq = _whole_cols(lw["w_uq"]).reshape(ql, MLA_HEADS, QK_NOPE + QK_ROPE)
    w_ukv = _whole_cols(lw["w_ukv"]).reshape(kvl, MLA_HEADS, QK_NOPE + V_HEAD)
    pad = lambda m, n: jnp.pad(m, ((0, 0), (0, 0), (0, n)))
    return dict(
        w_a=_chip_cols(w_in, 0, o1), w_b=w_b, w_c=_chip_cols(w_in, o2, o3), w_g=_chip_cols(w_in, o3, d_in),
        b_a=b_in[:, :o1], b_b=b_b, b_c=b_in[:, o2:o3], b_g=b_in[:, o3:],
        wr=block_diag(wg[..., :hd]), wi=block_diag(wg[..., hd:]),
        br=bg[:, :hd].reshape(1, -1), bi=bg[:, hd:].reshape(1, -1),
        w_uq=pad(w_uq, HEAD_PAD - QK_NOPE - QK_ROPE).reshape(ql, -1),
        w_k=pad(w_ukv[..., :QK_NOPE], HEAD_PAD - QK_NOPE).reshape(kvl, -1),
        w_v=w_ukv[..., QK_NOPE:].reshape(kvl, -1),
    )


def _mix_fwd(x, p, mw, rope, tag):
    d = x.shape[1]
    wl, ql, kvl = p["lru_lambda"].shape[-1], p["q_norm"].shape[-1], p["kv_norm"].shape[-1]
    row = lambda name: p[name][None, :]
    h, = _rowwise(lambda x_, g_: ([_rms(x_, g_)], []), [x], [row("mix_norm")], outs=[(d, BF16)], name=tag + "_rms")
    pa = _mm(h, mw["w_a"], bias=mw["b_a"], name=tag + "_pa")
    pb = _mm(h, mw["w_b"], bias=mw["b_b"], name=tag + "_pb")
    pc = _mm(h, mw["w_c"], bias=mw["b_c"], name=tag + "_pc")
    pg = _mm(h, mw["w_g"], bias=mw["b_g"], name=tag + "_pg")
    lru_args = (p["lru_conv_w"], row("lru_conv_b"), mw["wr"], mw["wi"], mw["br"], mw["bi"], row("lru_lambda"))
    ya_pre = _lru_fwd(pa, *lru_args, name=tag + "_lru")
    y_a = _mm(ya_pre, p["lru_w_out"], b_chips=True, name=tag + "_ya")
    mla_rows = [(pb, kvl, 0), (pb, ql, (kvl + HEAD_PAD) // ql)]
    assert (kvl + HEAD_PAD) % ql == 0 and kvl % HEAD_PAD == 0
    ckvn, cqn = _rowwise(lambda kv_, q_, gk, gq: ([_rms(kv_, gk), _rms(q_, gq)], []), mla_rows, [row("kv_norm"), row("q_norm")],
                         outs=[(kvl, BF16), (ql, BF16)], name=tag + "_lat_rms")
    q0 = _mm(cqn, mw["w_uq"], name=tag + "_q")
    k0 = _mm(ckvn, mw["w_k"], name=tag + "_k")
    v = _mm(ckvn, mw["w_v"], out_dtype=BF16, name=tag + "_v")

    def rope_fwd(q_, k_, kpe, c, s1, s2):
        kr = _rope(kpe, c, s1, s2)
        return [jnp.concatenate([_rope(z, c, s1, s2) for z in _heads(q_)], axis=-1),
                jnp.concatenate([z + kr for z in _heads(k_)], axis=-1)], []

    qh, kh = _rowwise(rope_fwd, [q0, k0, (pb, HEAD_PAD, kvl // HEAD_PAD), *rope],
                      outs=[(q0.shape[1], BF16), (k0.shape[1], BF16)], name=tag + "_rope")
    o = _attn_fwd(qh, kh, v, tag + "_attn")
    y_b = _mm(o, p["mla_w_o"], b_chips=True, name=tag + "_yb")
    c2 = _glu_conv_fwd(pc, p["conv_dw_w"], row("conv_dw_b"), tag + "_conv")
    c3, = _rowwise(lambda c_, g_, b_: ([_layer_norm_silu(c_, g_, b_)], []), [c2], [row("conv_ln_g"), row("conv_ln_b")],
                   outs=[(c2.shape[1], BF16)], name=tag + "_ln")
    y_c = _mm(c3, p["conv_w_out"], bias=row("conv_b_out"), b_chips=True, name=tag + "_yc")
    merged, = _rowwise(_merge, [y_a, y_b, y_c, (pg, d, 0), (pg, d, 1), (pg, d, 2)], outs=[(d, BF16)], name=tag + "_merge")
    out = _mm(merged, p["w_out"], res=x, name=tag + "_out")
    saved = dict(x=x, h=h, pa=pa, pb=pb, pc=pc, pg=pg, ya_pre=ya_pre, y_a=y_a, y_b=y_b, y_c=y_c, ckvn=ckvn, cqn=cqn,
                 qh=qh, kh=kh, v=v, o=o, c2=c2, c3=c3, merged=merged, lru_args=lru_args)
    return out, saved


def _merge(ya, yb, yc, g0, g1, g2):
    return [jax.nn.sigmoid(g0) * ya + jax.nn.sigmoid(g1) * yb + jax.nn.sigmoid(g2) * yc], []


def _mix_bwd(dy, s, p, mw, rope, tag):
    x, h = s["x"], s["h"]
    d = x.shape[1]
    wl, ql, kvl = p["lru_lambda"].shape[-1], p["q_norm"].shape[-1], p["kv_norm"].shape[-1]
    row = lambda name: p[name][None, :]
    g = {}
    dmerged = _mm(dy, p["w_out"], "nt", name=tag + "_dmerged")
    g["w_out"] = _mm(s["merged"], dy, "tn", name=tag + "_dw_out")

    def merge_bwd(ya, yb, yc, g0, g1, g2, dm):
        _, vjp = jax.vjp(lambda *a: _merge(*a)[0][0], ya, yb, yc, g0, g1, g2)
        dya, dyb, dyc, d0, d1, d2 = vjp(dm)
        dpg = jnp.concatenate([d0, d1, d2], axis=-1)
        return [dya, dyb, dyc, dpg], [jnp.sum(dyc, axis=0, keepdims=True), jnp.sum(dpg, axis=0, keepdims=True)]

    pg = s["pg"]
    dya, dyb, dyc, dpg, g["conv_b_out"], db_g = _rowwise(
        merge_bwd, [s["y_a"], s["y_b"], s["y_c"], (pg, d, 0), (pg, d, 1), (pg, d, 2), dmerged],
        outs=[(d, BF16), (d, BF16), (d, BF16), (3 * d, BF16)], accs=[((1, d), F32), ((1, 3 * d), F32)], tt=128, name=tag + "_dmerge")
    g["lru_w_out"] = _mm(s["ya_pre"], dya, "tn", out_chips=True, name=tag + "_dw_lru_out")
    dya_pre = _mm(dya, p["lru_w_out"], "nt", b_chips=True, name=tag + "_dya_pre")
    (dpa_x, dpa_g, g["lru_conv_w"], g["lru_conv_b"], g["wr"], g["wi"], g["br"], g["bi"], g["lru_lambda"], sb_x, sb_g) = _lru_bwd(
        s["pa"], dya_pre, *s["lru_args"], name=tag + "_dlru")
    dpa = jnp.concatenate([dpa_x, dpa_g], axis=1)
    db_a = jnp.concatenate([sb_x, sb_g], axis=1)
    g["conv_w_out"] = _mm(s["c3"], dyc, "tn", out_chips=True, name=tag + "_dw_conv_out")
    dc3 = _mm(dyc, p["conv_w_out"], "nt", b_chips=True, name=tag + "_dc3")

    def ln_bwd(c_, dc_, g_, b_):
        _, vjp = jax.vjp(_layer_norm_silu, c_, g_, b_)
        dc, dg_, db_ = vjp(dc_)
        return [dc], [dg_, db_]

    cc = s["c2"].shape[1]
    dc2, g["conv_ln_g"], g["conv_ln_b"] = _rowwise(ln_bwd, [s["c2"], dc3], [row("conv_ln_g"), row("conv_ln_b")], outs=[(cc, F32)],
                                                    accs=[((1, cc), F32)] * 2, name=tag + "_dln")
    dpc_v, dpc_g, g["conv_dw_w"], g["conv_dw_b"], sc_v, sc_g = _glu_conv_bwd(s["pc"], dc2, p["conv_dw_w"], tag + "_dconv")
    dpc = jnp.concatenate([dpc_v, dpc_g], axis=1)
    db_c = jnp.concatenate([sc_v, sc_g], axis=1)
    g["mla_w_o"] = _mm(s["o"], dyb, "tn", out_chips=True, name=tag + "_dw_o")
    do = _mm(dyb, p["mla_w_o"], "nt", out_dtype=BF16, b_chips=True, name=tag + "_do")
    dqh, dkh, dv = _attn_bwd(s["qh"], s["kh"], s["v"], do, tag + "_dattn")

    def rope_bwd(dq_, dk_, c, s1, s2):
        lane = lax.broadcasted_iota(jnp.int32, c.shape, 1)
        dkr = functools.reduce(lambda a, b: a + b, _heads(dk_))
        dkpe = jnp.where((lane >= QK_NOPE) & (lane < QK_NOPE + QK_ROPE), _rope_t(dkr, c, s1, s2), 0.0)
        return [jnp.concatenate([_rope_t(z, c, s1, s2) for z in _heads(dq_)], axis=-1), dk_, dkpe], []

    dq0, dk0, dkpe = _rowwise(rope_bwd, [dqh, dkh, *rope], outs=[(dqh.shape[1], BF16), (dkh.shape[1], BF16), (HEAD_PAD, F32)],
                              name=tag + "_drope")
    dvb = dv.astype(BF16)
    g["w_uq"] = _mm(s["cqn"], dq0, "tn", name=tag + "_dw_uq")
    g["w_k"] = _mm(s["ckvn"], dk0, "tn", name=tag + "_dw_k")
    g["w_v"] = _mm(s["ckvn"], dvb, "tn", name=tag + "_dw_v")
    dcqn = _mm(dq0, mw["w_uq"], "nt", name=tag + "_dcqn")
    dckvn = _mm(dk0, mw["w_k"], "nt", name=tag + "_dckvn_k")
    dckvn = _mm(dvb, mw["w_v"], "nt", res=dckvn, name=tag + "_dckvn_v")

    def lat_bwd(kv_, q_, dkv_, dq_, dkpe_, gk, gq):
        _, vjp_k = jax.vjp(_rms, kv_, gk)
        _, vjp_q = jax.vjp(_rms, q_, gq)
        (dkv, dgk), (dq, dgq) = vjp_k(dkv_), vjp_q(dq_)
        dpb = jnp.concatenate([dkv, dkpe_, dq], axis=-1)
        return [dpb], [dgk, dgq, jnp.sum(dpb, axis=0, keepdims=True)]

    pb = s["pb"]
    dpb, g["kv_norm"], g["q_norm"], db_b = _rowwise(
        lat_bwd, [(pb, kvl, 0), (pb, ql, (kvl + HEAD_PAD) // ql), dckvn, dcqn, dkpe], [row("kv_norm"), row("q_norm")],
        outs=[(pb.shape[1], BF16)], accs=[((1, kvl), F32), ((1, ql), F32), ((1, pb.shape[1]), F32)], name=tag + "_dlat")
    dh = None
    for part, dpart in (("a", dpa), ("b", dpb), ("c", dpc), ("g", dpg)):
        g["w_" + part] = _mm(h, dpart, "tn", name=tag + "_dw_" + part)
        dh = _mm(dpart, mw["w_" + part], "nt", res=dh, name=tag + "_dh_" + part)
    g["b_a"], g["b_b"], g["b_c"], g["b_g"] = db_a, db_b, db_c, db_g
    dx, g["mix_norm"] = _rms_bwd(x, dh, dy, row("mix_norm"), tag + "_drms")
    return dx, g


def _by_chip_cols(m):
    return m.reshape(m.shape[0], N_CHIPS, -1).transpose(1, 0, 2)


def _mix_grads_to_params(g, p):
    wl, ql, kvl, o1, o2, o3 = _mix_offsets(p)
    hd = p["lru_w_gate"].shape[-2]
    per = LANES // hd
    eye = jnp.eye(per, dtype=F32)
    diag = lambda m: jnp.einsum("cedfk,ef->cedk", m.reshape(-1, per, hd, per, hd), eye).reshape(-1, hd, hd)
    out = {k: g[k] for k in ("lru_w_out", "conv_w_out", "mla_w_o", "lru_conv_w", "conv_dw_w")}
    out["w_out"] = g["w_out"].reshape(N_CHIPS, -1, g["w_out"].shape[1])
    for k in ("mix_norm", "conv_b_out", "lru_conv_b", "lru_lambda", "conv_ln_g", "conv_ln_b", "conv_dw_b", "kv_norm", "q_norm"):
        out[k] = g[k][0]
    mla = lambda m: [(m, kvl + HEAD_PAD, ql), (m, 0, kvl), (m, kvl + QK_NOPE, QK_ROPE)]
    whole = lambda m: [(m, 0, m.shape[1])]
    out["w_in"] = _cols_by_chip(whole(g["w_a"]) + mla(g["w_b"]) + whole(g["w_c"]) + whole(g["w_g"]), p["w_in"].shape[-1])
    out["b_in"] = jnp.concatenate([g["b_a"]] + [m[:, a:a + w] for m, a, w in mla(g["b_b"])] + [g["b_c"], g["b_g"]], axis=1)[0]
    out["lru_w_gate"] = jnp.concatenate([diag(g["wr"]), diag(g["wi"])], axis=-1)
    out["lru_b_gate"] = jnp.concatenate([g["br"].reshape(-1, hd), g["bi"].reshape(-1, hd)], axis=-1)
    out["w_uq"] = _by_chip_cols(g["w_uq"].reshape(ql, MLA_HEADS, HEAD_PAD)[..., :QK_NOPE + QK_ROPE].reshape(ql, -1))
    out["w_ukv"] = _by_chip_cols(jnp.concatenate([g["w_k"].reshape(kvl, MLA_HEADS, HEAD_PAD)[..., :QK_NOPE],
                                                  g["w_v"].reshape(kvl, MLA_HEADS, V_HEAD)], axis=-1).reshape(kvl, -1))
    return out


def _loss_head(x, target, g, name):
    def fn(x_, t_, g_):
        y, vjp = jax.vjp(_rms, x_, g_)
        err = y - t_
        dx, dg = vjp(err * (1.0 / x_.shape[1]))
        loss = 0.5 * jnp.sum(jnp.mean(err * err, axis=-1, keepdims=True), axis=0, keepdims=True)
        return [dx], [dg, jnp.broadcast_to(loss, (1, LANES))]

    dx, dg, loss = _rowwise(fn, [x, target], [g], outs=[(x.shape[1], F32)], accs=[((1, x.shape[1]), F32), ((1, LANES), F32)], name=name)
    return loss[0, 0], dx, dg


def _layer_fwd(x, p, rope, tag):
    mw = _mix_weights(p)
    x, s1 = _ffn_fwd(x, p["ffn1_norm"][None, :], p["ffn1_w1"], p["ffn1_w2"], tag + "_ffn1")
    x, s2 = _mix_fwd(x, p, mw, rope, tag + "_mix")
    x, s3 = _ffn_fwd(x, p["ffn2_norm"][None, :], p["ffn2_w1"], p["ffn2_w2"], tag + "_ffn2")
    return x, (s1, s2, s3, mw)


def _layer_bwd(dx, saved, p, rope, tag):
    s1, s2, s3, mw = saved
    dx, (dn2, dw1_2, dw2_2) = _ffn_bwd(dx, s3, p["ffn2_norm"][None, :], p["ffn2_w1"], p["ffn2_w2"], tag + "_ffn2")
    dx, gm = _mix_bwd(dx, s2, p, mw, rope, tag + "_mix")
    dx, (dn1, dw1_1, dw2_1) = _ffn_bwd(dx, s1, p["ffn1_norm"][None, :], p["ffn1_w1"], p["ffn1_w2"], tag + "_ffn1")
    g = _mix_grads_to_params(gm, p)
    rows_by_chip = lambda m: m.reshape(N_CHIPS, -1, m.shape[1])
    g.update(ffn1_norm=dn1[0], ffn1_w1=dw1_1, ffn1_w2=rows_by_chip(dw2_1), ffn2_norm=dn2[0], ffn2_w1=dw1_2, ffn2_w2=rows_by_chip(dw2_2))
    return dx, g


ANY = pl.BlockSpec(memory_space=pl.ANY)
VMEM_WHOLE = pl.BlockSpec(memory_space=pltpu.VMEM)


def _place():
    x, y, c = (lax.axis_index(a) for a in MESH_AXES)
    return x, y, c, [(1 - x, y), (x, 1 - y), (1 - x, 1 - y)]


def _remote(src, dst, send_sem, recv_sem, device):
    return pltpu.make_async_remote_copy(src_ref=src, dst_ref=dst, send_sem=send_sem, recv_sem=recv_sem, device_id=device,
                                        device_id_type=pl.DeviceIdType.MESH)


def _half_rows(c, half):
    return pl.ds(pl.multiple_of(c * half, 16), half)


def _gather_chips(shards, name):
    n = len(shards)
    halves = [s.shape[0] // 2 for s in shards]

    def body(*refs):
        ins, outs, send_sems, recv_sems = refs[:n], refs[n:2 * n], refs[2 * n], refs[2 * n + 1]
        x, y, c, chips = _place()
        me, sibling = 2 * x + y, (x, y, 1 - c)
        sent = []
        for i in range(n):
            mine = _half_rows(c, halves[i])
            sent += [_remote(ins[i].at[mine], outs[i].at[me, mine], send_sems.at[i, j], recv_sems.at[i, j], (cx, cy, c))
                     for j, (cx, cy) in enumerate(chips)]
            sent.append(_remote(ins[i], outs[i].at[me], send_sems.at[i, 6], recv_sems.at[i, 6], sibling))
        for cp in sent:
            cp.start()
        for i in range(n):
            mine = _half_rows(c, halves[i])
            for j, (cx, cy) in enumerate(chips):
                rows = outs[i].at[2 * cx + cy, mine]
                _remote(rows, rows, send_sems.at[i, j], recv_sems.at[i, j], (cx, cy, c)).wait_recv()
                passed = _remote(rows, rows, send_sems.at[i, 3 + j], recv_sems.at[i, 3 + j], sibling)
                passed.start()
                sent.append(passed)
        for i in range(n):
            other = _half_rows(1 - c, halves[i])
            for j, (cx, cy) in enumerate(chips):
                rows = outs[i].at[2 * cx + cy, other]
                _remote(rows, rows, send_sems.at[i, 3 + j], recv_sems.at[i, 3 + j], sibling).wait_recv()
            own = outs[i].at[me]
            _remote(own, own, send_sems.at[i, 6], recv_sems.at[i, 6], sibling).wait_recv()
        for cp in sent:
            cp.wait_send()

    return pl.pallas_call(
        body, name=name, in_specs=[ANY] * n, out_specs=[ANY] * n, out_shape=[S((N_CHIPS,) + s.shape, s.dtype) for s in shards],
        scratch_shapes=[pltpu.SemaphoreType.DMA((n, 7)), pltpu.SemaphoreType.DMA((n, 7))],
    )(*shards)


def _allreduce_all(v, name):
    r, cols = v.shape

    def body(v_ref, out_ref, buf, send_sems, recv_sems):
        x, y, c, chips = _place()
        sibling = (x, y, 1 - c)
        slot = lambda px, py, pc: buf.at[4 * px + 2 * py + pc]
        buf[4 * x + 2 * y + c] = v_ref[...]
        sent = [_remote(v_ref, slot(x, y, c), send_sems.at[0], recv_sems.at[0], sibling)]
        sent += [_remote(v_ref, slot(x, y, c), send_sems.at[1 + j], recv_sems.at[1 + j], (cx, cy, c)) for j, (cx, cy) in enumerate(chips)]
        for cp in sent:
            cp.start()
        for j, (cx, cy) in enumerate(chips):
            blk = slot(cx, cy, c)
            _remote(blk, blk, send_sems.at[1 + j], recv_sems.at[1 + j], (cx, cy, c)).wait_recv()
            passed = _remote(blk, blk, send_sems.at[4 + j], recv_sems.at[4 + j], sibling)
            passed.start()
            sent.append(passed)
        blk = slot(x, y, 1 - c)
        _remote(blk, blk, send_sems.at[0], recv_sems.at[0], sibling).wait_recv()
        for j, (cx, cy) in enumerate(chips):
            blk = slot(cx, cy, 1 - c)
            _remote(blk, blk, send_sems.at[4 + j], recv_sems.at[4 + j], sibling).wait_recv()
        for cp in sent:
            cp.wait_send()
        acc = buf[0]
        for k in range(1, 2 * N_CHIPS):
            acc = acc + buf[k]
        out_ref[...] = acc

    return pl.pallas_call(
        body, name=name, in_specs=[VMEM_WHOLE], out_specs=VMEM_WHOLE, out_shape=S((r, cols), F32),
        scratch_shapes=[pltpu.VMEM((2 * N_CHIPS, r, cols), F32), pltpu.SemaphoreType.DMA((7,)), pltpu.SemaphoreType.DMA((7,))],
        compiler_params=pltpu.CompilerParams(vmem_limit_bytes=VMEM_LIMIT),
    )(v)


def _pair_exchange(gs, name):
    n = len(gs)

    def body(*refs):
        g_refs, a_refs, send_sems, recv_sems = refs[:n], refs[n:2 * n], refs[2 * n], refs[2 * n + 1]
        x, y, c, _ = _place()
        sent = [_remote(g_refs[i].at[pl.ds(0, N_CHIPS), _half_rows(1 - c, gs[i].shape[1] // 2)], a_refs[i], send_sems.at[i], recv_sems.at[i],
                        (x, y, 1 - c)) for i in range(n)]
        for cp in sent:
            cp.start()
        for cp in sent:
            cp.wait()

    return pl.pallas_call(body, name=name, in_specs=[ANY] * n, out_specs=[ANY] * n,
                          out_shape=[S((N_CHIPS, g.shape[1] // 2, g.shape[2]), g.dtype) for g in gs],
                          scratch_shapes=[pltpu.SemaphoreType.DMA((n,)), pltpu.SemaphoreType.DMA((n,))])(*gs)


def _pair_sum(g, a, name):
    n, r, cols = g.shape
    half = r // 2
    tr = _tile(half, 512, 16)
    n_blk = half // tr

    def body(c_ref, g_ref, a_ref, o_ref):
        o_ref[...] = (g_ref[...] + a_ref[...]).astype(o_ref.dtype)

    blk = pl.BlockSpec((1, tr, cols), lambda j, i, c_ref: (j, i, 0))
    return pl.pallas_call(
        body, name=name, out_shape=S((n, half, cols), BF16),
        grid_spec=pltpu.PrefetchScalarGridSpec(
            num_scalar_prefetch=1, grid=(n, n_blk),
            in_specs=[pl.BlockSpec((1, tr, cols), lambda j, i, c_ref: (j, c_ref[0] * n_blk + i, 0)), blk], out_specs=blk),
        compiler_params=_params(("parallel", "parallel")),
    )(lax.axis_index("c").reshape(1).astype(jnp.int32), g, a)


def _chip_exchange(ps, name):
    n = len(ps)

    def body(*refs):
        p_refs, q_refs, send_sems, recv_sems = refs[:n], refs[n:2 * n], refs[2 * n], refs[2 * n + 1]
        x, y, c, chips = _place()
        sent = [_remote(p_refs[i].at[2 * cx + cy], q_refs[i].at[j], send_sems.at[i, j], recv_sems.at[i, j], (cx, cy, c))
                for i in range(n) for j, (cx, cy) in enumerate(chips)]
        for cp in sent:
            cp.start()
        for i in range(n):
            for j, (cx, cy) in enumerate(chips):
                blk = q_refs[i].at[j]
                _remote(blk, blk, send_sems.at[i, j], recv_sems.at[i, j], (cx, cy, c)).wait_recv()
        for cp in sent:
            cp.wait_send()

    return pl.pallas_call(body, name=name, in_specs=[ANY] * n, out_specs=[ANY] * n,
                          out_shape=[S((N_CHIPS - 1,) + p.shape[1:], p.dtype) for p in ps],
                          scratch_shapes=[pltpu.SemaphoreType.DMA((n, 3)), pltpu.SemaphoreType.DMA((n, 3))])(*ps)


def _quad_sum(p, q, name):
    _, h, cols = p.shape
    tr = _tile(h, 512, 16)
    n_blk = h // tr
    x, y, c, _ = _place()

    def body(s_ref, p_ref, q0_ref, q1_ref, q2_ref, o_ref):
        o_ref[...] = p_ref[0].astype(F32) + q0_ref[0].astype(F32) + q1_ref[0].astype(F32) + q2_ref[0].astype(F32)

    in_specs = [pl.BlockSpec((1, tr, cols), lambda i, s_ref: (s_ref[0], i, 0))]
    in_specs += [pl.BlockSpec((1, tr, cols), functools.partial(lambda i, s_ref, k: (k, i, 0), k=k)) for k in range(N_CHIPS - 1)]
    return pl.pallas_call(
        body, name=name, out_shape=S((2 * h, cols), F32),
        grid_spec=pltpu.PrefetchScalarGridSpec(num_scalar_prefetch=1, grid=(n_blk,), in_specs=in_specs,
                                               out_specs=pl.BlockSpec((tr, cols), lambda i, s_ref: (s_ref[1] * n_blk + i, 0))),
        compiler_params=_params(("parallel",)),
    )(jnp.stack([2 * x + y, c]).astype(jnp.int32), p, q, q, q)


def _pair_share(bufs, name):
    n = len(bufs)

    def body(*refs):
        in_refs, out_refs, send_sems, recv_sems = refs[:n], refs[n:2 * n], refs[2 * n], refs[2 * n + 1]
        x, y, c, _ = _place()
        sent = []
        for i in range(n):
            mine = _half_rows(c, bufs[i].shape[0] // 2)
            sent.append(_remote(in_refs[i].at[mine], out_refs[i].at[mine], send_sems.at[i], recv_sems.at[i], (x, y, 1 - c)))
        for cp in sent:
            cp.start()
        for i in range(n):
            other = out_refs[i].at[_half_rows(1 - c, bufs[i].shape[0] // 2)]
            _remote(other, other, send_sems.at[i], recv_sems.at[i], (x, y, 1 - c)).wait_recv()
        for cp in sent:
            cp.wait_send()

    return pl.pallas_call(body, name=name, in_specs=[ANY] * n, out_specs=[ANY] * n, out_shape=[S(b.shape, b.dtype) for b in bufs],
                          input_output_aliases={i: i for i in range(n)},
                          scratch_shapes=[pltpu.SemaphoreType.DMA((n,)), pltpu.SemaphoreType.DMA((n,))])(*bufs)


def _adamw(w, gs, m, v, name):
    depth, r, cols = w.shape
    tr = _tile(r, 256, 8)

    def body(*refs):
        w_ref, m_ref, v_ref = refs[:3]
        g_refs = refs[3:3 + depth]
        go_ref, d_ref, mo_ref, vo_ref = refs[3 + depth:]
        for l in range(depth):
            @pl.when(pl.program_id(0) == l)
            def _(l=l):
                g_ = g_refs[l][...]
                m_ = ADAM_B1 * m_ref[...] + (1.0 - ADAM_B1) * g_
                v_ = ADAM_B2 * v_ref[...] + (1.0 - ADAM_B2) * jnp.square(g_)
                m_hat = m_ / (1.0 - ADAM_B1 ** ADAM_STEP)
                v_hat = v_ / (1.0 - ADAM_B2 ** ADAM_STEP)
                d_ref[...] = -ADAM_LR * (m_hat / (jnp.sqrt(v_hat) + ADAM_EPS) + ADAM_WD * w_ref[...])
                go_ref[...], mo_ref[...], vo_ref[...] = g_, m_, v_

    blk = pl.BlockSpec((None, tr, cols), lambda l, i: (l, i, 0))
    g_blk = pl.BlockSpec((tr, cols), lambda l, i: (i, 0))
    return pl.pallas_call(body, name=name, grid=(depth, r // tr), in_specs=[blk] * 3 + [g_blk] * depth, out_specs=[blk] * 4,
                          out_shape=[S(w.shape, F32)] * 4, compiler_params=_params(("parallel", "parallel")))(w, m, v, *gs)


WEIGHTS = ("ffn1_norm", "ffn1_w1", "ffn1_w2", "mix_norm", "w_in", "b_in", "lru_conv_w", "lru_conv_b", "lru_w_gate", "lru_b_gate",
           "lru_lambda", "lru_w_out", "q_norm", "w_uq", "kv_norm", "w_ukv", "mla_w_o", "conv_dw_w", "conv_dw_b", "conv_ln_g",
           "conv_ln_b", "conv_w_out", "conv_b_out", "w_out", "ffn2_norm", "ffn2_w1", "ffn2_w2", "final_norm")
ROW_SHARDED = ("ffn1_w2", "w_out", "ffn2_w2")
COL_SHARDED = ("ffn1_w1", "w_in", "lru_w_out", "w_uq", "w_ukv", "mla_w_o", "conv_w_out", "ffn2_w1")
SMALL_SHARDED = ("lru_conv_w", "conv_dw_w")
MXU_SHARDED = tuple(n for n in WEIGHTS if n in ROW_SHARDED + COL_SHARDED)
REPLICATED = tuple(n for n in WEIGHTS if n not in MXU_SHARDED + SMALL_SHARDED)
SMALL = REPLICATED + SMALL_SHARDED
INPUTS = ("x", "positions") + WEIGHTS + ("loss_target",) + tuple("m_" + n for n in WEIGHTS) + tuple("v_" + n for n in WEIGHTS)


def _pack(arrays, dtype, cols, row_unit):
    flat = jnp.concatenate([a.astype(dtype).reshape(-1) for a in arrays])
    unit = cols * row_unit
    return jnp.pad(flat, (0, -flat.shape[0] % unit)).reshape(-1, cols)


def _unpack(flat, shapes):
    out, off = [], 0
    for shp in shapes:
        n = 1
        for s_ in shp:
            n *= s_
        out.append(flat[..., off:off + n].reshape(flat.shape[:-1] + tuple(shp)))
        off += n
    return out


def _reduce_scatter(gs, tag):
    halves = _pair_exchange(gs, tag + "_pair_exchange")
    pairs = [_pair_sum(g, h, f"{tag}_pair_sum_{n}") for n, g, h in zip(MXU_SHARDED, gs, halves, strict=True)]
    others = _chip_exchange(pairs, tag + "_chip_exchange")
    sums = [_quad_sum(p, q, f"{tag}_chip_sum_{n}") for n, p, q in zip(MXU_SHARDED, pairs, others, strict=True)]
    return _pair_share(sums, tag + "_pair_share")


def _step(a):
    x, positions, target = a["x"][0], a["positions"][0], a["loss_target"][0]
    depth = a["ffn1_norm"].shape[0]
    me = 2 * lax.axis_index("x") + lax.axis_index("y")
    placed = [lax.dynamic_update_slice_in_dim(jnp.zeros(a[n].shape[:2] + (N_CHIPS,) + a[n].shape[2:], F32), 0.5 * a[n][:, :, None], me, 2)
              for n in SMALL_SHARDED]
    small_whole = _unpack(_allreduce_all(_pack(placed, F32, LANES, 8), "small_weights").reshape(-1), [p_.shape for p_ in placed])
    small_whole = {n: w.reshape(w.shape[:2] + (-1,)) for n, w in zip(SMALL_SHARDED, small_whole, strict=True)}
    layers = []
    for l in range(depth):
        gathered = _gather_chips([a[n][l].astype(BF16) for n in MXU_SHARDED], f"l{l}_gather")
        p = {n: a[n][l] for n in REPLICATED if a[n].ndim > 1}
        p.update({n: g.reshape(-1, g.shape[-1]) if n in ROW_SHARDED else g for n, g in zip(MXU_SHARDED, gathered, strict=True)})
        p.update({n: small_whole[n][l] for n in SMALL_SHARDED})
        layers.append(p)
    rope = _rope_tables(positions, "rope_tables")
    saved = []
    for l in range(depth):
        x, s = _layer_fwd(x, layers[l], rope, f"l{l}")
        saved.append(s)
    loss, dx, dfinal = _loss_head(x, target, a["final_norm"][None, :], "loss_head")
    loss = lax.psum(loss, MESH_AXES)
    grads, shard_grads = [None] * depth, [None] * depth
    for l in reversed(range(depth)):
        dx, grads[l] = _layer_bwd(dx, saved[l], layers[l], rope, f"l{l}")
        shard_grads[l] = _reduce_scatter([grads[l][n] for n in MXU_SHARDED], f"l{l}_grad")
    small = [jnp.stack([g[n] for g in grads]) if a[n].ndim > 1 else dfinal[0] for n in SMALL]
    g_small = _unpack(_allreduce_all(_pack(small, F32, LANES, 256), "grad_allreduce").reshape(-1), [s_.shape for s_ in small])
    g_small = [lax.dynamic_slice_in_dim(g, me * a[n].shape[-1], a[n].shape[-1], 2) if n in SMALL_SHARDED else g
               for n, g in zip(SMALL, g_small, strict=True)]
    g, delta, new_m, new_v = {}, {}, {}, {}
    for i, n in enumerate(MXU_SHARDED):
        g[n], delta[n], new_m[n], new_v[n] = _adamw(a[n], [shard_grads[l][i] for l in range(depth)], a["m_" + n], a["v_" + n], "adamw_" + n)
    shapes = [a[n].shape for n in SMALL]
    packed = [_pack([a[pre + n] for n in SMALL], F32, LANES, 256)[None] for pre in ("", "m_", "v_")]
    res = _adamw(packed[0], [_pack(g_small, F32, LANES, 256)], packed[1], packed[2], "adamw_small")
    for out, r in zip((g, delta, new_m, new_v), res, strict=True):
        out.update(zip(SMALL, _unpack(r.reshape(-1), shapes), strict=True))
    return (loss, dx[None], *[g[n] for n in WEIGHTS], *[delta[n] for n in WEIGHTS], *[new_m[n] for n in WEIGHTS], *[new_v[n] for n in WEIGHTS])


def kernel(x, positions, ffn1_norm, ffn1_w1, ffn1_w2, mix_norm, w_in, b_in, lru_conv_w, lru_conv_b, lru_w_gate, lru_b_gate, lru_lambda, lru_w_out, q_norm, w_uq, kv_norm, w_ukv, mla_w_o, conv_dw_w, conv_dw_b, conv_ln_g, conv_ln_b, conv_w_out, conv_b_out, w_out, ffn2_norm, ffn2_w1, ffn2_w2, final_norm, loss_target, m_ffn1_norm, m_ffn1_w1, m_ffn1_w2, m_mix_norm, m_w_in, m_b_in, m_lru_conv_w, m_lru_conv_b, m_lru_w_gate, m_lru_b_gate, m_lru_lambda, m_lru_w_out, m_q_norm, m_w_uq, m_kv_norm, m_w_ukv, m_mla_w_o, m_conv_dw_w, m_conv_dw_b, m_conv_ln_g, m_conv_ln_b, m_conv_w_out, m_conv_b_out, m_w_out, m_ffn2_norm, m_ffn2_w1, m_ffn2_w2, m_final_norm, v_ffn1_norm, v_ffn1_w1, v_ffn1_w2, v_mix_norm, v_w_in, v_b_in, v_lru_conv_w, v_lru_conv_b, v_lru_w_gate, v_lru_b_gate, v_lru_lambda, v_lru_w_out, v_q_norm, v_w_uq, v_kv_norm, v_w_ukv, v_mla_w_o, v_conv_dw_w, v_conv_dw_b, v_conv_ln_g, v_conv_ln_b, v_conv_w_out, v_conv_b_out, v_w_out, v_ffn2_norm, v_ffn2_w1, v_ffn2_w2, v_final_norm):
    return _step(dict(zip(INPUTS, (x, positions, ffn1_norm, ffn1_w1, ffn1_w2, mix_norm, w_in, b_in, lru_conv_w, lru_conv_b, lru_w_gate, lru_b_gate, lru_lambda, lru_w_out, q_norm, w_uq, kv_norm, w_ukv, mla_w_o, conv_dw_w, conv_dw_b, conv_ln_g, conv_ln_b, conv_w_out, conv_b_out, w_out, ffn2_norm, ffn2_w1, ffn2_w2, final_norm, loss_target, m_ffn1_norm, m_ffn1_w1, m_ffn1_w2, m_mix_norm, m_w_in, m_b_in, m_lru_conv_w, m_lru_conv_b, m_lru_w_gate, m_lru_b_gate, m_lru_lambda, m_lru_w_out, m_q_norm, m_w_uq, m_kv_norm, m_w_ukv, m_mla_w_o, m_conv_dw_w, m_conv_dw_b, m_conv_ln_g, m_conv_ln_b, m_conv_w_out, m_conv_b_out, m_w_out, m_ffn2_norm, m_ffn2_w1, m_ffn2_w2, m_final_norm, v_ffn1_norm, v_ffn1_w1, v_ffn1_w2, v_mix_norm, v_w_in, v_b_in, v_lru_conv_w, v_lru_conv_b, v_lru_w_gate, v_lru_b_gate, v_lru_lambda, v_lru_w_out, v_q_norm, v_w_uq, v_kv_norm, v_w_ukv, v_mla_w_o, v_conv_dw_w, v_conv_dw_b, v_conv_ln_g, v_conv_ln_b, v_conv_w_out, v_conv_b_out, v_w_out, v_ffn2_norm, v_ffn2_w1, v_ffn2_w2, v_final_norm), strict=True)))
```

```python
import functools

import jax
import jax.numpy as jnp
from jax import lax
from jax.experimental import pallas as pl
from jax.experimental.pallas import tpu as pltpu

F32, BF16 = jnp.float32, jnp.bfloat16
S = jax.ShapeDtypeStruct

LANES = 128
VMEM_LIMIT = 56 * 2**20
NORM_EPS = 1e-6
LRU_C = 8.0
MLA_HEADS = 8
QK_NOPE, QK_ROPE, V_HEAD = 64, 32, 64
HEAD_PAD = 128
ROPE_THETA = 10000.0
ADAM_LR, ADAM_B1, ADAM_B2, ADAM_EPS, ADAM_WD, ADAM_STEP = 0.001, 0.9, 0.999, 1e-08, 0.01, 10
MESH_AXES = ("x", "y", "c")
N_CHIPS = 4
NT =(((1,), (1,)), ((), ()))
TN = (((0,), (0,)), ((), ()))
NN = (((1,), (0,)), ((), ()))


def _tile(n, cap, unit=LANES):
    best = None
    for d in range(unit, min(n, cap) + 1, unit):
        if n % d == 0:
            best = d
    return best if best is not None else n


def _params(sem):
    return pltpu.CompilerParams(dimension_semantics=sem, vmem_limit_bytes=VMEM_LIMIT)


def _mm(a, b, mode="nn", out_dtype=F32, bias=None, res=None, alpha=1.0, b_chips=False, out_chips=False, name="mm"):
    n_unit = k_unit = None
    if b_chips:
        chips, rows, c = b.shape
        b_shape = (rows, chips * c)
        n_unit, k_unit = (c, None) if mode == "nn" else (None, c)
    else:
        b_shape = b.shape
    if mode == "nn":
        (m, k), (k2, n), dims = a.shape, b_shape, NN
    elif mode == "nt":
        (m, k), (n, k2), dims = a.shape, b_shape, NT
    else:
        (k, m), (k2, n), dims = a.shape, b_shape, TN
    assert k == k2 and not (b_chips and mode == "tn"), (name, a.shape, b.shape, mode)
    if out_chips:
        n_unit = n // N_CHIPS
    tm = _tile(m, 512)
    tn = _tile(n_unit or n, 1536)
    tk = k if (k <= 3072 and k_unit is None) else _tile(k_unit or k, 3072)
    nk = k // tk
    if mode == "tn":
        a_spec = pl.BlockSpec((tk, tm), lambda i, j, kk: (kk, i))
    else:
        a_spec = pl.BlockSpec((tm, tk), lambda i, j, kk: (i, kk))
    if b_chips and mode == "nn":
        b_spec = pl.BlockSpec((None, tk, tn), functools.partial(lambda i, j, kk, per: (j // per, kk, j % per), per=n_unit // tn))
    elif b_chips:
        b_spec = pl.BlockSpec((None, tn, tk), functools.partial(lambda i, j, kk, per: (kk // per, j, kk % per), per=k_unit // tk))
    elif mode == "nt":
        b_spec = pl.BlockSpec((tn, tk), lambda i, j, kk: (j, kk))
    else:
        b_spec = pl.BlockSpec((tk, tn), lambda i, j, kk: (kk, j))
    if out_chips:
        out_spec = pl.BlockSpec((None, tm, tn), functools.partial(lambda i, j, kk, per: (j // per, i, j % per), per=n_unit // tn))
        out_shape = S((N_CHIPS, m, n_unit), out_dtype)
    else:
        out_spec = pl.BlockSpec((tm, tn), lambda i, j, kk: (i, j))
        out_shape = S((m, n), out_dtype)
    operands, in_specs = [a, b], [a_spec, b_spec]
    if bias is not None:
        operands.append(bias)
        in_specs.append(pl.BlockSpec((1, tn), lambda i, j, kk: (0, j)))
    if res is not None:
        operands.append(res)
        in_specs.append(pl.BlockSpec((tm, tn), lambda i, j, kk: (i, j)))

    def body(*refs):
        a_ref, b_ref = refs[0], refs[1]
        pos = 2
        bias_ref = res_ref = None
        if bias is not None:
            bias_ref, pos = refs[pos], pos + 1
        if res is not None:
            res_ref, pos = refs[pos], pos + 1
        o_ref = refs[pos]
        part = lax.dot_general(a_ref[...].astype(BF16), b_ref[...].astype(BF16), dims, preferred_element_type=F32)

        def finish(acc):
            out = acc if alpha == 1.0 else acc * alpha
            if bias_ref is not None:
                out = out + bias_ref[...]
            if res_ref is not None:
                out = out + res_ref[...]
            o_ref[...] = out.astype(o_ref.dtype)

        if nk == 1:
            finish(part)
        else:
            acc_ref = refs[pos + 1]
            kk = pl.program_id(2)

            @pl.when(kk == 0)
            def _():
                acc_ref[...] = part

            @pl.when(kk > 0)
            def _():
                acc_ref[...] += part

            @pl.when(kk == nk - 1)
            def _():
                finish(acc_ref[...])

    return pl.pallas_call(
        body, name=name, grid=(m // tm, n // tn, nk), in_specs=in_specs, out_specs=out_spec, out_shape=out_shape,
        scratch_shapes=[pltpu.VMEM((tm, tn), F32)] if nk > 1 else [],
        compiler_params=_params(("parallel", "parallel", "arbitrary")),
    )(*operands)


def _rowwise(fn, rows, params=(), outs=(), accs=(), tt=256, name="rowwise"):
    rows = [r if isinstance(r, tuple) else (r, r.shape[1], 0) for r in rows]
    t = rows[0][0].shape[0]
    tt = min(tt, t)
    n_rows, n_par, n_out = len(rows), len(params), len(outs)
    in_specs = [pl.BlockSpec((tt, w), functools.partial(lambda i, cb: (i, cb), cb=cb)) for (_, w, cb) in rows]
    in_specs += [pl.BlockSpec(p.shape, functools.partial(lambda i, nd: (0,) * nd, nd=p.ndim)) for p in params]
    out_shape = [S((t, w), dt) for (w, dt) in outs] + [S(shape, dt) for (shape, dt) in accs]
    out_specs = [pl.BlockSpec((tt, w), lambda i: (i, 0)) for (w, _) in outs]
    out_specs += [pl.BlockSpec(shape, functools.partial(lambda i, nd: (0,) * nd, nd=len(shape))) for (shape, _) in accs]

    def body(*refs):
        vals = [r[...] for r in refs[:n_rows + n_par]]
        o_vals, a_vals = fn(*vals)
        o_refs = refs[n_rows + n_par:n_rows + n_par + n_out]
        a_refs = refs[n_rows + n_par + n_out:]
        for ref, val in zip(o_refs, o_vals, strict=True):
            ref[...] = val.astype(ref.dtype)
        i = pl.program_id(0)
        for ref, val in zip(a_refs, a_vals, strict=True):
            @pl.when(i == 0)
            def _(ref=ref, val=val):
                ref[...] = val.astype(ref.dtype)

            @pl.when(i > 0)
            def _(ref=ref, val=val):
                ref[...] += val.astype(ref.dtype)

    res = pl.pallas_call(
        body, name=name, grid=(t // tt,), in_specs=in_specs, out_specs=out_specs, out_shape=out_shape,
        compiler_params=_params(("arbitrary",) if accs else ("parallel",)),
    )(*[r[0] for r in rows], *params)
    return res


def _rms(x, g):
    x = x.astype(F32)
    return x * lax.rsqrt(jnp.mean(x * x, axis=-1, keepdims=True) + NORM_EPS) * g


def _layer_norm_silu(x, g, b):
    mu = jnp.mean(x, axis=-1, keepdims=True)
    var = jnp.mean(jnp.square(x - mu), axis=-1, keepdims=True)
    return jax.nn.silu((x - mu) * lax.rsqrt(var + NORM_EPS) * g + b)


def _neg_expm1(z):
    series = -z * (1.0 + z * (0.5 + z * (1.0 / 6.0 + z * (1.0 / 24.0 + z * (1.0 / 120.0)))))
    return jnp.where(z > -0.05, series, 1.0 - jnp.exp(z))


def _shift_down(x, s, fill=0.0):
    if s == 0:
        return x
    row = lax.broadcasted_iota(jnp.int32, x.shape, 0)
    return jnp.where(row >= s, pltpu.roll(x, s, 0), fill)


def _shift_up(x, s, fill=0.0):
    if s == 0:
        return x
    t = x.shape[0]
    row = lax.broadcasted_iota(jnp.int32, x.shape, 0)
    return jnp.where(row < t - s, pltpu.roll(x, t - s, 0), fill)


def _scan(a, u, shift):
    t, d = a.shape[0], 1
    while d < t:
        u = u + a * shift(u, d, 0.0)
        if 2 * d < t:
            a = a * shift(a, d, 1.0)
        d *= 2
    return u


def _lru_gates(xa, wr, wi, br, bi, lam):
    xb = xa.astype(BF16)
    r = jax.nn.sigmoid(jnp.dot(xb, wr.astype(BF16), preferred_element_type=F32) + br)
    i = jax.nn.sigmoid(jnp.dot(xb, wi.astype(BF16), preferred_element_type=F32) + bi)
    log_a = -LRU_C * r * jax.nn.softplus(-lam)
    return jnp.exp(log_a), jnp.sqrt(_neg_expm1(2.0 * log_a)) * (i * xa)


def _conv_fwd(x, w_ref, b, width):
    y = b + w_ref[pl.ds(width - 1, 1), :] * x
    for j in range(width - 1):
        y = y + w_ref[pl.ds(j, 1), :] * _shift_down(x, width - 1 - j)
    return y


def _conv_bwd(x, dy, w_ref, dw_ref, width):
    dx = w_ref[pl.ds(width - 1, 1), :] * dy
    dw_ref[pl.ds(width - 1, 1), :] = jnp.sum(dy * x, axis=0, keepdims=True)
    for j in range(width - 1):
        s = width - 1 - j
        dx = dx + w_ref[pl.ds(j, 1), :] * _shift_up(dy, s)
        dw_ref[pl.ds(j, 1), :] = jnp.sum(dy * _shift_down(x, s), axis=0, keepdims=True)
    return dx


def _rope(z, c, s1, s2):
    return z * c + pltpu.roll(z, HEAD_PAD - QK_ROPE // 2, 1) * s1 + pltpu.roll(z, QK_ROPE // 2, 1) * s2


def _rope_t(d, c, s1, s2):
    return d * c + pltpu.roll(d * s1, QK_ROPE // 2, 1) + pltpu.roll(d * s2, HEAD_PAD - QK_ROPE // 2, 1)


def _heads(z):
    return [z[:, h * HEAD_PAD:(h + 1) * HEAD_PAD] for h in range(z.shape[1] // HEAD_PAD)]


def _chan_spec(t, c_off=0):
    return pl.BlockSpec((t, LANES), lambda c: (0, c_off + c))


def _lru_specs(t, n_tiles, width):
    vec = pl.BlockSpec((1, LANES), lambda c: (0, c))
    mat = pl.BlockSpec((1, LANES, LANES), lambda c: (c, 0, 0))
    return [_chan_spec(t), _chan_spec(t, n_tiles), pl.BlockSpec((width, LANES), lambda c: (0, c)), vec, mat, mat, vec, vec, vec]


def _lru_fwd(pa, cw, cb, wr, wi, br, bi, lam, name):
    t, w = pa.shape[0], pa.shape[1] // 2
    n_tiles, width = w // LANES, cw.shape[0]

    def body(x_ref, g_ref, cw_ref, cb_ref, wr_ref, wi_ref, br_ref, bi_ref, lam_ref, y_ref):
        xa = _conv_fwd(x_ref[...], cw_ref, cb_ref[...], width)
        a, u = _lru_gates(xa, wr_ref[0], wi_ref[0], br_ref[...], bi_ref[...], lam_ref[...])
        h = _scan(a, u, _shift_down)
        y_ref[...] = (h * jax.nn.gelu(g_ref[...])).astype(y_ref.dtype)

    return pl.pallas_call(
        body, name=name, grid=(n_tiles,), in_specs=_lru_specs(t, n_tiles, width), out_specs=_chan_spec(t),
        out_shape=S((t, w), BF16), compiler_params=_params(("parallel",)),
    )(pa, pa, cw, cb, wr, wi, br, bi, lam)


def _lru_bwd(pa, dy, cw, cb, wr, wi, br, bi, lam, name):
    t, w = pa.shape[0], pa.shape[1] // 2
    n_tiles, width = w // LANES, cw.shape[0]

    def body(x_ref, g_ref, cw_ref, cb_ref, wr_ref, wi_ref, br_ref, bi_ref, lam_ref, dy_ref,
             dx_ref, dg_ref, dcw_ref, dcb_ref, dwr_ref, dwi_ref, dbr_ref, dbi_ref, dlam_ref, sx_ref, sg_ref):
        x = x_ref[...]
        xa = _conv_fwd(x, cw_ref, cb_ref[...], width)
        (a, u), gates_vjp = jax.vjp(_lru_gates, xa, wr_ref[0], wi_ref[0], br_ref[...], bi_ref[...], lam_ref[...])
        h = _scan(a, u, _shift_down)
        _, out_vjp = jax.vjp(lambda h_, g_: h_ * jax.nn.gelu(g_), h, g_ref[...])
        dh, dgate = out_vjp(dy_ref[...])
        adj = _scan(_shift_up(a, 1), dh, _shift_up)
        dxa, dwr, dwi, dbr, dbi, dlam = gates_vjp((adj * _shift_down(h, 1), adj))
        dx = _conv_bwd(x, dxa, cw_ref, dcw_ref, width)
        dcb_ref[...] = jnp.sum(dxa, axis=0, keepdims=True)
        dx_ref[...] = dx.astype(dx_ref.dtype)
        dg_ref[...] = dgate.astype(dg_ref.dtype)
        sx_ref[...] = jnp.sum(dx, axis=0, keepdims=True)
        sg_ref[...] = jnp.sum(dgate, axis=0, keepdims=True)
        dwr_ref[0], dwi_ref[0] = dwr, dwi
        dbr_ref[...], dbi_ref[...], dlam_ref[...] = dbr, dbi, dlam

    vec = pl.BlockSpec((1, LANES), lambda c: (0, c))
    mat = pl.BlockSpec((1, LANES, LANES), lambda c: (c, 0, 0))
    vec_s, mat_s = S((1, w), F32), S((n_tiles, LANES, LANES), F32)
    return pl.pallas_call(
        body, name=name, grid=(n_tiles,), in_specs=_lru_specs(t, n_tiles, width) + [_chan_spec(t)],
        out_specs=[_chan_spec(t), _chan_spec(t), pl.BlockSpec((width, LANES), lambda c: (0, c)), vec, mat, mat, vec, vec, vec, vec, vec],
        out_shape=[S((t, w), BF16), S((t, w), BF16), S((width, w), F32), vec_s, mat_s, mat_s, vec_s, vec_s, vec_s, vec_s, vec_s],
        compiler_params=_params(("parallel",)),
    )(pa, pa, cw, cb, wr, wi, br, bi, lam, dy)


def _glu_conv_fwd(pc, cw, cb, name):
    t, c = pc.shape[0], pc.shape[1] // 2
    n_tiles, width = c // LANES, cw.shape[0]

    def body(v_ref, g_ref, cw_ref, cb_ref, y_ref):
        y_ref[...] = _conv_fwd(v_ref[...] * jax.nn.sigmoid(g_ref[...]), cw_ref, cb_ref[...], width)

    return pl.pallas_call(
        body, name=name, grid=(n_tiles,),
        in_specs=[_chan_spec(t), _chan_spec(t, n_tiles), pl.BlockSpec((width, LANES), lambda i: (0, i)), pl.BlockSpec((1, LANES), lambda i: (0, i))],
        out_specs=_chan_spec(t), out_shape=S((t, c), F32), compiler_params=_params(("parallel",)),
    )(pc, pc, cw, cb)


def _glu_conv_bwd(pc, dy, cw, name):
    t, c = pc.shape[0], pc.shape[1] // 2
    n_tiles, width = c // LANES, cw.shape[0]

    def body(v_ref, g_ref, cw_ref, dy_ref, dv_ref, dg_ref, dcw_ref, dcb_ref, sv_ref, sg_ref):
        glu = lambda v_, g_: v_ * jax.nn.sigmoid(g_)
        x, glu_vjp = jax.vjp(glu, v_ref[...], g_ref[...])
        dy_ = dy_ref[...]
        dv, dg = glu_vjp(_conv_bwd(x, dy_, cw_ref, dcw_ref, width))
        dcb_ref[...] = jnp.sum(dy_, axis=0, keepdims=True)
        dv_ref[...] = dv.astype(dv_ref.dtype)
        dg_ref[...] = dg.astype(dg_ref.dtype)
        sv_ref[...] = jnp.sum(dv, axis=0, keepdims=True)
        sg_ref[...] = jnp.sum(dg, axis=0, keepdims=True)

    vec = pl.BlockSpec((1, LANES), lambda i: (0, i))
    wspec = pl.BlockSpec((width, LANES), lambda i: (0, i))
    return pl.pallas_call(
        body, name=name, grid=(n_tiles,), in_specs=[_chan_spec(t), _chan_spec(t, n_tiles), wspec, _chan_spec(t)],
        out_specs=[_chan_spec(t), _chan_spec(t), wspec, vec, vec, vec],
        out_shape=[S((t, c), BF16), S((t, c), BF16), S((width, c), F32), S((1, c), F32), S((1, c), F32), S((1, c), F32)],
        compiler_params=_params(("parallel",)),
    )(pc, pc, cw, dy)


def _softmax_rows(q, k, causal, scale):
    s = lax.dot_general(q, k, NT, preferred_element_type=F32) * scale
    s = jnp.where(causal, s, jnp.finfo(F32).min)
    p = jnp.exp(s - jnp.max(s, axis=-1, keepdims=True))
    return p / jnp.sum(p, axis=-1, keepdims=True)


def _attn_specs(t, tq):
    return [pl.BlockSpec((tq, 2 * HEAD_PAD), lambda hp, i: (i, hp)), pl.BlockSpec((t, 2 * HEAD_PAD), lambda hp, i: (0, hp)),
            pl.BlockSpec((t, 2 * V_HEAD), lambda hp, i: (0, hp))]


def _attn_masks(t, tq):
    i = pl.program_id(1)
    row = lax.broadcasted_iota(jnp.int32, (tq, t), 0) + i * tq
    col = lax.broadcasted_iota(jnp.int32, (tq, t), 1)
    return col <= row


def _attn_fwd(qh, kh, v, name):
    t = qh.shape[0]
    tq = min(256, t)
    scale = (QK_NOPE + QK_ROPE) ** -0.5

    def body(q_ref, k_ref, v_ref, o_ref):
        causal = _attn_masks(t, tq)
        lane = lax.broadcasted_iota(jnp.int32, (t, 2 * V_HEAD), 1)
        vv = v_ref[...]
        acc = jnp.zeros((tq, 2 * V_HEAD), F32)
        for e in range(2):
            p = _softmax_rows(q_ref[:, e * HEAD_PAD:(e + 1) * HEAD_PAD], k_ref[:, e * HEAD_PAD:(e + 1) * HEAD_PAD], causal, scale)
            ve = jnp.where((lane >= V_HEAD * e) & (lane < V_HEAD * (e + 1)), vv, jnp.zeros_like(vv))
            acc = acc + jnp.dot(p.astype(BF16), ve, preferred_element_type=F32)
        o_ref[...] = acc.astype(o_ref.dtype)

    return pl.pallas_call(
        body, name=name, grid=(MLA_HEADS // 2, t // tq), in_specs=_attn_specs(t, tq),
        out_specs=pl.BlockSpec((tq, 2 * V_HEAD), lambda hp, i: (i, hp)), out_shape=S((t, MLA_HEADS * V_HEAD), BF16),
        compiler_params=_params(("parallel", "parallel")),
    )(qh, kh, v)


def _attn_bwd(qh, kh, v, do, name):
    t = qh.shape[0]
    tq = min(256, t)
    scale = (QK_NOPE + QK_ROPE) ** -0.5

    def body(q_ref, k_ref, v_ref, do_ref, dq_ref, dk_ref, dv_ref):
        causal = _attn_masks(t, tq)
        lane = lax.broadcasted_iota(jnp.int32, (tq, 2 * V_HEAD), 1)
        vv, dd = v_ref[...], do_ref[...]
        dqs, dks = [], []
        dv = jnp.zeros((t, 2 * V_HEAD), F32)
        for e in range(2):
            q, k = q_ref[:, e * HEAD_PAD:(e + 1) * HEAD_PAD], k_ref[:, e * HEAD_PAD:(e + 1) * HEAD_PAD]
            p = _softmax_rows(q, k, causal, scale)
            de = jnp.where((lane >= V_HEAD * e) & (lane < V_HEAD * (e + 1)), dd, jnp.zeros_like(dd))
            dp = lax.dot_general(de, vv, NT, preferred_element_type=F32)
            ds = (p * (dp - jnp.sum(p * dp, axis=-1, keepdims=True)) * scale).astype(BF16)
            dqs.append(jnp.dot(ds, k, preferred_element_type=F32))
            dks.append(lax.dot_general(ds, q, TN, preferred_element_type=F32))
            dv = dv + lax.dot_general(p.astype(BF16), de, TN, preferred_element_type=F32)
        dq_ref[...] = jnp.concatenate(dqs, axis=-1)
        dk = jnp.concatenate(dks, axis=-1)
        i = pl.program_id(1)

        @pl.when(i == 0)
        def _():
            dk_ref[...] = dk
            dv_ref[...] = dv

        @pl.when(i > 0)
        def _():
            dk_ref[...] += dk
            dv_ref[...] += dv

    return pl.pallas_call(
        body, name=name, grid=(MLA_HEADS // 2, t // tq),
        in_specs=_attn_specs(t, tq) + [pl.BlockSpec((tq, 2 * V_HEAD), lambda hp, i: (i, hp))],
        out_specs=[pl.BlockSpec((tq, 2 * HEAD_PAD), lambda hp, i: (i, hp)), pl.BlockSpec((t, 2 * HEAD_PAD), lambda hp, i: (0, hp)),
                   pl.BlockSpec((t, 2 * V_HEAD), lambda hp, i: (0, hp))],
        out_shape=[S((t, MLA_HEADS * HEAD_PAD), F32), S((t, MLA_HEADS * HEAD_PAD), F32), S((t, MLA_HEADS * V_HEAD), F32)],
        compiler_params=_params(("parallel", "arbitrary")),
    )(qh, kh, v, do)


def _ffn_fwd(x, g, w1, w2, tag):
    d, f = x.shape[1], w2.shape[0]
    h, = _rowwise(lambda x_, g_: ([_rms(x_, g_)], []), [x], [g], outs=[(d, BF16)], name=tag + "_rms")
    gu = _mm(h, w1, out_dtype=BF16, b_chips=True, name=tag + "_up")
    a, = _rowwise(lambda g_, u_: ([jax.nn.silu(g_.astype(F32)) * u_.astype(F32)], []), [(gu, f, 0), (gu, f, 1)],
                  outs=[(f, BF16)], name=tag + "_act")
    return _mm(a, w2, res=x, alpha=0.5, name=tag + "_down"), (x, h, gu, a)


def _after(dep, n):
    return None if dep is None else jnp.zeros((1, n), F32) + dep


def _ffn_bwd(dy, saved, g, w1, w2, tag, dep=None):
    x, h, gu, a = saved
    d, f = x.shape[1], w2.shape[0]
    da = _mm(dy, w2, "nt", out_dtype=BF16, alpha=0.5, bias=_after(dep, f), name=tag + "_dact")
    dw2 = _mm(a, dy, "tn", alpha=0.5, name=tag + "_dw2")

    def act_bwd(g_, u_, da_):
        _, vjp = jax.vjp(lambda p, q: jax.nn.silu(p) * q, g_.astype(F32), u_.astype(F32))
        return [jnp.concatenate(vjp(da_.astype(F32)), axis=-1)], []

    dgu, = _rowwise(act_bwd, [(gu, f, 0), (gu, f, 1), da], outs=[(2 * f, BF16)], tt=128, name=tag + "_dgu")
    dw1 = _mm(h, dgu, "tn", out_chips=True, name=tag + "_dw1")
    dh = _mm(dgu, w1, "nt", b_chips=True, name=tag + "_dh")
    dx, dg = _rms_bwd(x, dh, dy, g, tag + "_drms")
    return dx, (dg, dw1, dw2)


def _rms_bwd(x, dh, dres, g, name):
    def fn(x_, dh_, dres_, g_):
        _, vjp = jax.vjp(_rms, x_, g_)
        dx, dg = vjp(dh_)
        return [dx + dres_], [dg]

    return _rowwise(fn, [x, dh, dres], [g], outs=[(x.shape[1], F32)], accs=[((1, x.shape[1]), F32)], name=name)


def _rope_tables(positions, name):
    half = QK_ROPE // 2
    inv = ROPE_THETA ** (-jnp.arange(0, QK_ROPE, 2, dtype=F32) / QK_ROPE)
    inv_lanes = jnp.zeros((1, HEAD_PAD), F32).at[0, QK_NOPE:QK_NOPE + QK_ROPE].set(jnp.tile(inv, 2))

    def fn(pos, inv_):
        ang = pos.astype(F32) * inv_
        lane = lax.broadcasted_iota(jnp.int32, ang.shape, 1)
        cos, sin = jnp.cos(ang), jnp.sin(ang)
        c = jnp.where(lane < QK_NOPE, 1.0, jnp.where(lane < QK_NOPE + QK_ROPE, cos, 0.0))
        s1 = jnp.where((lane >= QK_NOPE) & (lane < QK_NOPE + half), -sin, 0.0)
        s2 = jnp.where((lane >= QK_NOPE + half) & (lane < QK_NOPE + QK_ROPE), sin, 0.0)
        return [c, s1, s2], []

    return _rowwise(fn, [positions.reshape(-1, 1)], [inv_lanes], outs=[(HEAD_PAD, F32)] * 3, name=name)


def _chip_cols(g, lo, hi):
    c = g.shape[-1]
    parts = [g[j, :, max(lo, j * c) - j * c:min(hi, (j + 1) * c) - j * c] for j in range(g.shape[0]) if max(lo, j * c) < min(hi, (j + 1) * c)]
    return parts[0] if len(parts) == 1 else jnp.concatenate(parts, axis=-1)


def _cols_by_chip(segments, c):
    out, start = [[] for _ in range(N_CHIPS)], 0
    for arr, first, width in segments:
        for j in range(N_CHIPS):
            lo, hi = max(start, j * c), min(start + width, (j + 1) * c)
            if lo < hi:
                out[j].append(arr[:, first + lo - start:first + hi - start])
        start += width
    assert start == N_CHIPS * c
    return jnp.stack([jnp.concatenate(parts, axis=-1) for parts in out])


def _whole_cols(g):
    return g.transpose(1, 0, 2).reshape(g.shape[1], -1)


def _mix_offsets(lw):
    wl, ql, kvl = lw["lru_lambda"].shape[-1], lw["q_norm"].shape[-1], lw["kv_norm"].shape[-1]
    o1 = 2 * wl
    o2 = o1 + ql + kvl + QK_ROPE
    return wl, ql, kvl, o1, o2, o2 + 2 * lw["conv_ln_g"].shape[-1]


def _mix_weights(lw):
    wl, ql, kvl, o1, o2, o3 = _mix_offsets(lw)
    w_in, b_in = lw["w_in"], lw["b_in"][None, :]
    d_in = b_in.shape[1]
    z = lambda m, n: jnp.zeros(m.shape[:-1] + (n,), m.dtype)
    w_b = jnp.concatenate([_chip_cols(w_in, o1 + ql, o1 + ql + kvl), z(w_in[0], QK_NOPE), _chip_cols(w_in, o1 + ql + kvl, o2),
                           z(w_in[0], HEAD_PAD - QK_NOPE - QK_ROPE), _chip_cols(w_in, o1, o1 + ql)], axis=-1)
    b_b = jnp.concatenate([b_in[:, o1 + ql:o1 + ql + kvl], z(b_in, QK_NOPE), b_in[:, o1 + ql + kvl:o2],
                           z(b_in, HEAD_PAD - QK_NOPE - QK_ROPE), b_in[:, o1:o1 + ql]], axis=-1)
    hd = lw["lru_w_gate"].shape[-2]
    per = LANES // hd
    eye = jnp.eye(per, dtype=F32)
    wg = lw["lru_w_gate"].reshape(-1, per, hd, 2 * hd)
    block_diag = lambda m: jnp.einsum("cedk,ef->cedfk", m, eye).reshape(-1, LANES, LANES)
    bg = lw["lru_b_gate"]
    w_uq = _whole_cols(lw["w_uq"]).reshape(ql, MLA_HEADS, QK_NOPE + QK_ROPE)
    w_ukv = _whole_cols(lw["w_ukv"]).reshape(kvl, MLA_HEADS, QK_NOPE + V_HEAD)
    pad = lambda m, n: jnp.pad(m, ((0, 0), (0, 0), (0, n)))
    return dict(
        w_a=_chip_cols(w_in, 0, o1), w_b=w_b, w_c=_chip_cols(w_in, o2, o3), w_g=_chip_cols(w_in, o3, d_in),
        b_a=b_in[:, :o1], b_b=b_b, b_c=b_in[:, o2:o3], b_g=b_in[:, o3:],
        wr=block_diag(wg[..., :hd]), wi=block_diag(wg[..., hd:]),
        br=bg[:, :hd].reshape(1, -1), bi=bg[:, hd:].reshape(1, -1),
        w_uq=pad(w_uq, HEAD_PAD - QK_NOPE - QK_ROPE).reshape(ql, -1),
        w_k=pad(w_ukv[..., :QK_NOPE], HEAD_PAD - QK_NOPE).reshape(kvl, -1),
        w_v=w_ukv[..., QK_NOPE:].reshape(kvl, -1),
    )


def _mix_fwd(x, p, mw, rope, tag):
    d = x.shape[1]
    wl, ql, kvl = p["lru_lambda"].shape[-1], p["q_norm"].shape[-1], p["kv_norm"].shape[-1]
    row = lambda name: p[name][None, :]
    h, = _rowwise(lambda x_, g_: ([_rms(x_, g_)], []), [x], [row("mix_norm")], outs=[(d, BF16)], name=tag + "_rms")
    pa = _mm(h, mw["w_a"], bias=mw["b_a"], name=tag + "_pa")
    pb = _mm(h, mw["w_b"], bias=mw["b_b"], name=tag + "_pb")
    pc = _mm(h, mw["w_c"], bias=mw["b_c"], name=tag + "_pc")
    pg = _mm(h, mw["w_g"], bias=mw["b_g"], name=tag + "_pg")
    lru_args = (p["lru_conv_w"], row("lru_conv_b"), mw["wr"], mw["wi"], mw["br"], mw["bi"], row("lru_lambda"))
    ya_pre = _lru_fwd(pa, *lru_args, name=tag + "_lru")
    y_a = _mm(ya_pre, p["lru_w_out"], b_chips=True, name=tag + "_ya")
    mla_rows = [(pb, kvl, 0), (pb, ql, (kvl + HEAD_PAD) // ql)]
    assert (kvl + HEAD_PAD) % ql == 0 and kvl % HEAD_PAD == 0
    ckvn, cqn = _rowwise(lambda kv_, q_, gk, gq: ([_rms(kv_, gk), _rms(q_, gq)], []), mla_rows, [row("kv_norm"), row("q_norm")],
                         outs=[(kvl, BF16), (ql, BF16)], name=tag + "_lat_rms")
    q0 = _mm(cqn, mw["w_uq"], name=tag + "_q")
    k0 = _mm(ckvn, mw["w_k"], name=tag + "_k")
    v = _mm(ckvn, mw["w_v"], out_dtype=BF16, name=tag + "_v")

    def rope_fwd(q_, k_, kpe, c, s1, s2):
        kr = _rope(kpe, c, s1, s2)
        return [jnp.concatenate([_rope(z, c, s1, s2) for z in _heads(q_)], axis=-1),
                jnp.concatenate([z + kr for z in _heads(k_)], axis=-1)], []

    qh, kh = _rowwise(rope_fwd, [q0, k0, (pb, HEAD_PAD, kvl // HEAD_PAD), *rope],
                      outs=[(q0.shape[1], BF16), (k0.shape[1], BF16)], name=tag + "_rope")
    o = _attn_fwd(qh, kh, v, tag + "_attn")
    y_b = _mm(o, p["mla_w_o"], b_chips=True, name=tag + "_yb")
    c2 = _glu_conv_fwd(pc, p["conv_dw_w"], row("conv_dw_b"), tag + "_conv")
    c3, = _rowwise(lambda c_, g_, b_: ([_layer_norm_silu(c_, g_, b_)], []), [c2], [row("conv_ln_g"), row("conv_ln_b")],
                   outs=[(c2.shape[1], BF16)], name=tag + "_ln")
    y_c = _mm(c3, p["conv_w_out"], bias=row("conv_b_out"), b_chips=True, name=tag + "_yc")
    merged, = _rowwise(_merge, [y_a, y_b, y_c, (pg, d, 0), (pg, d, 1), (pg, d, 2)], outs=[(d, BF16)], name=tag + "_merge")
    out = _mm(merged, p["w_out"], res=x, name=tag + "_out")
    saved = dict(x=x, h=h, pa=pa, pb=pb, pc=pc, pg=pg, ya_pre=ya_pre, y_a=y_a, y_b=y_b, y_c=y_c, ckvn=ckvn, cqn=cqn,
                 qh=qh, kh=kh, v=v, o=o, c2=c2, c3=c3, merged=merged, lru_args=lru_args)
    return out, saved


def _merge(ya, yb, yc, g0, g1, g2):
    return [jax.nn.sigmoid(g0) * ya + jax.nn.sigmoid(g1) * yb + jax.nn.sigmoid(g2) * yc], []


def _mix_bwd(dy, s, p, mw, rope, tag, dep=None):
    x, h = s["x"], s["h"]
    d = x.shape[1]
    wl, ql, kvl = p["lru_lambda"].shape[-1], p["q_norm"].shape[-1], p["kv_norm"].shape[-1]
    row = lambda name: p[name][None, :]
    g = {}
    dmerged = _mm(dy, p["w_out"], "nt", bias=_after(dep, d), name=tag + "_dmerged")
    g["w_out"] = _mm(s["merged"], dy, "tn", name=tag + "_dw_out")

    def merge_bwd(ya, yb, yc, g0, g1, g2, dm):
        _, vjp = jax.vjp(lambda *a: _merge(*a)[0][0], ya, yb, yc, g0, g1, g2)
        dya, dyb, dyc, d0, d1, d2 = vjp(dm)
        dpg = jnp.concatenate([d0, d1, d2], axis=-1)
        return [dya, dyb, dyc, dpg], [jnp.sum(dyc, axis=0, keepdims=True), jnp.sum(dpg, axis=0, keepdims=True)]

    pg = s["pg"]
    dya, dyb, dyc, dpg, g["conv_b_out"], db_g = _rowwise(
        merge_bwd, [s["y_a"], s["y_b"], s["y_c"], (pg, d, 0), (pg, d, 1), (pg, d, 2), dmerged],
        outs=[(d, BF16), (d, BF16), (d, BF16), (3 * d, BF16)], accs=[((1, d), F32), ((1, 3 * d), F32)], tt=128, name=tag + "_dmerge")
    g["lru_w_out"] = _mm(s["ya_pre"], dya, "tn", out_chips=True, name=tag + "_dw_lru_out")
    dya_pre = _mm(dya, p["lru_w_out"], "nt", b_chips=True, name=tag + "_dya_pre")
    (dpa_x, dpa_g, g["lru_conv_w"], g["lru_conv_b"], g["wr"], g["wi"], g["br"], g["bi"], g["lru_lambda"], sb_x, sb_g) = _lru_bwd(
        s["pa"], dya_pre, *s["lru_args"], name=tag + "_dlru")
    dpa = jnp.concatenate([dpa_x, dpa_g], axis=1)
    db_a = jnp.concatenate([sb_x, sb_g], axis=1)
    g["conv_w_out"] = _mm(s["c3"], dyc, "tn", out_chips=True, name=tag + "_dw_conv_out")
    dc3 = _mm(dyc, p["conv_w_out"], "nt", b_chips=True, name=tag + "_dc3")

    def ln_bwd(c_, dc_, g_, b_):
        _, vjp = jax.vjp(_layer_norm_silu, c_, g_, b_)
        dc, dg_, db_ = vjp(dc_)
        return [dc], [dg_, db_]

    cc = s["c2"].shape[1]
    dc2, g["conv_ln_g"], g["conv_ln_b"] = _rowwise(ln_bwd, [s["c2"], dc3], [row("conv_ln_g"), row("conv_ln_b")], outs=[(cc, F32)],
                                                    accs=[((1, cc), F32)] * 2, name=tag + "_dln")
    dpc_v, dpc_g, g["conv_dw_w"], g["conv_dw_b"], sc_v, sc_g = _glu_conv_bwd(s["pc"], dc2, p["conv_dw_w"], tag + "_dconv")
    dpc = jnp.concatenate([dpc_v, dpc_g], axis=1)
    db_c = jnp.concatenate([sc_v, sc_g], axis=1)
    g["mla_w_o"] = _mm(s["o"], dyb, "tn", out_chips=True, name=tag + "_dw_o")
    do = _mm(dyb, p["mla_w_o"], "nt", out_dtype=BF16, b_chips=True, name=tag + "_do")
    dqh, dkh, dv = _attn_bwd(s["qh"], s["kh"], s["v"], do, tag + "_dattn")

    def rope_bwd(dq_, dk_, c, s1, s2):
        lane = lax.broadcasted_iota(jnp.int32, c.shape, 1)
        dkr = functools.reduce(lambda a, b: a + b, _heads(dk_))
        dkpe = jnp.where((lane >= QK_NOPE) & (lane < QK_NOPE + QK_ROPE), _rope_t(dkr, c, s1, s2), 0.0)
        return [jnp.concatenate([_rope_t(z, c, s1, s2) for z in _heads(dq_)], axis=-1), dk_, dkpe], []

    dq0, dk0, dkpe = _rowwise(rope_bwd, [dqh, dkh, *rope], outs=[(dqh.shape[1], BF16), (dkh.shape[1], BF16), (HEAD_PAD, F32)],
                              name=tag + "_drope")
    dvb = dv.astype(BF16)
    g["w_uq"] = _mm(s["cqn"], dq0, "tn", name=tag + "_dw_uq")
    g["w_k"] = _mm(s["ckvn"], dk0, "tn", name=tag + "_dw_k")
    g["w_v"] = _mm(s["ckvn"], dvb, "tn", name=tag + "_dw_v")
    dcqn = _mm(dq0, mw["w_uq"], "nt", name=tag + "_dcqn")
    dckvn = _mm(dk0, mw["w_k"], "nt", name=tag + "_dckvn_k")
    dckvn = _mm(dvb, mw["w_v"], "nt", res=dckvn, name=tag + "_dckvn_v")

    def lat_bwd(kv_, q_, dkv_, dq_, dkpe_, gk, gq):
        _, vjp_k = jax.vjp(_rms, kv_, gk)
        _, vjp_q = jax.vjp(_rms, q_, gq)
        (dkv, dgk), (dq, dgq) = vjp_k(dkv_), vjp_q(dq_)
        dpb = jnp.concatenate([dkv, dkpe_, dq], axis=-1)
        return [dpb], [dgk, dgq, jnp.sum(dpb, axis=0, keepdims=True)]

    pb = s["pb"]
    dpb, g["kv_norm"], g["q_norm"], db_b = _rowwise(
        lat_bwd, [(pb, kvl, 0), (pb, ql, (kvl + HEAD_PAD) // ql), dckvn, dcqn, dkpe], [row("kv_norm"), row("q_norm")],
        outs=[(pb.shape[1], BF16)], accs=[((1, kvl), F32), ((1, ql), F32), ((1, pb.shape[1]), F32)], name=tag + "_dlat")
    dh = None
    for part, dpart in (("a", dpa), ("b", dpb), ("c", dpc), ("g", dpg)):
        g["w_" + part] = _mm(h, dpart, "tn", name=tag + "_dw_" + part)
        dh = _mm(dpart, mw["w_" + part], "nt", res=dh, name=tag + "_dh_" + part)
    g["b_a"], g["b_b"], g["b_c"], g["b_g"] = db_a, db_b, db_c, db_g
    dx, g["mix_norm"] = _rms_bwd(x, dh, dy, row("mix_norm"), tag + "_drms")
    return dx, g


def _by_chip_cols(m):
    return m.reshape(m.shape[0], N_CHIPS, -1).transpose(1, 0, 2)


def _mix_grads_to_params(g, p):
    wl, ql, kvl, o1, o2, o3 = _mix_offsets(p)
    hd = p["lru_w_gate"].shape[-2]
    per = LANES // hd
    eye = jnp.eye(per, dtype=F32)
    diag = lambda m: jnp.einsum("cedfk,ef->cedk", m.reshape(-1, per, hd, per, hd), eye).reshape(-1, hd, hd)
    out = {k: g[k] for k in ("lru_w_out", "conv_w_out", "mla_w_o", "lru_conv_w", "conv_dw_w")}
    out["w_out"] = g["w_out"].reshape(N_CHIPS, -1, g["w_out"].shape[1])
    for k in ("mix_norm", "conv_b_out", "lru_conv_b", "lru_lambda", "conv_ln_g", "conv_ln_b", "conv_dw_b", "kv_norm", "q_norm"):
        out[k] = g[k][0]
    mla = lambda m: [(m, kvl + HEAD_PAD, ql), (m, 0, kvl), (m, kvl + QK_NOPE, QK_ROPE)]
    whole = lambda m: [(m, 0, m.shape[1])]
    out["w_in"] = _cols_by_chip(whole(g["w_a"]) + mla(g["w_b"]) + whole(g["w_c"]) + whole(g["w_g"]), p["w_in"].shape[-1])
    out["b_in"] = jnp.concatenate([g["b_a"]] + [m[:, a:a + w] for m, a, w in mla(g["b_b"])] + [g["b_c"], g["b_g"]], axis=1)[0]
    out["lru_w_gate"] = jnp.concatenate([diag(g["wr"]), diag(g["wi"])], axis=-1)
    out["lru_b_gate"] = jnp.concatenate([g["br"].reshape(-1, hd), g["bi"].reshape(-1, hd)], axis=-1)
    out["w_uq"] = _by_chip_cols(g["w_uq"].reshape(ql, MLA_HEADS, HEAD_PAD)[..., :QK_NOPE + QK_ROPE].reshape(ql, -1))
    out["w_ukv"] = _by_chip_cols(jnp.concatenate([g["w_k"].reshape(kvl, MLA_HEADS, HEAD_PAD)[..., :QK_NOPE],
                                                  g["w_v"].reshape(kvl, MLA_HEADS, V_HEAD)], axis=-1).reshape(kvl, -1))
    return out


def _loss_head(x, target, g, name):
    def fn(x_, t_, g_):
        y, vjp = jax.vjp(_rms, x_, g_)
        err = y - t_
        dx, dg = vjp(err * (1.0 / x_.shape[1]))
        loss = 0.5 * jnp.sum(jnp.mean(err * err, axis=-1, keepdims=True), axis=0, keepdims=True)
        return [dx], [dg, jnp.broadcast_to(loss, (1, LANES))]

    dx, dg, loss = _rowwise(fn, [x, target], [g], outs=[(x.shape[1], F32)], accs=[((1, x.shape[1]), F32), ((1, LANES), F32)], name=name)
    return loss[0, 0], dx, dg


def _part_fwd(part, x, p, rope, tag, dep=None):
    norm = part + "_norm"
    if dep is not None:
        p = dict(p, **{norm: p[norm] + dep})
    if part == "mix":
        mw = _mix_weights(p)
        x, s = _mix_fwd(x, p, mw, rope, tag)
        return x, (s, mw, p)
    x, s = _ffn_fwd(x, p[norm][None, :], p[part + "_w1"], p[part + "_w2"], tag)
    return x, (s, None, p)


def _part_bwd(part, dx, saved, rope, tag, dep=None):
    s, mw, p = saved
    if part == "mix":
        dx, gm = _mix_bwd(dx, s, p, mw, rope, tag, dep)
        return dx, _mix_grads_to_params(gm, p)
    dx, (dn, dw1, dw2) = _ffn_bwd(dx, s, p[part + "_norm"][None, :], p[part + "_w1"], p[part + "_w2"], tag, dep)
    return dx, {part + "_norm": dn[0], part + "_w1": dw1, part + "_w2": dw2.reshape(N_CHIPS, -1, dw2.shape[1])}


ANY = pl.BlockSpec(memory_space=pl.ANY)
VMEM_WHOLE = pl.BlockSpec(memory_space=pltpu.VMEM)


def _place():
    x, y, c = (lax.axis_index(a) for a in MESH_AXES)
    return x, y, c, [(1 - x, y), (x, 1 - y), (1 - x, 1 - y)]


def _remote(src, dst, send_sem, recv_sem, device):
    return pltpu.make_async_remote_copy(src_ref=src, dst_ref=dst, send_sem=send_sem, recv_sem=recv_sem, device_id=device,
                                        device_id_type=pl.DeviceIdType.MESH)


def _half_rows(c, half):
    return pl.ds(pl.multiple_of(c * half, 16), half)


def _gather_chips(shards, name):
    n = len(shards)
    halves = [s.shape[0] // 2 for s in shards]

    def body(*refs):
        ins, outs, send_sems, recv_sems = refs[:n], refs[n:2 * n], refs[2 * n], refs[2 * n + 1]
        x, y, c, chips = _place()
        me, sibling = 2 * x + y, (x, y, 1 - c)
        sent = []
        for i in range(n):
            mine = _half_rows(c, halves[i])
            sent += [_remote(ins[i].at[mine], outs[i].at[me, mine], send_sems.at[i, j], recv_sems.at[i, j], (cx, cy, c))
                     for j, (cx, cy) in enumerate(chips)]
            sent.append(_remote(ins[i], outs[i].at[me], send_sems.at[i, 6], recv_sems.at[i, 6], sibling))
        for cp in sent:
            cp.start()
        for i in range(n):
            mine = _half_rows(c, halves[i])
            for j, (cx, cy) in enumerate(chips):
                rows = outs[i].at[2 * cx + cy, mine]
                _remote(rows, rows, send_sems.at[i, j], recv_sems.at[i, j], (cx, cy, c)).wait_recv()
                passed = _remote(rows, rows, send_sems.at[i, 3 + j], recv_sems.at[i, 3 + j], sibling)
                passed.start()
                sent.append(passed)
        for i in range(n):
            other = _half_rows(1 - c, halves[i])
            for j, (cx, cy) in enumerate(chips):
                rows = outs[i].at[2 * cx + cy, other]
                _remote(rows, rows, send_sems.at[i, 3 + j], recv_sems.at[i, 3 + j], sibling).wait_recv()
            own = outs[i].at[me]
            _remote(own, own, send_sems.at[i, 6], recv_sems.at[i, 6], sibling).wait_recv()
        for cp in sent:
            cp.wait_send()

    return pl.pallas_call(
        body, name=name, in_specs=[ANY] * n, out_specs=[ANY] * n, out_shape=[S((N_CHIPS,) + s.shape, s.dtype) for s in shards],
        scratch_shapes=[pltpu.SemaphoreType.DMA((n, 7)), pltpu.SemaphoreType.DMA((n, 7))],
    )(*shards)


def _allreduce_all(v, name):
    r, cols = v.shape

    def body(v_ref, out_ref, buf, send_sems, recv_sems):
        x, y, c, chips = _place()
        sibling = (x, y, 1 - c)
        slot = lambda px, py, pc: buf.at[4 * px + 2 * py + pc]
        buf[4 * x + 2 * y + c] = v_ref[...]
        sent = [_remote(v_ref, slot(x, y, c), send_sems.at[0], recv_sems.at[0], sibling)]
        sent += [_remote(v_ref, slot(x, y, c), send_sems.at[1 + j], recv_sems.at[1 + j], (cx, cy, c)) for j, (cx, cy) in enumerate(chips)]
        for cp in sent:
            cp.start()
        for j, (cx, cy) in enumerate(chips):
            blk = slot(cx, cy, c)
            _remote(blk, blk, send_sems.at[1 + j], recv_sems.at[1 + j], (cx, cy, c)).wait_recv()
            passed = _remote(blk, blk, send_sems.at[4 + j], recv_sems.at[4 + j], sibling)
            passed.start()
            sent.append(passed)
        blk = slot(x, y, 1 - c)
        _remote(blk, blk, send_sems.at[0], recv_sems.at[0], sibling).wait_recv()
        for j, (cx, cy) in enumerate(chips):
            blk = slot(cx, cy, 1 - c)
            _remote(blk, blk, send_sems.at[4 + j], recv_sems.at[4 + j], sibling).wait_recv()
        for cp in sent:
            cp.wait_send()
        acc = buf[0]
        for k in range(1, 2 * N_CHIPS):
            acc = acc + buf[k]
        out_ref[...] = acc

    return pl.pallas_call(
        body, name=name, in_specs=[VMEM_WHOLE], out_specs=VMEM_WHOLE, out_shape=S((r, cols), F32),
        scratch_shapes=[pltpu.VMEM((2 * N_CHIPS, r, cols), F32), pltpu.SemaphoreType.DMA((7,)), pltpu.SemaphoreType.DMA((7,))],
        compiler_params=pltpu.CompilerParams(vmem_limit_bytes=VMEM_LIMIT),
    )(v)


def _pair_exchange(gs, name):
    n = len(gs)

    def body(*refs):
        g_refs, a_refs, send_sems, recv_sems = refs[:n], refs[n:2 * n], refs[2 * n], refs[2 * n + 1]
        x, y, c, _ = _place()
        sent = [_remote(g_refs[i].at[pl.ds(0, N_CHIPS), _half_rows(1 - c, gs[i].shape[1] // 2)], a_refs[i], send_sems.at[i], recv_sems.at[i],
                        (x, y, 1 - c)) for i in range(n)]
        for cp in sent:
            cp.start()
        for cp in sent:
            cp.wait()

    return pl.pallas_call(body, name=name, in_specs=[ANY] * n, out_specs=[ANY] * n,
                          out_shape=[S((N_CHIPS, g.shape[1] // 2, g.shape[2]), g.dtype) for g in gs],
                          scratch_shapes=[pltpu.SemaphoreType.DMA((n,)), pltpu.SemaphoreType.DMA((n,))])(*gs)


def _pair_sum(g, a, name):
    n, r, cols = g.shape
    half = r // 2
    tr = _tile(half, 512, 16)
    n_blk = half // tr

    def body(c_ref, g_ref, a_ref, o_ref):
        o_ref[...] = (g_ref[...] + a_ref[...]).astype(o_ref.dtype)

    blk = pl.BlockSpec((1, tr, cols), lambda j, i, c_ref: (j, i, 0))
    return pl.pallas_call(
        body, name=name, out_shape=S((n, half, cols), BF16),
        grid_spec=pltpu.PrefetchScalarGridSpec(
            num_scalar_prefetch=1, grid=(n, n_blk),
            in_specs=[pl.BlockSpec((1, tr, cols), lambda j, i, c_ref: (j, c_ref[0] * n_blk + i, 0)), blk], out_specs=blk),
        compiler_params=_params(("parallel", "parallel")),
    )(lax.axis_index("c").reshape(1).astype(jnp.int32), g, a)


HBM = pl.BlockSpec(memory_space=pltpu.HBM)
SEM = pl.BlockSpec(memory_space=pltpu.SEMAPHORE)
SPLIT_COPY = pltpu.CompilerParams(has_side_effects=pltpu.SideEffectType.DATAFLOW_SIDE_EFFECTING)


def _ici_begin(srcs, lands, views, name):
    n = len(srcs)

    def body(*refs):
        s_refs, l_refs, send_sems, recv_sems, token = refs[:n], refs[n:2 * n], refs[2 * n], refs[2 * n + 1], refs[-1]
        x, y, c, chips = _place()
        for i in range(n):
            for j, (cx, cy) in enumerate(chips):
                src, dst, _ = views(s_refs[i], l_refs[i], i, j, x, y, c, cx, cy)
                k = (N_CHIPS - 1) * i + j
                _remote(src, dst, send_sems.at[k], recv_sems.at[k], (cx, cy, c)).start()
        token[...] = jnp.zeros_like(token)

    bufs = list(srcs) + list(lands)
    sems = pltpu.SemaphoreType.DMA((n * (N_CHIPS - 1),))
    return pl.pallas_call(
        body, name=name, out_shape=(sems, sems, *[pltpu.HBM(b.shape, b.dtype) for b in bufs], S((8, LANES), F32)),
        in_specs=[HBM] * (2 * n), out_specs=(SEM, SEM, *[HBM] * (2 * n), VMEM_WHOLE),
        input_output_aliases={i: 2 + i for i in range(2 * n)}, compiler_params=SPLIT_COPY,
    )(*[pltpu.with_memory_space_constraint(b, pltpu.HBM) for b in bufs])


def _ici_end(handle, after, views, name):
    send_sems, recv_sems, *bufs, _ = handle
    n = len(bufs) // 2

    def body(*refs):
        s_refs, l_refs, send_sems_, recv_sems_ = refs[:n], refs[n:2 * n], refs[2 * n], refs[2 * n + 1]
        x, y, c, chips = _place()
        for i in range(n):
            for j, (cx, cy) in enumerate(chips):
                src, _, arrival = views(s_refs[i], l_refs[i], i, j, x, y, c, cx, cy)
                k = (N_CHIPS - 1) * i + j
                cp = _remote(src, arrival, send_sems_.at[k], recv_sems_.at[k], (cx, cy, c))
                cp.wait_send()
                cp.wait_recv()

    out = pl.pallas_call(
        body, name=name, out_shape=[pltpu.HBM(b.shape, b.dtype) for b in bufs], in_specs=[HBM] * (2 * n) + [SEM, SEM, ANY],
        out_specs=[HBM] * (2 * n), input_output_aliases={i: i for i in range(2 * n)}, compiler_params=SPLIT_COPY,
    )(*bufs, send_sems, recv_sems, after)
    return out[:n], out[n:]


def _gather_views(halves):
    def views(src, land, i, j, x, y, c, cx, cy):
        mine = _half_rows(c, halves[i])
        return src.at[mine], land.at[2 * x + y, mine], land.at[2 * cx + cy, mine]
    return views


def _gather_begin(shards, name):
    lands = [lax.empty((N_CHIPS,) + s.shape, s.dtype) for s in shards]
    return _ici_begin(shards, lands, _gather_views([s.shape[0] // 2 for s in shards]), name)


def _gather_end(handle, after, name):
    n = (len(handle) - 3) // 2
    shards, lands = _ici_end(handle, after, _gather_views([s.shape[0] // 2 for s in handle[2:2 + n]]), name + "_wait")
    return _gather_finish(shards, lands, name + "_finish")


def _gather_finish(shards, lands, name):
    n = len(shards)

    def body(*refs):
        ins, l_refs, outs, send_sems, recv_sems = refs[:n], refs[n:2 * n], refs[2 * n:3 * n], refs[3 * n], refs[3 * n + 1]
        x, y, c, chips = _place()
        me, sibling = 2 * x + y, (x, y, 1 - c)
        sent = []
        for i in range(n):
            mine = _half_rows(c, shards[i].shape[0] // 2)
            for j, (cx, cy) in enumerate(chips):
                sent.append(_remote(l_refs[i].at[2 * cx + cy, mine], outs[i].at[2 * cx + cy, mine], send_sems.at[i, j], recv_sems.at[i, j], sibling))
            sent.append(_remote(ins[i], outs[i].at[me], send_sems.at[i, 3], recv_sems.at[i, 3], sibling))
        for cp in sent:
            cp.start()
        for i in range(n):
            other = _half_rows(1 - c, shards[i].shape[0] // 2)
            for j, (cx, cy) in enumerate(chips):
                rows = outs[i].at[2 * cx + cy, other]
                _remote(rows, rows, send_sems.at[i, j], recv_sems.at[i, j], sibling).wait_recv()
            own = outs[i].at[me]
            _remote(own, own, send_sems.at[i, 3], recv_sems.at[i, 3], sibling).wait_recv()
        for cp in sent:
            cp.wait_send()

    return pl.pallas_call(
        body, name=name, in_specs=[ANY] * (2 * n), out_specs=[ANY] * n, out_shape=[S(l_.shape, l_.dtype) for l_ in lands],
        input_output_aliases={n + i: i for i in range(n)},
        scratch_shapes=[pltpu.SemaphoreType.DMA((n, 4)), pltpu.SemaphoreType.DMA((n, 4))],
    )(*shards, *lands)


def _exchange_views(src, land, i, j, x, y, c, cx, cy):
    return src.at[2 * cx + cy], land.at[j], land.at[j]


def _chip_exchange_begin(ps, name):
    return _ici_begin(ps, [lax.empty((N_CHIPS - 1,) + p.shape[1:], p.dtype) for p in ps], _exchange_views, name)


def _chip_exchange_end(handle, after, name):
    return _ici_end(handle, after, _exchange_views, name)


def _quad_sum(p, q, name):
    _, h, cols = p.shape
    tr = _tile(h, 512, 16)
    n_blk = h // tr
    x, y, c, _ = _place()

    def body(s_ref, p_ref, q0_ref, q1_ref, q2_ref, o_ref):
        o_ref[...] = p_ref[0].astype(F32) + q0_ref[0].astype(F32) + q1_ref[0].astype(F32) + q2_ref[0].astype(F32)

    in_specs = [pl.BlockSpec((1, tr, cols), lambda i, s_ref: (s_ref[0], i, 0))]
    in_specs += [pl.BlockSpec((1, tr, cols), functools.partial(lambda i, s_ref, k: (k, i, 0), k=k)) for k in range(N_CHIPS - 1)]
    return pl.pallas_call(
        body, name=name, out_shape=S((2 * h, cols), F32),
        grid_spec=pltpu.PrefetchScalarGridSpec(num_scalar_prefetch=1, grid=(n_blk,), in_specs=in_specs,
                                               out_specs=pl.BlockSpec((tr, cols), lambda i, s_ref: (s_ref[1] * n_blk + i, 0))),
        compiler_params=_params(("parallel",)),
    )(jnp.stack([2 * x + y, c]).astype(jnp.int32), p, q, q, q)


def _pair_share(bufs, name):
    n = len(bufs)

    def body(*refs):
        in_refs, out_refs, send_sems, recv_sems = refs[:n], refs[n:2 * n], refs[2 * n], refs[2 * n + 1]
        x, y, c, _ = _place()
        sent = []
        for i in range(n):
            mine = _half_rows(c, bufs[i].shape[0] // 2)
            sent.append(_remote(in_refs[i].at[mine], out_refs[i].at[mine], send_sems.at[i], recv_sems.at[i], (x, y, 1 - c)))
        for cp in sent:
            cp.start()
        for i in range(n):
            other = out_refs[i].at[_half_rows(1 - c, bufs[i].shape[0] // 2)]
            _remote(other, other, send_sems.at[i], recv_sems.at[i], (x, y, 1 - c)).wait_recv()
        for cp in sent:
            cp.wait_send()

    return pl.pallas_call(body, name=name, in_specs=[ANY] * n, out_specs=[ANY] * n, out_shape=[S(b.shape, b.dtype) for b in bufs],
                          input_output_aliases={i: i for i in range(n)},
                          scratch_shapes=[pltpu.SemaphoreType.DMA((n,)), pltpu.SemaphoreType.DMA((n,))])(*bufs)


def _adamw(w, gs, m, v, name):
    depth, r, cols = w.shape
    tr = _tile(r, 256, 8)

    def body(*refs):
        w_ref, m_ref, v_ref = refs[:3]
        g_refs = refs[3:3 + depth]
        go_ref, d_ref, mo_ref, vo_ref = refs[3 + depth:]
        for l in range(depth):
            @pl.when(pl.program_id(0) == l)
            def _(l=l):
                g_ = g_refs[l][...]
                m_ = ADAM_B1 * m_ref[...] + (1.0 - ADAM_B1) * g_
                v_ = ADAM_B2 * v_ref[...] + (1.0 - ADAM_B2) * jnp.square(g_)
                m_hat = m_ / (1.0 - ADAM_B1 ** ADAM_STEP)
                v_hat = v_ / (1.0 - ADAM_B2 ** ADAM_STEP)
                d_ref[...] = -ADAM_LR * (m_hat / (jnp.sqrt(v_hat) + ADAM_EPS) + ADAM_WD * w_ref[...])
                go_ref[...], mo_ref[...], vo_ref[...] = g_, m_, v_

    blk = pl.BlockSpec((None, tr, cols), lambda l, i: (l, i, 0))
    g_blk = pl.BlockSpec((tr, cols), lambda l, i: (i, 0))
    return pl.pallas_call(body, name=name, grid=(depth, r // tr), in_specs=[blk] * 3 + [g_blk] * depth, out_specs=[blk] * 4,
                          out_shape=[S(w.shape, F32)] * 4, compiler_params=_params(("parallel", "parallel")))(w, m, v, *gs)


WEIGHTS = ("ffn1_norm", "ffn1_w1", "ffn1_w2", "mix_norm", "w_in", "b_in", "lru_conv_w", "lru_conv_b", "lru_w_gate", "lru_b_gate",
           "lru_lambda", "lru_w_out", "q_norm", "w_uq", "kv_norm", "w_ukv", "mla_w_o", "conv_dw_w", "conv_dw_b", "conv_ln_g",
           "conv_ln_b", "conv_w_out", "conv_b_out", "w_out", "ffn2_norm", "ffn2_w1", "ffn2_w2", "final_norm")
ROW_SHARDED = ("ffn1_w2", "w_out", "ffn2_w2")
COL_SHARDED = ("ffn1_w1", "w_in", "lru_w_out", "w_uq", "w_ukv", "mla_w_o", "conv_w_out", "ffn2_w1")
SMALL_SHARDED = ("lru_conv_w", "conv_dw_w")
MXU_SHARDED = tuple(n for n in WEIGHTS if n in ROW_SHARDED + COL_SHARDED)
REPLICATED = tuple(n for n in WEIGHTS if n not in MXU_SHARDED + SMALL_SHARDED)
SMALL = REPLICATED + SMALL_SHARDED
INPUTS = ("x", "positions") + WEIGHTS + ("loss_target",) + tuple("m_" + n for n in WEIGHTS) + tuple("v_" + n for n in WEIGHTS)


def _pack(arrays, dtype, cols, row_unit):
    flat = jnp.concatenate([a.astype(dtype).reshape(-1) for a in arrays])
    unit = cols * row_unit
    return jnp.pad(flat, (0, -flat.shape[0] % unit)).reshape(-1, cols)


def _unpack(flat, shapes):
    out, off = [], 0
    for shp in shapes:
        n = 1
        for s_ in shp:
            n *= s_
        out.append(flat[..., off:off + n].reshape(flat.shape[:-1] + tuple(shp)))
        off += n
    return out


def _reduce_scatter_begin(gs, names, tag):
    halves = _pair_exchange(gs, tag + "_pair_exchange")
    pairs = [_pair_sum(g, h, f"{tag}_pair_sum_{n}") for n, g, h in zip(names, gs, halves, strict=True)]
    return _chip_exchange_begin(pairs, tag + "_chip_exchange")


def _reduce_scatter_end(handle, names, after, tag):
    pairs, others = _chip_exchange_end(handle, after, tag + "_chip_exchange_wait")
    sums = [_quad_sum(p, q, f"{tag}_chip_sum_{n}") for n, p, q in zip(names, pairs, others, strict=True)]
    return dict(zip(names, _pair_share(sums, tag + "_pair_share"), strict=True))


PARTS = (("ffn1", ("ffn1_w1", "ffn1_w2")), ("mix", ("w_in", "lru_w_out", "w_uq", "w_ukv", "mla_w_o", "conv_w_out", "w_out")),
         ("ffn2", ("ffn2_w1", "ffn2_w2")))


def _step(a):
    x, positions, target = a["x"][0], a["positions"][0], a["loss_target"][0]
    depth = a["ffn1_norm"].shape[0]
    me = 2 * lax.axis_index("x") + lax.axis_index("y")
    placed = [lax.dynamic_update_slice_in_dim(jnp.zeros(a[n].shape[:2] + (N_CHIPS,) + a[n].shape[2:], F32), 0.5 * a[n][:, :, None], me, 2)
              for n in SMALL_SHARDED]
    small_whole = _unpack(_allreduce_all(_pack(placed, F32, LANES, 8), "small_weights").reshape(-1), [p_.shape for p_ in placed])
    small_whole = {n: w.reshape(w.shape[:2] + (-1,)) for n, w in zip(SMALL_SHARDED, small_whole, strict=True)}
    base = [{n: a[n][l] for n in REPLICATED if a[n].ndim > 1} | {n: small_whole[n][l] for n in SMALL_SHARDED} for l in range(depth)]
    order = [(l, part, names) for l in range(depth) for part, names in PARTS]
    shards = lambda l, names: [a[n][l].astype(BF16) for n in names]
    rope = _rope_tables(positions, "rope_tables")
    gathered = _gather_chips(shards(0, order[0][2]), "l0_ffn1_gather")
    saved = []
    for k, (l, part, names) in enumerate(order):
        p = base[l] | {n: g.reshape(-1, g.shape[-1]) if n in ROW_SHARDED else g for n, g in zip(names, gathered, strict=True)}
        handle = None
        if k + 1 < len(order):
            l2, part2, names2 = order[k + 1]
            handle = _gather_begin(shards(l2, names2), f"l{l2}_{part2}_gather")
        x, s = _part_fwd(part, x, p, rope, f"l{l}_{part}", None if handle is None else handle[-1][0, 0])
        saved.append(s)
        if handle is not None:
            gathered = _gather_end(handle, x, f"l{l2}_{part2}_gather")
    loss, dx, dfinal = _loss_head(x, target, a["final_norm"][None, :], "loss_head")
    loss = lax.psum(loss, MESH_AXES)
    grads, shard_grads, pending = [{} for _ in range(depth)], [{} for _ in range(depth)], None
    for l, part, names in reversed(order):
        dx, g = _part_bwd(part, dx, saved.pop(), rope, f"l{l}_{part}", None if pending is None else pending[0][-1][0, 0])
        if pending is not None:
            shard_grads[pending[1]].update(_reduce_scatter_end(pending[0], pending[2], dx, pending[3]))
        pending = (_reduce_scatter_begin([g[n] for n in names], names, f"l{l}_{part}_grad"), l, names, f"l{l}_{part}_grad")
        grads[l].update({n: g[n] for n in g if n in SMALL})
    shard_grads[pending[1]].update(_reduce_scatter_end(pending[0], pending[2], dx, pending[3]))
    small = [jnp.stack([g[n] for g in grads]) if a[n].ndim > 1 else dfinal[0] for n in SMALL]
    g_small = _unpack(_allreduce_all(_pack(small, F32, LANES, 256), "grad_allreduce").reshape(-1), [s_.shape for s_ in small])
    g_small = [lax.dynamic_slice_in_dim(g, me * a[n].shape[-1], a[n].shape[-1], 2) if n in SMALL_SHARDED else g
               for n, g in zip(SMALL, g_small, strict=True)]
    g, delta, new_m, new_v = {}, {}, {}, {}
    for n in MXU_SHARDED:
        g[n], delta[n], new_m[n], new_v[n] = _adamw(a[n], [shard_grads[l][n] for l in range(depth)], a["m_" + n], a["v_" + n], "adamw_" + n)
    shapes = [a[n].shape for n in SMALL]
    packed = [_pack([a[pre + n] for n in SMALL], F32, LANES, 256)[None] for pre in ("", "m_", "v_")]
    res = _adamw(packed[0], [_pack(g_small, F32, LANES, 256)], packed[1], packed[2], "adamw_small")
    for out, r in zip((g, delta, new_m, new_v), res, strict=True):
        out.update(zip(SMALL, _unpack(r.reshape(-1), shapes), strict=True))
    return (loss, dx[None], *[g[n] for n in WEIGHTS], *[delta[n] for n in WEIGHTS], *[new_m[n] for n in WEIGHTS], *[new_v[n] for n in WEIGHTS])


def kernel(x, positions, ffn1_norm, ffn1_w1, ffn1_w2, mix_norm, w_in, b_in, lru_conv_w, lru_conv_b, lru_w_gate, lru_b_gate, lru_lambda, lru_w_out, q_norm, w_uq, kv_norm, w_ukv, mla_w_o, conv_dw_w, conv_dw_b, conv_ln_g, conv_ln_b, conv_w_out, conv_b_out, w_out, ffn2_norm, ffn2_w1, ffn2_w2, final_norm, loss_target, m_ffn1_norm, m_ffn1_w1, m_ffn1_w2, m_mix_norm, m_w_in, m_b_in, m_lru_conv_w, m_lru_conv_b, m_lru_w_gate, m_lru_b_gate, m_lru_lambda, m_lru_w_out, m_q_norm, m_w_uq, m_kv_norm, m_w_ukv, m_mla_w_o, m_conv_dw_w, m_conv_dw_b, m_conv_ln_g, m_conv_ln_b, m_conv_w_out, m_conv_b_out, m_w_out, m_ffn2_norm, m_ffn2_w1, m_ffn2_w2, m_final_norm, v_ffn1_norm, v_ffn1_w1, v_ffn1_w2, v_mix_norm, v_w_in, v_b_in, v_lru_conv_w, v_lru_conv_b, v_lru_w_gate, v_lru_b_gate, v_lru_lambda, v_lru_w_out, v_q_norm, v_w_uq, v_kv_norm, v_w_ukv, v_mla_w_o, v_conv_dw_w, v_conv_dw_b, v_conv_ln_g, v_conv_ln_b, v_conv_w_out, v_conv_b_out, v_w_out, v_ffn2_norm, v_ffn2_w1, v_ffn2_w2, v_final_norm):
    return _step(dict(zip(INPUTS, (x, positions, ffn1_norm, ffn1_w1, ffn1_w2, mix_norm, w_in, b_in, lru_conv_w, lru_conv_b, lru_w_gate, lru_b_gate, lru_lambda, lru_w_out, q_norm, w_uq, kv_norm, w_ukv, mla_w_o, conv_dw_w, conv_dw_b, conv_ln_g, conv_ln_b, conv_w_out, conv_b_out, w_out, ffn2_norm, ffn2_w1, ffn2_w2, final_norm, loss_target, m_ffn1_norm, m_ffn1_w1, m_ffn1_w2, m_mix_norm, m_w_in, m_b_in, m_lru_conv_w, m_lru_conv_b, m_lru_w_gate, m_lru_b_gate, m_lru_lambda, m_lru_w_out, m_q_norm, m_w_uq, m_kv_norm, m_w_ukv, m_mla_w_o, m_conv_dw_w, m_conv_dw_b, m_conv_ln_g, m_conv_ln_b, m_conv_w_out, m_conv_b_out, m_w_out, m_ffn2_norm, m_ffn2_w1, m_ffn2_w2, m_final_norm, v_ffn1_norm, v_ffn1_w1, v_ffn1_w2, v_mix_norm, v_w_in, v_b_in, v_lru_conv_w, v_lru_conv_b, v_lru_w_gate, v_lru_b_gate, v_lru_lambda, v_lru_w_out, v_q_norm, v_w_uq, v_kv_norm, v_w_ukv, v_mla_w_o, v_conv_dw_w, v_conv_dw_b, v_conv_ln_g, v_conv_ln_b, v_conv_w_out, v_conv_b_out, v_w_out, v_ffn2_norm, v_ffn2_w1, v_ffn2_w2, v_final_norm), strict=True)))
```

```python
import functools

import jax
import jax.numpy as jnp
from jax import lax
from jax.experimental import pallas as pl
from jax.experimental.pallas import tpu as pltpu

F32, BF16 = jnp.float32, jnp.bfloat16
S = jax.ShapeDtypeStruct

LANES = 128
VMEM_LIMIT = 56 * 2**20
NORM_EPS = 1e-6
LRU_C = 8.0
MLA_HEADS = 8
QK_NOPE, QK_ROPE, V_HEAD = 64, 32, 64
HEAD_PAD = 128
ROPE_THETA = 10000.0
ADAM_LR, ADAM_B1, ADAM_B2, ADAM_EPS, ADAM_WD, ADAM_STEP = 0.001, 0.9, 0.999, 1e-08, 0.01, 10
MESH_AXES = ("x", "y", "c")
N_CHIPS = 4
GRAD_DTYPE = BF16
NT =(((1,), (1,)), ((), ()))
TN = (((0,), (0,)), ((), ()))
NN = (((1,), (0,)), ((), ()))


def _tile(n, cap, unit=LANES):
    best = None
    for d in range(unit, min(n, cap) + 1, unit):
        if n % d == 0:
            best = d
    return best if best is not None else n


def _params(sem):
    return pltpu.CompilerParams(dimension_semantics=sem, vmem_limit_bytes=VMEM_LIMIT)


def _mm(a, b, mode="nn", out_dtype=F32, bias=None, res=None, alpha=1.0, b_chips=False, out_chips=False, name="mm"):
    n_unit = k_unit = None
    if b_chips:
        chips, rows, c = b.shape
        b_shape = (rows, chips * c)
        n_unit, k_unit = (c, None) if mode == "nn" else (None, c)
    else:
        b_shape = b.shape
    if mode == "nn":
        (m, k), (k2, n), dims = a.shape, b_shape, NN
    elif mode == "nt":
        (m, k), (n, k2), dims = a.shape, b_shape, NT
    else:
        (k, m), (k2, n), dims = a.shape, b_shape, TN
    assert k == k2 and not (b_chips and mode == "tn"), (name, a.shape, b.shape, mode)
    if out_chips:
        n_unit = n // N_CHIPS
    tm = _tile(m, 512)
    tn = _tile(n_unit or n, 1536)
    tk = k if (k <= 3072 and k_unit is None) else _tile(k_unit or k, 3072)
    nk = k // tk
    if mode == "tn":
        a_spec = pl.BlockSpec((tk, tm), lambda i, j, kk: (kk, i))
    else:
        a_spec = pl.BlockSpec((tm, tk), lambda i, j, kk: (i, kk))
    if b_chips and mode == "nn":
        b_spec = pl.BlockSpec((None, tk, tn), functools.partial(lambda i, j, kk, per: (j // per, kk, j % per), per=n_unit // tn))
    elif b_chips:
        b_spec = pl.BlockSpec((None, tn, tk), functools.partial(lambda i, j, kk, per: (kk // per, j, kk % per), per=k_unit // tk))
    elif mode == "nt":
        b_spec = pl.BlockSpec((tn, tk), lambda i, j, kk: (j, kk))
    else:
        b_spec = pl.BlockSpec((tk, tn), lambda i, j, kk: (kk, j))
    if out_chips:
        out_spec = pl.BlockSpec((None, tm, tn), functools.partial(lambda i, j, kk, per: (j // per, i, j % per), per=n_unit // tn))
        out_shape = S((N_CHIPS, m, n_unit), out_dtype)
    else:
        out_spec = pl.BlockSpec((tm, tn), lambda i, j, kk: (i, j))
        out_shape = S((m, n), out_dtype)
    operands, in_specs = [a, b], [a_spec, b_spec]
    if bias is not None:
        operands.append(bias)
        in_specs.append(pl.BlockSpec((1, tn), lambda i, j, kk: (0, j)))
    if res is not None:
        operands.append(res)
        in_specs.append(pl.BlockSpec((tm, tn), lambda i, j, kk: (i, j)))

    def body(*refs):
        a_ref, b_ref = refs[0], refs[1]
        pos = 2
        bias_ref = res_ref = None
        if bias is not None:
            bias_ref, pos = refs[pos], pos + 1
        if res is not None:
            res_ref, pos = refs[pos], pos + 1
        o_ref = refs[pos]
        part = lax.dot_general(a_ref[...].astype(BF16), b_ref[...].astype(BF16), dims, preferred_element_type=F32)

        def finish(acc):
            out = acc if alpha == 1.0 else acc * alpha
            if bias_ref is not None:
                out = out + bias_ref[...]
            if res_ref is not None:
                out = out + res_ref[...]
            o_ref[...] = out.astype(o_ref.dtype)

        if nk == 1:
            finish(part)
        else:
            acc_ref = refs[pos + 1]
            kk = pl.program_id(2)

            @pl.when(kk == 0)
            def _():
                acc_ref[...] = part

            @pl.when(kk > 0)
            def _():
                acc_ref[...] += part

            @pl.when(kk == nk - 1)
            def _():
                finish(acc_ref[...])

    return pl.pallas_call(
        body, name=name, grid=(m // tm, n // tn, nk), in_specs=in_specs, out_specs=out_spec, out_shape=out_shape,
        scratch_shapes=[pltpu.VMEM((tm, tn), F32)] if nk > 1 else [],
        compiler_params=_params(("parallel", "parallel", "arbitrary")),
    )(*operands)


def _rowwise(fn, rows, params=(), outs=(), accs=(), tt=256, name="rowwise"):
    rows = [r if isinstance(r, tuple) else (r, r.shape[1], 0) for r in rows]
    t = rows[0][0].shape[0]
    tt = min(tt, t)
    n_rows, n_par, n_out = len(rows), len(params), len(outs)
    in_specs = [pl.BlockSpec((tt, w), functools.partial(lambda i, cb: (i, cb), cb=cb)) for (_, w, cb) in rows]
    in_specs += [pl.BlockSpec(p.shape, functools.partial(lambda i, nd: (0,) * nd, nd=p.ndim)) for p in params]
    out_shape = [S((t, w), dt) for (w, dt) in outs] + [S(shape, dt) for (shape, dt) in accs]
    out_specs = [pl.BlockSpec((tt, w), lambda i: (i, 0)) for (w, _) in outs]
    out_specs += [pl.BlockSpec(shape, functools.partial(lambda i, nd: (0,) * nd, nd=len(shape))) for (shape, _) in accs]

    def body(*refs):
        vals = [r[...] for r in refs[:n_rows + n_par]]
        o_vals, a_vals = fn(*vals)
        o_refs = refs[n_rows + n_par:n_rows + n_par + n_out]
        a_refs = refs[n_rows + n_par + n_out:]
        for ref, val in zip(o_refs, o_vals, strict=True):
            ref[...] = val.astype(ref.dtype)
        i = pl.program_id(0)
        for ref, val in zip(a_refs, a_vals, strict=True):
            @pl.when(i == 0)
            def _(ref=ref, val=val):
                ref[...] = val.astype(ref.dtype)

            @pl.when(i > 0)
            def _(ref=ref, val=val):
                ref[...] += val.astype(ref.dtype)

    res = pl.pallas_call(
        body, name=name, grid=(t // tt,), in_specs=in_specs, out_specs=out_specs, out_shape=out_shape,
        compiler_params=_params(("arbitrary",) if accs else ("parallel",)),
    )(*[r[0] for r in rows], *params)
    return res


def _rms(x, g):
    x = x.astype(F32)
    return x * lax.rsqrt(jnp.mean(x * x, axis=-1, keepdims=True) + NORM_EPS) * g


def _layer_norm_silu(x, g, b):
    mu = jnp.mean(x, axis=-1, keepdims=True)
    var = jnp.mean(jnp.square(x - mu), axis=-1, keepdims=True)
    return jax.nn.silu((x - mu) * lax.rsqrt(var + NORM_EPS) * g + b)


def _neg_expm1(z):
    series = -z * (1.0 + z * (0.5 + z * (1.0 / 6.0 + z * (1.0 / 24.0 + z * (1.0 / 120.0)))))
    return jnp.where(z > -0.05, series, 1.0 - jnp.exp(z))


def _shift_down(x, s, fill=0.0):
    if s == 0:
        return x
    row = lax.broadcasted_iota(jnp.int32, x.shape, 0)
    return jnp.where(row >= s, pltpu.roll(x, s, 0), fill)


def _shift_up(x, s, fill=0.0):
    if s == 0:
        return x
    t = x.shape[0]
    row = lax.broadcasted_iota(jnp.int32, x.shape, 0)
    return jnp.where(row < t - s, pltpu.roll(x, t - s, 0), fill)


def _scan(a, u, shift):
    t, d = a.shape[0], 1
    while d < t:
        u = u + a * shift(u, d, 0.0)
        if 2 * d < t:
            a = a * shift(a, d, 1.0)
        d *= 2
    return u


def _lru_gates(xa, wr, wi, br, bi, lam):
    xb = xa.astype(BF16)
    r = jax.nn.sigmoid(jnp.dot(xb, wr.astype(BF16), preferred_element_type=F32) + br)
    i = jax.nn.sigmoid(jnp.dot(xb, wi.astype(BF16), preferred_element_type=F32) + bi)
    log_a = -LRU_C * r * jax.nn.softplus(-lam)
    return jnp.exp(log_a), jnp.sqrt(_neg_expm1(2.0 * log_a)) * (i * xa)


def _conv_fwd(x, w_ref, b, width):
    y = b + w_ref[pl.ds(width - 1, 1), :] * x
    for j in range(width - 1):
        y = y + w_ref[pl.ds(j, 1), :] * _shift_down(x, width - 1 - j)
    return y


def _conv_bwd(x, dy, w_ref, dw_ref, width):
    dx = w_ref[pl.ds(width - 1, 1), :] * dy
    dw_ref[pl.ds(width - 1, 1), :] = jnp.sum(dy * x, axis=0, keepdims=True)
    for j in range(width - 1):
        s = width - 1 - j
        dx = dx + w_ref[pl.ds(j, 1), :] * _shift_up(dy, s)
        dw_ref[pl.ds(j, 1), :] = jnp.sum(dy * _shift_down(x, s), axis=0, keepdims=True)
    return dx


def _rope(z, c, s1, s2):
    return z * c + pltpu.roll(z, HEAD_PAD - QK_ROPE // 2, 1) * s1 + pltpu.roll(z, QK_ROPE // 2, 1) * s2


def _rope_t(d, c, s1, s2):
    return d * c + pltpu.roll(d * s1, QK_ROPE // 2, 1) + pltpu.roll(d * s2, HEAD_PAD - QK_ROPE // 2, 1)


def _heads(z):
    return [z[:, h * HEAD_PAD:(h + 1) * HEAD_PAD] for h in range(z.shape[1] // HEAD_PAD)]


def _chan_spec(t, c_off=0):
    return pl.BlockSpec((t, LANES), lambda c: (0, c_off + c))


def _lru_specs(t, n_tiles, width):
    vec = pl.BlockSpec((1, LANES), lambda c: (0, c))
    mat = pl.BlockSpec((1, LANES, LANES), lambda c: (c, 0, 0))
    return [_chan_spec(t), _chan_spec(t, n_tiles), pl.BlockSpec((width, LANES), lambda c: (0, c)), vec, mat, mat, vec, vec, vec]


def _lru_fwd(pa, cw, cb, wr, wi, br, bi, lam, name):
    t, w = pa.shape[0], pa.shape[1] // 2
    n_tiles, width = w // LANES, cw.shape[0]

    def body(x_ref, g_ref, cw_ref, cb_ref, wr_ref, wi_ref, br_ref, bi_ref, lam_ref, y_ref):
        xa = _conv_fwd(x_ref[...], cw_ref, cb_ref[...], width)
        a, u = _lru_gates(xa, wr_ref[0], wi_ref[0], br_ref[...], bi_ref[...], lam_ref[...])
        h = _scan(a, u, _shift_down)
        y_ref[...] = (h * jax.nn.gelu(g_ref[...])).astype(y_ref.dtype)

    return pl.pallas_call(
        body, name=name, grid=(n_tiles,), in_specs=_lru_specs(t, n_tiles, width), out_specs=_chan_spec(t),
        out_shape=S((t, w), BF16), compiler_params=_params(("parallel",)),
    )(pa, pa, cw, cb, wr, wi, br, bi, lam)


def _lru_bwd(pa, dy, cw, cb, wr, wi, br, bi, lam, name):
    t, w = pa.shape[0], pa.shape[1] // 2
    n_tiles, width = w // LANES, cw.shape[0]

    def body(x_ref, g_ref, cw_ref, cb_ref, wr_ref, wi_ref, br_ref, bi_ref, lam_ref, dy_ref,
             dx_ref, dg_ref, dcw_ref, dcb_ref, dwr_ref, dwi_ref, dbr_ref, dbi_ref, dlam_ref, sx_ref, sg_ref):
        x = x_ref[...]
        xa = _conv_fwd(x, cw_ref, cb_ref[...], width)
        (a, u), gates_vjp = jax.vjp(_lru_gates, xa, wr_ref[0], wi_ref[0], br_ref[...], bi_ref[...], lam_ref[...])
        h = _scan(a, u, _shift_down)
        _, out_vjp = jax.vjp(lambda h_, g_: h_ * jax.nn.gelu(g_), h, g_ref[...])
        dh, dgate = out_vjp(dy_ref[...])
        adj = _scan(_shift_up(a, 1), dh, _shift_up)
        dxa, dwr, dwi, dbr, dbi, dlam = gates_vjp((adj * _shift_down(h, 1), adj))
        dx = _conv_bwd(x, dxa, cw_ref, dcw_ref, width)
        dcb_ref[...] = jnp.sum(dxa, axis=0, keepdims=True)
        dx_ref[...] = dx.astype(dx_ref.dtype)
        dg_ref[...] = dgate.astype(dg_ref.dtype)
        sx_ref[...] = jnp.sum(dx, axis=0, keepdims=True)
        sg_ref[...] = jnp.sum(dgate, axis=0, keepdims=True)
        dwr_ref[0], dwi_ref[0] = dwr, dwi
        dbr_ref[...], dbi_ref[...], dlam_ref[...] = dbr, dbi, dlam

    vec = pl.BlockSpec((1, LANES), lambda c: (0, c))
    mat = pl.BlockSpec((1, LANES, LANES), lambda c: (c, 0, 0))
    vec_s, mat_s = S((1, w), F32), S((n_tiles, LANES, LANES), F32)
    return pl.pallas_call(
        body, name=name, grid=(n_tiles,), in_specs=_lru_specs(t, n_tiles, width) + [_chan_spec(t)],
        out_specs=[_chan_spec(t), _chan_spec(t), pl.BlockSpec((width, LANES), lambda c: (0, c)), vec, mat, mat, vec, vec, vec, vec, vec],
        out_shape=[S((t, w), BF16), S((t, w), BF16), S((width, w), F32), vec_s, mat_s, mat_s, vec_s, vec_s, vec_s, vec_s, vec_s],
        compiler_params=_params(("parallel",)),
    )(pa, pa, cw, cb, wr, wi, br, bi, lam, dy)


def _glu_conv_fwd(pc, cw, cb, name):
    t, c = pc.shape[0], pc.shape[1] // 2
    n_tiles, width = c // LANES, cw.shape[0]

    def body(v_ref, g_ref, cw_ref, cb_ref, y_ref):
        y_ref[...] = _conv_fwd(v_ref[...] * jax.nn.sigmoid(g_ref[...]), cw_ref, cb_ref[...], width)

    return pl.pallas_call(
        body, name=name, grid=(n_tiles,),
        in_specs=[_chan_spec(t), _chan_spec(t, n_tiles), pl.BlockSpec((width, LANES), lambda i: (0, i)), pl.BlockSpec((1, LANES), lambda i: (0, i))],
        out_specs=_chan_spec(t), out_shape=S((t, c), F32), compiler_params=_params(("parallel",)),
    )(pc, pc, cw, cb)


def _glu_conv_bwd(pc, dy, cw, name):
    t, c = pc.shape[0], pc.shape[1] // 2
    n_tiles, width = c // LANES, cw.shape[0]

    def body(v_ref, g_ref, cw_ref, dy_ref, dv_ref, dg_ref, dcw_ref, dcb_ref, sv_ref, sg_ref):
        glu = lambda v_, g_: v_ * jax.nn.sigmoid(g_)
        x, glu_vjp = jax.vjp(glu, v_ref[...], g_ref[...])
        dy_ = dy_ref[...]
        dv, dg = glu_vjp(_conv_bwd(x, dy_, cw_ref, dcw_ref, width))
        dcb_ref[...] = jnp.sum(dy_, axis=0, keepdims=True)
        dv_ref[...] = dv.astype(dv_ref.dtype)
        dg_ref[...] = dg.astype(dg_ref.dtype)
        sv_ref[...] = jnp.sum(dv, axis=0, keepdims=True)
        sg_ref[...] = jnp.sum(dg, axis=0, keepdims=True)

    vec = pl.BlockSpec((1, LANES), lambda i: (0, i))
    wspec = pl.BlockSpec((width, LANES), lambda i: (0, i))
    return pl.pallas_call(
        body, name=name, grid=(n_tiles,), in_specs=[_chan_spec(t), _chan_spec(t, n_tiles), wspec, _chan_spec(t)],
        out_specs=[_chan_spec(t), _chan_spec(t), wspec, vec, vec, vec],
        out_shape=[S((t, c), BF16), S((t, c), BF16), S((width, c), F32), S((1, c), F32), S((1, c), F32), S((1, c), F32)],
        compiler_params=_params(("parallel",)),
    )(pc, pc, cw, dy)


def _softmax_rows(q, k, causal, scale):
    s = lax.dot_general(q, k, NT, preferred_element_type=F32) * scale
    s = jnp.where(causal, s, jnp.finfo(F32).min)
    p = jnp.exp(s - jnp.max(s, axis=-1, keepdims=True))
    return p / jnp.sum(p, axis=-1, keepdims=True)


def _attn_specs(t, tq):
    return [pl.BlockSpec((tq, 2 * HEAD_PAD), lambda hp, i: (i, hp)), pl.BlockSpec((t, 2 * HEAD_PAD), lambda hp, i: (0, hp)),
            pl.BlockSpec((t, 2 * V_HEAD), lambda hp, i: (0, hp))]


def _causal(n, tq):
    row = lax.broadcasted_iota(jnp.int32, (tq, (n + 1) * tq), 0) + n * tq
    col = lax.broadcasted_iota(jnp.int32, (tq, (n + 1) * tq), 1)
    return col <= row


def _per_query_block(n_blocks, fn):
    for n in range(n_blocks):
        pl.when(pl.program_id(1) == n)(functools.partial(fn, n))


def _attn_fwd(qh, kh, v, name):
    t = qh.shape[0]
    tq = min(256, t)
    scale = (QK_NOPE + QK_ROPE) ** -0.5

    def body(q_ref, k_ref, v_ref, o_ref):
        def block(n):
            keys = (n + 1) * tq
            causal = _causal(n, tq)
            lane = lax.broadcasted_iota(jnp.int32, (keys, 2 * V_HEAD), 1)
            vv = v_ref[0:keys, :]
            acc = jnp.zeros((tq, 2 * V_HEAD), F32)
            for e in range(2):
                p = _softmax_rows(q_ref[:, e * HEAD_PAD:(e + 1) * HEAD_PAD], k_ref[0:keys, e * HEAD_PAD:(e + 1) * HEAD_PAD], causal, scale)
                ve = jnp.where((lane >= V_HEAD * e) & (lane < V_HEAD * (e + 1)), vv, jnp.zeros_like(vv))
                acc = acc + jnp.dot(p.astype(BF16), ve, preferred_element_type=F32)
            o_ref[...] = acc.astype(o_ref.dtype)

        _per_query_block(t // tq, block)

    return pl.pallas_call(
        body, name=name, grid=(MLA_HEADS // 2, t // tq), in_specs=_attn_specs(t, tq),
        out_specs=pl.BlockSpec((tq, 2 * V_HEAD), lambda hp, i: (i, hp)), out_shape=S((t, MLA_HEADS * V_HEAD), BF16),
        compiler_params=_params(("parallel", "parallel")),
    )(qh, kh, v)


def _attn_bwd(qh, kh, v, do, name):
    t = qh.shape[0]
    tq = min(256, t)
    scale = (QK_NOPE + QK_ROPE) ** -0.5

    def body(q_ref, k_ref, v_ref, do_ref, dq_ref, dk_ref, dv_ref):
        def block(n):
            keys = (n + 1) * tq
            causal = _causal(n, tq)
            lane = lax.broadcasted_iota(jnp.int32, (tq, 2 * V_HEAD), 1)
            vv, dd = v_ref[0:keys, :], do_ref[...]
            dqs, dks = [], []
            dv = jnp.zeros((keys, 2 * V_HEAD), F32)
            for e in range(2):
                q, k = q_ref[:, e * HEAD_PAD:(e + 1) * HEAD_PAD], k_ref[0:keys, e * HEAD_PAD:(e + 1) * HEAD_PAD]
                p = _softmax_rows(q, k, causal, scale)
                de = jnp.where((lane >= V_HEAD * e) & (lane < V_HEAD * (e + 1)), dd, jnp.zeros_like(dd))
                dp = lax.dot_general(de, vv, NT, preferred_element_type=F32)
                ds = (p * (dp - jnp.sum(p * dp, axis=-1, keepdims=True)) * scale).astype(BF16)
                dqs.append(jnp.dot(ds, k, preferred_element_type=F32))
                dks.append(lax.dot_general(ds, q, TN, preferred_element_type=F32))
                dv = dv + lax.dot_general(p.astype(BF16), de, TN, preferred_element_type=F32)
            dq_ref[...] = jnp.concatenate(dqs, axis=-1)
            dk = jnp.concatenate(dks, axis=-1)
            if n == 0:
                dk_ref[0:keys, :], dv_ref[0:keys, :] = dk, dv
                if keys < t:
                    dk_ref[keys:t, :] = jnp.zeros((t - keys, 2 * HEAD_PAD), F32)
                    dv_ref[keys:t, :] = jnp.zeros((t - keys, 2 * V_HEAD), F32)
            else:
                dk_ref[0:keys, :] += dk
                dv_ref[0:keys, :] += dv

        _per_query_block(t // tq, block)

    return pl.pallas_call(
        body, name=name, grid=(MLA_HEADS // 2, t // tq),
        in_specs=_attn_specs(t, tq) + [pl.BlockSpec((tq, 2 * V_HEAD), lambda hp, i: (i, hp))],
        out_specs=[pl.BlockSpec((tq, 2 * HEAD_PAD), lambda hp, i: (i, hp)), pl.BlockSpec((t, 2 * HEAD_PAD), lambda hp, i: (0, hp)),
                   pl.BlockSpec((t, 2 * V_HEAD), lambda hp, i: (0, hp))],
        out_shape=[S((t, MLA_HEADS * HEAD_PAD), F32), S((t, MLA_HEADS * HEAD_PAD), F32), S((t, MLA_HEADS * V_HEAD), F32)],
        compiler_params=_params(("parallel", "arbitrary")),
    )(qh, kh, v, do)


def _ffn_fwd(x, g, w1, w2, tag):
    d, f = x.shape[1], w2.shape[0]
    h, = _rowwise(lambda x_, g_: ([_rms(x_, g_)], []), [x], [g], outs=[(d, BF16)], name=tag + "_rms")
    gu = _mm(h, w1, out_dtype=BF16, b_chips=True, name=tag + "_up")
    a, = _rowwise(lambda g_, u_: ([jax.nn.silu(g_.astype(F32)) * u_.astype(F32)], []), [(gu, f, 0), (gu, f, 1)],
                  outs=[(f, BF16)], name=tag + "_act")
    return _mm(a, w2, res=x, alpha=0.5, name=tag + "_down"), (x, h, gu, a)


def _after(dep, n):
    return None if dep is None else jnp.zeros((1, n), F32) + dep


def _ffn_bwd(dy, saved, g, w1, w2, tag, dep=None):
    x, h, gu, a = saved
    d, f = x.shape[1], w2.shape[0]
    da = _mm(dy, w2, "nt", out_dtype=BF16, alpha=0.5, bias=_after(dep, f), name=tag + "_dact")
    dw2 = _mm(a, dy, "tn", GRAD_DTYPE, alpha=0.5, name=tag + "_dw2")

    def act_bwd(g_, u_, da_):
        _, vjp = jax.vjp(lambda p, q: jax.nn.silu(p) * q, g_.astype(F32), u_.astype(F32))
        return [jnp.concatenate(vjp(da_.astype(F32)), axis=-1)], []

    dgu, = _rowwise(act_bwd, [(gu, f, 0), (gu, f, 1), da], outs=[(2 * f, BF16)], tt=128, name=tag + "_dgu")
    dw1 = _mm(h, dgu, "tn", GRAD_DTYPE, out_chips=True, name=tag + "_dw1")
    dh = _mm(dgu, w1, "nt", b_chips=True, name=tag + "_dh")
    dx, dg = _rms_bwd(x, dh, dy, g, tag + "_drms")
    return dx, (dg, dw1, dw2)


def _rms_bwd(x, dh, dres, g, name):
    def fn(x_, dh_, dres_, g_):
        _, vjp = jax.vjp(_rms, x_, g_)
        dx, dg = vjp(dh_)
        return [dx + dres_], [dg]

    return _rowwise(fn, [x, dh, dres], [g], outs=[(x.shape[1], F32)], accs=[((1, x.shape[1]), F32)], name=name)


def _rope_tables(positions, name):
    half = QK_ROPE // 2
    inv = ROPE_THETA ** (-jnp.arange(0, QK_ROPE, 2, dtype=F32) / QK_ROPE)
    inv_lanes = jnp.zeros((1, HEAD_PAD), F32).at[0, QK_NOPE:QK_NOPE + QK_ROPE].set(jnp.tile(inv, 2))

    def fn(pos, inv_):
        ang = pos.astype(F32) * inv_
        lane = lax.broadcasted_iota(jnp.int32, ang.shape, 1)
        cos, sin = jnp.cos(ang), jnp.sin(ang)
        c = jnp.where(lane < QK_NOPE, 1.0, jnp.where(lane < QK_NOPE + QK_ROPE, cos, 0.0))
        s1 = jnp.where((lane >= QK_NOPE) & (lane < QK_NOPE + half), -sin, 0.0)
        s2 = jnp.where((lane >= QK_NOPE + half) & (lane < QK_NOPE + QK_ROPE), sin, 0.0)
        return [c, s1, s2], []

    return _rowwise(fn, [positions.reshape(-1, 1)], [inv_lanes], outs=[(HEAD_PAD, F32)] * 3, name=name)


def _chip_cols(g, lo, hi):
    c = g.shape[-1]
    parts = [g[j, :, max(lo, j * c) - j * c:min(hi, (j + 1) * c) - j * c] for j in range(g.shape[0]) if max(lo, j * c) < min(hi, (j + 1) * c)]
    return parts[0] if len(parts) == 1 else jnp.concatenate(parts, axis=-1)


def _cols_by_chip(segments, c):
    out, start = [[] for _ in range(N_CHIPS)], 0
    for arr, first, width in segments:
        for j in range(N_CHIPS):
            lo, hi = max(start, j * c), min(start + width, (j + 1) * c)
            if lo < hi:
                out[j].append(arr[:, first + lo - start:first + hi - start])
        start += width
    assert start == N_CHIPS * c
    return jnp.stack([jnp.concatenate(parts, axis=-1) for parts in out])


def _whole_cols(g):
    return g.transpose(1, 0, 2).reshape(g.shape[1], -1)


def _mix_offsets(lw):
    wl, ql, kvl = lw["lru_lambda"].shape[-1], lw["q_norm"].shape[-1], lw["kv_norm"].shape[-1]
    o1 = 2 * wl
    o2 = o1 + ql + kvl + QK_ROPE
    return wl, ql, kvl, o1, o2, o2 + 2 * lw["conv_ln_g"].shape[-1]


def _mix_weights(lw):
    wl, ql, kvl, o1, o2, o3 = _mix_offsets(lw)
    w_in, b_in = lw["w_in"], lw["b_in"][None, :]
    d_in = b_in.shape[1]
    z = lambda m, n: jnp.zeros(m.shape[:-1] + (n,), m.dtype)
    w_b = jnp.concatenate([_chip_cols(w_in, o1 + ql, o1 + ql + kvl), z(w_in[0], QK_NOPE), _chip_cols(w_in, o1 + ql + kvl, o2),
                           z(w_in[0], HEAD_PAD - QK_NOPE - QK_ROPE), _chip_cols(w_in, o1, o1 + ql)], axis=-1)
    b_b = jnp.concatenate([b_in[:, o1 + ql:o1 + ql + kvl], z(b_in, QK_NOPE), b_in[:, o1 + ql + kvl:o2],
                           z(b_in, HEAD_PAD - QK_NOPE - QK_ROPE), b_in[:, o1:o1 + ql]], axis=-1)
    hd = lw["lru_w_gate"].shape[-2]
    per = LANES // hd
    eye = jnp.eye(per, dtype=F32)
    wg = lw["lru_w_gate"].reshape(-1, per, hd, 2 * hd)
    block_diag = lambda m: jnp.einsum("cedk,ef->cedfk", m, eye).reshape(-1, LANES, LANES)
    bg = lw["lru_b_gate"]
    w_uq = _whole_cols(lw["w_uq"]).reshape(ql, MLA_HEADS, QK_NOPE + QK_ROPE)
    w_ukv = _whole_cols(lw["w_ukv"]).reshape(kvl, MLA_HEADS, QK_NOPE + V_HEAD)
    pad = lambda m, n: jnp.pad(m, ((0, 0), (0, 0), (0, n)))
    return dict(
        w_a=_chip_cols(w_in, 0, o1), w_b=w_b, w_c=_chip_cols(w_in, o2, o3), w_g=_chip_cols(w_in, o3, d_in),
        b_a=b_in[:, :o1], b_b=b_b, b_c=b_in[:, o2:o3], b_g=b_in[:, o3:],
        wr=block_diag(wg[..., :hd]), wi=block_diag(wg[..., hd:]),
        br=bg[:, :hd].reshape(1, -1), bi=bg[:, hd:].reshape(1, -1),
        w_uq=pad(w_uq, HEAD_PAD - QK_NOPE - QK_ROPE).reshape(ql, -1),
        w_k=pad(w_ukv[..., :QK_NOPE], HEAD_PAD - QK_NOPE).reshape(kvl, -1),
        w_v=w_ukv[..., QK_NOPE:].reshape(kvl, -1),
    )


def _mix_fwd(x, p, mw, rope, tag):
    d = x.shape[1]
    wl, ql, kvl = p["lru_lambda"].shape[-1], p["q_norm"].shape[-1], p["kv_norm"].shape[-1]
    row = lambda name: p[name][None, :]
    h, = _rowwise(lambda x_, g_: ([_rms(x_, g_)], []), [x], [row("mix_norm")], outs=[(d, BF16)], name=tag + "_rms")
    pa = _mm(h, mw["w_a"], bias=mw["b_a"], name=tag + "_pa")
    pb = _mm(h, mw["w_b"], bias=mw["b_b"], name=tag + "_pb")
    pc = _mm(h, mw["w_c"], bias=mw["b_c"], name=tag + "_pc")
    pg = _mm(h, mw["w_g"], bias=mw["b_g"], name=tag + "_pg")
    lru_args = (p["lru_conv_w"], row("lru_conv_b"), mw["wr"], mw["wi"], mw["br"], mw["bi"], row("lru_lambda"))
    ya_pre = _lru_fwd(pa, *lru_args, name=tag + "_lru")
    y_a = _mm(ya_pre, p["lru_w_out"], b_chips=True, name=tag + "_ya")
    mla_rows = [(pb, kvl, 0), (pb, ql, (kvl + HEAD_PAD) // ql)]
    assert (kvl + HEAD_PAD) % ql == 0 and kvl % HEAD_PAD == 0
    ckvn, cqn = _rowwise(lambda kv_, q_, gk, gq: ([_rms(kv_, gk), _rms(q_, gq)], []), mla_rows, [row("kv_norm"), row("q_norm")],
                         outs=[(kvl, BF16), (ql, BF16)], name=tag + "_lat_rms")
    q0 = _mm(cqn, mw["w_uq"], name=tag + "_q")
    k0 = _mm(ckvn, mw["w_k"], name=tag + "_k")
    v = _mm(ckvn, mw["w_v"], out_dtype=BF16, name=tag + "_v")

    def rope_fwd(q_, k_, kpe, c, s1, s2):
        kr = _rope(kpe, c, s1, s2)
        return [jnp.concatenate([_rope(z, c, s1, s2) for z in _heads(q_)], axis=-1),
                jnp.concatenate([z + kr for z in _heads(k_)], axis=-1)], []

    qh, kh = _rowwise(rope_fwd, [q0, k0, (pb, HEAD_PAD, kvl // HEAD_PAD), *rope],
                      outs=[(q0.shape[1], BF16), (k0.shape[1], BF16)], name=tag + "_rope")
    o = _attn_fwd(qh, kh, v, tag + "_attn")
    y_b = _mm(o, p["mla_w_o"], b_chips=True, name=tag + "_yb")
    c2 = _glu_conv_fwd(pc, p["conv_dw_w"], row("conv_dw_b"), tag + "_conv")
    c3, = _rowwise(lambda c_, g_, b_: ([_layer_norm_silu(c_, g_, b_)], []), [c2], [row("conv_ln_g"), row("conv_ln_b")],
                   outs=[(c2.shape[1], BF16)], name=tag + "_ln")
    y_c = _mm(c3, p["conv_w_out"], bias=row("conv_b_out"), b_chips=True, name=tag + "_yc")
    merged, = _rowwise(_merge, [y_a, y_b, y_c, (pg, d, 0), (pg, d, 1), (pg, d, 2)], outs=[(d, BF16)], name=tag + "_merge")
    out = _mm(merged, p["w_out"], res=x, name=tag + "_out")
    saved = dict(x=x, h=h, pa=pa, pb=pb, pc=pc, pg=pg, ya_pre=ya_pre, y_a=y_a, y_b=y_b, y_c=y_c, ckvn=ckvn, cqn=cqn,
                 qh=qh, kh=kh, v=v, o=o, c2=c2, c3=c3, merged=merged, lru_args=lru_args)
    return out, saved


def _merge(ya, yb, yc, g0, g1, g2):
    return [jax.nn.sigmoid(g0) * ya + jax.nn.sigmoid(g1) * yb + jax.nn.sigmoid(g2) * yc], []


def _mix_bwd(dy, s, p, mw, rope, tag, dep=None):
    x, h = s["x"], s["h"]
    d = x.shape[1]
    wl, ql, kvl = p["lru_lambda"].shape[-1], p["q_norm"].shape[-1], p["kv_norm"].shape[-1]
    row = lambda name: p[name][None, :]
    g = {}
    dmerged = _mm(dy, p["w_out"], "nt", bias=_after(dep, d), name=tag + "_dmerged")
    g["w_out"] = _mm(s["merged"], dy, "tn", GRAD_DTYPE, name=tag + "_dw_out")

    def merge_bwd(ya, yb, yc, g0, g1, g2, dm):
        _, vjp = jax.vjp(lambda *a: _merge(*a)[0][0], ya, yb, yc, g0, g1, g2)
        dya, dyb, dyc, d0, d1, d2 = vjp(dm)
        dpg = jnp.concatenate([d0, d1, d2], axis=-1)
        return [dya, dyb, dyc, dpg], [jnp.sum(dyc, axis=0, keepdims=True), jnp.sum(dpg, axis=0, keepdims=True)]

    pg = s["pg"]
    dya, dyb, dyc, dpg, g["conv_b_out"], db_g = _rowwise(
        merge_bwd, [s["y_a"], s["y_b"], s["y_c"], (pg, d, 0), (pg, d, 1), (pg, d, 2), dmerged],
        outs=[(d, BF16), (d, BF16), (d, BF16), (3 * d, BF16)], accs=[((1, d), F32), ((1, 3 * d), F32)], tt=128, name=tag + "_dmerge")
    g["lru_w_out"] = _mm(s["ya_pre"], dya, "tn", GRAD_DTYPE, out_chips=True, name=tag + "_dw_lru_out")
    dya_pre = _mm(dya, p["lru_w_out"], "nt", b_chips=True, name=tag + "_dya_pre")
    (dpa_x, dpa_g, g["lru_conv_w"], g["lru_conv_b"], g["wr"], g["wi"], g["br"], g["bi"], g["lru_lambda"], sb_x, sb_g) = _lru_bwd(
        s["pa"], dya_pre, *s["lru_args"], name=tag + "_dlru")
    dpa = jnp.concatenate([dpa_x, dpa_g], axis=1)
    db_a = jnp.concatenate([sb_x, sb_g], axis=1)
    g["conv_w_out"] = _mm(s["c3"], dyc, "tn", GRAD_DTYPE, out_chips=True, name=tag + "_dw_conv_out")
    dc3 = _mm(dyc, p["conv_w_out"], "nt", b_chips=True, name=tag + "_dc3")

    def ln_bwd(c_, dc_, g_, b_):
        _, vjp = jax.vjp(_layer_norm_silu, c_, g_, b_)
        dc, dg_, db_ = vjp(dc_)
        return [dc], [dg_, db_]

    cc = s["c2"].shape[1]
    dc2, g["conv_ln_g"], g["conv_ln_b"] = _rowwise(ln_bwd, [s["c2"], dc3], [row("conv_ln_g"), row("conv_ln_b")], outs=[(cc, F32)],
                                                    accs=[((1, cc), F32)] * 2, name=tag + "_dln")
    dpc_v, dpc_g, g["conv_dw_w"], g["conv_dw_b"], sc_v, sc_g = _glu_conv_bwd(s["pc"], dc2, p["conv_dw_w"], tag + "_dconv")
    dpc = jnp.concatenate([dpc_v, dpc_g], axis=1)
    db_c = jnp.concatenate([sc_v, sc_g], axis=1)
    g["mla_w_o"] = _mm(s["o"], dyb, "tn", GRAD_DTYPE, out_chips=True, name=tag + "_dw_o")
    do = _mm(dyb, p["mla_w_o"], "nt", out_dtype=BF16, b_chips=True, name=tag + "_do")
    dqh, dkh, dv = _attn_bwd(s["qh"], s["kh"], s["v"], do, tag + "_dattn")

    def rope_bwd(dq_, dk_, c, s1, s2):
        lane = lax.broadcasted_iota(jnp.int32, c.shape, 1)
        dkr = functools.reduce(lambda a, b: a + b, _heads(dk_))
        dkpe = jnp.where((lane >= QK_NOPE) & (lane < QK_NOPE + QK_ROPE), _rope_t(dkr, c, s1, s2), 0.0)
        return [jnp.concatenate([_rope_t(z, c, s1, s2) for z in _heads(dq_)], axis=-1), dk_, dkpe], []

    dq0, dk0, dkpe = _rowwise(rope_bwd, [dqh, dkh, *rope], outs=[(dqh.shape[1], BF16), (dkh.shape[1], BF16), (HEAD_PAD, F32)],
                              name=tag + "_drope")
    dvb = dv.astype(BF16)
    g["w_uq"] = _mm(s["cqn"], dq0, "tn", GRAD_DTYPE, name=tag + "_dw_uq")
    g["w_k"] = _mm(s["ckvn"], dk0, "tn", GRAD_DTYPE, name=tag + "_dw_k")
    g["w_v"] = _mm(s["ckvn"], dvb, "tn", GRAD_DTYPE, name=tag + "_dw_v")
    dcqn = _mm(dq0, mw["w_uq"], "nt", name=tag + "_dcqn")
    dckvn = _mm(dk0, mw["w_k"], "nt", name=tag + "_dckvn_k")
    dckvn = _mm(dvb, mw["w_v"], "nt", res=dckvn, name=tag + "_dckvn_v")

    def lat_bwd(kv_, q_, dkv_, dq_, dkpe_, gk, gq):
        _, vjp_k = jax.vjp(_rms, kv_, gk)
        _, vjp_q = jax.vjp(_rms, q_, gq)
        (dkv, dgk), (dq, dgq) = vjp_k(dkv_), vjp_q(dq_)
        dpb = jnp.concatenate([dkv, dkpe_, dq], axis=-1)
        return [dpb], [dgk, dgq, jnp.sum(dpb, axis=0, keepdims=True)]

    pb = s["pb"]
    dpb, g["kv_norm"], g["q_norm"], db_b = _rowwise(
        lat_bwd, [(pb, kvl, 0), (pb, ql, (kvl + HEAD_PAD) // ql), dckvn, dcqn, dkpe], [row("kv_norm"), row("q_norm")],
        outs=[(pb.shape[1], BF16)], accs=[((1, kvl), F32), ((1, ql), F32), ((1, pb.shape[1]), F32)], name=tag + "_dlat")
    dh = None
    for part, dpart in (("a", dpa), ("b", dpb), ("c", dpc), ("g", dpg)):
        g["w_" + part] = _mm(h, dpart, "tn", GRAD_DTYPE, name=tag + "_dw_" + part)
        dh = _mm(dpart, mw["w_" + part], "nt", res=dh, name=tag + "_dh_" + part)
    g["b_a"], g["b_b"], g["b_c"], g["b_g"] = db_a, db_b, db_c, db_g
    dx, g["mix_norm"] = _rms_bwd(x, dh, dy, row("mix_norm"), tag + "_drms")
    return dx, g


def _by_chip_cols(m):
    return m.reshape(m.shape[0], N_CHIPS, -1).transpose(1, 0, 2)


def _mix_grads_to_params(g, p):
    wl, ql, kvl, o1, o2, o3 = _mix_offsets(p)
    hd = p["lru_w_gate"].shape[-2]
    per = LANES // hd
    eye = jnp.eye(per, dtype=F32)
    diag = lambda m: jnp.einsum("cedfk,ef->cedk", m.reshape(-1, per, hd, per, hd), eye).reshape(-1, hd, hd)
    out = {k: g[k] for k in ("lru_w_out", "conv_w_out", "mla_w_o", "lru_conv_w", "conv_dw_w")}
    out["w_out"] = g["w_out"].reshape(N_CHIPS, -1, g["w_out"].shape[1])
    for k in ("mix_norm", "conv_b_out", "lru_conv_b", "lru_lambda", "conv_ln_g", "conv_ln_b", "conv_dw_b", "kv_norm", "q_norm"):
        out[k] = g[k][0]
    mla = lambda m: [(m, kvl + HEAD_PAD, ql), (m, 0, kvl), (m, kvl + QK_NOPE, QK_ROPE)]
    whole = lambda m: [(m, 0, m.shape[1])]
    out["w_in"] = _cols_by_chip(whole(g["w_a"]) + mla(g["w_b"]) + whole(g["w_c"]) + whole(g["w_g"]), p["w_in"].shape[-1])
    out["b_in"] = jnp.concatenate([g["b_a"]] + [m[:, a:a + w] for m, a, w in mla(g["b_b"])] + [g["b_c"], g["b_g"]], axis=1)[0]
    out["lru_w_gate"] = jnp.concatenate([diag(g["wr"]), diag(g["wi"])], axis=-1)
    out["lru_b_gate"] = jnp.concatenate([g["br"].reshape(-1, hd), g["bi"].reshape(-1, hd)], axis=-1)
    out["w_uq"] = _by_chip_cols(g["w_uq"].reshape(ql, MLA_HEADS, HEAD_PAD)[..., :QK_NOPE + QK_ROPE].reshape(ql, -1))
    out["w_ukv"] = _by_chip_cols(jnp.concatenate([g["w_k"].reshape(kvl, MLA_HEADS, HEAD_PAD)[..., :QK_NOPE],
                                                  g["w_v"].reshape(kvl, MLA_HEADS, V_HEAD)], axis=-1).reshape(kvl, -1))
    return out


def _loss_head(x, target, g, name):
    def fn(x_, t_, g_):
        y, vjp = jax.vjp(_rms, x_, g_)
        err = y - t_
        dx, dg = vjp(err * (1.0 / x_.shape[1]))
        loss = 0.5 * jnp.sum(jnp.mean(err * err, axis=-1, keepdims=True), axis=0, keepdims=True)
        return [dx], [dg, jnp.broadcast_to(loss, (1, LANES))]

    dx, dg, loss = _rowwise(fn, [x, target], [g], outs=[(x.shape[1], F32)], accs=[((1, x.shape[1]), F32), ((1, LANES), F32)], name=name)
    return loss[0, 0], dx, dg


def _part_fwd(part, x, p, rope, tag, dep=None):
    norm = part + "_norm"
    if dep is not None:
        p = dict(p, **{norm: p[norm] + dep})
    if part == "mix":
        mw = _mix_weights(p)
        x, s = _mix_fwd(x, p, mw, rope, tag)
        return x, (s, mw, p)
    x, s = _ffn_fwd(x, p[norm][None, :], p[part + "_w1"], p[part + "_w2"], tag)
    return x, (s, None, p)


def _part_bwd(part, dx, saved, rope, tag, dep=None):
    s, mw, p = saved
    if part == "mix":
        dx, gm = _mix_bwd(dx, s, p, mw, rope, tag, dep)
        return dx, _mix_grads_to_params(gm, p)
    dx, (dn, dw1, dw2) = _ffn_bwd(dx, s, p[part + "_norm"][None, :], p[part + "_w1"], p[part + "_w2"], tag, dep)
    return dx, {part + "_norm": dn[0], part + "_w1": dw1, part + "_w2": dw2.reshape(N_CHIPS, -1, dw2.shape[1])}


ANY = pl.BlockSpec(memory_space=pl.ANY)
VMEM_WHOLE = pl.BlockSpec(memory_space=pltpu.VMEM)


def _place():
    x, y, c = (lax.axis_index(a) for a in MESH_AXES)
    return x, y, c, [(1 - x, y), (x, 1 - y), (1 - x, 1 - y)]


def _remote(src, dst, send_sem, recv_sem, device):
    return pltpu.make_async_remote_copy(src_ref=src, dst_ref=dst, send_sem=send_sem, recv_sem=recv_sem, device_id=device,
                                        device_id_type=pl.DeviceIdType.MESH)


def _half_rows(c, half):
    return pl.ds(pl.multiple_of(c * half, 16), half)


def _gather_chips(shards, name):
    n = len(shards)
    halves = [s.shape[0] // 2 for s in shards]

    def body(*refs):
        ins, outs, send_sems, recv_sems = refs[:n], refs[n:2 * n], refs[2 * n], refs[2 * n + 1]
        x, y, c, chips = _place()
        me, sibling = 2 * x + y, (x, y, 1 - c)
        sent = []
        for i in range(n):
            mine = _half_rows(c, halves[i])
            sent += [_remote(ins[i].at[mine], outs[i].at[me, mine], send_sems.at[i, j], recv_sems.at[i, j], (cx, cy, c))
                     for j, (cx, cy) in enumerate(chips)]
            sent.append(_remote(ins[i], outs[i].at[me], send_sems.at[i, 6], recv_sems.at[i, 6], sibling))
        for cp in sent:
            cp.start()
        for i in range(n):
            mine = _half_rows(c, halves[i])
            for j, (cx, cy) in enumerate(chips):
                rows = outs[i].at[2 * cx + cy, mine]
                _remote(rows, rows, send_sems.at[i, j], recv_sems.at[i, j], (cx, cy, c)).wait_recv()
                passed = _remote(rows, rows, send_sems.at[i, 3 + j], recv_sems.at[i, 3 + j], sibling)
                passed.start()
                sent.append(passed)
        for i in range(n):
            other = _half_rows(1 - c, halves[i])
            for j, (cx, cy) in enumerate(chips):
                rows = outs[i].at[2 * cx + cy, other]
                _remote(rows, rows, send_sems.at[i, 3 + j], recv_sems.at[i, 3 + j], sibling).wait_recv()
            own = outs[i].at[me]
            _remote(own, own, send_sems.at[i, 6], recv_sems.at[i, 6], sibling).wait_recv()
        for cp in sent:
            cp.wait_send()

    return pl.pallas_call(
        body, name=name, in_specs=[ANY] * n, out_specs=[ANY] * n, out_shape=[S((N_CHIPS,) + s.shape, s.dtype) for s in shards],
        scratch_shapes=[pltpu.SemaphoreType.DMA((n, 7)), pltpu.SemaphoreType.DMA((n, 7))],
    )(*shards)


def _allreduce_all(v, name):
    r, cols = v.shape

    def body(v_ref, out_ref, buf, send_sems, recv_sems):
        x, y, c, chips = _place()
        sibling = (x, y, 1 - c)
        slot = lambda px, py, pc: buf.at[4 * px + 2 * py + pc]
        buf[4 * x + 2 * y + c] = v_ref[...]
        sent = [_remote(v_ref, slot(x, y, c), send_sems.at[0], recv_sems.at[0], sibling)]
        sent += [_remote(v_ref, slot(x, y, c), send_sems.at[1 + j], recv_sems.at[1 + j], (cx, cy, c)) for j, (cx, cy) in enumerate(chips)]
        for cp in sent:
            cp.start()
        for j, (cx, cy) in enumerate(chips):
            blk = slot(cx, cy, c)
            _remote(blk, blk, send_sems.at[1 + j], recv_sems.at[1 + j], (cx, cy, c)).wait_recv()
            passed = _remote(blk, blk, send_sems.at[4 + j], recv_sems.at[4 + j], sibling)
            passed.start()
            sent.append(passed)
        blk = slot(x, y, 1 - c)
        _remote(blk, blk, send_sems.at[0], recv_sems.at[0], sibling).wait_recv()
        for j, (cx, cy) in enumerate(chips):
            blk = slot(cx, cy, 1 - c)
            _remote(blk, blk, send_sems.at[4 + j], recv_sems.at[4 + j], sibling).wait_recv()
        for cp in sent:
            cp.wait_send()
        acc = buf[0]
        for k in range(1, 2 * N_CHIPS):
            acc = acc + buf[k]
        out_ref[...] = acc

    return pl.pallas_call(
        body, name=name, in_specs=[VMEM_WHOLE], out_specs=VMEM_WHOLE, out_shape=S((r, cols), F32),
        scratch_shapes=[pltpu.VMEM((2 * N_CHIPS, r, cols), F32), pltpu.SemaphoreType.DMA((7,)), pltpu.SemaphoreType.DMA((7,))],
        compiler_params=pltpu.CompilerParams(vmem_limit_bytes=VMEM_LIMIT),
    )(v)


def _pair_exchange(gs, name):
    n = len(gs)

    def body(*refs):
        g_refs, a_refs, send_sems, recv_sems = refs[:n], refs[n:2 * n], refs[2 * n], refs[2 * n + 1]
        x, y, c, _ = _place()
        sent = [_remote(g_refs[i].at[pl.ds(0, N_CHIPS), _half_rows(1 - c, gs[i].shape[1] // 2)], a_refs[i], send_sems.at[i], recv_sems.at[i],
                        (x, y, 1 - c)) for i in range(n)]
        for cp in sent:
            cp.start()
        for cp in sent:
            cp.wait()

    return pl.pallas_call(body, name=name, in_specs=[ANY] * n, out_specs=[ANY] * n,
                          out_shape=[S((N_CHIPS, g.shape[1] // 2, g.shape[2]), g.dtype) for g in gs],
                          scratch_shapes=[pltpu.SemaphoreType.DMA((n,)), pltpu.SemaphoreType.DMA((n,))])(*gs)


def _pair_sum(g, a, name):
    n, r, cols = g.shape
    half = r // 2
    tr = _tile(half, 512, 16)
    n_blk = half // tr

    def body(c_ref, g_ref, a_ref, o_ref):
        o_ref[...] = (g_ref[...].astype(F32) + a_ref[...].astype(F32)).astype(o_ref.dtype)

    blk = pl.BlockSpec((1, tr, cols), lambda j, i, c_ref: (j, i, 0))
    return pl.pallas_call(
        body, name=name, out_shape=S((n, half, cols), BF16),
        grid_spec=pltpu.PrefetchScalarGridSpec(
            num_scalar_prefetch=1, grid=(n, n_blk),
            in_specs=[pl.BlockSpec((1, tr, cols), lambda j, i, c_ref: (j, c_ref[0] * n_blk + i, 0)), blk], out_specs=blk),
        compiler_params=_params(("parallel", "parallel")),
    )(lax.axis_index("c").reshape(1).astype(jnp.int32), g, a)


HBM = pl.BlockSpec(memory_space=pltpu.HBM)
SEM = pl.BlockSpec(memory_space=pltpu.SEMAPHORE)
SPLIT_COPY = pltpu.CompilerParams(has_side_effects=pltpu.SideEffectType.DATAFLOW_SIDE_EFFECTING)


def _ici_begin(srcs, lands, views, name, after=None):
    n = len(srcs)
    extra = [] if after is None else [after]

    def body(*refs):
        s_refs, l_refs, send_sems, recv_sems, token = refs[:n], refs[n:2 * n], refs[-3 - 2 * n], refs[-2 - 2 * n], refs[-1]
        x, y, c, chips = _place()
        for i in range(n):
            for j, (cx, cy) in enumerate(chips):
                src, dst, _ = views(s_refs[i], l_refs[i], i, j, x, y, c, cx, cy)
                k = (N_CHIPS - 1) * i + j
                _remote(src, dst, send_sems.at[k], recv_sems.at[k], (cx, cy, c)).start()
        token[...] = jnp.zeros_like(token)

    bufs = list(srcs) + list(lands)
    sems = pltpu.SemaphoreType.DMA((n * (N_CHIPS - 1),))
    return pl.pallas_call(
        body, name=name, out_shape=(sems, sems, *[pltpu.HBM(b.shape, b.dtype) for b in bufs], S((8, LANES), F32)),
        in_specs=[HBM] * (2 * n) + [ANY] * len(extra), out_specs=(SEM, SEM, *[HBM] * (2 * n), VMEM_WHOLE),
        input_output_aliases={i: 2 + i for i in range(2 * n)}, compiler_params=SPLIT_COPY,
    )(*[pltpu.with_memory_space_constraint(b, pltpu.HBM) for b in bufs], *extra)


def _ici_end(handle, after, views, name):
    send_sems, recv_sems, *bufs, _ = handle
    n = len(bufs) // 2

    def body(*refs):
        s_refs, l_refs, send_sems_, recv_sems_ = refs[:n], refs[n:2 * n], refs[2 * n], refs[2 * n + 1]
        x, y, c, chips = _place()
        for i in range(n):
            for j, (cx, cy) in enumerate(chips):
                src, _, arrival = views(s_refs[i], l_refs[i], i, j, x, y, c, cx, cy)
                k = (N_CHIPS - 1) * i + j
                cp = _remote(src, arrival, send_sems_.at[k], recv_sems_.at[k], (cx, cy, c))
                cp.wait_send()
                cp.wait_recv()

    out = pl.pallas_call(
        body, name=name, out_shape=[pltpu.HBM(b.shape, b.dtype) for b in bufs], in_specs=[HBM] * (2 * n) + [SEM, SEM, ANY],
        out_specs=[HBM] * (2 * n), input_output_aliases={i: i for i in range(2 * n)}, compiler_params=SPLIT_COPY,
    )(*bufs, send_sems, recv_sems, after)
    return out[:n], out[n:]


def _gather_views(halves):
    def views(src, land, i, j, x, y, c, cx, cy):
        mine = _half_rows(c, halves[i])
        return src.at[mine], land.at[2 * x + y, mine], land.at[2 * cx + cy, mine]
    return views


def _gather_begin(shards, name, after=None):
    lands = [lax.empty((N_CHIPS,) + s.shape, s.dtype) for s in shards]
    return _ici_begin(shards, lands, _gather_views([s.shape[0] // 2 for s in shards]), name, after)


def _gather_end(handle, after, name):
    n = (len(handle) - 3) // 2
    shards, lands = _ici_end(handle, after, _gather_views([s.shape[0] // 2 for s in handle[2:2 + n]]), name + "_wait")
    return _gather_finish(shards, lands, name + "_finish")


def _gather_finish(shards, lands, name):
    n = len(shards)

    def body(*refs):
        ins, l_refs, outs, send_sems, recv_sems = refs[:n], refs[n:2 * n], refs[2 * n:3 * n], refs[3 * n], refs[3 * n + 1]
        x, y, c, chips = _place()
        me, sibling = 2 * x + y, (x, y, 1 - c)
        sent = []
        for i in range(n):
            mine = _half_rows(c, shards[i].shape[0] // 2)
            for j, (cx, cy) in enumerate(chips):
                sent.append(_remote(l_refs[i].at[2 * cx + cy, mine], outs[i].at[2 * cx + cy, mine], send_sems.at[i, j], recv_sems.at[i, j], sibling))
            sent.append(_remote(ins[i], outs[i].at[me], send_sems.at[i, 3], recv_sems.at[i, 3], sibling))
        for cp in sent:
            cp.start()
        for i in range(n):
            other = _half_rows(1 - c, shards[i].shape[0] // 2)
            for j, (cx, cy) in enumerate(chips):
                rows = outs[i].at[2 * cx + cy, other]
                _remote(rows, rows, send_sems.at[i, j], recv_sems.at[i, j], sibling).wait_recv()
            own = outs[i].at[me]
            _remote(own, own, send_sems.at[i, 3], recv_sems.at[i, 3], sibling).wait_recv()
        for cp in sent:
            cp.wait_send()

    return pl.pallas_call(
        body, name=name, in_specs=[ANY] * (2 * n), out_specs=[ANY] * n, out_shape=[S(l_.shape, l_.dtype) for l_ in lands],
        input_output_aliases={n + i: i for i in range(n)},
        scratch_shapes=[pltpu.SemaphoreType.DMA((n, 4)), pltpu.SemaphoreType.DMA((n, 4))],
    )(*shards, *lands)


def _exchange_views(src, land, i, j, x, y, c, cx, cy):
    return src.at[2 * cx + cy], land.at[j], land.at[j]


def _chip_exchange_begin(ps, name):
    return _ici_begin(ps, [lax.empty((N_CHIPS - 1,) + p.shape[1:], p.dtype) for p in ps], _exchange_views, name)


def _chip_exchange_end(handle, after, name):
    return _ici_end(handle, after, _exchange_views, name)


def _quad_sum(p, q, name):
    _, h, cols = p.shape
    tr = _tile(h, 512, 16)
    n_blk = h // tr
    x, y, c, _ = _place()

    def body(s_ref, p_ref, q0_ref, q1_ref, q2_ref, o_ref):
        o_ref[...] = p_ref[0].astype(F32) + q0_ref[0].astype(F32) + q1_ref[0].astype(F32) + q2_ref[0].astype(F32)

    in_specs = [pl.BlockSpec((1, tr, cols), lambda i, s_ref: (s_ref[0], i, 0))]
    in_specs += [pl.BlockSpec((1, tr, cols), functools.partial(lambda i, s_ref, k: (k, i, 0), k=k)) for k in range(N_CHIPS - 1)]
    return pl.pallas_call(
        body, name=name, out_shape=S((2 * h, cols), F32),
        grid_spec=pltpu.PrefetchScalarGridSpec(num_scalar_prefetch=1, grid=(n_blk,), in_specs=in_specs,
                                               out_specs=pl.BlockSpec((tr, cols), lambda i, s_ref: (s_ref[1] * n_blk + i, 0))),
        compiler_params=_params(("parallel",)),
    )(jnp.stack([2 * x + y, c]).astype(jnp.int32), p, q, q, q)


def _pair_share(bufs, name):
    n = len(bufs)

    def body(*refs):
        in_refs, out_refs, send_sems, recv_sems = refs[:n], refs[n:2 * n], refs[2 * n], refs[2 * n + 1]
        x, y, c, _ = _place()
        sent = []
        for i in range(n):
            mine = _half_rows(c, bufs[i].shape[0] // 2)
            sent.append(_remote(in_refs[i].at[mine], out_refs[i].at[mine], send_sems.at[i], recv_sems.at[i], (x, y, 1 - c)))
        for cp in sent:
            cp.start()
        for i in range(n):
            other = out_refs[i].at[_half_rows(1 - c, bufs[i].shape[0] // 2)]
            _remote(other, other, send_sems.at[i], recv_sems.at[i], (x, y, 1 - c)).wait_recv()
        for cp in sent:
            cp.wait_send()

    return pl.pallas_call(body, name=name, in_specs=[ANY] * n, out_specs=[ANY] * n, out_shape=[S(b.shape, b.dtype) for b in bufs],
                          input_output_aliases={i: i for i in range(n)},
                          scratch_shapes=[pltpu.SemaphoreType.DMA((n,)), pltpu.SemaphoreType.DMA((n,))])(*bufs)


def _adamw(w, gs, m, v, name):
    depth, r, cols = w.shape
    tr = _tile(r, 256, 8)

    def body(*refs):
        w_ref, m_ref, v_ref = refs[:3]
        g_refs = refs[3:3 + depth]
        go_ref, d_ref, mo_ref, vo_ref = refs[3 + depth:]
        for l in range(depth):
            @pl.when(pl.program_id(0) == l)
            def _(l=l):
                g_ = g_refs[l][...]
                m_ = ADAM_B1 * m_ref[...] + (1.0 - ADAM_B1) * g_
                v_ = ADAM_B2 * v_ref[...] + (1.0 - ADAM_B2) * jnp.square(g_)
                m_hat = m_ / (1.0 - ADAM_B1 ** ADAM_STEP)
                v_hat = v_ / (1.0 - ADAM_B2 ** ADAM_STEP)
                d_ref[...] = -ADAM_LR * (m_hat / (jnp.sqrt(v_hat) + ADAM_EPS) + ADAM_WD * w_ref[...])
                go_ref[...], mo_ref[...], vo_ref[...] = g_, m_, v_

    blk = pl.BlockSpec((None, tr, cols), lambda l, i: (l, i, 0))
    g_blk = pl.BlockSpec((tr, cols), lambda l, i: (i, 0))
    return pl.pallas_call(body, name=name, grid=(depth, r // tr), in_specs=[blk] * 3 + [g_blk] * depth, out_specs=[blk] * 4,
                          out_shape=[S(w.shape, F32)] * 4, compiler_params=_params(("parallel", "parallel")))(w, m, v, *gs)


WEIGHTS = ("ffn1_norm", "ffn1_w1", "ffn1_w2", "mix_norm", "w_in", "b_in", "lru_conv_w", "lru_conv_b", "lru_w_gate", "lru_b_gate",
           "lru_lambda", "lru_w_out", "q_norm", "w_uq", "kv_norm", "w_ukv", "mla_w_o", "conv_dw_w", "conv_dw_b", "conv_ln_g",
           "conv_ln_b", "conv_w_out", "conv_b_out", "w_out", "ffn2_norm", "ffn2_w1", "ffn2_w2", "final_norm")
ROW_SHARDED = ("ffn1_w2", "w_out", "ffn2_w2")
COL_SHARDED = ("ffn1_w1", "w_in", "lru_w_out", "w_uq", "w_ukv", "mla_w_o", "conv_w_out", "ffn2_w1")
SMALL_SHARDED = ("lru_conv_w", "conv_dw_w")
MXU_SHARDED = tuple(n for n in WEIGHTS if n in ROW_SHARDED + COL_SHARDED)
REPLICATED = tuple(n for n in WEIGHTS if n not in MXU_SHARDED + SMALL_SHARDED)
SMALL = REPLICATED + SMALL_SHARDED
INPUTS = ("x", "positions") + WEIGHTS + ("loss_target",) + tuple("m_" + n for n in WEIGHTS) + tuple("v_" + n for n in WEIGHTS)


def _pack(arrays, dtype, cols, row_unit):
    flat = jnp.concatenate([a.astype(dtype).reshape(-1) for a in arrays])
    unit = cols * row_unit
    return jnp.pad(flat, (0, -flat.shape[0] % unit)).reshape(-1, cols)


def _unpack(flat, shapes):
    out, off = [], 0
    for shp in shapes:
        n = 1
        for s_ in shp:
            n *= s_
        out.append(flat[..., off:off + n].reshape(flat.shape[:-1] + tuple(shp)))
        off += n
    return out


def _reduce_scatter_begin(gs, names, tag):
    halves = _pair_exchange(gs, tag + "_pair_exchange")
    pairs = [_pair_sum(g, h, f"{tag}_pair_sum_{n}") for n, g, h in zip(names, gs, halves, strict=True)]
    return _chip_exchange_begin(pairs, tag + "_chip_exchange")


def _reduce_scatter_end(handle, names, after, tag):
    pairs, others = _chip_exchange_end(handle, after, tag + "_chip_exchange_wait")
    sums = [_quad_sum(p, q, f"{tag}_chip_sum_{n}") for n, p, q in zip(names, pairs, others, strict=True)]
    return dict(zip(names, _pair_share(sums, tag + "_pair_share"), strict=True))


PARTS = (("ffn1", ("ffn1_w1", "ffn1_w2")), ("mix", ("w_in", "lru_w_out", "w_uq", "w_ukv", "mla_w_o", "conv_w_out", "w_out")),
         ("ffn2", ("ffn2_w1", "ffn2_w2")))


def _step(a):
    x, positions, target = a["x"][0], a["positions"][0], a["loss_target"][0]
    depth = a["ffn1_norm"].shape[0]
    me = 2 * lax.axis_index("x") + lax.axis_index("y")
    placed = [lax.dynamic_update_slice_in_dim(jnp.zeros(a[n].shape[:2] + (N_CHIPS,) + a[n].shape[2:], F32), 0.5 * a[n][:, :, None], me, 2)
              for n in SMALL_SHARDED]
    small_whole = _unpack(_allreduce_all(_pack(placed, F32, LANES, 8), "small_weights").reshape(-1), [p_.shape for p_ in placed])
    small_whole = {n: w.reshape(w.shape[:2] + (-1,)) for n, w in zip(SMALL_SHARDED, small_whole, strict=True)}
    base = [{n: a[n][l] for n in REPLICATED if a[n].ndim > 1} | {n: small_whole[n][l] for n in SMALL_SHARDED} for l in range(depth)]
    order = [(l, part, names) for l in range(depth) for part, names in PARTS]
    shards = lambda l, names: [a[n][l].astype(BF16) for n in names]
    rope = _rope_tables(positions, "rope_tables")
    handle = _gather_begin(shards(0, order[0][2]), "l0_ffn1_gather", small_whole[SMALL_SHARDED[0]])
    gathered = _gather_end(handle, handle[-1], "l0_ffn1_gather")
    saved = []
    for k, (l, part, names) in enumerate(order):
        p = base[l] | {n: g.reshape(-1, g.shape[-1]) if n in ROW_SHARDED else g for n, g in zip(names, gathered, strict=True)}
        dep = None
        if k + 1 < len(order):
            l2, part2, names2 = order[k + 1]
            handle = _gather_begin(shards(l2, names2), f"l{l2}_{part2}_gather", handle[-1])
            dep = handle[-1][0, 0]
        x, s = _part_fwd(part, x, p, rope, f"l{l}_{part}", dep)
        saved.append(s)
        if dep is not None:
            gathered = _gather_end(handle, x, f"l{l2}_{part2}_gather")
    loss, dx, dfinal = _loss_head(x, target, a["final_norm"][None, :], "loss_head")
    loss = lax.psum(loss, MESH_AXES)
    grads, shard_grads, pending = [{} for _ in range(depth)], [{} for _ in range(depth)], None
    for l, part, names in reversed(order):
        dx, g = _part_bwd(part, dx, saved.pop(), rope, f"l{l}_{part}", None if pending is None else pending[0][-1][0, 0])
        if pending is not None:
            shard_grads[pending[1]].update(_reduce_scatter_end(pending[0], pending[2], dx, pending[3]))
        pending = (_reduce_scatter_begin([g[n] for n in names], names, f"l{l}_{part}_grad"), l, names, f"l{l}_{part}_grad")
        grads[l].update({n: g[n] for n in g if n in SMALL})
    shard_grads[pending[1]].update(_reduce_scatter_end(pending[0], pending[2], dx, pending[3]))
    small = [jnp.stack([g[n] for g in grads]) if a[n].ndim > 1 else dfinal[0] for n in SMALL]
    g_small = _unpack(_allreduce_all(_pack(small, F32, LANES, 256), "grad_allreduce").reshape(-1), [s_.shape for s_ in small])
    g_small = [lax.dynamic_slice_in_dim(g, me * a[n].shape[-1], a[n].shape[-1], 2) if n in SMALL_SHARDED else g
               for n, g in zip(SMALL, g_small, strict=True)]
    g, delta, new_m, new_v = {}, {}, {}, {}
    for n in MXU_SHARDED:
        g[n], delta[n], new_m[n], new_v[n] = _adamw(a[n], [shard_grads[l][n] for l in range(depth)], a["m_" + n], a["v_" + n], "adamw_" + n)
    shapes = [a[n].shape for n in SMALL]
    packed = [_pack([a[pre + n] for n in SMALL], F32, LANES, 256)[None] for pre in ("", "m_", "v_")]
    res = _adamw(packed[0], [_pack(g_small, F32, LANES, 256)], packed[1], packed[2], "adamw_small")
    for out, r in zip((g, delta, new_m, new_v), res, strict=True):
        out.update(zip(SMALL, _unpack(r.reshape(-1), shapes), strict=True))
    return (loss, dx[None], *[g[n] for n in WEIGHTS], *[delta[n] for n in WEIGHTS], *[new_m[n] for n in WEIGHTS], *[new_v[n] for n in WEIGHTS])


def kernel(x, positions, ffn1_norm, ffn1_w1, ffn1_w2, mix_norm, w_in, b_in, lru_conv_w, lru_conv_b, lru_w_gate, lru_b_gate, lru_lambda, lru_w_out, q_norm, w_uq, kv_norm, w_ukv, mla_w_o, conv_dw_w, conv_dw_b, conv_ln_g, conv_ln_b, conv_w_out, conv_b_out, w_out, ffn2_norm, ffn2_w1, ffn2_w2, final_norm, loss_target, m_ffn1_norm, m_ffn1_w1, m_ffn1_w2, m_mix_norm, m_w_in, m_b_in, m_lru_conv_w, m_lru_conv_b, m_lru_w_gate, m_lru_b_gate, m_lru_lambda, m_lru_w_out, m_q_norm, m_w_uq, m_kv_norm, m_w_ukv, m_mla_w_o, m_conv_dw_w, m_conv_dw_b, m_conv_ln_g, m_conv_ln_b, m_conv_w_out, m_conv_b_out, m_w_out, m_ffn2_norm, m_ffn2_w1, m_ffn2_w2, m_final_norm, v_ffn1_norm, v_ffn1_w1, v_ffn1_w2, v_mix_norm, v_w_in, v_b_in, v_lru_conv_w, v_lru_conv_b, v_lru_w_gate, v_lru_b_gate, v_lru_lambda, v_lru_w_out, v_q_norm, v_w_uq, v_kv_norm, v_w_ukv, v_mla_w_o, v_conv_dw_w, v_conv_dw_b, v_conv_ln_g, v_conv_ln_b, v_conv_w_out, v_conv_b_out, v_w_out, v_ffn2_norm, v_ffn2_w1, v_ffn2_w2, v_final_norm):
    return _step(dict(zip(INPUTS, (x, positions, ffn1_norm, ffn1_w1, ffn1_w2, mix_norm, w_in, b_in, lru_conv_w, lru_conv_b, lru_w_gate, lru_b_gate, lru_lambda, lru_w_out, q_norm, w_uq, kv_norm, w_ukv, mla_w_o, conv_dw_w, conv_dw_b, conv_ln_g, conv_ln_b, conv_w_out, conv_b_out, w_out, ffn2_norm, ffn2_w1, ffn2_w2, final_norm, loss_target, m_ffn1_norm, m_ffn1_w1, m_ffn1_w2, m_mix_norm, m_w_in, m_b_in, m_lru_conv_w, m_lru_conv_b, m_lru_w_gate, m_lru_b_gate, m_lru_lambda, m_lru_w_out, m_q_norm, m_w_uq, m_kv_norm, m_w_ukv, m_mla_w_o, m_conv_dw_w, m_conv_dw_b, m_conv_ln_g, m_conv_ln_b, m_conv_w_out, m_conv_b_out, m_w_out, m_ffn2_norm, m_ffn2_w1, m_ffn2_w2, m_final_norm, v_ffn1_norm, v_ffn1_w1, v_ffn1_w2, v_mix_norm, v_w_in, v_b_in, v_lru_conv_w, v_lru_conv_b, v_lru_w_gate, v_lru_b_gate, v_lru_lambda, v_lru_w_out, v_q_norm, v_w_uq, v_kv_norm, v_w_ukv, v_mla_w_o, v_conv_dw_w, v_conv_dw_b, v_conv_ln_g, v_conv_ln_b, v_conv_w_out, v_conv_b_out, v_w_out, v_ffn2_norm, v_ffn2_w1, v_ffn2_w2, v_final_norm), strict=True)))
```

```python
import functools

import jax
import jax.numpy as jnp
from jax import lax
from jax.experimental import pallas as pl
from jax.experimental.pallas import tpu as pltpu

F32, BF16 = jnp.float32, jnp.bfloat16
S = jax.ShapeDtypeStruct

LANES = 128
VMEM_LIMIT = 56 * 2**20
NORM_EPS = 1e-6
LRU_C = 8.0
MLA_HEADS = 8
QK_NOPE, QK_ROPE, V_HEAD = 64, 32, 64
HEAD_PAD = 128
ROPE_THETA = 10000.0
ADAM_LR, ADAM_B1, ADAM_B2, ADAM_EPS, ADAM_WD, ADAM_STEP = 0.001, 0.9, 0.999, 1e-08, 0.01, 10
MESH_AXES = ("x", "y", "c")
N_CHIPS = 4
GRAD_DTYPE = BF16
NT =(((1,), (1,)), ((), ()))
TN = (((0,), (0,)), ((), ()))
NN = (((1,), (0,)), ((), ()))


def _tile(n, cap, unit=LANES):
    best = None
    for d in range(unit, min(n, cap) + 1, unit):
        if n % d == 0:
            best = d
    return best if best is not None else n


def _params(sem):
    return pltpu.CompilerParams(dimension_semantics=sem, vmem_limit_bytes=VMEM_LIMIT)


def _mm(a, b, mode="nn", out_dtype=F32, bias=None, res=None, alpha=1.0, a_blocks=False, b_chips=False, out_chips=False, name="mm"):
    units_n, units_k = [], []
    if b_chips:
        blocks, rows, c = b.shape
        b_shape = (rows, blocks * c)
        (units_k if mode == "nt" else units_n).append(c)
    else:
        b_shape = b.shape
    if a_blocks:
        blocks_a, rows_a, c_a = a.shape
        a_shape = (rows_a, blocks_a * c_a)
        units_k.append(c_a)
    else:
        a_shape = a.shape
    if mode == "nn":
        (m, k), (k2, n), dims = a_shape, b_shape, NN
    elif mode == "nt":
        (m, k), (n, k2), dims = a_shape, b_shape, NT
    else:
        (k, m), (k2, n), dims = a_shape, b_shape, TN
    assert k == k2 and not (a_blocks and mode == "tn"), (name, a.shape, b.shape, mode)
    if out_chips:
        units_n.append(n // N_CHIPS)
    tm = _tile(m, 512)
    tn = _tile(min(units_n) if units_n else n, 1536)
    tk = k if (k <= 3072 and not units_k) else _tile(min(units_k) if units_k else k, 3072)
    nk = k // tk
    assert all(u % tn == 0 for u in units_n) and all(u % tk == 0 for u in units_k), (name, units_n, units_k, tn, tk)
    if a_blocks:
        a_spec = pl.BlockSpec((None, tm, tk), functools.partial(lambda i, j, kk, per: (kk // per, i, kk % per), per=c_a // tk))
    elif mode == "tn":
        a_spec = pl.BlockSpec((tk, tm), lambda i, j, kk: (kk, i))
    else:
        a_spec = pl.BlockSpec((tm, tk), lambda i, j, kk: (i, kk))
    if b_chips and mode != "nt":
        b_spec = pl.BlockSpec((None, tk, tn), functools.partial(lambda i, j, kk, per: (j // per, kk, j % per), per=c // tn))
    elif b_chips:
        b_spec = pl.BlockSpec((None, tn, tk), functools.partial(lambda i, j, kk, per: (kk // per, j, kk % per), per=c // tk))
    elif mode == "nt":
        b_spec = pl.BlockSpec((tn, tk), lambda i, j, kk: (j, kk))
    else:
        b_spec = pl.BlockSpec((tk, tn), lambda i, j, kk: (kk, j))
    if out_chips:
        out_spec = pl.BlockSpec((None, tm, tn), functools.partial(lambda i, j, kk, per: (j // per, i, j % per), per=n // N_CHIPS // tn))
        out_shape = S((N_CHIPS, m, n // N_CHIPS), out_dtype)
    else:
        out_spec = pl.BlockSpec((tm, tn), lambda i, j, kk: (i, j))
        out_shape = S((m, n), out_dtype)
    operands, in_specs = [a, b], [a_spec, b_spec]
    if bias is not None:
        operands.append(bias)
        in_specs.append(pl.BlockSpec((1, tn), lambda i, j, kk: (0, j)))
    if res is not None:
        operands.append(res)
        in_specs.append(pl.BlockSpec((tm, tn), lambda i, j, kk: (i, j)))

    def body(*refs):
        a_ref, b_ref = refs[0], refs[1]
        pos = 2
        bias_ref = res_ref = None
        if bias is not None:
            bias_ref, pos = refs[pos], pos + 1
        if res is not None:
            res_ref, pos = refs[pos], pos + 1
        o_ref = refs[pos]
        part = lax.dot_general(a_ref[...].astype(BF16), b_ref[...].astype(BF16), dims, preferred_element_type=F32)

        def finish(acc):
            out = acc if alpha == 1.0 else acc * alpha
            if bias_ref is not None:
                out = out + bias_ref[...]
            if res_ref is not None:
                out = out + res_ref[...]
            o_ref[...] = out.astype(o_ref.dtype)

        if nk == 1:
            finish(part)
        else:
            acc_ref = refs[pos + 1]
            kk = pl.program_id(2)

            @pl.when(kk == 0)
            def _():
                acc_ref[...] = part

            @pl.when(kk > 0)
            def _():
                acc_ref[...] += part

            @pl.when(kk == nk - 1)
            def _():
                finish(acc_ref[...])

    return pl.pallas_call(
        body, name=name, grid=(m // tm, n // tn, nk), in_specs=in_specs, out_specs=out_spec, out_shape=out_shape,
        scratch_shapes=[pltpu.VMEM((tm, tn), F32)] if nk > 1 else [],
        compiler_params=_params(("parallel", "parallel", "arbitrary")),
    )(*operands)


def _rowwise(fn, rows, params=(), outs=(), accs=(), tt=256, name="rowwise"):
    rows = [r if isinstance(r, tuple) else (r, r.shape[1], 0) for r in rows]
    t = rows[0][0].shape[0]
    tt = min(tt, t)
    n_rows, n_par, n_out = len(rows), len(params), len(outs)
    in_specs = [pl.BlockSpec((tt, w), functools.partial(lambda i, cb: (i, cb), cb=cb)) for (_, w, cb) in rows]
    in_specs += [pl.BlockSpec(p.shape, functools.partial(lambda i, nd: (0,) * nd, nd=p.ndim)) for p in params]
    out_shape = [S((t, w), dt) for (w, dt) in outs] + [S(shape, dt) for (shape, dt) in accs]
    out_specs = [pl.BlockSpec((tt, w), lambda i: (i, 0)) for (w, _) in outs]
    out_specs += [pl.BlockSpec(shape, functools.partial(lambda i, nd: (0,) * nd, nd=len(shape))) for (shape, _) in accs]

    def body(*refs):
        vals = [r[...] for r in refs[:n_rows + n_par]]
        o_vals, a_vals = fn(*vals)
        o_refs = refs[n_rows + n_par:n_rows + n_par + n_out]
        a_refs = refs[n_rows + n_par + n_out:]
        for ref, val in zip(o_refs, o_vals, strict=True):
            ref[...] = val.astype(ref.dtype)
        i = pl.program_id(0)
        for ref, val in zip(a_refs, a_vals, strict=True):
            @pl.when(i == 0)
            def _(ref=ref, val=val):
                ref[...] = val.astype(ref.dtype)

            @pl.when(i > 0)
            def _(ref=ref, val=val):
                ref[...] += val.astype(ref.dtype)

    res = pl.pallas_call(
        body, name=name, grid=(t // tt,), in_specs=in_specs, out_specs=out_specs, out_shape=out_shape,
        compiler_params=_params(("arbitrary",) if accs else ("parallel",)),
    )(*[r[0] for r in rows], *params)
    return res


def _rms(x, g):
    x = x.astype(F32)
    return x * lax.rsqrt(jnp.mean(x * x, axis=-1, keepdims=True) + NORM_EPS) * g


def _layer_norm_silu(x, g, b):
    mu = jnp.mean(x, axis=-1, keepdims=True)
    var = jnp.mean(jnp.square(x - mu), axis=-1, keepdims=True)
    return jax.nn.silu((x - mu) * lax.rsqrt(var + NORM_EPS) * g + b)


def _neg_expm1(z):
    series = -z * (1.0 + z * (0.5 + z * (1.0 / 6.0 + z * (1.0 / 24.0 + z * (1.0 / 120.0)))))
    return jnp.where(z > -0.05, series, 1.0 - jnp.exp(z))


def _shift_down(x, s, fill=0.0):
    if s == 0:
        return x
    row = lax.broadcasted_iota(jnp.int32, x.shape, 0)
    return jnp.where(row >= s, pltpu.roll(x, s, 0), fill)


def _shift_up(x, s, fill=0.0):
    if s == 0:
        return x
    t = x.shape[0]
    row = lax.broadcasted_iota(jnp.int32, x.shape, 0)
    return jnp.where(row < t - s, pltpu.roll(x, t - s, 0), fill)


def _scan(a, u, shift):
    t, d = a.shape[0], 1
    while d < t:
        u = u + a * shift(u, d, 0.0)
        if 2 * d < t:
            a = a * shift(a, d, 1.0)
        d *= 2
    return u


def _lru_gates(xa, wr, wi, br, bi, lam):
    xb = xa.astype(BF16)
    r = jax.nn.sigmoid(jnp.dot(xb, wr.astype(BF16), preferred_element_type=F32) + br)
    i = jax.nn.sigmoid(jnp.dot(xb, wi.astype(BF16), preferred_element_type=F32) + bi)
    log_a = -LRU_C * r * jax.nn.softplus(-lam)
    return jnp.exp(log_a), jnp.sqrt(_neg_expm1(2.0 * log_a)) * (i * xa)


def _conv_fwd(x, w_ref, b, width):
    y = b + w_ref[pl.ds(width - 1, 1), :] * x
    for j in range(width - 1):
        y = y + w_ref[pl.ds(j, 1), :] * _shift_down(x, width - 1 - j)
    return y


def _conv_bwd(x, dy, w_ref, dw_ref, width):
    dx = w_ref[pl.ds(width - 1, 1), :] * dy
    dw_ref[pl.ds(width - 1, 1), :] = jnp.sum(dy * x, axis=0, keepdims=True)
    for j in range(width - 1):
        s = width - 1 - j
        dx = dx + w_ref[pl.ds(j, 1), :] * _shift_up(dy, s)
        dw_ref[pl.ds(j, 1), :] = jnp.sum(dy * _shift_down(x, s), axis=0, keepdims=True)
    return dx


def _rope(z, c, s1, s2):
    return z * c + pltpu.roll(z, HEAD_PAD - QK_ROPE // 2, 1) * s1 + pltpu.roll(z, QK_ROPE // 2, 1) * s2


def _rope_t(d, c, s1, s2):
    return d * c + pltpu.roll(d * s1, QK_ROPE // 2, 1) + pltpu.roll(d * s2, HEAD_PAD - QK_ROPE // 2, 1)


def _heads(z):
    return [z[:, h * HEAD_PAD:(h + 1) * HEAD_PAD] for h in range(z.shape[1] // HEAD_PAD)]


def _chan_spec(t, c_off=0):
    return pl.BlockSpec((t, LANES), lambda c: (0, c_off + c))


def _lru_specs(t, n_tiles, width):
    vec = pl.BlockSpec((1, LANES), lambda c: (0, c))
    mat = pl.BlockSpec((1, LANES, LANES), lambda c: (c, 0, 0))
    return [_chan_spec(t), _chan_spec(t, n_tiles), pl.BlockSpec((width, LANES), lambda c: (0, c)), vec, mat, mat, vec, vec, vec]


def _lru_fwd(pa, cw, cb, wr, wi, br, bi, lam, name):
    t, w = pa.shape[0], pa.shape[1] // 2
    n_tiles, width = w // LANES, cw.shape[0]

    def body(x_ref, g_ref, cw_ref, cb_ref, wr_ref, wi_ref, br_ref, bi_ref, lam_ref, y_ref):
        xa = _conv_fwd(x_ref[...], cw_ref, cb_ref[...], width)
        a, u = _lru_gates(xa, wr_ref[0], wi_ref[0], br_ref[...], bi_ref[...], lam_ref[...])
        h = _scan(a, u, _shift_down)
        y_ref[...] = (h * jax.nn.gelu(g_ref[...])).astype(y_ref.dtype)

    return pl.pallas_call(
        body, name=name, grid=(n_tiles,), in_specs=_lru_specs(t, n_tiles, width), out_specs=_chan_spec(t),
        out_shape=S((t, w), BF16), compiler_params=_params(("parallel",)),
    )(pa, pa, cw, cb, wr, wi, br, bi, lam)


def _lru_bwd(pa, dy, cw, cb, wr, wi, br, bi, lam, name):
    t, w = pa.shape[0], pa.shape[1] // 2
    n_tiles, width = w // LANES, cw.shape[0]

    def body(x_ref, g_ref, cw_ref, cb_ref, wr_ref, wi_ref, br_ref, bi_ref, lam_ref, dy_ref,
             dx_ref, dg_ref, dcw_ref, dcb_ref, dwr_ref, dwi_ref, dbr_ref, dbi_ref, dlam_ref, sx_ref, sg_ref):
        x = x_ref[...]
        xa = _conv_fwd(x, cw_ref, cb_ref[...], width)
        (a, u), gates_vjp = jax.vjp(_lru_gates, xa, wr_ref[0], wi_ref[0], br_ref[...], bi_ref[...], lam_ref[...])
        h = _scan(a, u, _shift_down)
        _, out_vjp = jax.vjp(lambda h_, g_: h_ * jax.nn.gelu(g_), h, g_ref[...])
        dh, dgate = out_vjp(dy_ref[...])
        adj = _scan(_shift_up(a, 1), dh, _shift_up)
        dxa, dwr, dwi, dbr, dbi, dlam = gates_vjp((adj * _shift_down(h, 1), adj))
        dx = _conv_bwd(x, dxa, cw_ref, dcw_ref, width)
        dcb_ref[...] = jnp.sum(dxa, axis=0, keepdims=True)
        dx_ref[...] = dx.astype(dx_ref.dtype)
        dg_ref[...] = dgate.astype(dg_ref.dtype)
        sx_ref[...] = jnp.sum(dx, axis=0, keepdims=True)
        sg_ref[...] = jnp.sum(dgate, axis=0, keepdims=True)
        dwr_ref[0], dwi_ref[0] = dwr, dwi
        dbr_ref[...], dbi_ref[...], dlam_ref[...] = dbr, dbi, dlam

    vec = pl.BlockSpec((1, LANES), lambda c: (0, c))
    mat = pl.BlockSpec((1, LANES, LANES), lambda c: (c, 0, 0))
    vec_s, mat_s = S((1, w), F32), S((n_tiles, LANES, LANES), F32)
    return pl.pallas_call(
        body, name=name, grid=(n_tiles,), in_specs=_lru_specs(t, n_tiles, width) + [_chan_spec(t)],
        out_specs=[_chan_spec(t), _chan_spec(t), pl.BlockSpec((width, LANES), lambda c: (0, c)), vec, mat, mat, vec, vec, vec, vec, vec],
        out_shape=[S((t, w), BF16), S((t, w), BF16), S((width, w), F32), vec_s, mat_s, mat_s, vec_s, vec_s, vec_s, vec_s, vec_s],
        compiler_params=_params(("parallel",)),
    )(pa, pa, cw, cb, wr, wi, br, bi, lam, dy)


def _glu_conv_fwd(pc, cw, cb, name):
    t, c = pc.shape[0], pc.shape[1] // 2
    n_tiles, width = c // LANES, cw.shape[0]

    def body(v_ref, g_ref, cw_ref, cb_ref, y_ref):
        y_ref[...] = _conv_fwd(v_ref[...] * jax.nn.sigmoid(g_ref[...]), cw_ref, cb_ref[...], width)

    return pl.pallas_call(
        body, name=name, grid=(n_tiles,),
        in_specs=[_chan_spec(t), _chan_spec(t, n_tiles), pl.BlockSpec((width, LANES), lambda i: (0, i)), pl.BlockSpec((1, LANES), lambda i: (0, i))],
        out_specs=_chan_spec(t), out_shape=S((t, c), F32), compiler_params=_params(("parallel",)),
    )(pc, pc, cw, cb)


def _glu_conv_bwd(pc, dy, cw, name):
    t, c = pc.shape[0], pc.shape[1] // 2
    n_tiles, width = c // LANES, cw.shape[0]

    def body(v_ref, g_ref, cw_ref, dy_ref, dv_ref, dg_ref, dcw_ref, dcb_ref, sv_ref, sg_ref):
        glu = lambda v_, g_: v_ * jax.nn.sigmoid(g_)
        x, glu_vjp = jax.vjp(glu, v_ref[...], g_ref[...])
        dy_ = dy_ref[...]
        dv, dg = glu_vjp(_conv_bwd(x, dy_, cw_ref, dcw_ref, width))
        dcb_ref[...] = jnp.sum(dy_, axis=0, keepdims=True)
        dv_ref[...] = dv.astype(dv_ref.dtype)
        dg_ref[...] = dg.astype(dg_ref.dtype)
        sv_ref[...] = jnp.sum(dv, axis=0, keepdims=True)
        sg_ref[...] = jnp.sum(dg, axis=0, keepdims=True)

    vec = pl.BlockSpec((1, LANES), lambda i: (0, i))
    wspec = pl.BlockSpec((width, LANES), lambda i: (0, i))
    return pl.pallas_call(
        body, name=name, grid=(n_tiles,), in_specs=[_chan_spec(t), _chan_spec(t, n_tiles), wspec, _chan_spec(t)],
        out_specs=[_chan_spec(t), _chan_spec(t), wspec, vec, vec, vec],
        out_shape=[S((t, c), BF16), S((t, c), BF16), S((width, c), F32), S((1, c), F32), S((1, c), F32), S((1, c), F32)],
        compiler_params=_params(("parallel",)),
    )(pc, pc, cw, dy)


def _softmax_rows(q, k, causal, scale):
    s = lax.dot_general(q, k, NT, preferred_element_type=F32) * scale
    s = jnp.where(causal, s, jnp.finfo(F32).min)
    p = jnp.exp(s - jnp.max(s, axis=-1, keepdims=True))
    return p / jnp.sum(p, axis=-1, keepdims=True)


def _attn_specs(t, tq):
    return [pl.BlockSpec((tq, 2 * HEAD_PAD), lambda hp, i: (i, hp)), pl.BlockSpec((t, 2 * HEAD_PAD), lambda hp, i: (0, hp)),
            pl.BlockSpec((t, 2 * V_HEAD), lambda hp, i: (0, hp))]


def _causal(n, tq):
    row = lax.broadcasted_iota(jnp.int32, (tq, (n + 1) * tq), 0) + n * tq
    col = lax.broadcasted_iota(jnp.int32, (tq, (n + 1) * tq), 1)
    return col <= row


def _per_query_block(n_blocks, fn):
    for n in range(n_blocks):
        pl.when(pl.program_id(1) == n)(functools.partial(fn, n))


def _attn_fwd(qh, kh, v, name):
    t = qh.shape[0]
    tq = min(256, t)
    scale = (QK_NOPE + QK_ROPE) ** -0.5

    def body(q_ref, k_ref, v_ref, o_ref):
        def block(n):
            keys = (n + 1) * tq
            causal = _causal(n, tq)
            lane = lax.broadcasted_iota(jnp.int32, (keys, 2 * V_HEAD), 1)
            vv = v_ref[0:keys, :]
            acc = jnp.zeros((tq, 2 * V_HEAD), F32)
            for e in range(2):
                p = _softmax_rows(q_ref[:, e * HEAD_PAD:(e + 1) * HEAD_PAD], k_ref[0:keys, e * HEAD_PAD:(e + 1) * HEAD_PAD], causal, scale)
                ve = jnp.where((lane >= V_HEAD * e) & (lane < V_HEAD * (e + 1)), vv, jnp.zeros_like(vv))
                acc = acc + jnp.dot(p.astype(BF16), ve, preferred_element_type=F32)
            o_ref[...] = acc.astype(o_ref.dtype)

        _per_query_block(t // tq, block)

    return pl.pallas_call(
        body, name=name, grid=(MLA_HEADS // 2, t // tq), in_specs=_attn_specs(t, tq),
        out_specs=pl.BlockSpec((tq, 2 * V_HEAD), lambda hp, i: (i, hp)), out_shape=S((t, MLA_HEADS * V_HEAD), BF16),
        compiler_params=_params(("parallel", "parallel")),
    )(qh, kh, v)


def _attn_bwd(qh, kh, v, do, name):
    t = qh.shape[0]
    tq = min(256, t)
    scale = (QK_NOPE + QK_ROPE) ** -0.5

    def body(q_ref, k_ref, v_ref, do_ref, dq_ref, dk_ref, dv_ref):
        def block(n):
            keys = (n + 1) * tq
            causal = _causal(n, tq)
            lane = lax.broadcasted_iota(jnp.int32, (tq, 2 * V_HEAD), 1)
            vv, dd = v_ref[0:keys, :], do_ref[...]
            dqs, dks = [], []
            dv = jnp.zeros((keys, 2 * V_HEAD), F32)
            for e in range(2):
                q, k = q_ref[:, e * HEAD_PAD:(e + 1) * HEAD_PAD], k_ref[0:keys, e * HEAD_PAD:(e + 1) * HEAD_PAD]
                p = _softmax_rows(q, k, causal, scale)
                de = jnp.where((lane >= V_HEAD * e) & (lane < V_HEAD * (e + 1)), dd, jnp.zeros_like(dd))
                dp = lax.dot_general(de, vv, NT, preferred_element_type=F32)
                ds = (p * (dp - jnp.sum(p * dp, axis=-1, keepdims=True)) * scale).astype(BF16)
                dqs.append(jnp.dot(ds, k, preferred_element_type=F32))
                dks.append(lax.dot_general(ds, q, TN, preferred_element_type=F32))
                dv = dv + lax.dot_general(p.astype(BF16), de, TN, preferred_element_type=F32)
            dq_ref[...] = jnp.concatenate(dqs, axis=-1)
            dk = jnp.concatenate(dks, axis=-1)
            if n == 0:
                dk_ref[0:keys, :], dv_ref[0:keys, :] = dk, dv
                if keys < t:
                    dk_ref[keys:t, :] = jnp.zeros((t - keys, 2 * HEAD_PAD), F32)
                    dv_ref[keys:t, :] = jnp.zeros((t - keys, 2 * V_HEAD), F32)
            else:
                dk_ref[0:keys, :] += dk
                dv_ref[0:keys, :] += dv

        _per_query_block(t // tq, block)

    return pl.pallas_call(
        body, name=name, grid=(MLA_HEADS // 2, t // tq),
        in_specs=_attn_specs(t, tq) + [pl.BlockSpec((tq, 2 * V_HEAD), lambda hp, i: (i, hp))],
        out_specs=[pl.BlockSpec((tq, 2 * HEAD_PAD), lambda hp, i: (i, hp)), pl.BlockSpec((t, 2 * HEAD_PAD), lambda hp, i: (0, hp)),
                   pl.BlockSpec((t, 2 * V_HEAD), lambda hp, i: (0, hp))],
        out_shape=[S((t, MLA_HEADS * HEAD_PAD), F32), S((t, MLA_HEADS * HEAD_PAD), F32), S((t, MLA_HEADS * V_HEAD), F32)],
        compiler_params=_params(("parallel", "arbitrary")),
    )(qh, kh, v, do)


def _swiglu(gate, up):
    return jax.nn.silu(gate) * up


def _ffn_up(h, w1, name):
    t, d = h.shape
    chips, _, c = w1.shape
    f = chips * c // 2
    tm, tn = _tile(t, 512), _tile(c, 1536)
    per = c // tn

    def body(h_ref, wg_ref, wu_ref, gu_ref, a_ref):
        hb = h_ref[...]
        gate = jnp.dot(hb, wg_ref[...], preferred_element_type=F32).astype(BF16)
        up = jnp.dot(hb, wu_ref[...], preferred_element_type=F32).astype(BF16)
        gu_ref[0], gu_ref[1] = gate, up
        a_ref[...] = _swiglu(gate.astype(F32), up.astype(F32)).astype(a_ref.dtype)

    w_spec = lambda first: pl.BlockSpec((None, d, tn), lambda i, j: (first + j // per, 0, j % per))
    return pl.pallas_call(
        body, name=name, grid=(t // tm, f // tn), in_specs=[pl.BlockSpec((tm, d), lambda i, j: (i, 0)), w_spec(0), w_spec(chips // 2)],
        out_specs=[pl.BlockSpec((2, tm, tn), lambda i, j: (0, i, j)), pl.BlockSpec((tm, tn), lambda i, j: (i, j))],
        out_shape=[S((2, t, f), BF16), S((t, f), BF16)], compiler_params=_params(("parallel", "parallel")),
    )(h, w1, w1)


def _ffn_dgu(dy, w2, gu, bias, name):
    t, d = dy.shape
    f = w2.shape[0]
    tm, tn = _tile(t, 512), _tile(f, 1536)

    def body(*refs):
        dy_ref, w2_ref, gu_ref, o_ref = refs[0], refs[1], refs[2], refs[-1]
        da = 0.5 * lax.dot_general(dy_ref[...].astype(BF16), w2_ref[...], NT, preferred_element_type=F32)
        if bias is not None:
            da = da + refs[3][...]
        _, vjp = jax.vjp(_swiglu, gu_ref[0].astype(F32), gu_ref[1].astype(F32))
        dgate, dup = vjp(da)
        o_ref[0], o_ref[1] = dgate.astype(o_ref.dtype), dup.astype(o_ref.dtype)

    blk = pl.BlockSpec((2, tm, tn), lambda i, j: (0, i, j))
    in_specs = [pl.BlockSpec((tm, d), lambda i, j: (i, 0)), pl.BlockSpec((tn, d), lambda i, j: (j, 0)), blk]
    if bias is not None:
        in_specs.append(pl.BlockSpec((1, tn), lambda i, j: (0, j)))
    return pl.pallas_call(body, name=name, grid=(t // tm, f // tn), in_specs=in_specs, out_specs=blk, out_shape=S((2, t, f), BF16),
                          compiler_params=_params(("parallel", "parallel")))(dy, w2, gu, *([] if bias is None else [bias]))


def _ffn_fwd(x, g, w1, w2, tag):
    h, = _rowwise(lambda x_, g_: ([_rms(x_, g_)], []), [x], [g], outs=[(x.shape[1], BF16)], name=tag + "_rms")
    gu, a = _ffn_up(h, w1, tag + "_up")
    return _mm(a, w2, res=x, alpha=0.5, name=tag + "_down"), (x, h, gu, a)


def _after(dep, n):
    return None if dep is None else jnp.zeros((1, n), F32) + dep


def _ffn_bwd(dy, saved, g, w1, w2, tag, dep=None):
    x, h, gu, a = saved
    d, f = x.shape[1], w2.shape[0]
    dgu = _ffn_dgu(dy, w2, gu, _after(dep, f), tag + "_dgu")
    dw2 = _mm(a, dy, "tn", GRAD_DTYPE, alpha=0.5, name=tag + "_dw2")
    dw1 = _mm(h, dgu, "tn", GRAD_DTYPE, b_chips=True, out_chips=True, name=tag + "_dw1")
    dh = _mm(dgu, w1, "nt", a_blocks=True, b_chips=True, name=tag + "_dh")
    dx, dg = _rms_bwd(x, dh, dy, g, tag + "_drms")
    return dx, (dg, dw1, dw2)


def _rms_bwd(x, dh, dres, g, name):
    def fn(x_, dh_, dres_, g_):
        _, vjp = jax.vjp(_rms, x_, g_)
        dx, dg = vjp(dh_)
        return [dx + dres_], [dg]

    return _rowwise(fn, [x, dh, dres], [g], outs=[(x.shape[1], F32)], accs=[((1, x.shape[1]), F32)], name=name)


def _rope_tables(positions, name):
    half = QK_ROPE // 2
    inv = ROPE_THETA ** (-jnp.arange(0, QK_ROPE, 2, dtype=F32) / QK_ROPE)
    inv_lanes = jnp.zeros((1, HEAD_PAD), F32).at[0, QK_NOPE:QK_NOPE + QK_ROPE].set(jnp.tile(inv, 2))

    def fn(pos, inv_):
        ang = pos.astype(F32) * inv_
        lane = lax.broadcasted_iota(jnp.int32, ang.shape, 1)
        cos, sin = jnp.cos(ang), jnp.sin(ang)
        c = jnp.where(lane < QK_NOPE, 1.0, jnp.where(lane < QK_NOPE + QK_ROPE, cos, 0.0))
        s1 = jnp.where((lane >= QK_NOPE) & (lane < QK_NOPE + half), -sin, 0.0)
        s2 = jnp.where((lane >= QK_NOPE + half) & (lane < QK_NOPE + QK_ROPE), sin, 0.0)
        return [c, s1, s2], []

    return _rowwise(fn, [positions.reshape(-1, 1)], [inv_lanes], outs=[(HEAD_PAD, F32)] * 3, name=name)


def _chip_cols(g, lo, hi):
    c = g.shape[-1]
    parts = [g[j, :, max(lo, j * c) - j * c:min(hi, (j + 1) * c) - j * c] for j in range(g.shape[0]) if max(lo, j * c) < min(hi, (j + 1) * c)]
    return parts[0] if len(parts) == 1 else jnp.concatenate(parts, axis=-1)


def _cols_by_chip(segments, c):
    out, start = [[] for _ in range(N_CHIPS)], 0
    for arr, first, width in segments:
        for j in range(N_CHIPS):
            lo, hi = max(start, j * c), min(start + width, (j + 1) * c)
            if lo < hi:
                out[j].append(arr[:, first + lo - start:first + hi - start])
        start += width
    assert start == N_CHIPS * c
    return jnp.stack([jnp.concatenate(parts, axis=-1) for parts in out])


def _whole_cols(g):
    return g.transpose(1, 0, 2).reshape(g.shape[1], -1)


def _mix_offsets(lw):
    wl, ql, kvl = lw["lru_lambda"].shape[-1], lw["q_norm"].shape[-1], lw["kv_norm"].shape[-1]
    o1 = 2 * wl
    o2 = o1 + ql + kvl + QK_ROPE
    return wl, ql, kvl, o1, o2, o2 + 2 * lw["conv_ln_g"].shape[-1]


def _mix_weights(lw):
    wl, ql, kvl, o1, o2, o3 = _mix_offsets(lw)
    w_in, b_in = lw["w_in"], lw["b_in"][None, :]
    d_in = b_in.shape[1]
    z = lambda m, n: jnp.zeros(m.shape[:-1] + (n,), m.dtype)
    w_b = jnp.concatenate([_chip_cols(w_in, o1 + ql, o1 + ql + kvl), z(w_in[0], QK_NOPE), _chip_cols(w_in, o1 + ql + kvl, o2),
                           z(w_in[0], HEAD_PAD - QK_NOPE - QK_ROPE), _chip_cols(w_in, o1, o1 + ql)], axis=-1)
    b_b = jnp.concatenate([b_in[:, o1 + ql:o1 + ql + kvl], z(b_in, QK_NOPE), b_in[:, o1 + ql + kvl:o2],
                           z(b_in, HEAD_PAD - QK_NOPE - QK_ROPE), b_in[:, o1:o1 + ql]], axis=-1)
    hd = lw["lru_w_gate"].shape[-2]
    per = LANES // hd
    eye = jnp.eye(per, dtype=F32)
    wg = lw["lru_w_gate"].reshape(-1, per, hd, 2 * hd)
    block_diag = lambda m: jnp.einsum("cedk,ef->cedfk", m, eye).reshape(-1, LANES, LANES)
    bg = lw["lru_b_gate"]
    w_uq = _whole_cols(lw["w_uq"]).reshape(ql, MLA_HEADS, QK_NOPE + QK_ROPE)
    w_ukv = _whole_cols(lw["w_ukv"]).reshape(kvl, MLA_HEADS, QK_NOPE + V_HEAD)
    pad = lambda m, n: jnp.pad(m, ((0, 0), (0, 0), (0, n)))
    return dict(
        w_a=_chip_cols(w_in, 0, o1), w_b=w_b, w_c=_chip_cols(w_in, o2, o3), w_g=_chip_cols(w_in, o3, d_in),
        b_a=b_in[:, :o1], b_b=b_b, b_c=b_in[:, o2:o3], b_g=b_in[:, o3:],
        wr=block_diag(wg[..., :hd]), wi=block_diag(wg[..., hd:]),
        br=bg[:, :hd].reshape(1, -1), bi=bg[:, hd:].reshape(1, -1),
        w_uq=pad(w_uq, HEAD_PAD - QK_NOPE - QK_ROPE).reshape(ql, -1),
        w_k=pad(w_ukv[..., :QK_NOPE], HEAD_PAD - QK_NOPE).reshape(kvl, -1),
        w_v=w_ukv[..., QK_NOPE:].reshape(kvl, -1),
    )


def _mix_fwd(x, p, mw, rope, tag):
    d = x.shape[1]
    wl, ql, kvl = p["lru_lambda"].shape[-1], p["q_norm"].shape[-1], p["kv_norm"].shape[-1]
    row = lambda name: p[name][None, :]
    h, = _rowwise(lambda x_, g_: ([_rms(x_, g_)], []), [x], [row("mix_norm")], outs=[(d, BF16)], name=tag + "_rms")
    pa = _mm(h, mw["w_a"], bias=mw["b_a"], name=tag + "_pa")
    pb = _mm(h, mw["w_b"], bias=mw["b_b"], name=tag + "_pb")
    pc = _mm(h, mw["w_c"], bias=mw["b_c"], name=tag + "_pc")
    pg = _mm(h, mw["w_g"], bias=mw["b_g"], name=tag + "_pg")
    lru_args = (p["lru_conv_w"], row("lru_conv_b"), mw["wr"], mw["wi"], mw["br"], mw["bi"], row("lru_lambda"))
    ya_pre = _lru_fwd(pa, *lru_args, name=tag + "_lru")
    y_a = _mm(ya_pre, p["lru_w_out"], b_chips=True, name=tag + "_ya")
    mla_rows = [(pb, kvl, 0), (pb, ql, (kvl + HEAD_PAD) // ql)]
    assert (kvl + HEAD_PAD) % ql == 0 and kvl % HEAD_PAD == 0
    ckvn, cqn = _rowwise(lambda kv_, q_, gk, gq: ([_rms(kv_, gk), _rms(q_, gq)], []), mla_rows, [row("kv_norm"), row("q_norm")],
                         outs=[(kvl, BF16), (ql, BF16)], name=tag + "_lat_rms")
    q0 = _mm(cqn, mw["w_uq"], name=tag + "_q")
    k0 = _mm(ckvn, mw["w_k"], name=tag + "_k")
    v = _mm(ckvn, mw["w_v"], out_dtype=BF16, name=tag + "_v")

    def rope_fwd(q_, k_, kpe, c, s1, s2):
        kr = _rope(kpe, c, s1, s2)
        return [jnp.concatenate([_rope(z, c, s1, s2) for z in _heads(q_)], axis=-1),
                jnp.concatenate([z + kr for z in _heads(k_)], axis=-1)], []

    qh, kh = _rowwise(rope_fwd, [q0, k0, (pb, HEAD_PAD, kvl // HEAD_PAD), *rope],
                      outs=[(q0.shape[1], BF16), (k0.shape[1], BF16)], name=tag + "_rope")
    o = _attn_fwd(qh, kh, v, tag + "_attn")
    y_b = _mm(o, p["mla_w_o"], b_chips=True, name=tag + "_yb")
    c2 = _glu_conv_fwd(pc, p["conv_dw_w"], row("conv_dw_b"), tag + "_conv")
    c3, = _rowwise(lambda c_, g_, b_: ([_layer_norm_silu(c_, g_, b_)], []), [c2], [row("conv_ln_g"), row("conv_ln_b")],
                   outs=[(c2.shape[1], BF16)], name=tag + "_ln")
    y_c = _mm(c3, p["conv_w_out"], bias=row("conv_b_out"), b_chips=True, name=tag + "_yc")
    merged, = _rowwise(_merge, [y_a, y_b, y_c, (pg, d, 0), (pg, d, 1), (pg, d, 2)], outs=[(d, BF16)], name=tag + "_merge")
    out = _mm(merged, p["w_out"], res=x, name=tag + "_out")
    saved = dict(x=x, h=h, pa=pa, pb=pb, pc=pc, pg=pg, ya_pre=ya_pre, y_a=y_a, y_b=y_b, y_c=y_c, ckvn=ckvn, cqn=cqn,
                 qh=qh, kh=kh, v=v, o=o, c2=c2, c3=c3, merged=merged, lru_args=lru_args)
    return out, saved


def _merge(ya, yb, yc, g0, g1, g2):
    return [jax.nn.sigmoid(g0) * ya + jax.nn.sigmoid(g1) * yb + jax.nn.sigmoid(g2) * yc], []


def _mix_bwd(dy, s, p, mw, rope, tag, dep=None):
    x, h = s["x"], s["h"]
    d = x.shape[1]
    wl, ql, kvl = p["lru_lambda"].shape[-1], p["q_norm"].shape[-1], p["kv_norm"].shape[-1]
    row = lambda name: p[name][None, :]
    g = {}
    dmerged = _mm(dy, p["w_out"], "nt", bias=_after(dep, d), name=tag + "_dmerged")
    g["w_out"] = _mm(s["merged"], dy, "tn", GRAD_DTYPE, name=tag + "_dw_out")

    def merge_bwd(ya, yb, yc, g0, g1, g2, dm):
        _, vjp = jax.vjp(lambda *a: _merge(*a)[0][0], ya, yb, yc, g0, g1, g2)
        dya, dyb, dyc, d0, d1, d2 = vjp(dm)
        dpg = jnp.concatenate([d0, d1, d2], axis=-1)
        return [dya, dyb, dyc, dpg], [jnp.sum(dyc, axis=0, keepdims=True), jnp.sum(dpg, axis=0, keepdims=True)]

    pg = s["pg"]
    dya, dyb, dyc, dpg, g["conv_b_out"], db_g = _rowwise(
        merge_bwd, [s["y_a"], s["y_b"], s["y_c"], (pg, d, 0), (pg, d, 1), (pg, d, 2), dmerged],
        outs=[(d, BF16), (d, BF16), (d, BF16), (3 * d, BF16)], accs=[((1, d), F32), ((1, 3 * d), F32)], tt=128, name=tag + "_dmerge")
    g["lru_w_out"] = _mm(s["ya_pre"], dya, "tn", GRAD_DTYPE, out_chips=True, name=tag + "_dw_lru_out")
    dya_pre = _mm(dya, p["lru_w_out"], "nt", b_chips=True, name=tag + "_dya_pre")
    (dpa_x, dpa_g, g["lru_conv_w"], g["lru_conv_b"], g["wr"], g["wi"], g["br"], g["bi"], g["lru_lambda"], sb_x, sb_g) = _lru_bwd(
        s["pa"], dya_pre, *s["lru_args"], name=tag + "_dlru")
    dpa = jnp.concatenate([dpa_x, dpa_g], axis=1)
    db_a = jnp.concatenate([sb_x, sb_g], axis=1)
    g["conv_w_out"] = _mm(s["c3"], dyc, "tn", GRAD_DTYPE, out_chips=True, name=tag + "_dw_conv_out")
    dc3 = _mm(dyc, p["conv_w_out"], "nt", b_chips=True, name=tag + "_dc3")

    def ln_bwd(c_, dc_, g_, b_):
        _, vjp = jax.vjp(_layer_norm_silu, c_, g_, b_)
        dc, dg_, db_ = vjp(dc_)
        return [dc], [dg_, db_]

    cc = s["c2"].shape[1]
    dc2, g["conv_ln_g"], g["conv_ln_b"] = _rowwise(ln_bwd, [s["c2"], dc3], [row("conv_ln_g"), row("conv_ln_b")], outs=[(cc, F32)],
                                                    accs=[((1, cc), F32)] * 2, name=tag + "_dln")
    dpc_v, dpc_g, g["conv_dw_w"], g["conv_dw_b"], sc_v, sc_g = _glu_conv_bwd(s["pc"], dc2, p["conv_dw_w"], tag + "_dconv")
    dpc = jnp.concatenate([dpc_v, dpc_g], axis=1)
    db_c = jnp.concatenate([sc_v, sc_g], axis=1)
    g["mla_w_o"] = _mm(s["o"], dyb, "tn", GRAD_DTYPE, out_chips=True, name=tag + "_dw_o")
    do = _mm(dyb, p["mla_w_o"], "nt", out_dtype=BF16, b_chips=True, name=tag + "_do")
    dqh, dkh, dv = _attn_bwd(s["qh"], s["kh"], s["v"], do, tag + "_dattn")

    def rope_bwd(dq_, dk_, c, s1, s2):
        lane = lax.broadcasted_iota(jnp.int32, c.shape, 1)
        dkr = functools.reduce(lambda a, b: a + b, _heads(dk_))
        dkpe = jnp.where((lane >= QK_NOPE) & (lane < QK_NOPE + QK_ROPE), _rope_t(dkr, c, s1, s2), 0.0)
        return [jnp.concatenate([_rope_t(z, c, s1, s2) for z in _heads(dq_)], axis=-1), dk_, dkpe], []

    dq0, dk0, dkpe = _rowwise(rope_bwd, [dqh, dkh, *rope], outs=[(dqh.shape[1], BF16), (dkh.shape[1], BF16), (HEAD_PAD, F32)],
                              name=tag + "_drope")
    dvb = dv.astype(BF16)
    g["w_uq"] = _mm(s["cqn"], dq0, "tn", GRAD_DTYPE, name=tag + "_dw_uq")
    g["w_k"] = _mm(s["ckvn"], dk0, "tn", GRAD_DTYPE, name=tag + "_dw_k")
    g["w_v"] = _mm(s["ckvn"], dvb, "tn", GRAD_DTYPE, name=tag + "_dw_v")
    dcqn = _mm(dq0, mw["w_uq"], "nt", name=tag + "_dcqn")
    dckvn = _mm(dk0, mw["w_k"], "nt", name=tag + "_dckvn_k")
    dckvn = _mm(dvb, mw["w_v"], "nt", res=dckvn, name=tag + "_dckvn_v")

    def lat_bwd(kv_, q_, dkv_, dq_, dkpe_, gk, gq):
        _, vjp_k = jax.vjp(_rms, kv_, gk)
        _, vjp_q = jax.vjp(_rms, q_, gq)
        (dkv, dgk), (dq, dgq) = vjp_k(dkv_), vjp_q(dq_)
        dpb = jnp.concatenate([dkv, dkpe_, dq], axis=-1)
        return [dpb], [dgk, dgq, jnp.sum(dpb, axis=0, keepdims=True)]

    pb = s["pb"]
    dpb, g["kv_norm"], g["q_norm"], db_b = _rowwise(
        lat_bwd, [(pb, kvl, 0), (pb, ql, (kvl + HEAD_PAD) // ql), dckvn, dcqn, dkpe], [row("kv_norm"), row("q_norm")],
        outs=[(pb.shape[1], BF16)], accs=[((1, kvl), F32), ((1, ql), F32), ((1, pb.shape[1]), F32)], name=tag + "_dlat")
    dh = None
    for part, dpart in (("a", dpa), ("b", dpb), ("c", dpc), ("g", dpg)):
        g["w_" + part] = _mm(h, dpart, "tn", GRAD_DTYPE, name=tag + "_dw_" + part)
        dh = _mm(dpart, mw["w_" + part], "nt", res=dh, name=tag + "_dh_" + part)
    g["b_a"], g["b_b"], g["b_c"], g["b_g"] = db_a, db_b, db_c, db_g
    dx, g["mix_norm"] = _rms_bwd(x, dh, dy, row("mix_norm"), tag + "_drms")
    return dx, g


def _by_chip_cols(m):
    return m.reshape(m.shape[0], N_CHIPS, -1).transpose(1, 0, 2)


def _mix_grads_to_params(g, p):
    wl, ql, kvl, o1, o2, o3 = _mix_offsets(p)
    hd = p["lru_w_gate"].shape[-2]
    per = LANES // hd
    eye = jnp.eye(per, dtype=F32)
    diag = lambda m: jnp.einsum("cedfk,ef->cedk", m.reshape(-1, per, hd, per, hd), eye).reshape(-1, hd, hd)
    out = {k: g[k] for k in ("lru_w_out", "conv_w_out", "mla_w_o", "lru_conv_w", "conv_dw_w")}
    out["w_out"] = g["w_out"].reshape(N_CHIPS, -1, g["w_out"].shape[1])
    for k in ("mix_norm", "conv_b_out", "lru_conv_b", "lru_lambda", "conv_ln_g", "conv_ln_b", "conv_dw_b", "kv_norm", "q_norm"):
        out[k] = g[k][0]
    mla = lambda m: [(m, kvl + HEAD_PAD, ql), (m, 0, kvl), (m, kvl + QK_NOPE, QK_ROPE)]
    whole = lambda m: [(m, 0, m.shape[1])]
    out["w_in"] = _cols_by_chip(whole(g["w_a"]) + mla(g["w_b"]) + whole(g["w_c"]) + whole(g["w_g"]), p["w_in"].shape[-1])
    out["b_in"] = jnp.concatenate([g["b_a"]] + [m[:, a:a + w] for m, a, w in mla(g["b_b"])] + [g["b_c"], g["b_g"]], axis=1)[0]
    out["lru_w_gate"] = jnp.concatenate([diag(g["wr"]), diag(g["wi"])], axis=-1)
    out["lru_b_gate"] = jnp.concatenate([g["br"].reshape(-1, hd), g["bi"].reshape(-1, hd)], axis=-1)
    out["w_uq"] = _by_chip_cols(g["w_uq"].reshape(ql, MLA_HEADS, HEAD_PAD)[..., :QK_NOPE + QK_ROPE].reshape(ql, -1))
    out["w_ukv"] = _by_chip_cols(jnp.concatenate([g["w_k"].reshape(kvl, MLA_HEADS, HEAD_PAD)[..., :QK_NOPE],
                                                  g["w_v"].reshape(kvl, MLA_HEADS, V_HEAD)], axis=-1).reshape(kvl, -1))
    return out


def _loss_head(x, target, g, name):
    def fn(x_, t_, g_):
        y, vjp = jax.vjp(_rms, x_, g_)
        err = y - t_
        dx, dg = vjp(err * (1.0 / x_.shape[1]))
        loss = 0.5 * jnp.sum(jnp.mean(err * err, axis=-1, keepdims=True), axis=0, keepdims=True)
        return [dx], [dg, jnp.broadcast_to(loss, (1, LANES))]

    dx, dg, loss = _rowwise(fn, [x, target], [g], outs=[(x.shape[1], F32)], accs=[((1, x.shape[1]), F32), ((1, LANES), F32)], name=name)
    return loss[0, 0], dx, dg


def _part_fwd(part, x, p, rope, tag, dep=None):
    norm = part + "_norm"
    if dep is not None:
        p = dict(p, **{norm: p[norm] + dep})
    if part == "mix":
        mw = _mix_weights(p)
        x, s = _mix_fwd(x, p, mw, rope, tag)
        return x, (s, mw, p)
    x, s = _ffn_fwd(x, p[norm][None, :], p[part + "_w1"], p[part + "_w2"], tag)
    return x, (s, None, p)


def _part_bwd(part, dx, saved, rope, tag, dep=None):
    s, mw, p = saved
    if part == "mix":
        dx, gm = _mix_bwd(dx, s, p, mw, rope, tag, dep)
        return dx, _mix_grads_to_params(gm, p)
    dx, (dn, dw1, dw2) = _ffn_bwd(dx, s, p[part + "_norm"][None, :], p[part + "_w1"], p[part + "_w2"], tag, dep)
    return dx, {part + "_norm": dn[0], part + "_w1": dw1, part + "_w2": dw2.reshape(N_CHIPS, -1, dw2.shape[1])}


ANY = pl.BlockSpec(memory_space=pl.ANY)
VMEM_WHOLE = pl.BlockSpec(memory_space=pltpu.VMEM)


def _place():
    x, y, c = (lax.axis_index(a) for a in MESH_AXES)
    return x, y, c, [(1 - x, y), (x, 1 - y), (1 - x, 1 - y)]


def _remote(src, dst, send_sem, recv_sem, device):
    return pltpu.make_async_remote_copy(src_ref=src, dst_ref=dst, send_sem=send_sem, recv_sem=recv_sem, device_id=device,
                                        device_id_type=pl.DeviceIdType.MESH)


def _half_rows(c, half):
    return pl.ds(pl.multiple_of(c * half, 16), half)


def _gather_chips(shards, name):
    n = len(shards)
    halves = [s.shape[0] // 2 for s in shards]

    def body(*refs):
        ins, outs, send_sems, recv_sems = refs[:n], refs[n:2 * n], refs[2 * n], refs[2 * n + 1]
        x, y, c, chips = _place()
        me, sibling = 2 * x + y, (x, y, 1 - c)
        sent = []
        for i in range(n):
            mine = _half_rows(c, halves[i])
            sent += [_remote(ins[i].at[mine], outs[i].at[me, mine], send_sems.at[i, j], recv_sems.at[i, j], (cx, cy, c))
                     for j, (cx, cy) in enumerate(chips)]
            sent.append(_remote(ins[i], outs[i].at[me], send_sems.at[i, 6], recv_sems.at[i, 6], sibling))
        for cp in sent:
            cp.start()
        for i in range(n):
            mine = _half_rows(c, halves[i])
            for j, (cx, cy) in enumerate(chips):
                rows = outs[i].at[2 * cx + cy, mine]
                _remote(rows, rows, send_sems.at[i, j], recv_sems.at[i, j], (cx, cy, c)).wait_recv()
                passed = _remote(rows, rows, send_sems.at[i, 3 + j], recv_sems.at[i, 3 + j], sibling)
                passed.start()
                sent.append(passed)
        for i in range(n):
            other = _half_rows(1 - c, halves[i])
            for j, (cx, cy) in enumerate(chips):
                rows = outs[i].at[2 * cx + cy, other]
                _remote(rows, rows, send_sems.at[i, 3 + j], recv_sems.at[i, 3 + j], sibling).wait_recv()
            own = outs[i].at[me]
            _remote(own, own, send_sems.at[i, 6], recv_sems.at[i, 6], sibling).wait_recv()
        for cp in sent:
            cp.wait_send()

    return pl.pallas_call(
        body, name=name, in_specs=[ANY] * n, out_specs=[ANY] * n, out_shape=[S((N_CHIPS,) + s.shape, s.dtype) for s in shards],
        scratch_shapes=[pltpu.SemaphoreType.DMA((n, 7)), pltpu.SemaphoreType.DMA((n, 7))],
    )(*shards)


def _allreduce_all(v, name):
    r, cols = v.shape

    def body(v_ref, out_ref, buf, send_sems, recv_sems):
        x, y, c, chips = _place()
        sibling = (x, y, 1 - c)
        slot = lambda px, py, pc: buf.at[4 * px + 2 * py + pc]
        buf[4 * x + 2 * y + c] = v_ref[...]
        sent = [_remote(v_ref, slot(x, y, c), send_sems.at[0], recv_sems.at[0], sibling)]
        sent += [_remote(v_ref, slot(x, y, c), send_sems.at[1 + j], recv_sems.at[1 + j], (cx, cy, c)) for j, (cx, cy) in enumerate(chips)]
        for cp in sent:
            cp.start()
        for j, (cx, cy) in enumerate(chips):
            blk = slot(cx, cy, c)
            _remote(blk, blk, send_sems.at[1 + j], recv_sems.at[1 + j], (cx, cy, c)).wait_recv()
            passed = _remote(blk, blk, send_sems.at[4 + j], recv_sems.at[4 + j], sibling)
            passed.start()
            sent.append(passed)
        blk = slot(x, y, 1 - c)
        _remote(blk, blk, send_sems.at[0], recv_sems.at[0], sibling).wait_recv()
        for j, (cx, cy) in enumerate(chips):
            blk = slot(cx, cy, 1 - c)
            _remote(blk, blk, send_sems.at[4 + j], recv_sems.at[4 + j], sibling).wait_recv()
        for cp in sent:
            cp.wait_send()
        acc = buf[0]
        for k in range(1, 2 * N_CHIPS):
            acc = acc + buf[k]
        out_ref[...] = acc

    return pl.pallas_call(
        body, name=name, in_specs=[VMEM_WHOLE], out_specs=VMEM_WHOLE, out_shape=S((r, cols), F32),
        scratch_shapes=[pltpu.VMEM((2 * N_CHIPS, r, cols), F32), pltpu.SemaphoreType.DMA((7,)), pltpu.SemaphoreType.DMA((7,))],
        compiler_params=pltpu.CompilerParams(vmem_limit_bytes=VMEM_LIMIT),
    )(v)


def _pair_exchange(gs, name):
    n = len(gs)

    def body(*refs):
        g_refs, a_refs, send_sems, recv_sems = refs[:n], refs[n:2 * n], refs[2 * n], refs[2 * n + 1]
        x, y, c, _ = _place()
        sent = [_remote(g_refs[i].at[pl.ds(0, N_CHIPS), _half_rows(1 - c, gs[i].shape[1] // 2)], a_refs[i], send_sems.at[i], recv_sems.at[i],
                        (x, y, 1 - c)) for i in range(n)]
        for cp in sent:
            cp.start()
        for cp in sent:
            cp.wait()

    return pl.pallas_call(body, name=name, in_specs=[ANY] * n, out_specs=[ANY] * n,
                          out_shape=[S((N_CHIPS, g.shape[1] // 2, g.shape[2]), g.dtype) for g in gs],
                          scratch_shapes=[pltpu.SemaphoreType.DMA((n,)), pltpu.SemaphoreType.DMA((n,))])(*gs)


def _pair_sum(g, a, name):
    n, r, cols = g.shape
    half = r // 2
    tr = _tile(half, 512, 16)
    n_blk = half // tr

    def body(c_ref, g_ref, a_ref, o_ref):
        o_ref[...] = (g_ref[...].astype(F32) + a_ref[...].astype(F32)).astype(o_ref.dtype)

    blk = pl.BlockSpec((1, tr, cols), lambda j, i, c_ref: (j, i, 0))
    return pl.pallas_call(
        body, name=name, out_shape=S((n, half, cols), BF16),
        grid_spec=pltpu.PrefetchScalarGridSpec(
            num_scalar_prefetch=1, grid=(n, n_blk),
            in_specs=[pl.BlockSpec((1, tr, cols), lambda j, i, c_ref: (j, c_ref[0] * n_blk + i, 0)), blk], out_specs=blk),
        compiler_params=_params(("parallel", "parallel")),
    )(lax.axis_index("c").reshape(1).astype(jnp.int32), g, a)


HBM = pl.BlockSpec(memory_space=pltpu.HBM)
SEM = pl.BlockSpec(memory_space=pltpu.SEMAPHORE)
SPLIT_COPY = pltpu.CompilerParams(has_side_effects=pltpu.SideEffectType.DATAFLOW_SIDE_EFFECTING)


def _ici_begin(srcs, lands, views, name, after=None):
    n = len(srcs)
    extra = [] if after is None else [after]

    def body(*refs):
        s_refs, l_refs, send_sems, recv_sems, token = refs[:n], refs[n:2 * n], refs[-3 - 2 * n], refs[-2 - 2 * n], refs[-1]
        x, y, c, chips = _place()
        for i in range(n):
            for j, (cx, cy) in enumerate(chips):
                src, dst, _ = views(s_refs[i], l_refs[i], i, j, x, y, c, cx, cy)
                k = (N_CHIPS - 1) * i + j
                _remote(src, dst, send_sems.at[k], recv_sems.at[k], (cx, cy, c)).start()
        token[...] = jnp.zeros_like(token)

    bufs = list(srcs) + list(lands)
    sems = pltpu.SemaphoreType.DMA((n * (N_CHIPS - 1),))
    return pl.pallas_call(
        body, name=name, out_shape=(sems, sems, *[pltpu.HBM(b.shape, b.dtype) for b in bufs], S((8, LANES), F32)),
        in_specs=[HBM] * (2 * n) + [ANY] * len(extra), out_specs=(SEM, SEM, *[HBM] * (2 * n), VMEM_WHOLE),
        input_output_aliases={i: 2 + i for i in range(2 * n)}, compiler_params=SPLIT_COPY,
    )(*[pltpu.with_memory_space_constraint(b, pltpu.HBM) for b in bufs], *extra)


def _ici_end(handle, after, views, name):
    send_sems, recv_sems, *bufs, _ = handle
    n = len(bufs) // 2

    def body(*refs):
        s_refs, l_refs, send_sems_, recv_sems_ = refs[:n], refs[n:2 * n], refs[2 * n], refs[2 * n + 1]
        x, y, c, chips = _place()
        for i in range(n):
            for j, (cx, cy) in enumerate(chips):
                src, _, arrival = views(s_refs[i], l_refs[i], i, j, x, y, c, cx, cy)
                k = (N_CHIPS - 1) * i + j
                cp = _remote(src, arrival, send_sems_.at[k], recv_sems_.at[k], (cx, cy, c))
                cp.wait_send()
                cp.wait_recv()

    out = pl.pallas_call(
        body, name=name, out_shape=[pltpu.HBM(b.shape, b.dtype) for b in bufs], in_specs=[HBM] * (2 * n) + [SEM, SEM, ANY],
        out_specs=[HBM] * (2 * n), input_output_aliases={i: i for i in range(2 * n)}, compiler_params=SPLIT_COPY,
    )(*bufs, send_sems, recv_sems, after)
    return out[:n], out[n:]


def _gather_views(halves):
    def views(src, land, i, j, x, y, c, cx, cy):
        mine = _half_rows(c, halves[i])
        return src.at[mine], land.at[2 * x + y, mine], land.at[2 * cx + cy, mine]
    return views


def _gather_begin(shards, name, after=None):
    lands = [lax.empty((N_CHIPS,) + s.shape, s.dtype) for s in shards]
    return _ici_begin(shards, lands, _gather_views([s.shape[0] // 2 for s in shards]), name, after)


def _gather_end(handle, after, name):
    n = (len(handle) - 3) // 2
    shards, lands = _ici_end(handle, after, _gather_views([s.shape[0] // 2 for s in handle[2:2 + n]]), name + "_wait")
    return _gather_finish(shards, lands, name + "_finish")


def _gather_finish(shards, lands, name):
    n = len(shards)

    def body(*refs):
        ins, l_refs, outs, send_sems, recv_sems = refs[:n], refs[n:2 * n], refs[2 * n:3 * n], refs[3 * n], refs[3 * n + 1]
        x, y, c, chips = _place()
        me, sibling = 2 * x + y, (x, y, 1 - c)
        sent = []
        for i in range(n):
            mine = _half_rows(c, shards[i].shape[0] // 2)
            for j, (cx, cy) in enumerate(chips):
                sent.append(_remote(l_refs[i].at[2 * cx + cy, mine], outs[i].at[2 * cx + cy, mine], send_sems.at[i, j], recv_sems.at[i, j], sibling))
            sent.append(_remote(ins[i], outs[i].at[me], send_sems.at[i, 3], recv_sems.at[i, 3], sibling))
        for cp in sent:
            cp.start()
        for i in range(n):
            other = _half_rows(1 - c, shards[i].shape[0] // 2)
            for j, (cx, cy) in enumerate(chips):
                rows = outs[i].at[2 * cx + cy, other]
                _remote(rows, rows, send_sems.at[i, j], recv_sems.at[i, j], sibling).wait_recv()
            own = outs[i].at[me]
            _remote(own, own, send_sems.at[i, 3], recv_sems.at[i, 3], sibling).wait_recv()
        for cp in sent:
            cp.wait_send()

    return pl.pallas_call(
        body, name=name, in_specs=[ANY] * (2 * n), out_specs=[ANY] * n, out_shape=[S(l_.shape, l_.dtype) for l_ in lands],
        input_output_aliases={n + i: i for i in range(n)},
        scratch_shapes=[pltpu.SemaphoreType.DMA((n, 4)), pltpu.SemaphoreType.DMA((n, 4))],
    )(*shards, *lands)


def _exchange_views(src, land, i, j, x, y, c, cx, cy):
    return src.at[2 * cx + cy], land.at[j], land.at[j]


def _chip_exchange_begin(ps, name):
    return _ici_begin(ps, [lax.empty((N_CHIPS - 1,) + p.shape[1:], p.dtype) for p in ps], _exchange_views, name)


def _chip_exchange_end(handle, after, name):
    return _ici_end(handle, after, _exchange_views, name)


def _quad_sum(p, q, name):
    _, h, cols = p.shape
    tr = _tile(h, 512, 16)
    n_blk = h // tr
    x, y, c, _ = _place()

    def body(s_ref, p_ref, q0_ref, q1_ref, q2_ref, o_ref):
        o_ref[...] = p_ref[0].astype(F32) + q0_ref[0].astype(F32) + q1_ref[0].astype(F32) + q2_ref[0].astype(F32)

    in_specs = [pl.BlockSpec((1, tr, cols), lambda i, s_ref: (s_ref[0], i, 0))]
    in_specs += [pl.BlockSpec((1, tr, cols), functools.partial(lambda i, s_ref, k: (k, i, 0), k=k)) for k in range(N_CHIPS - 1)]
    return pl.pallas_call(
        body, name=name, out_shape=S((2 * h, cols), F32),
        grid_spec=pltpu.PrefetchScalarGridSpec(num_scalar_prefetch=1, grid=(n_blk,), in_specs=in_specs,
                                               out_specs=pl.BlockSpec((tr, cols), lambda i, s_ref: (s_ref[1] * n_blk + i, 0))),
        compiler_params=_params(("parallel",)),
    )(jnp.stack([2 * x + y, c]).astype(jnp.int32), p, q, q, q)


def _pair_share(bufs, name):
    n = len(bufs)

    def body(*refs):
        in_refs, out_refs, send_sems, recv_sems = refs[:n], refs[n:2 * n], refs[2 * n], refs[2 * n + 1]
        x, y, c, _ = _place()
        sent = []
        for i in range(n):
            mine = _half_rows(c, bufs[i].shape[0] // 2)
            sent.append(_remote(in_refs[i].at[mine], out_refs[i].at[mine], send_sems.at[i], recv_sems.at[i], (x, y, 1 - c)))
        for cp in sent:
            cp.start()
        for i in range(n):
            other = out_refs[i].at[_half_rows(1 - c, bufs[i].shape[0] // 2)]
            _remote(other, other, send_sems.at[i], recv_sems.at[i], (x, y, 1 - c)).wait_recv()
        for cp in sent:
            cp.wait_send()

    return pl.pallas_call(body, name=name, in_specs=[ANY] * n, out_specs=[ANY] * n, out_shape=[S(b.shape, b.dtype) for b in bufs],
                          input_output_aliases={i: i for i in range(n)},
                          scratch_shapes=[pltpu.SemaphoreType.DMA((n,)), pltpu.SemaphoreType.DMA((n,))])(*bufs)


def _adamw(w, gs, m, v, name):
    depth, r, cols = w.shape
    tr, tc = _tile(r, 256, 8), cols
    if tr < 64 and r > 256 and cols % LANES == 0:
        tr, tc = r, _tile(cols, 256)

    def body(*refs):
        w_ref, m_ref, v_ref = refs[:3]
        g_refs = refs[3:3 + depth]
        go_ref, d_ref, mo_ref, vo_ref = refs[3 + depth:]
        for l in range(depth):
            @pl.when(pl.program_id(0) == l)
            def _(l=l):
                g_ = g_refs[l][...]
                m_ = ADAM_B1 * m_ref[...] + (1.0 - ADAM_B1) * g_
                v_ = ADAM_B2 * v_ref[...] + (1.0 - ADAM_B2) * jnp.square(g_)
                m_hat = m_ / (1.0 - ADAM_B1 ** ADAM_STEP)
                v_hat = v_ / (1.0 - ADAM_B2 ** ADAM_STEP)
                d_ref[...] = -ADAM_LR * (m_hat / (jnp.sqrt(v_hat) + ADAM_EPS) + ADAM_WD * w_ref[...])
                go_ref[...], mo_ref[...], vo_ref[...] = g_, m_, v_

    blk = pl.BlockSpec((None, tr, tc), lambda l, i: (l, i, 0) if tc == cols else (l, 0, i))
    g_blk = pl.BlockSpec((tr, tc), lambda l, i: (i, 0) if tc == cols else (0, i))
    return pl.pallas_call(body, name=name, grid=(depth, (r // tr) * (cols // tc)), in_specs=[blk] * 3 + [g_blk] * depth, out_specs=[blk] * 4,
                          out_shape=[S(w.shape, F32)] * 4, compiler_params=_params(("parallel", "parallel")))(w, m, v, *gs)


WEIGHTS = ("ffn1_norm", "ffn1_w1", "ffn1_w2", "mix_norm", "w_in", "b_in", "lru_conv_w", "lru_conv_b", "lru_w_gate", "lru_b_gate",
           "lru_lambda", "lru_w_out", "q_norm", "w_uq", "kv_norm", "w_ukv", "mla_w_o", "conv_dw_w", "conv_dw_b", "conv_ln_g",
           "conv_ln_b", "conv_w_out", "conv_b_out", "w_out", "ffn2_norm", "ffn2_w1", "ffn2_w2", "final_norm")
ROW_SHARDED = ("ffn1_w2", "w_out", "ffn2_w2")
COL_SHARDED = ("ffn1_w1", "w_in", "lru_w_out", "w_uq", "w_ukv", "mla_w_o", "conv_w_out", "ffn2_w1")
SMALL_SHARDED = ("lru_conv_w", "conv_dw_w")
MXU_SHARDED = tuple(n for n in WEIGHTS if n in ROW_SHARDED + COL_SHARDED)
REPLICATED = tuple(n for n in WEIGHTS if n not in MXU_SHARDED + SMALL_SHARDED)
SMALL = REPLICATED + SMALL_SHARDED
INPUTS = ("x", "positions") + WEIGHTS + ("loss_target",) + tuple("m_" + n for n in WEIGHTS) + tuple("v_" + n for n in WEIGHTS)


def _pack(arrays, dtype, cols, row_unit):
    flat = jnp.concatenate([a.astype(dtype).reshape(-1) for a in arrays])
    unit = cols * row_unit
    return jnp.pad(flat, (0, -flat.shape[0] % unit)).reshape(-1, cols)


def _unpack(flat, shapes):
    out, off = [], 0
    for shp in shapes:
        n = 1
        for s_ in shp:
            n *= s_
        out.append(flat[..., off:off + n].reshape(flat.shape[:-1] + tuple(shp)))
        off += n
    return out


def _reduce_scatter_begin(gs, names, tag):
    halves = _pair_exchange(gs, tag + "_pair_exchange")
    pairs = [_pair_sum(g, h, f"{tag}_pair_sum_{n}") for n, g, h in zip(names, gs, halves, strict=True)]
    return _chip_exchange_begin(pairs, tag + "_chip_exchange")


def _reduce_scatter_end(handle, names, after, tag):
    pairs, others = _chip_exchange_end(handle, after, tag + "_chip_exchange_wait")
    sums = [_quad_sum(p, q, f"{tag}_chip_sum_{n}") for n, p, q in zip(names, pairs, others, strict=True)]
    return dict(zip(names, _pair_share(sums, tag + "_pair_share"), strict=True))


PARTS = (("ffn1", ("ffn1_w1", "ffn1_w2")), ("mix", ("w_in", "lru_w_out", "w_uq", "w_ukv", "mla_w_o", "conv_w_out", "w_out")),
         ("ffn2", ("ffn2_w1", "ffn2_w2")))


def _step(a):
    x, positions, target = a["x"][0], a["positions"][0], a["loss_target"][0]
    depth = a["ffn1_norm"].shape[0]
    me = 2 * lax.axis_index("x") + lax.axis_index("y")
    placed = [lax.dynamic_update_slice_in_dim(jnp.zeros(a[n].shape[:2] + (N_CHIPS,) + a[n].shape[2:], F32), 0.5 * a[n][:, :, None], me, 2)
              for n in SMALL_SHARDED]
    small_whole = _unpack(_allreduce_all(_pack(placed, F32, LANES, 8), "small_weights").reshape(-1), [p_.shape for p_ in placed])
    small_whole = {n: w.reshape(w.shape[:2] + (-1,)) for n, w in zip(SMALL_SHARDED, small_whole, strict=True)}
    base = [{n: a[n][l] for n in REPLICATED if a[n].ndim > 1} | {n: small_whole[n][l] for n in SMALL_SHARDED} for l in range(depth)]
    order = [(l, part, names) for l in range(depth) for part, names in PARTS]
    shards = lambda l, names: [a[n][l].astype(BF16) for n in names]
    rope = _rope_tables(positions, "rope_tables")
    handle = _gather_begin(shards(0, order[0][2]), "l0_ffn1_gather", small_whole[SMALL_SHARDED[0]])
    gathered = _gather_end(handle, handle[-1], "l0_ffn1_gather")
    saved = []
    for k, (l, part, names) in enumerate(order):
        p = base[l] | {n: g.reshape(-1, g.shape[-1]) if n in ROW_SHARDED else g for n, g in zip(names, gathered, strict=True)}
        dep = None
        if k + 1 < len(order):
            l2, part2, names2 = order[k + 1]
            handle = _gather_begin(shards(l2, names2), f"l{l2}_{part2}_gather", handle[-1])
            dep = handle[-1][0, 0]
        x, s = _part_fwd(part, x, p, rope, f"l{l}_{part}", dep)
        saved.append(s)
        if dep is not None:
            gathered = _gather_end(handle, x, f"l{l2}_{part2}_gather")
    loss, dx, dfinal = _loss_head(x, target, a["final_norm"][None, :], "loss_head")
    loss = lax.psum(loss, MESH_AXES)
    grads, shard_grads, pending = [{} for _ in range(depth)], [{} for _ in range(depth)], None
    for l, part, names in reversed(order):
        dx, g = _part_bwd(part, dx, saved.pop(), rope, f"l{l}_{part}", None if pending is None else pending[0][-1][0, 0])
        if pending is not None:
            shard_grads[pending[1]].update(_reduce_scatter_end(pending[0], pending[2], dx, pending[3]))
        pending = (_reduce_scatter_begin([g[n] for n in names], names, f"l{l}_{part}_grad"), l, names, f"l{l}_{part}_grad")
        grads[l].update({n: g[n] for n in g if n in SMALL})
    shard_grads[pending[1]].update(_reduce_scatter_end(pending[0], pending[2], dx, pending[3]))
    small = [jnp.stack([g[n] for g in grads]) if a[n].ndim > 1 else dfinal[0] for n in SMALL]
    g_small = _unpack(_allreduce_all(_pack(small, F32, LANES, 256), "grad_allreduce").reshape(-1), [s_.shape for s_ in small])
    g_small = [lax.dynamic_slice_in_dim(g, me * a[n].shape[-1], a[n].shape[-1], 2) if n in SMALL_SHARDED else g
               for n, g in zip(SMALL, g_small, strict=True)]
    g, delta, new_m, new_v = {}, {}, {}, {}
    for n in MXU_SHARDED:
        flip = (lambda t: jnp.swapaxes(t, -1, -2)) if (a[n].shape[-1] % LANES and not a[n].shape[-2] % LANES) else (lambda t: t)
        res = _adamw(flip(a[n]), [flip(shard_grads[l][n]) for l in range(depth)], flip(a["m_" + n]), flip(a["v_" + n]), "adamw_" + n)
        g[n], delta[n], new_m[n], new_v[n] = [flip(r) for r in res]
    shapes = [a[n].shape for n in SMALL]
    packed = [_pack([a[pre + n] for n in SMALL], F32, LANES, 256)[None] for pre in ("", "m_", "v_")]
    res = _adamw(packed[0], [_pack(g_small, F32, LANES, 256)], packed[1], packed[2], "adamw_small")
    for out, r in zip((g, delta, new_m, new_v), res, strict=True):
        out.update(zip(SMALL, _unpack(r.reshape(-1), shapes), strict=True))
    return (loss, dx[None], *[g[n] for n in WEIGHTS], *[delta[n] for n in WEIGHTS], *[new_m[n] for n in WEIGHTS], *[new_v[n] for n in WEIGHTS])


def kernel(x, positions, ffn1_norm, ffn1_w1, ffn1_w2, mix_norm, w_in, b_in, lru_conv_w, lru_conv_b, lru_w_gate, lru_b_gate, lru_lambda, lru_w_out, q_norm, w_uq, kv_norm, w_ukv, mla_w_o, conv_dw_w, conv_dw_b, conv_ln_g, conv_ln_b, conv_w_out, conv_b_out, w_out, ffn2_norm, ffn2_w1, ffn2_w2, final_norm, loss_target, m_ffn1_norm, m_ffn1_w1, m_ffn1_w2, m_mix_norm, m_w_in, m_b_in, m_lru_conv_w, m_lru_conv_b, m_lru_w_gate, m_lru_b_gate, m_lru_lambda, m_lru_w_out, m_q_norm, m_w_uq, m_kv_norm, m_w_ukv, m_mla_w_o, m_conv_dw_w, m_conv_dw_b, m_conv_ln_g, m_conv_ln_b, m_conv_w_out, m_conv_b_out, m_w_out, m_ffn2_norm, m_ffn2_w1, m_ffn2_w2, m_final_norm, v_ffn1_norm, v_ffn1_w1, v_ffn1_w2, v_mix_norm, v_w_in, v_b_in, v_lru_conv_w, v_lru_conv_b, v_lru_w_gate, v_lru_b_gate, v_lru_lambda, v_lru_w_out, v_q_norm, v_w_uq, v_kv_norm, v_w_ukv, v_mla_w_o, v_conv_dw_w, v_conv_dw_b, v_conv_ln_g, v_conv_ln_b, v_conv_w_out, v_conv_b_out, v_w_out, v_ffn2_norm, v_ffn2_w1, v_ffn2_w2, v_final_norm):
    return _step(dict(zip(INPUTS, (x, positions, ffn1_norm, ffn1_w1, ffn1_w2, mix_norm, w_in, b_in, lru_conv_w, lru_conv_b, lru_w_gate, lru_b_gate, lru_lambda, lru_w_out, q_norm, w_uq, kv_norm, w_ukv, mla_w_o, conv_dw_w, conv_dw_b, conv_ln_g, conv_ln_b, conv_w_out, conv_b_out, w_out, ffn2_norm, ffn2_w1, ffn2_w2, final_norm, loss_target, m_ffn1_norm, m_ffn1_w1, m_ffn1_w2, m_mix_norm, m_w_in, m_b_in, m_lru_conv_w, m_lru_conv_b, m_lru_w_gate, m_lru_b_gate, m_lru_lambda, m_lru_w_out, m_q_norm, m_w_uq, m_kv_norm, m_w_ukv, m_mla_w_o, m_conv_dw_w, m_conv_dw_b, m_conv_ln_g, m_conv_ln_b, m_conv_w_out, m_conv_b_out, m_w_out, m_ffn2_norm, m_ffn2_w1, m_ffn2_w2, m_final_norm, v_ffn1_norm, v_ffn1_w1, v_ffn1_w2, v_mix_norm, v_w_in, v_b_in, v_lru_conv_w, v_lru_conv_b, v_lru_w_gate, v_lru_b_gate, v_lru_lambda, v_lru_w_out, v_q_norm, v_w_uq, v_kv_norm, v_w_ukv, v_mla_w_o, v_conv_dw_w, v_conv_dw_b, v_conv_ln_g, v_conv_ln_b, v_conv_w_out, v_conv_b_out, v_w_out, v_ffn2_norm, v_ffn2_w1, v_ffn2_w2, v_final_norm), strict=True)))
```

```python
import functools

import jax
import jax.numpy as jnp
from jax import lax
from jax.experimental import pallas as pl
from jax.experimental.pallas import tpu as pltpu

F32, BF16 = jnp.float32, jnp.bfloat16
S = jax.ShapeDtypeStruct

LANES = 128
VMEM_LIMIT = 56 * 2**20
NORM_EPS = 1e-6
LRU_C = 8.0
MLA_HEADS = 8
QK_NOPE, QK_ROPE, V_HEAD = 64, 32, 64
HEAD_PAD = 128
ROPE_THETA = 10000.0
ADAM_LR, ADAM_B1, ADAM_B2, ADAM_EPS, ADAM_WD, ADAM_STEP = 0.001, 0.9, 0.999, 1e-08, 0.01, 10
MESH_AXES = ("x", "y", "c")
N_CHIPS = 4
GRAD_DTYPE = BF16
NT =(((1,), (1,)), ((), ()))
TN = (((0,), (0,)), ((), ()))
NN = (((1,), (0,)), ((), ()))


def _tile(n, cap, unit=LANES):
    best = None
    for d in range(unit, min(n, cap) + 1, unit):
        if n % d == 0:
            best = d
    return best if best is not None else n


def _params(sem):
    return pltpu.CompilerParams(dimension_semantics=sem, vmem_limit_bytes=VMEM_LIMIT)


def _mm(a, b, mode="nn", out_dtype=F32, bias=None, res=None, alpha=1.0, a_blocks=False, b_chips=False, out_chips=False, name="mm"):
    units_n, units_k = [], []
    if b_chips:
        blocks, rows, c = b.shape
        b_shape = (rows, blocks * c)
        (units_k if mode == "nt" else units_n).append(c)
    else:
        b_shape = b.shape
    if a_blocks:
        blocks_a, rows_a, c_a = a.shape
        a_shape = (rows_a, blocks_a * c_a)
        units_k.append(c_a)
    else:
        a_shape = a.shape
    if mode == "nn":
        (m, k), (k2, n), dims = a_shape, b_shape, NN
    elif mode == "nt":
        (m, k), (n, k2), dims = a_shape, b_shape, NT
    else:
        (k, m), (k2, n), dims = a_shape, b_shape, TN
    assert k == k2 and not (a_blocks and mode == "tn"), (name, a.shape, b.shape, mode)
    if out_chips:
        units_n.append(n // N_CHIPS)
    tm = _tile(m, 512)
    tn = _tile(min(units_n) if units_n else n, 1536)
    tk = k if (k <= 3072 and not units_k) else _tile(min(units_k) if units_k else k, 3072)
    nk = k // tk
    assert all(u % tn == 0 for u in units_n) and all(u % tk == 0 for u in units_k), (name, units_n, units_k, tn, tk)
    if a_blocks:
        a_spec = pl.BlockSpec((None, tm, tk), functools.partial(lambda i, j, kk, per: (kk // per, i, kk % per), per=c_a // tk))
    elif mode == "tn":
        a_spec = pl.BlockSpec((tk, tm), lambda i, j, kk: (kk, i))
    else:
        a_spec = pl.BlockSpec((tm, tk), lambda i, j, kk: (i, kk))
    if b_chips and mode != "nt":
        b_spec = pl.BlockSpec((None, tk, tn), functools.partial(lambda i, j, kk, per: (j // per, kk, j % per), per=c // tn))
    elif b_chips:
        b_spec = pl.BlockSpec((None, tn, tk), functools.partial(lambda i, j, kk, per: (kk // per, j, kk % per), per=c // tk))
    elif mode == "nt":
        b_spec = pl.BlockSpec((tn, tk), lambda i, j, kk: (j, kk))
    else:
        b_spec = pl.BlockSpec((tk, tn), lambda i, j, kk: (kk, j))
    if out_chips:
        out_spec = pl.BlockSpec((None, tm, tn), functools.partial(lambda i, j, kk, per: (j // per, i, j % per), per=n // N_CHIPS // tn))
        out_shape = S((N_CHIPS, m, n // N_CHIPS), out_dtype)
    else:
        out_spec = pl.BlockSpec((tm, tn), lambda i, j, kk: (i, j))
        out_shape = S((m, n), out_dtype)
    operands, in_specs = [a, b], [a_spec, b_spec]
    if bias is not None:
        operands.append(bias)
        in_specs.append(pl.BlockSpec((1, tn), lambda i, j, kk: (0, j)))
    if res is not None:
        operands.append(res)
        in_specs.append(pl.BlockSpec((tm, tn), lambda i, j, kk: (i, j)))

    def body(*refs):
        a_ref, b_ref = refs[0], refs[1]
        pos = 2
        bias_ref = res_ref = None
        if bias is not None:
            bias_ref, pos = refs[pos], pos + 1
        if res is not None:
            res_ref, pos = refs[pos], pos + 1
        o_ref = refs[pos]
        part = lax.dot_general(a_ref[...].astype(BF16), b_ref[...].astype(BF16), dims, preferred_element_type=F32)

        def finish(acc):
            out = acc if alpha == 1.0 else acc * alpha
            if bias_ref is not None:
                out = out + bias_ref[...]
            if res_ref is not None:
                out = out + res_ref[...]
            o_ref[...] = out.astype(o_ref.dtype)

        if nk == 1:
            finish(part)
        else:
            acc_ref = refs[pos + 1]
            kk = pl.program_id(2)

            @pl.when(kk == 0)
            def _():
                acc_ref[...] = part

            @pl.when(kk > 0)
            def _():
                acc_ref[...] += part

            @pl.when(kk == nk - 1)
            def _():
                finish(acc_ref[...])

    return pl.pallas_call(
        body, name=name, grid=(m // tm, n // tn, nk), in_specs=in_specs, out_specs=out_spec, out_shape=out_shape,
        scratch_shapes=[pltpu.VMEM((tm, tn), F32)] if nk > 1 else [],
        compiler_params=_params(("parallel", "parallel", "arbitrary")),
    )(*operands)


def _rowwise(fn, rows, params=(), outs=(), accs=(), tt=256, name="rowwise"):
    rows = [r if isinstance(r, tuple) else (r, r.shape[1], 0) for r in rows]
    t = rows[0][0].shape[0]
    tt = min(tt, t)
    n_rows, n_par, n_out = len(rows), len(params), len(outs)
    in_specs = [pl.BlockSpec((tt, w), functools.partial(lambda i, cb: (i, cb), cb=cb)) for (_, w, cb) in rows]
    in_specs += [pl.BlockSpec(p.shape, functools.partial(lambda i, nd: (0,) * nd, nd=p.ndim)) for p in params]
    out_shape = [S((t, w), dt) for (w, dt) in outs] + [S(shape, dt) for (shape, dt) in accs]
    out_specs = [pl.BlockSpec((tt, w), lambda i: (i, 0)) for (w, _) in outs]
    out_specs += [pl.BlockSpec(shape, functools.partial(lambda i, nd: (0,) * nd, nd=len(shape))) for (shape, _) in accs]

    def body(*refs):
        vals = [r[...] for r in refs[:n_rows + n_par]]
        o_vals, a_vals = fn(*vals)
        o_refs = refs[n_rows + n_par:n_rows + n_par + n_out]
        a_refs = refs[n_rows + n_par + n_out:]
        for ref, val in zip(o_refs, o_vals, strict=True):
            ref[...] = val.astype(ref.dtype)
        i = pl.program_id(0)
        for ref, val in zip(a_refs, a_vals, strict=True):
            @pl.when(i == 0)
            def _(ref=ref, val=val):
                ref[...] = val.astype(ref.dtype)

            @pl.when(i > 0)
            def _(ref=ref, val=val):
                ref[...] += val.astype(ref.dtype)

    res = pl.pallas_call(
        body, name=name, grid=(t // tt,), in_specs=in_specs, out_specs=out_specs, out_shape=out_shape,
        compiler_params=_params(("arbitrary",) if accs else ("parallel",)),
    )(*[r[0] for r in rows], *params)
    return res


def _rms(x, g):
    x = x.astype(F32)
    return x * lax.rsqrt(jnp.mean(x * x, axis=-1, keepdims=True) + NORM_EPS) * g


def _layer_norm_silu(x, g, b):
    mu = jnp.mean(x, axis=-1, keepdims=True)
    var = jnp.mean(jnp.square(x - mu), axis=-1, keepdims=True)
    return jax.nn.silu((x - mu) * lax.rsqrt(var + NORM_EPS) * g + b)


def _neg_expm1(z):
    series = -z * (1.0 + z * (0.5 + z * (1.0 / 6.0 + z * (1.0 / 24.0 + z * (1.0 / 120.0)))))
    return jnp.where(z > -0.05, series, 1.0 - jnp.exp(z))


def _shift_down(x, s, fill=0.0):
    if s == 0:
        return x
    row = lax.broadcasted_iota(jnp.int32, x.shape, 0)
    return jnp.where(row >= s, pltpu.roll(x, s, 0), fill)


def _shift_up(x, s, fill=0.0):
    if s == 0:
        return x
    t = x.shape[0]
    row = lax.broadcasted_iota(jnp.int32, x.shape, 0)
    return jnp.where(row < t - s, pltpu.roll(x, t - s, 0), fill)


def _scan(a, u, shift):
    t, d = a.shape[0], 1
    while d < t:
        u = u + a * shift(u, d, 0.0)
        if 2 * d < t:
            a = a * shift(a, d, 1.0)
        d *= 2
    return u


def _lru_gates(xa, wr, wi, br, bi, lam):
    xb = xa.astype(BF16)
    r = jax.nn.sigmoid(jnp.dot(xb, wr.astype(BF16), preferred_element_type=F32) + br)
    i = jax.nn.sigmoid(jnp.dot(xb, wi.astype(BF16), preferred_element_type=F32) + bi)
    log_a = -LRU_C * r * jax.nn.softplus(-lam)
    return jnp.exp(log_a), jnp.sqrt(_neg_expm1(2.0 * log_a)) * (i * xa)


def _conv_fwd(x, w_ref, b, width):
    y = b + w_ref[pl.ds(width - 1, 1), :] * x
    for j in range(width - 1):
        y = y + w_ref[pl.ds(j, 1), :] * _shift_down(x, width - 1 - j)
    return y


def _conv_bwd(x, dy, w_ref, dw_ref, width):
    dx = w_ref[pl.ds(width - 1, 1), :] * dy
    dw_ref[pl.ds(width - 1, 1), :] = jnp.sum(dy * x, axis=0, keepdims=True)
    for j in range(width - 1):
        s = width - 1 - j
        dx = dx + w_ref[pl.ds(j, 1), :] * _shift_up(dy, s)
        dw_ref[pl.ds(j, 1), :] = jnp.sum(dy * _shift_down(x, s), axis=0, keepdims=True)
    return dx


def _rope(z, c, s1, s2):
    return z * c + pltpu.roll(z, HEAD_PAD - QK_ROPE // 2, 1) * s1 + pltpu.roll(z, QK_ROPE // 2, 1) * s2


def _rope_t(d, c, s1, s2):
    return d * c + pltpu.roll(d * s1, QK_ROPE // 2, 1) + pltpu.roll(d * s2, HEAD_PAD - QK_ROPE // 2, 1)


def _heads(z):
    return [z[:, h * HEAD_PAD:(h + 1) * HEAD_PAD] for h in range(z.shape[1] // HEAD_PAD)]


def _chan_spec(t, c_off=0):
    return pl.BlockSpec((t, LANES), lambda c: (0, c_off + c))


def _lru_specs(t, n_tiles, width):
    vec = pl.BlockSpec((1, LANES), lambda c: (0, c))
    mat = pl.BlockSpec((1, LANES, LANES), lambda c: (c, 0, 0))
    return [_chan_spec(t), _chan_spec(t, n_tiles), pl.BlockSpec((width, LANES), lambda c: (0, c)), vec, mat, mat, vec, vec, vec]


def _lru_fwd(pa, cw, cb, wr, wi, br, bi, lam, name):
    t, w = pa.shape[0], pa.shape[1] // 2
    n_tiles, width = w // LANES, cw.shape[0]

    def body(x_ref, g_ref, cw_ref, cb_ref, wr_ref, wi_ref, br_ref, bi_ref, lam_ref, y_ref):
        xa = _conv_fwd(x_ref[...], cw_ref, cb_ref[...], width)
        a, u = _lru_gates(xa, wr_ref[0], wi_ref[0], br_ref[...], bi_ref[...], lam_ref[...])
        h = _scan(a, u, _shift_down)
        y_ref[...] = (h * jax.nn.gelu(g_ref[...])).astype(y_ref.dtype)

    return pl.pallas_call(
        body, name=name, grid=(n_tiles,), in_specs=_lru_specs(t, n_tiles, width), out_specs=_chan_spec(t),
        out_shape=S((t, w), BF16), compiler_params=_params(("parallel",)),
    )(pa, pa, cw, cb, wr, wi, br, bi, lam)


def _lru_bwd(pa, dy, cw, cb, wr, wi, br, bi, lam, name):
    t, w = pa.shape[0], pa.shape[1] // 2
    n_tiles, width = w // LANES, cw.shape[0]

    def body(x_ref, g_ref, cw_ref, cb_ref, wr_ref, wi_ref, br_ref, bi_ref, lam_ref, dy_ref,
             dx_ref, dg_ref, dcw_ref, dcb_ref, dwr_ref, dwi_ref, dbr_ref, dbi_ref, dlam_ref, sx_ref, sg_ref):
        x = x_ref[...]
        xa = _conv_fwd(x, cw_ref, cb_ref[...], width)
        (a, u), gates_vjp = jax.vjp(_lru_gates, xa, wr_ref[0], wi_ref[0], br_ref[...], bi_ref[...], lam_ref[...])
        h = _scan(a, u, _shift_down)
        _, out_vjp = jax.vjp(lambda h_, g_: h_ * jax.nn.gelu(g_), h, g_ref[...])
        dh, dgate = out_vjp(dy_ref[...])
        adj = _scan(_shift_up(a, 1), dh, _shift_up)
        dxa, dwr, dwi, dbr, dbi, dlam = gates_vjp((adj * _shift_down(h, 1), adj))
        dx = _conv_bwd(x, dxa, cw_ref, dcw_ref, width)
        dcb_ref[...] = jnp.sum(dxa, axis=0, keepdims=True)
        dx_ref[...] = dx.astype(dx_ref.dtype)
        dg_ref[...] = dgate.astype(dg_ref.dtype)
        sx_ref[...] = jnp.sum(dx, axis=0, keepdims=True)
        sg_ref[...] = jnp.sum(dgate, axis=0, keepdims=True)
        dwr_ref[0], dwi_ref[0] = dwr, dwi
        dbr_ref[...], dbi_ref[...], dlam_ref[...] = dbr, dbi, dlam

    vec = pl.BlockSpec((1, LANES), lambda c: (0, c))
    mat = pl.BlockSpec((1, LANES, LANES), lambda c: (c, 0, 0))
    vec_s, mat_s = S((1, w), F32), S((n_tiles, LANES, LANES), F32)
    return pl.pallas_call(
        body, name=name, grid=(n_tiles,), in_specs=_lru_specs(t, n_tiles, width) + [_chan_spec(t)],
        out_specs=[_chan_spec(t), _chan_spec(t), pl.BlockSpec((width, LANES), lambda c: (0, c)), vec, mat, mat, vec, vec, vec, vec, vec],
        out_shape=[S((t, w), BF16), S((t, w), BF16), S((width, w), F32), vec_s, mat_s, mat_s, vec_s, vec_s, vec_s, vec_s, vec_s],
        compiler_params=_params(("parallel",)),
    )(pa, pa, cw, cb, wr, wi, br, bi, lam, dy)


def _glu_conv_fwd(pc, cw, cb, name):
    t, c = pc.shape[0], pc.shape[1] // 2
    n_tiles, width = c // LANES, cw.shape[0]

    def body(v_ref, g_ref, cw_ref, cb_ref, y_ref):
        y_ref[...] = _conv_fwd(v_ref[...] * jax.nn.sigmoid(g_ref[...]), cw_ref, cb_ref[...], width)

    return pl.pallas_call(
        body, name=name, grid=(n_tiles,),
        in_specs=[_chan_spec(t), _chan_spec(t, n_tiles), pl.BlockSpec((width, LANES), lambda i: (0, i)), pl.BlockSpec((1, LANES), lambda i: (0, i))],
        out_specs=_chan_spec(t), out_shape=S((t, c), F32), compiler_params=_params(("parallel",)),
    )(pc, pc, cw, cb)


def _glu_conv_bwd(pc, dy, cw, name):
    t, c = pc.shape[0], pc.shape[1] // 2
    n_tiles, width = c // LANES, cw.shape[0]

    def body(v_ref, g_ref, cw_ref, dy_ref, dv_ref, dg_ref, dcw_ref, dcb_ref, sv_ref, sg_ref):
        glu = lambda v_, g_: v_ * jax.nn.sigmoid(g_)
        x, glu_vjp = jax.vjp(glu, v_ref[...], g_ref[...])
        dy_ = dy_ref[...]
        dv, dg = glu_vjp(_conv_bwd(x, dy_, cw_ref, dcw_ref, width))
        dcb_ref[...] = jnp.sum(dy_, axis=0, keepdims=True)
        dv_ref[...] = dv.astype(dv_ref.dtype)
        dg_ref[...] = dg.astype(dg_ref.dtype)
        sv_ref[...] = jnp.sum(dv, axis=0, keepdims=True)
        sg_ref[...] = jnp.sum(dg, axis=0, keepdims=True)

    vec = pl.BlockSpec((1, LANES), lambda i: (0, i))
    wspec = pl.BlockSpec((width, LANES), lambda i: (0, i))
    return pl.pallas_call(
        body, name=name, grid=(n_tiles,), in_specs=[_chan_spec(t), _chan_spec(t, n_tiles), wspec, _chan_spec(t)],
        out_specs=[_chan_spec(t), _chan_spec(t), wspec, vec, vec, vec],
        out_shape=[S((t, c), BF16), S((t, c), BF16), S((width, c), F32), S((1, c), F32), S((1, c), F32), S((1, c), F32)],
        compiler_params=_params(("parallel",)),
    )(pc, pc, cw, dy)


def _softmax_rows(q, k, causal, scale):
    s = lax.dot_general(q, k, NT, preferred_element_type=F32) * scale
    s = jnp.where(causal, s, jnp.finfo(F32).min)
    p = jnp.exp(s - jnp.max(s, axis=-1, keepdims=True))
    return p / jnp.sum(p, axis=-1, keepdims=True)


def _attn_specs(t, tq):
    return [pl.BlockSpec((tq, 2 * HEAD_PAD), lambda hp, i: (i, hp)), pl.BlockSpec((t, 2 * HEAD_PAD), lambda hp, i: (0, hp)),
            pl.BlockSpec((t, 2 * V_HEAD), lambda hp, i: (0, hp))]


def _causal(n, tq):
    row = lax.broadcasted_iota(jnp.int32, (tq, (n + 1) * tq), 0) + n * tq
    col = lax.broadcasted_iota(jnp.int32, (tq, (n + 1) * tq), 1)
    return col <= row


def _per_query_block(n_blocks, fn):
    for n in range(n_blocks):
        pl.when(pl.program_id(1) == n)(functools.partial(fn, n))


def _attn_fwd(qh, kh, v, name):
    t = qh.shape[0]
    tq = min(256, t)
    scale = (QK_NOPE + QK_ROPE) ** -0.5

    def body(q_ref, k_ref, v_ref, o_ref):
        def block(n):
            keys = (n + 1) * tq
            causal = _causal(n, tq)
            lane = lax.broadcasted_iota(jnp.int32, (keys, 2 * V_HEAD), 1)
            vv = v_ref[0:keys, :]
            acc = jnp.zeros((tq, 2 * V_HEAD), F32)
            for e in range(2):
                p = _softmax_rows(q_ref[:, e * HEAD_PAD:(e + 1) * HEAD_PAD], k_ref[0:keys, e * HEAD_PAD:(e + 1) * HEAD_PAD], causal, scale)
                ve = jnp.where((lane >= V_HEAD * e) & (lane < V_HEAD * (e + 1)), vv, jnp.zeros_like(vv))
                acc = acc + jnp.dot(p.astype(BF16), ve, preferred_element_type=F32)
            o_ref[...] = acc.astype(o_ref.dtype)

        _per_query_block(t // tq, block)

    return pl.pallas_call(
        body, name=name, grid=(MLA_HEADS // 2, t // tq), in_specs=_attn_specs(t, tq),
        out_specs=pl.BlockSpec((tq, 2 * V_HEAD), lambda hp, i: (i, hp)), out_shape=S((t, MLA_HEADS * V_HEAD), BF16),
        compiler_params=_params(("parallel", "parallel")),
    )(qh, kh, v)


def _attn_bwd(qh, kh, v, do, name):
    t = qh.shape[0]
    tq = min(256, t)
    scale = (QK_NOPE + QK_ROPE) ** -0.5

    def body(q_ref, k_ref, v_ref, do_ref, dq_ref, dk_ref, dv_ref):
        def block(n):
            keys = (n + 1) * tq
            causal = _causal(n, tq)
            lane = lax.broadcasted_iota(jnp.int32, (tq, 2 * V_HEAD), 1)
            vv, dd = v_ref[0:keys, :], do_ref[...]
            dqs, dks = [], []
            dv = jnp.zeros((keys, 2 * V_HEAD), F32)
            for e in range(2):
                q, k = q_ref[:, e * HEAD_PAD:(e + 1) * HEAD_PAD], k_ref[0:keys, e * HEAD_PAD:(e + 1) * HEAD_PAD]
                p = _softmax_rows(q, k, causal, scale)
                de = jnp.where((lane >= V_HEAD * e) & (lane < V_HEAD * (e + 1)), dd, jnp.zeros_like(dd))
                dp = lax.dot_general(de, vv, NT, preferred_element_type=F32)
                ds = (p * (dp - jnp.sum(p * dp, axis=-1, keepdims=True)) * scale).astype(BF16)
                dqs.append(jnp.dot(ds, k, preferred_element_type=F32))
                dks.append(lax.dot_general(ds, q, TN, preferred_element_type=F32))
                dv = dv + lax.dot_general(p.astype(BF16), de, TN, preferred_element_type=F32)
            dq_ref[...] = jnp.concatenate(dqs, axis=-1)
            dk = jnp.concatenate(dks, axis=-1)
            if n == 0:
                dk_ref[0:keys, :], dv_ref[0:keys, :] = dk, dv
                if keys < t:
                    dk_ref[keys:t, :] = jnp.zeros((t - keys, 2 * HEAD_PAD), F32)
                    dv_ref[keys:t, :] = jnp.zeros((t - keys, 2 * V_HEAD), F32)
            else:
                dk_ref[0:keys, :] += dk
                dv_ref[0:keys, :] += dv

        _per_query_block(t // tq, block)

    return pl.pallas_call(
        body, name=name, grid=(MLA_HEADS // 2, t // tq),
        in_specs=_attn_specs(t, tq) + [pl.BlockSpec((tq, 2 * V_HEAD), lambda hp, i: (i, hp))],
        out_specs=[pl.BlockSpec((tq, 2 * HEAD_PAD), lambda hp, i: (i, hp)), pl.BlockSpec((t, 2 * HEAD_PAD), lambda hp, i: (0, hp)),
                   pl.BlockSpec((t, 2 * V_HEAD), lambda hp, i: (0, hp))],
        out_shape=[S((t, MLA_HEADS * HEAD_PAD), F32), S((t, MLA_HEADS * HEAD_PAD), F32), S((t, MLA_HEADS * V_HEAD), F32)],
        compiler_params=_params(("parallel", "arbitrary")),
    )(qh, kh, v, do)


def _swiglu(gate, up):
    return jax.nn.silu(gate) * up


def _ffn_up(h, w1, name):
    t, d = h.shape
    chips, _, c = w1.shape
    f = chips * c // 2
    tm, tn = _tile(t, 512), _tile(c, 1536)
    per = c // tn

    def body(h_ref, wg_ref, wu_ref, gu_ref, a_ref):
        hb = h_ref[...]
        gate = jnp.dot(hb, wg_ref[...], preferred_element_type=F32).astype(BF16)
        up = jnp.dot(hb, wu_ref[...], preferred_element_type=F32).astype(BF16)
        gu_ref[0], gu_ref[1] = gate, up
        a_ref[...] = _swiglu(gate.astype(F32), up.astype(F32)).astype(a_ref.dtype)

    w_spec = lambda first: pl.BlockSpec((None, d, tn), lambda i, j: (first + j // per, 0, j % per))
    return pl.pallas_call(
        body, name=name, grid=(t // tm, f // tn), in_specs=[pl.BlockSpec((tm, d), lambda i, j: (i, 0)), w_spec(0), w_spec(chips // 2)],
        out_specs=[pl.BlockSpec((2, tm, tn), lambda i, j: (0, i, j)), pl.BlockSpec((tm, tn), lambda i, j: (i, j))],
        out_shape=[S((2, t, f), BF16), S((t, f), BF16)], compiler_params=_params(("parallel", "parallel")),
    )(h, w1, w1)


def _ffn_dgu(dy, w2, gu, bias, name):
    t, d = dy.shape
    f = w2.shape[0]
    tm, tn = _tile(t, 512), _tile(f, 1536)

    def body(*refs):
        dy_ref, w2_ref, gu_ref, o_ref = refs[0], refs[1], refs[2], refs[-1]
        da = 0.5 * lax.dot_general(dy_ref[...].astype(BF16), w2_ref[...], NT, preferred_element_type=F32)
        if bias is not None:
            da = da + refs[3][...]
        _, vjp = jax.vjp(_swiglu, gu_ref[0].astype(F32), gu_ref[1].astype(F32))
        dgate, dup = vjp(da)
        o_ref[0], o_ref[1] = dgate.astype(o_ref.dtype), dup.astype(o_ref.dtype)

    blk = pl.BlockSpec((2, tm, tn), lambda i, j: (0, i, j))
    in_specs = [pl.BlockSpec((tm, d), lambda i, j: (i, 0)), pl.BlockSpec((tn, d), lambda i, j: (j, 0)), blk]
    if bias is not None:
        in_specs.append(pl.BlockSpec((1, tn), lambda i, j: (0, j)))
    return pl.pallas_call(body, name=name, grid=(t // tm, f // tn), in_specs=in_specs, out_specs=blk, out_shape=S((2, t, f), BF16),
                          compiler_params=_params(("parallel", "parallel")))(dy, w2, gu, *([] if bias is None else [bias]))


def _ffn_fwd(x, g, w1, w2, tag):
    h, = _rowwise(lambda x_, g_: ([_rms(x_, g_)], []), [x], [g], outs=[(x.shape[1], BF16)], name=tag + "_rms")
    gu, a = _ffn_up(h, w1, tag + "_up")
    return _mm(a, w2, res=x, alpha=0.5, name=tag + "_down"), (x, h, gu, a)


def _after(dep, n):
    return None if dep is None else jnp.zeros((1, n), F32) + dep


def _ffn_bwd(dy, saved, g, w1, w2, tag, dep=None, mid=None):
    x, h, gu, a = saved
    d, f = x.shape[1], w2.shape[0]
    dgu = _ffn_dgu(dy, w2, gu, _after(dep, f), tag + "_dgu")
    dep = None if mid is None else mid(dgu)
    dw2 = _mm(a, dy, "tn", GRAD_DTYPE, alpha=0.5, bias=_after(dep, d), name=tag + "_dw2")
    dw1 = _mm(h, dgu, "tn", GRAD_DTYPE, bias=_after(dep, 2 * f), b_chips=True, out_chips=True, name=tag + "_dw1")
    dh = _mm(dgu, w1, "nt", bias=_after(dep, d), a_blocks=True, b_chips=True, name=tag + "_dh")
    dx, dg = _rms_bwd(x, dh, dy, g, tag + "_drms")
    return dx, (dg, dw1, dw2)


def _rms_bwd(x, dh, dres, g, name):
    def fn(x_, dh_, dres_, g_):
        _, vjp = jax.vjp(_rms, x_, g_)
        dx, dg = vjp(dh_)
        return [dx + dres_], [dg]

    return _rowwise(fn, [x, dh, dres], [g], outs=[(x.shape[1], F32)], accs=[((1, x.shape[1]), F32)], name=name)


def _rope_tables(positions, name):
    half = QK_ROPE // 2
    inv = ROPE_THETA ** (-jnp.arange(0, QK_ROPE, 2, dtype=F32) / QK_ROPE)
    inv_lanes = jnp.zeros((1, HEAD_PAD), F32).at[0, QK_NOPE:QK_NOPE + QK_ROPE].set(jnp.tile(inv, 2))

    def fn(pos, inv_):
        ang = pos.astype(F32) * inv_
        lane = lax.broadcasted_iota(jnp.int32, ang.shape, 1)
        cos, sin = jnp.cos(ang), jnp.sin(ang)
        c = jnp.where(lane < QK_NOPE, 1.0, jnp.where(lane < QK_NOPE + QK_ROPE, cos, 0.0))
        s1 = jnp.where((lane >= QK_NOPE) & (lane < QK_NOPE + half), -sin, 0.0)
        s2 = jnp.where((lane >= QK_NOPE + half) & (lane < QK_NOPE + QK_ROPE), sin, 0.0)
        return [c, s1, s2], []

    return _rowwise(fn, [positions.reshape(-1, 1)], [inv_lanes], outs=[(HEAD_PAD, F32)] * 3, name=name)


def _chip_cols(g, lo, hi):
    c = g.shape[-1]
    parts = [g[j, :, max(lo, j * c) - j * c:min(hi, (j + 1) * c) - j * c] for j in range(g.shape[0]) if max(lo, j * c) < min(hi, (j + 1) * c)]
    return parts[0] if len(parts) == 1 else jnp.concatenate(parts, axis=-1)


def _cols_by_chip(segments, c):
    out, start = [[] for _ in range(N_CHIPS)], 0
    for arr, first, width in segments:
        for j in range(N_CHIPS):
            lo, hi = max(start, j * c), min(start + width, (j + 1) * c)
            if lo < hi:
                out[j].append(arr[:, first + lo - start:first + hi - start])
        start += width
    assert start == N_CHIPS * c
    return jnp.stack([jnp.concatenate(parts, axis=-1) for parts in out])


def _whole_cols(g):
    return g.transpose(1, 0, 2).reshape(g.shape[1], -1)


def _mix_offsets(lw):
    wl, ql, kvl = lw["lru_lambda"].shape[-1], lw["q_norm"].shape[-1], lw["kv_norm"].shape[-1]
    o1 = 2 * wl
    o2 = o1 + ql + kvl + QK_ROPE
    return wl, ql, kvl, o1, o2, o2 + 2 * lw["conv_ln_g"].shape[-1]


def _mix_weights(lw):
    wl, ql, kvl, o1, o2, o3 = _mix_offsets(lw)
    w_in, b_in = lw["w_in"], lw["b_in"][None, :]
    d_in = b_in.shape[1]
    z = lambda m, n: jnp.zeros(m.shape[:-1] + (n,), m.dtype)
    w_b = jnp.concatenate([_chip_cols(w_in, o1 + ql, o1 + ql + kvl), z(w_in[0], QK_NOPE), _chip_cols(w_in, o1 + ql + kvl, o2),
                           z(w_in[0], HEAD_PAD - QK_NOPE - QK_ROPE), _chip_cols(w_in, o1, o1 + ql)], axis=-1)
    b_b = jnp.concatenate([b_in[:, o1 + ql:o1 + ql + kvl], z(b_in, QK_NOPE), b_in[:, o1 + ql + kvl:o2],
                           z(b_in, HEAD_PAD - QK_NOPE - QK_ROPE), b_in[:, o1:o1 + ql]], axis=-1)
    hd = lw["lru_w_gate"].shape[-2]
    per = LANES // hd
    eye = jnp.eye(per, dtype=F32)
    wg = lw["lru_w_gate"].reshape(-1, per, hd, 2 * hd)
    block_diag = lambda m: jnp.einsum("cedk,ef->cedfk", m, eye).reshape(-1, LANES, LANES)
    bg = lw["lru_b_gate"]
    w_uq = _whole_cols(lw["w_uq"]).reshape(ql, MLA_HEADS, QK_NOPE + QK_ROPE)
    w_ukv = _whole_cols(lw["w_ukv"]).reshape(kvl, MLA_HEADS, QK_NOPE + V_HEAD)
    pad = lambda m, n: jnp.pad(m, ((0, 0), (0, 0), (0, n)))
    return dict(
        w_a=_chip_cols(w_in, 0, o1), w_b=w_b, w_c=_chip_cols(w_in, o2, o3), w_g=_chip_cols(w_in, o3, d_in),
        b_a=b_in[:, :o1], b_b=b_b, b_c=b_in[:, o2:o3], b_g=b_in[:, o3:],
        wr=block_diag(wg[..., :hd]), wi=block_diag(wg[..., hd:]),
        br=bg[:, :hd].reshape(1, -1), bi=bg[:, hd:].reshape(1, -1),
        w_uq=pad(w_uq, HEAD_PAD - QK_NOPE - QK_ROPE).reshape(ql, -1),
        w_k=pad(w_ukv[..., :QK_NOPE], HEAD_PAD - QK_NOPE).reshape(kvl, -1),
        w_v=w_ukv[..., QK_NOPE:].reshape(kvl, -1),
    )


def _mix_fwd(x, p, mw, rope, tag):
    d = x.shape[1]
    wl, ql, kvl = p["lru_lambda"].shape[-1], p["q_norm"].shape[-1], p["kv_norm"].shape[-1]
    row = lambda name: p[name][None, :]
    h, = _rowwise(lambda x_, g_: ([_rms(x_, g_)], []), [x], [row("mix_norm")], outs=[(d, BF16)], name=tag + "_rms")
    pa = _mm(h, mw["w_a"], bias=mw["b_a"], name=tag + "_pa")
    pb = _mm(h, mw["w_b"], bias=mw["b_b"], name=tag + "_pb")
    pc = _mm(h, mw["w_c"], bias=mw["b_c"], name=tag + "_pc")
    pg = _mm(h, mw["w_g"], bias=mw["b_g"], name=tag + "_pg")
    lru_args = (p["lru_conv_w"], row("lru_conv_b"), mw["wr"], mw["wi"], mw["br"], mw["bi"], row("lru_lambda"))
    ya_pre = _lru_fwd(pa, *lru_args, name=tag + "_lru")
    y_a = _mm(ya_pre, p["lru_w_out"], b_chips=True, name=tag + "_ya")
    mla_rows = [(pb, kvl, 0), (pb, ql, (kvl + HEAD_PAD) // ql)]
    assert (kvl + HEAD_PAD) % ql == 0 and kvl % HEAD_PAD == 0
    ckvn, cqn = _rowwise(lambda kv_, q_, gk, gq: ([_rms(kv_, gk), _rms(q_, gq)], []), mla_rows, [row("kv_norm"), row("q_norm")],
                         outs=[(kvl, BF16), (ql, BF16)], name=tag + "_lat_rms")
    q0 = _mm(cqn, mw["w_uq"], name=tag + "_q")
    k0 = _mm(ckvn, mw["w_k"], name=tag + "_k")
    v = _mm(ckvn, mw["w_v"], out_dtype=BF16, name=tag + "_v")

    def rope_fwd(q_, k_, kpe, c, s1, s2):
        kr = _rope(kpe, c, s1, s2)
        return [jnp.concatenate([_rope(z, c, s1, s2) for z in _heads(q_)], axis=-1),
                jnp.concatenate([z + kr for z in _heads(k_)], axis=-1)], []

    qh, kh = _rowwise(rope_fwd, [q0, k0, (pb, HEAD_PAD, kvl // HEAD_PAD), *rope],
                      outs=[(q0.shape[1], BF16), (k0.shape[1], BF16)], name=tag + "_rope")
    o = _attn_fwd(qh, kh, v, tag + "_attn")
    y_b = _mm(o, p["mla_w_o"], b_chips=True, name=tag + "_yb")
    c2 = _glu_conv_fwd(pc, p["conv_dw_w"], row("conv_dw_b"), tag + "_conv")
    c3, = _rowwise(lambda c_, g_, b_: ([_layer_norm_silu(c_, g_, b_)], []), [c2], [row("conv_ln_g"), row("conv_ln_b")],
                   outs=[(c2.shape[1], BF16)], name=tag + "_ln")
    y_c = _mm(c3, p["conv_w_out"], bias=row("conv_b_out"), b_chips=True, name=tag + "_yc")
    merged, = _rowwise(_merge, [y_a, y_b, y_c, (pg, d, 0), (pg, d, 1), (pg, d, 2)], outs=[(d, BF16)], name=tag + "_merge")
    out = _mm(merged, p["w_out"], res=x, name=tag + "_out")
    saved = dict(x=x, h=h, pa=pa, pb=pb, pc=pc, pg=pg, ya_pre=ya_pre, y_a=y_a, y_b=y_b, y_c=y_c, ckvn=ckvn, cqn=cqn,
                 qh=qh, kh=kh, v=v, o=o, c2=c2, c3=c3, merged=merged, lru_args=lru_args)
    return out, saved


def _merge(ya, yb, yc, g0, g1, g2):
    return [jax.nn.sigmoid(g0) * ya + jax.nn.sigmoid(g1) * yb + jax.nn.sigmoid(g2) * yc], []


def _mix_bwd(dy, s, p, mw, rope, tag, dep=None, mid=None):
    x, h = s["x"], s["h"]
    d = x.shape[1]
    wl, ql, kvl = p["lru_lambda"].shape[-1], p["q_norm"].shape[-1], p["kv_norm"].shape[-1]
    row = lambda name: p[name][None, :]
    g = {}
    dmerged = _mm(dy, p["w_out"], "nt", bias=_after(dep, d), name=tag + "_dmerged")
    g["w_out"] = _mm(s["merged"], dy, "tn", GRAD_DTYPE, name=tag + "_dw_out")

    def merge_bwd(ya, yb, yc, g0, g1, g2, dm):
        _, vjp = jax.vjp(lambda *a: _merge(*a)[0][0], ya, yb, yc, g0, g1, g2)
        dya, dyb, dyc, d0, d1, d2 = vjp(dm)
        dpg = jnp.concatenate([d0, d1, d2], axis=-1)
        return [dya, dyb, dyc, dpg], [jnp.sum(dyc, axis=0, keepdims=True), jnp.sum(dpg, axis=0, keepdims=True)]

    pg = s["pg"]
    dya, dyb, dyc, dpg, g["conv_b_out"], db_g = _rowwise(
        merge_bwd, [s["y_a"], s["y_b"], s["y_c"], (pg, d, 0), (pg, d, 1), (pg, d, 2), dmerged],
        outs=[(d, BF16), (d, BF16), (d, BF16), (3 * d, BF16)], accs=[((1, d), F32), ((1, 3 * d), F32)], tt=128, name=tag + "_dmerge")
    dep = None if mid is None else mid(dya)
    g["lru_w_out"] = _mm(s["ya_pre"], dya, "tn", GRAD_DTYPE, bias=_after(dep, d), out_chips=True, name=tag + "_dw_lru_out")
    dya_pre = _mm(dya, p["lru_w_out"], "nt", bias=_after(dep, wl), b_chips=True, name=tag + "_dya_pre")
    (dpa_x, dpa_g, g["lru_conv_w"], g["lru_conv_b"], g["wr"], g["wi"], g["br"], g["bi"], g["lru_lambda"], sb_x, sb_g) = _lru_bwd(
        s["pa"], dya_pre, *s["lru_args"], name=tag + "_dlru")
    dpa = jnp.concatenate([dpa_x, dpa_g], axis=1)
    db_a = jnp.concatenate([sb_x, sb_g], axis=1)
    g["conv_w_out"] = _mm(s["c3"], dyc, "tn", GRAD_DTYPE, bias=_after(dep, d), out_chips=True, name=tag + "_dw_conv_out")
    dc3 = _mm(dyc, p["conv_w_out"], "nt", bias=_after(dep, s["c3"].shape[1]), b_chips=True, name=tag + "_dc3")

    def ln_bwd(c_, dc_, g_, b_):
        _, vjp = jax.vjp(_layer_norm_silu, c_, g_, b_)
        dc, dg_, db_ = vjp(dc_)
        return [dc], [dg_, db_]

    cc = s["c2"].shape[1]
    dc2, g["conv_ln_g"], g["conv_ln_b"] = _rowwise(ln_bwd, [s["c2"], dc3], [row("conv_ln_g"), row("conv_ln_b")], outs=[(cc, F32)],
                                                    accs=[((1, cc), F32)] * 2, name=tag + "_dln")
    dpc_v, dpc_g, g["conv_dw_w"], g["conv_dw_b"], sc_v, sc_g = _glu_conv_bwd(s["pc"], dc2, p["conv_dw_w"], tag + "_dconv")
    dpc = jnp.concatenate([dpc_v, dpc_g], axis=1)
    db_c = jnp.concatenate([sc_v, sc_g], axis=1)
    g["mla_w_o"] = _mm(s["o"], dyb, "tn", GRAD_DTYPE, bias=_after(dep, d), out_chips=True, name=tag + "_dw_o")
    do = _mm(dyb, p["mla_w_o"], "nt", out_dtype=BF16, bias=_after(dep, s["o"].shape[1]), b_chips=True, name=tag + "_do")
    dqh, dkh, dv = _attn_bwd(s["qh"], s["kh"], s["v"], do, tag + "_dattn")

    def rope_bwd(dq_, dk_, c, s1, s2):
        lane = lax.broadcasted_iota(jnp.int32, c.shape, 1)
        dkr = functools.reduce(lambda a, b: a + b, _heads(dk_))
        dkpe = jnp.where((lane >= QK_NOPE) & (lane < QK_NOPE + QK_ROPE), _rope_t(dkr, c, s1, s2), 0.0)
        return [jnp.concatenate([_rope_t(z, c, s1, s2) for z in _heads(dq_)], axis=-1), dk_, dkpe], []

    dq0, dk0, dkpe = _rowwise(rope_bwd, [dqh, dkh, *rope], outs=[(dqh.shape[1], BF16), (dkh.shape[1], BF16), (HEAD_PAD, F32)],
                              name=tag + "_drope")
    dvb = dv.astype(BF16)
    g["w_uq"] = _mm(s["cqn"], dq0, "tn", GRAD_DTYPE, name=tag + "_dw_uq")
    g["w_k"] = _mm(s["ckvn"], dk0, "tn", GRAD_DTYPE, name=tag + "_dw_k")
    g["w_v"] = _mm(s["ckvn"], dvb, "tn", GRAD_DTYPE, name=tag + "_dw_v")
    dcqn = _mm(dq0, mw["w_uq"], "nt", name=tag + "_dcqn")
    dckvn = _mm(dk0, mw["w_k"], "nt", name=tag + "_dckvn_k")
    dckvn = _mm(dvb, mw["w_v"], "nt", res=dckvn, name=tag + "_dckvn_v")

    def lat_bwd(kv_, q_, dkv_, dq_, dkpe_, gk, gq):
        _, vjp_k = jax.vjp(_rms, kv_, gk)
        _, vjp_q = jax.vjp(_rms, q_, gq)
        (dkv, dgk), (dq, dgq) = vjp_k(dkv_), vjp_q(dq_)
        dpb = jnp.concatenate([dkv, dkpe_, dq], axis=-1)
        return [dpb], [dgk, dgq, jnp.sum(dpb, axis=0, keepdims=True)]

    pb = s["pb"]
    dpb, g["kv_norm"], g["q_norm"], db_b = _rowwise(
        lat_bwd, [(pb, kvl, 0), (pb, ql, (kvl + HEAD_PAD) // ql), dckvn, dcqn, dkpe], [row("kv_norm"), row("q_norm")],
        outs=[(pb.shape[1], BF16)], accs=[((1, kvl), F32), ((1, ql), F32), ((1, pb.shape[1]), F32)], name=tag + "_dlat")
    dh = None
    for part, dpart in (("a", dpa), ("b", dpb), ("c", dpc), ("g", dpg)):
        g["w_" + part] = _mm(h, dpart, "tn", GRAD_DTYPE, name=tag + "_dw_" + part)
        dh = _mm(dpart, mw["w_" + part], "nt", res=dh, name=tag + "_dh_" + part)
    g["b_a"], g["b_b"], g["b_c"], g["b_g"] = db_a, db_b, db_c, db_g
    dx, g["mix_norm"] = _rms_bwd(x, dh, dy, row("mix_norm"), tag + "_drms")
    return dx, g


def _by_chip_cols(m):
    return m.reshape(m.shape[0], N_CHIPS, -1).transpose(1, 0, 2)


def _mix_grads_to_params(g, p):
    wl, ql, kvl, o1, o2, o3 = _mix_offsets(p)
    hd = p["lru_w_gate"].shape[-2]
    per = LANES // hd
    eye = jnp.eye(per, dtype=F32)
    diag = lambda m: jnp.einsum("cedfk,ef->cedk", m.reshape(-1, per, hd, per, hd), eye).reshape(-1, hd, hd)
    out = {k: g[k] for k in ("lru_w_out", "conv_w_out", "mla_w_o", "lru_conv_w", "conv_dw_w")}
    out["w_out"] = g["w_out"].reshape(N_CHIPS, -1, g["w_out"].shape[1])
    for k in ("mix_norm", "conv_b_out", "lru_conv_b", "lru_lambda", "conv_ln_g", "conv_ln_b", "conv_dw_b", "kv_norm", "q_norm"):
        out[k] = g[k][0]
    mla = lambda m: [(m, kvl + HEAD_PAD, ql), (m, 0, kvl), (m, kvl + QK_NOPE, QK_ROPE)]
    whole = lambda m: [(m, 0, m.shape[1])]
    out["w_in"] = _cols_by_chip(whole(g["w_a"]) + mla(g["w_b"]) + whole(g["w_c"]) + whole(g["w_g"]), p["w_in"].shape[-1])
    out["b_in"] = jnp.concatenate([g["b_a"]] + [m[:, a:a + w] for m, a, w in mla(g["b_b"])] + [g["b_c"], g["b_g"]], axis=1)[0]
    out["lru_w_gate"] = jnp.concatenate([diag(g["wr"]), diag(g["wi"])], axis=-1)
    out["lru_b_gate"] = jnp.concatenate([g["br"].reshape(-1, hd), g["bi"].reshape(-1, hd)], axis=-1)
    out["w_uq"] = _by_chip_cols(g["w_uq"].reshape(ql, MLA_HEADS, HEAD_PAD)[..., :QK_NOPE + QK_ROPE].reshape(ql, -1))
    out["w_ukv"] = _by_chip_cols(jnp.concatenate([g["w_k"].reshape(kvl, MLA_HEADS, HEAD_PAD)[..., :QK_NOPE],
                                                  g["w_v"].reshape(kvl, MLA_HEADS, V_HEAD)], axis=-1).reshape(kvl, -1))
    return out


def _loss_head(x, target, g, name):
    def fn(x_, t_, g_):
        y, vjp = jax.vjp(_rms, x_, g_)
        err = y - t_
        dx, dg = vjp(err * (1.0 / x_.shape[1]))
        loss = 0.5 * jnp.sum(jnp.mean(err * err, axis=-1, keepdims=True), axis=0, keepdims=True)
        return [dx], [dg, jnp.broadcast_to(loss, (1, LANES))]

    dx, dg, loss = _rowwise(fn, [x, target], [g], outs=[(x.shape[1], F32)], accs=[((1, x.shape[1]), F32), ((1, LANES), F32)], name=name)
    return loss[0, 0], dx, dg


def _part_fwd(part, x, p, rope, tag, dep=None):
    norm = part + "_norm"
    if dep is not None:
        p = dict(p, **{norm: p[norm] + dep})
    if part == "mix":
        mw = _mix_weights(p)
        x, s = _mix_fwd(x, p, mw, rope, tag)
        return x, (s, mw, p)
    x, s = _ffn_fwd(x, p[norm][None, :], p[part + "_w1"], p[part + "_w2"], tag)
    return x, (s, None, p)


def _part_bwd(part, dx, saved, rope, tag, dep=None, mid=None):
    s, mw, p = saved
    if part == "mix":
        dx, gm = _mix_bwd(dx, s, p, mw, rope, tag, dep, mid)
        return dx, _mix_grads_to_params(gm, p)
    dx, (dn, dw1, dw2) = _ffn_bwd(dx, s, p[part + "_norm"][None, :], p[part + "_w1"], p[part + "_w2"], tag, dep, mid)
    return dx, {part + "_norm": dn[0], part + "_w1": dw1, part + "_w2": dw2.reshape(N_CHIPS, -1, dw2.shape[1])}


ANY = pl.BlockSpec(memory_space=pl.ANY)
VMEM_WHOLE = pl.BlockSpec(memory_space=pltpu.VMEM)


def _place():
    x, y, c = (lax.axis_index(a) for a in MESH_AXES)
    return x, y, c, [(1 - x, y), (x, 1 - y), (1 - x, 1 - y)]


def _remote(src, dst, send_sem, recv_sem, device):
    return pltpu.make_async_remote_copy(src_ref=src, dst_ref=dst, send_sem=send_sem, recv_sem=recv_sem, device_id=device,
                                        device_id_type=pl.DeviceIdType.MESH)


def _half_rows(c, half):
    return pl.ds(pl.multiple_of(c * half, 16), half)


def _allreduce_all(v, name):
    r, cols = v.shape

    def body(v_ref, out_ref, buf, send_sems, recv_sems):
        x, y, c, chips = _place()
        sibling = (x, y, 1 - c)
        slot = lambda px, py, pc: buf.at[4 * px + 2 * py + pc]
        buf[4 * x + 2 * y + c] = v_ref[...]
        sent = [_remote(v_ref, slot(x, y, c), send_sems.at[0], recv_sems.at[0], sibling)]
        sent += [_remote(v_ref, slot(x, y, c), send_sems.at[1 + j], recv_sems.at[1 + j], (cx, cy, c)) for j, (cx, cy) in enumerate(chips)]
        for cp in sent:
            cp.start()
        for j, (cx, cy) in enumerate(chips):
            blk = slot(cx, cy, c)
            _remote(blk, blk, send_sems.at[1 + j], recv_sems.at[1 + j], (cx, cy, c)).wait_recv()
            passed = _remote(blk, blk, send_sems.at[4 + j], recv_sems.at[4 + j], sibling)
            passed.start()
            sent.append(passed)
        blk = slot(x, y, 1 - c)
        _remote(blk, blk, send_sems.at[0], recv_sems.at[0], sibling).wait_recv()
        for j, (cx, cy) in enumerate(chips):
            blk = slot(cx, cy, 1 - c)
            _remote(blk, blk, send_sems.at[4 + j], recv_sems.at[4 + j], sibling).wait_recv()
        for cp in sent:
            cp.wait_send()
        acc = buf[0]
        for k in range(1, 2 * N_CHIPS):
            acc = acc + buf[k]
        out_ref[...] = acc

    return pl.pallas_call(
        body, name=name, in_specs=[VMEM_WHOLE], out_specs=VMEM_WHOLE, out_shape=S((r, cols), F32),
        scratch_shapes=[pltpu.VMEM((2 * N_CHIPS, r, cols), F32), pltpu.SemaphoreType.DMA((7,)), pltpu.SemaphoreType.DMA((7,))],
        compiler_params=pltpu.CompilerParams(vmem_limit_bytes=VMEM_LIMIT),
    )(v)


def _pair_views(halves):
    def views(src, land, i, j, x, y, c, px, py):
        return src.at[pl.ds(0, N_CHIPS), _half_rows(1 - c, halves[i])], land, land
    return views


def _pair_exchange_begin(gs, name):
    lands = [lax.empty((N_CHIPS, g.shape[1] // 2, g.shape[2]), g.dtype) for g in gs]
    return _ici_begin(gs, lands, _pair_views([g.shape[1] // 2 for g in gs]), name, sibling=True)


def _pair_exchange_end(handle, after, name):
    n = (len(handle) - 3) // 2
    return _ici_end(handle, after, _pair_views([g.shape[1] // 2 for g in handle[2:2 + n]]), name, sibling=True)


def _pair_sum(g, a, name):
    n, r, cols = g.shape
    half = r // 2
    tr = _tile(half, 512, 16)
    n_blk = half // tr

    def body(c_ref, g_ref, a_ref, o_ref):
        o_ref[...] = (g_ref[...].astype(F32) + a_ref[...].astype(F32)).astype(o_ref.dtype)

    blk = pl.BlockSpec((1, tr, cols), lambda j, i, c_ref: (j, i, 0))
    return pl.pallas_call(
        body, name=name, out_shape=S((n, half, cols), BF16),
        grid_spec=pltpu.PrefetchScalarGridSpec(
            num_scalar_prefetch=1, grid=(n, n_blk),
            in_specs=[pl.BlockSpec((1, tr, cols), lambda j, i, c_ref: (j, c_ref[0] * n_blk + i, 0)), blk], out_specs=blk),
        compiler_params=_params(("parallel", "parallel")),
    )(lax.axis_index("c").reshape(1).astype(jnp.int32), g, a)


HBM = pl.BlockSpec(memory_space=pltpu.HBM)
SEM = pl.BlockSpec(memory_space=pltpu.SEMAPHORE)
SPLIT_COPY = pltpu.CompilerParams(has_side_effects=pltpu.SideEffectType.DATAFLOW_SIDE_EFFECTING)


def _peers(sibling):
    x, y, c, chips = _place()
    return x, y, c, ([(x, y, 1 - c)] if sibling else [(cx, cy, c) for cx, cy in chips])


def _ici_begin(srcs, lands, views, name, after=None, sibling=False):
    n = len(srcs)
    n_peers = 1 if sibling else N_CHIPS - 1
    extra = [] if after is None else [after]

    def body(*refs):
        s_refs, l_refs, send_sems, recv_sems, token = refs[:n], refs[n:2 * n], refs[-3 - 2 * n], refs[-2 - 2 * n], refs[-1]
        x, y, c, peers = _peers(sibling)
        for i in range(n):
            for j, peer in enumerate(peers):
                src, dst, _ = views(s_refs[i], l_refs[i], i, j, x, y, c, peer[0], peer[1])
                k = n_peers * i + j
                _remote(src, dst, send_sems.at[k], recv_sems.at[k], peer).start()
        token[...] = jnp.zeros_like(token)

    bufs = list(srcs) + list(lands)
    sems = pltpu.SemaphoreType.DMA((n * n_peers,))
    return pl.pallas_call(
        body, name=name, out_shape=(sems, sems, *[pltpu.HBM(b.shape, b.dtype) for b in bufs], S((8, LANES), F32)),
        in_specs=[HBM] * (2 * n) + [ANY] * len(extra), out_specs=(SEM, SEM, *[HBM] * (2 * n), VMEM_WHOLE),
        input_output_aliases={i: 2 + i for i in range(2 * n)}, compiler_params=SPLIT_COPY,
    )(*[pltpu.with_memory_space_constraint(b, pltpu.HBM) for b in bufs], *extra)


def _ici_end(handle, after, views, name, sibling=False):
    send_sems, recv_sems, *bufs, _ = handle
    n = len(bufs) // 2
    n_peers = 1 if sibling else N_CHIPS - 1

    def body(*refs):
        s_refs, l_refs, send_sems_, recv_sems_ = refs[:n], refs[n:2 * n], refs[2 * n], refs[2 * n + 1]
        x, y, c, peers = _peers(sibling)
        for i in range(n):
            for j, peer in enumerate(peers):
                src, _, arrival = views(s_refs[i], l_refs[i], i, j, x, y, c, peer[0], peer[1])
                k = n_peers * i + j
                cp = _remote(src, arrival, send_sems_.at[k], recv_sems_.at[k], peer)
                cp.wait_send()
                cp.wait_recv()

    out = pl.pallas_call(
        body, name=name, out_shape=[pltpu.HBM(b.shape, b.dtype) for b in bufs], in_specs=[HBM] * (2 * n) + [SEM, SEM, ANY],
        out_specs=[HBM] * (2 * n), input_output_aliases={i: i for i in range(2 * n)}, compiler_params=SPLIT_COPY,
    )(*bufs, send_sems, recv_sems, after)
    return out[:n], out[n:]


def _gather_views(halves):
    def views(src, land, i, j, x, y, c, cx, cy):
        mine = _half_rows(c, halves[i])
        return src.at[mine], land.at[2 * x + y, mine], land.at[2 * cx + cy, mine]
    return views


def _gather_begin(shards, name, after=None):
    lands = [lax.empty((N_CHIPS,) + s.shape, s.dtype) for s in shards]
    return _ici_begin(shards, lands, _gather_views([s.shape[0] // 2 for s in shards]), name, after)


def _gather_end(handle, after, name):
    n = (len(handle) - 3) // 2
    shards, lands = _ici_end(handle, after, _gather_views([s.shape[0] // 2 for s in handle[2:2 + n]]), name + "_wait")
    return _gather_finish(shards, lands, name + "_finish")


def _gather_finish(shards, lands, name):
    n = len(shards)

    def body(*refs):
        ins, l_refs, outs, send_sems, recv_sems = refs[:n], refs[n:2 * n], refs[2 * n:3 * n], refs[3 * n], refs[3 * n + 1]
        x, y, c, chips = _place()
        me, sibling = 2 * x + y, (x, y, 1 - c)
        sent = []
        for i in range(n):
            mine = _half_rows(c, shards[i].shape[0] // 2)
            for j, (cx, cy) in enumerate(chips):
                sent.append(_remote(l_refs[i].at[2 * cx + cy, mine], outs[i].at[2 * cx + cy, mine], send_sems.at[i, j], recv_sems.at[i, j], sibling))
            sent.append(_remote(ins[i], outs[i].at[me], send_sems.at[i, 3], recv_sems.at[i, 3], sibling))
        for cp in sent:
            cp.start()
        for i in range(n):
            other = _half_rows(1 - c, shards[i].shape[0] // 2)
            for j, (cx, cy) in enumerate(chips):
                rows = outs[i].at[2 * cx + cy, other]
                _remote(rows, rows, send_sems.at[i, j], recv_sems.at[i, j], sibling).wait_recv()
            own = outs[i].at[me]
            _remote(own, own, send_sems.at[i, 3], recv_sems.at[i, 3], sibling).wait_recv()
        for cp in sent:
            cp.wait_send()

    return pl.pallas_call(
        body, name=name, in_specs=[ANY] * (2 * n), out_specs=[ANY] * n, out_shape=[S(l_.shape, l_.dtype) for l_ in lands],
        input_output_aliases={n + i: i for i in range(n)},
        scratch_shapes=[pltpu.SemaphoreType.DMA((n, 4)), pltpu.SemaphoreType.DMA((n, 4))],
    )(*shards, *lands)


def _exchange_views(src, land, i, j, x, y, c, cx, cy):
    return src.at[2 * cx + cy], land.at[j], land.at[j]


def _chip_exchange_begin(ps, name):
    return _ici_begin(ps, [lax.empty((N_CHIPS - 1,) + p.shape[1:], p.dtype) for p in ps], _exchange_views, name)


def _chip_exchange_end(handle, after, name):
    return _ici_end(handle, after, _exchange_views, name)


def _quad_sum(p, q, name):
    _, h, cols = p.shape
    tr = _tile(h, 512, 16)
    n_blk = h // tr
    x, y, c, _ = _place()

    def body(s_ref, p_ref, q0_ref, q1_ref, q2_ref, o_ref):
        o_ref[...] = p_ref[0].astype(F32) + q0_ref[0].astype(F32) + q1_ref[0].astype(F32) + q2_ref[0].astype(F32)

    in_specs = [pl.BlockSpec((1, tr, cols), lambda i, s_ref: (s_ref[0], i, 0))]
    in_specs += [pl.BlockSpec((1, tr, cols), functools.partial(lambda i, s_ref, k: (k, i, 0), k=k)) for k in range(N_CHIPS - 1)]
    return pl.pallas_call(
        body, name=name, out_shape=S((2 * h, cols), F32),
        grid_spec=pltpu.PrefetchScalarGridSpec(num_scalar_prefetch=1, grid=(n_blk,), in_specs=in_specs,
                                               out_specs=pl.BlockSpec((tr, cols), lambda i, s_ref: (s_ref[1] * n_blk + i, 0))),
        compiler_params=_params(("parallel",)),
    )(jnp.stack([2 * x + y, c]).astype(jnp.int32), p, q, q, q)


def _pair_share(bufs, name):
    n = len(bufs)

    def body(*refs):
        in_refs, out_refs, send_sems, recv_sems = refs[:n], refs[n:2 * n], refs[2 * n], refs[2 * n + 1]
        x, y, c, _ = _place()
        sent = []
        for i in range(n):
            mine = _half_rows(c, bufs[i].shape[0] // 2)
            sent.append(_remote(in_refs[i].at[mine], out_refs[i].at[mine], send_sems.at[i], recv_sems.at[i], (x, y, 1 - c)))
        for cp in sent:
            cp.start()
        for i in range(n):
            other = out_refs[i].at[_half_rows(1 - c, bufs[i].shape[0] // 2)]
            _remote(other, other, send_sems.at[i], recv_sems.at[i], (x, y, 1 - c)).wait_recv()
        for cp in sent:
            cp.wait_send()

    return pl.pallas_call(body, name=name, in_specs=[ANY] * n, out_specs=[ANY] * n, out_shape=[S(b.shape, b.dtype) for b in bufs],
                          input_output_aliases={i: i for i in range(n)},
                          scratch_shapes=[pltpu.SemaphoreType.DMA((n,)), pltpu.SemaphoreType.DMA((n,))])(*bufs)


def _adamw(w, gs, m, v, name):
    depth, r, cols = w.shape
    tr, tc = _tile(r, 256, 8), cols
    if tr < 64 and r > 256 and cols % LANES == 0:
        tr, tc = r, _tile(cols, 256)

    def body(*refs):
        w_ref, m_ref, v_ref = refs[:3]
        g_refs = refs[3:3 + depth]
        go_ref, d_ref, mo_ref, vo_ref = refs[3 + depth:]
        for l in range(depth):
            @pl.when(pl.program_id(0) == l)
            def _(l=l):
                g_ = g_refs[l][...]
                m_ = ADAM_B1 * m_ref[...] + (1.0 - ADAM_B1) * g_
                v_ = ADAM_B2 * v_ref[...] + (1.0 - ADAM_B2) * jnp.square(g_)
                m_hat = m_ / (1.0 - ADAM_B1 ** ADAM_STEP)
                v_hat = v_ / (1.0 - ADAM_B2 ** ADAM_STEP)
                d_ref[...] = -ADAM_LR * (m_hat / (jnp.sqrt(v_hat) + ADAM_EPS) + ADAM_WD * w_ref[...])
                go_ref[...], mo_ref[...], vo_ref[...] = g_, m_, v_

    blk = pl.BlockSpec((None, tr, tc), lambda l, i: (l, i, 0) if tc == cols else (l, 0, i))
    g_blk = pl.BlockSpec((tr, tc), lambda l, i: (i, 0) if tc == cols else (0, i))
    return pl.pallas_call(body, name=name, grid=(depth, (r // tr) * (cols // tc)), in_specs=[blk] * 3 + [g_blk] * depth, out_specs=[blk] * 4,
                          out_shape=[S(w.shape, F32)] * 4, compiler_params=_params(("parallel", "parallel")))(w, m, v, *gs)


WEIGHTS = ("ffn1_norm", "ffn1_w1", "ffn1_w2", "mix_norm", "w_in", "b_in", "lru_conv_w", "lru_conv_b", "lru_w_gate", "lru_b_gate",
           "lru_lambda", "lru_w_out", "q_norm", "w_uq", "kv_norm", "w_ukv", "mla_w_o", "conv_dw_w", "conv_dw_b", "conv_ln_g",
           "conv_ln_b", "conv_w_out", "conv_b_out", "w_out", "ffn2_norm", "ffn2_w1", "ffn2_w2", "final_norm")
ROW_SHARDED = ("ffn1_w2", "w_out", "ffn2_w2")
COL_SHARDED = ("ffn1_w1", "w_in", "lru_w_out", "w_uq", "w_ukv", "mla_w_o", "conv_w_out", "ffn2_w1")
SMALL_SHARDED = ("lru_conv_w", "conv_dw_w")
MXU_SHARDED = tuple(n for n in WEIGHTS if n in ROW_SHARDED + COL_SHARDED)
REPLICATED = tuple(n for n in WEIGHTS if n not in MXU_SHARDED + SMALL_SHARDED)
SMALL = REPLICATED + SMALL_SHARDED
INPUTS = ("x", "positions") + WEIGHTS + ("loss_target",) + tuple("m_" + n for n in WEIGHTS) + tuple("v_" + n for n in WEIGHTS)


def _pack(arrays, dtype, cols, row_unit):
    flat = jnp.concatenate([a.astype(dtype).reshape(-1) for a in arrays])
    unit = cols * row_unit
    return jnp.pad(flat, (0, -flat.shape[0] % unit)).reshape(-1, cols)


def _unpack(flat, shapes):
    out, off = [], 0
    for shp in shapes:
        n = 1
        for s_ in shp:
            n *= s_
        out.append(flat[..., off:off + n].reshape(flat.shape[:-1] + tuple(shp)))
        off += n
    return out


def _reduce_scatter_begin(gs, tag):
    return _pair_exchange_begin(gs, tag + "_pair_exchange")


def _reduce_scatter_mid(handle, names, after, tag):
    gs, halves = _pair_exchange_end(handle, after, tag + "_pair_exchange_wait")
    pairs = [_pair_sum(g, h, f"{tag}_pair_sum_{n}") for n, g, h in zip(names, gs, halves, strict=True)]
    return _chip_exchange_begin(pairs, tag + "_chip_exchange")


def _reduce_scatter_end(handle, names, after, tag):
    pairs, others = _chip_exchange_end(handle, after, tag + "_chip_exchange_wait")
    sums = [_quad_sum(p, q, f"{tag}_chip_sum_{n}") for n, p, q in zip(names, pairs, others, strict=True)]
    return dict(zip(names, _pair_share(sums, tag + "_pair_share"), strict=True))


PARTS = (("ffn1", ("ffn1_w1", "ffn1_w2")), ("mix", ("w_in", "lru_w_out", "w_uq", "w_ukv", "mla_w_o", "conv_w_out", "w_out")),
         ("ffn2", ("ffn2_w1", "ffn2_w2")))


def _step(a):
    x, positions, target = a["x"][0], a["positions"][0], a["loss_target"][0]
    depth = a["ffn1_norm"].shape[0]
    me = 2 * lax.axis_index("x") + lax.axis_index("y")
    placed = [lax.dynamic_update_slice_in_dim(jnp.zeros(a[n].shape[:2] + (N_CHIPS,) + a[n].shape[2:], F32), 0.5 * a[n][:, :, None], me, 2)
              for n in SMALL_SHARDED]
    small_whole = _unpack(_allreduce_all(_pack(placed, F32, LANES, 8), "small_weights").reshape(-1), [p_.shape for p_ in placed])
    small_whole = {n: w.reshape(w.shape[:2] + (-1,)) for n, w in zip(SMALL_SHARDED, small_whole, strict=True)}
    base = [{n: a[n][l] for n in REPLICATED if a[n].ndim > 1} | {n: small_whole[n][l] for n in SMALL_SHARDED} for l in range(depth)]
    order = [(l, part, names) for l in range(depth) for part, names in PARTS]
    shards = lambda l, names: [a[n][l].astype(BF16) for n in names]
    rope = _rope_tables(positions, "rope_tables")
    handle = _gather_begin(shards(0, order[0][2]), "l0_ffn1_gather", small_whole[SMALL_SHARDED[0]])
    gathered = _gather_end(handle, handle[-1], "l0_ffn1_gather")
    saved = []
    for k, (l, part, names) in enumerate(order):
        p = base[l] | {n: g.reshape(-1, g.shape[-1]) if n in ROW_SHARDED else g for n, g in zip(names, gathered, strict=True)}
        dep = None
        if k + 1 < len(order):
            l2, part2, names2 = order[k + 1]
            handle = _gather_begin(shards(l2, names2), f"l{l2}_{part2}_gather", handle[-1])
            dep = handle[-1][0, 0]
        x, s = _part_fwd(part, x, p, rope, f"l{l}_{part}", dep)
        saved.append(s)
        if dep is not None:
            gathered = _gather_end(handle, x, f"l{l2}_{part2}_gather")
    loss, dx, dfinal = _loss_head(x, target, a["final_norm"][None, :], "loss_head")
    loss = lax.psum(loss, MESH_AXES)
    grads, shard_grads, pending = [{} for _ in range(depth)], [{} for _ in range(depth)], None
    for l, part, names in reversed(order):
        flying = []

        def mid(after, pending=pending, flying=flying):
            flying.append(_reduce_scatter_mid(pending[0], pending[2], after, pending[3]))
            return flying[0][-1][0, 0]

        deps = (None, None) if pending is None else (pending[0][-1][0, 0], mid)
        dx, g = _part_bwd(part, dx, saved.pop(), rope, f"l{l}_{part}", *deps)
        if pending is not None:
            shard_grads[pending[1]].update(_reduce_scatter_end(flying[0], pending[2], dx, pending[3]))
        pending = (_reduce_scatter_begin([g[n] for n in names], f"l{l}_{part}_grad"), l, names, f"l{l}_{part}_grad")
        grads[l].update({n: g[n] for n in g if n in SMALL})
    last = _reduce_scatter_mid(pending[0], pending[2], pending[0][-1], pending[3])
    shard_grads[pending[1]].update(_reduce_scatter_end(last, pending[2], last[-1], pending[3]))
    small = [jnp.stack([g[n] for g in grads]) if a[n].ndim > 1 else dfinal[0] for n in SMALL]
    g_small = _unpack(_allreduce_all(_pack(small, F32, LANES, 256), "grad_allreduce").reshape(-1), [s_.shape for s_ in small])
    g_small = [lax.dynamic_slice_in_dim(g, me * a[n].shape[-1], a[n].shape[-1], 2) if n in SMALL_SHARDED else g
               for n, g in zip(SMALL, g_small, strict=True)]
    g, delta, new_m, new_v = {}, {}, {}, {}
    for n in MXU_SHARDED:
        flip = (lambda t: jnp.swapaxes(t, -1, -2)) if (a[n].shape[-1] % LANES and not a[n].shape[-2] % LANES) else (lambda t: t)
        res = _adamw(flip(a[n]), [flip(shard_grads[l][n]) for l in range(depth)], flip(a["m_" + n]), flip(a["v_" + n]), "adamw_" + n)
        g[n], delta[n], new_m[n], new_v[n] = [flip(r) for r in res]
    shapes = [a[n].shape for n in SMALL]
    packed = [_pack([a[pre + n] for n in SMALL], F32, LANES, 256)[None] for pre in ("", "m_", "v_")]
    res = _adamw(packed[0], [_pack(g_small, F32, LANES, 256)], packed[1], packed[2], "adamw_small")
    for out, r in zip((g, delta, new_m, new_v), res, strict=True):
        out.update(zip(SMALL, _unpack(r.reshape(-1), shapes), strict=True))
    return (loss, dx[None], *[g[n] for n in WEIGHTS], *[delta[n] for n in WEIGHTS], *[new_m[n] for n in WEIGHTS], *[new_v[n] for n in WEIGHTS])


def kernel(x, positions, ffn1_norm, ffn1_w1, ffn1_w2, mix_norm, w_in, b_in, lru_conv_w, lru_conv_b, lru_w_gate, lru_b_gate, lru_lambda, lru_w_out, q_norm, w_uq, kv_norm, w_ukv, mla_w_o, conv_dw_w, conv_dw_b, conv_ln_g, conv_ln_b, conv_w_out, conv_b_out, w_out, ffn2_norm, ffn2_w1, ffn2_w2, final_norm, loss_target, m_ffn1_norm, m_ffn1_w1, m_ffn1_w2, m_mix_norm, m_w_in, m_b_in, m_lru_conv_w, m_lru_conv_b, m_lru_w_gate, m_lru_b_gate, m_lru_lambda, m_lru_w_out, m_q_norm, m_w_uq, m_kv_norm, m_w_ukv, m_mla_w_o, m_conv_dw_w, m_conv_dw_b, m_conv_ln_g, m_conv_ln_b, m_conv_w_out, m_conv_b_out, m_w_out, m_ffn2_norm, m_ffn2_w1, m_ffn2_w2, m_final_norm, v_ffn1_norm, v_ffn1_w1, v_ffn1_w2, v_mix_norm, v_w_in, v_b_in, v_lru_conv_w, v_lru_conv_b, v_lru_w_gate, v_lru_b_gate, v_lru_lambda, v_lru_w_out, v_q_norm, v_w_uq, v_kv_norm, v_w_ukv, v_mla_w_o, v_conv_dw_w, v_conv_dw_b, v_conv_ln_g, v_conv_ln_b, v_conv_w_out, v_conv_b_out, v_w_out, v_ffn2_norm, v_ffn2_w1, v_ffn2_w2, v_final_norm):
    return _step(dict(zip(INPUTS, (x, positions, ffn1_norm, ffn1_w1, ffn1_w2, mix_norm, w_in, b_in, lru_conv_w, lru_conv_b, lru_w_gate, lru_b_gate, lru_lambda, lru_w_out, q_norm, w_uq, kv_norm, w_ukv, mla_w_o, conv_dw_w, conv_dw_b, conv_ln_g, conv_ln_b, conv_w_out, conv_b_out, w_out, ffn2_norm, ffn2_w1, ffn2_w2, final_norm, loss_target, m_ffn1_norm, m_ffn1_w1, m_ffn1_w2, m_mix_norm, m_w_in, m_b_in, m_lru_conv_w, m_lru_conv_b, m_lru_w_gate, m_lru_b_gate, m_lru_lambda, m_lru_w_out, m_q_norm, m_w_uq, m_kv_norm, m_w_ukv, m_mla_w_o, m_conv_dw_w, m_conv_dw_b, m_conv_ln_g, m_conv_ln_b, m_conv_w_out, m_conv_b_out, m_w_out, m_ffn2_norm, m_ffn2_w1, m_ffn2_w2, m_final_norm, v_ffn1_norm, v_ffn1_w1, v_ffn1_w2, v_mix_norm, v_w_in, v_b_in, v_lru_conv_w, v_lru_conv_b, v_lru_w_gate, v_lru_b_gate, v_lru_lambda, v_lru_w_out, v_q_norm, v_w_uq, v_kv_norm, v_w_ukv, v_mla_w_o, v_conv_dw_w, v_conv_dw_b, v_conv_ln_g, v_conv_ln_b, v_conv_w_out, v_conv_b_out, v_w_out, v_ffn2_norm, v_ffn2_w1, v_ffn2_w2, v_final_norm), strict=True)))
```

```python
import functools

import jax
import jax.numpy as jnp
from jax import lax
from jax.experimental import pallas as pl
from jax.experimental.pallas import tpu as pltpu

F32, BF16 = jnp.float32, jnp.bfloat16
S = jax.ShapeDtypeStruct

LANES = 128
VMEM_LIMIT = 56 * 2**20
NORM_EPS = 1e-6
LRU_C = 8.0
MLA_HEADS = 8
QK_NOPE, QK_ROPE, V_HEAD = 64, 32, 64
HEAD_PAD = 128
ROPE_THETA = 10000.0
ADAM_LR, ADAM_B1, ADAM_B2, ADAM_EPS, ADAM_WD, ADAM_STEP = 0.001, 0.9, 0.999, 1e-08, 0.01, 10
MESH_AXES = ("x", "y", "c")
N_CHIPS = 4
GRAD_DTYPE = BF16
NT =(((1,), (1,)), ((), ()))
TN = (((0,), (0,)), ((), ()))
NN = (((1,), (0,)), ((), ()))


def _tile(n, cap, unit=LANES):
    best = None
    for d in range(unit, min(n, cap) + 1, unit):
        if n % d == 0:
            best = d
    return best if best is not None else n


def _params(sem):
    return pltpu.CompilerParams(dimension_semantics=sem, vmem_limit_bytes=VMEM_LIMIT)


def _mm(a, b, mode="nn", out_dtype=F32, bias=None, res=None, alpha=1.0, a_blocks=False, b_chips=False, out_chips=False, name="mm"):
    units_n, units_k = [], []
    if b_chips:
        blocks, rows, c = b.shape
        b_shape = (rows, blocks * c)
        (units_k if mode == "nt" else units_n).append(c)
    else:
        b_shape = b.shape
    if a_blocks:
        blocks_a, rows_a, c_a = a.shape
        a_shape = (rows_a, blocks_a * c_a)
        units_k.append(c_a)
    else:
        a_shape = a.shape
    if mode == "nn":
        (m, k), (k2, n), dims = a_shape, b_shape, NN
    elif mode == "nt":
        (m, k), (n, k2), dims = a_shape, b_shape, NT
    else:
        (k, m), (k2, n), dims = a_shape, b_shape, TN
    assert k == k2 and not (a_blocks and mode == "tn"), (name, a.shape, b.shape, mode)
    if out_chips:
        units_n.append(n // N_CHIPS)
    tm = _tile(m, 512)
    tn = _tile(min(units_n) if units_n else n, 1536)
    tk = k if (k <= 3072 and not units_k) else _tile(min(units_k) if units_k else k, 3072)
    nk = k // tk
    assert all(u % tn == 0 for u in units_n) and all(u % tk == 0 for u in units_k), (name, units_n, units_k, tn, tk)
    if a_blocks:
        a_spec = pl.BlockSpec((None, tm, tk), functools.partial(lambda i, j, kk, per: (kk // per, i, kk % per), per=c_a // tk))
    elif mode == "tn":
        a_spec = pl.BlockSpec((tk, tm), lambda i, j, kk: (kk, i))
    else:
        a_spec = pl.BlockSpec((tm, tk), lambda i, j, kk: (i, kk))
    if b_chips and mode != "nt":
        b_spec = pl.BlockSpec((None, tk, tn), functools.partial(lambda i, j, kk, per: (j // per, kk, j % per), per=c // tn))
    elif b_chips:
        b_spec = pl.BlockSpec((None, tn, tk), functools.partial(lambda i, j, kk, per: (kk // per, j, kk % per), per=c // tk))
    elif mode == "nt":
        b_spec = pl.BlockSpec((tn, tk), lambda i, j, kk: (j, kk))
    else:
        b_spec = pl.BlockSpec((tk, tn), lambda i, j, kk: (kk, j))
    if out_chips:
        out_spec = pl.BlockSpec((None, tm, tn), functools.partial(lambda i, j, kk, per: (j // per, i, j % per), per=n // N_CHIPS // tn))
        out_shape = S((N_CHIPS, m, n // N_CHIPS), out_dtype)
    else:
        out_spec = pl.BlockSpec((tm, tn), lambda i, j, kk: (i, j))
        out_shape = S((m, n), out_dtype)
    operands, in_specs = [a, b], [a_spec, b_spec]
    if bias is not None:
        operands.append(bias)
        in_specs.append(pl.BlockSpec((1, tn), lambda i, j, kk: (0, j)))
    if res is not None:
        operands.append(res)
        in_specs.append(pl.BlockSpec((tm, tn), lambda i, j, kk: (i, j)))

    def body(*refs):
        a_ref, b_ref = refs[0], refs[1]
        pos = 2
        bias_ref = res_ref = None
        if bias is not None:
            bias_ref, pos = refs[pos], pos + 1
        if res is not None:
            res_ref, pos = refs[pos], pos + 1
        o_ref = refs[pos]
        part = lax.dot_general(a_ref[...].astype(BF16), b_ref[...].astype(BF16), dims, preferred_element_type=F32)

        def finish(acc):
            out = acc if alpha == 1.0 else acc * alpha
            if bias_ref is not None:
                out = out + bias_ref[...]
            if res_ref is not None:
                out = out + res_ref[...]
            o_ref[...] = out.astype(o_ref.dtype)

        if nk == 1:
            finish(part)
        else:
            acc_ref = refs[pos + 1]
            kk = pl.program_id(2)

            @pl.when(kk == 0)
            def _():
                acc_ref[...] = part

            @pl.when(kk > 0)
            def _():
                acc_ref[...] += part

            @pl.when(kk == nk - 1)
            def _():
                finish(acc_ref[...])

    return pl.pallas_call(
        body, name=name, grid=(m // tm, n // tn, nk), in_specs=in_specs, out_specs=out_spec, out_shape=out_shape,
        scratch_shapes=[pltpu.VMEM((tm, tn), F32)] if nk > 1 else [],
        compiler_params=_params(("parallel", "parallel", "arbitrary")),
    )(*operands)


def _rowwise(fn, rows, params=(), outs=(), accs=(), tt=256, name="rowwise"):
    rows = [r if isinstance(r, tuple) else (r, r.shape[1], 0) for r in rows]
    t = rows[0][0].shape[0]
    tt = min(tt, t)
    n_rows, n_par, n_out = len(rows), len(params), len(outs)
    in_specs = [pl.BlockSpec((tt, w), functools.partial(lambda i, cb: (i, cb), cb=cb)) for (_, w, cb) in rows]
    in_specs += [pl.BlockSpec(p.shape, functools.partial(lambda i, nd: (0,) * nd, nd=p.ndim)) for p in params]
    out_shape = [S((t, w), dt) for (w, dt) in outs] + [S(shape, dt) for (shape, dt) in accs]
    out_specs = [pl.BlockSpec((tt, w), lambda i: (i, 0)) for (w, _) in outs]
    out_specs += [pl.BlockSpec(shape, functools.partial(lambda i, nd: (0,) * nd, nd=len(shape))) for (shape, _) in accs]

    def body(*refs):
        vals = [r[...] for r in refs[:n_rows + n_par]]
        o_vals, a_vals = fn(*vals)
        o_refs = refs[n_rows + n_par:n_rows + n_par + n_out]
        a_refs = refs[n_rows + n_par + n_out:]
        for ref, val in zip(o_refs, o_vals, strict=True):
            ref[...] = val.astype(ref.dtype)
        i = pl.program_id(0)
        for ref, val in zip(a_refs, a_vals, strict=True):
            @pl.when(i == 0)
            def _(ref=ref, val=val):
                ref[...] = val.astype(ref.dtype)

            @pl.when(i > 0)
            def _(ref=ref, val=val):
                ref[...] += val.astype(ref.dtype)

    res = pl.pallas_call(
        body, name=name, grid=(t // tt,), in_specs=in_specs, out_specs=out_specs, out_shape=out_shape,
        compiler_params=_params(("arbitrary",) if accs else ("parallel",)),
    )(*[r[0] for r in rows], *params)
    return res


def _rms(x, g):
    x = x.astype(F32)
    return x * lax.rsqrt(jnp.mean(x * x, axis=-1, keepdims=True) + NORM_EPS) * g


def _layer_norm_silu(x, g, b):
    mu = jnp.mean(x, axis=-1, keepdims=True)
    var = jnp.mean(jnp.square(x - mu), axis=-1, keepdims=True)
    return jax.nn.silu((x - mu) * lax.rsqrt(var + NORM_EPS) * g + b)


def _neg_expm1(z):
    series = -z * (1.0 + z * (0.5 + z * (1.0 / 6.0 + z * (1.0 / 24.0 + z * (1.0 / 120.0)))))
    return jnp.where(z > -0.05, series, 1.0 - jnp.exp(z))


def _shift_down(x, s, fill=0.0):
    if s == 0:
        return x
    row = lax.broadcasted_iota(jnp.int32, x.shape, 0)
    return jnp.where(row >= s, pltpu.roll(x, s, 0), fill)


def _shift_up(x, s, fill=0.0):
    if s == 0:
        return x
    t = x.shape[0]
    row = lax.broadcasted_iota(jnp.int32, x.shape, 0)
    return jnp.where(row < t - s, pltpu.roll(x, t - s, 0), fill)


def _scan(a, u, shift):
    t, d = a.shape[0], 1
    while d < t:
        u = u + a * shift(u, d, 0.0)
        if 2 * d < t:
            a = a * shift(a, d, 1.0)
        d *= 2
    return u


def _lru_gates(xa, wr, wi, br, bi, lam):
    xb = xa.astype(BF16)
    r = jax.nn.sigmoid(jnp.dot(xb, wr.astype(BF16), preferred_element_type=F32) + br)
    i = jax.nn.sigmoid(jnp.dot(xb, wi.astype(BF16), preferred_element_type=F32) + bi)
    log_a = -LRU_C * r * jax.nn.softplus(-lam)
    return jnp.exp(log_a), jnp.sqrt(_neg_expm1(2.0 * log_a)) * (i * xa)


def _conv_fwd(x, w_ref, b, width):
    y = b + w_ref[pl.ds(width - 1, 1), :] * x
    for j in range(width - 1):
        y = y + w_ref[pl.ds(j, 1), :] * _shift_down(x, width - 1 - j)
    return y


def _conv_bwd(x, dy, w_ref, dw_ref, width):
    dx = w_ref[pl.ds(width - 1, 1), :] * dy
    dw_ref[pl.ds(width - 1, 1), :] = jnp.sum(dy * x, axis=0, keepdims=True)
    for j in range(width - 1):
        s = width - 1 - j
        dx = dx + w_ref[pl.ds(j, 1), :] * _shift_up(dy, s)
        dw_ref[pl.ds(j, 1), :] = jnp.sum(dy * _shift_down(x, s), axis=0, keepdims=True)
    return dx


def _rope(z, c, s1, s2):
    return z * c + pltpu.roll(z, HEAD_PAD - QK_ROPE // 2, 1) * s1 + pltpu.roll(z, QK_ROPE // 2, 1) * s2


def _rope_t(d, c, s1, s2):
    return d * c + pltpu.roll(d * s1, QK_ROPE // 2, 1) + pltpu.roll(d * s2, HEAD_PAD - QK_ROPE // 2, 1)


def _heads(z):
    return [z[:, h * HEAD_PAD:(h + 1) * HEAD_PAD] for h in range(z.shape[1] // HEAD_PAD)]


def _chan_spec(t, c_off=0):
    return pl.BlockSpec((t, LANES), lambda c: (0, c_off + c))


def _lru_specs(t, n_tiles, width):
    vec = pl.BlockSpec((1, LANES), lambda c: (0, c))
    mat = pl.BlockSpec((1, LANES, LANES), lambda c: (c, 0, 0))
    return [_chan_spec(t), _chan_spec(t, n_tiles), pl.BlockSpec((width, LANES), lambda c: (0, c)), vec, mat, mat, vec, vec, vec]


def _lru_fwd(pa, cw, cb, wr, wi, br, bi, lam, name):
    t, w = pa.shape[0], pa.shape[1] // 2
    n_tiles, width = w // LANES, cw.shape[0]

    def body(x_ref, g_ref, cw_ref, cb_ref, wr_ref, wi_ref, br_ref, bi_ref, lam_ref, y_ref):
        xa = _conv_fwd(x_ref[...], cw_ref, cb_ref[...], width)
        a, u = _lru_gates(xa, wr_ref[0], wi_ref[0], br_ref[...], bi_ref[...], lam_ref[...])
        h = _scan(a, u, _shift_down)
        y_ref[...] = (h * jax.nn.gelu(g_ref[...])).astype(y_ref.dtype)

    return pl.pallas_call(
        body, name=name, grid=(n_tiles,), in_specs=_lru_specs(t, n_tiles, width), out_specs=_chan_spec(t),
        out_shape=S((t, w), BF16), compiler_params=_params(("parallel",)),
    )(pa, pa, cw, cb, wr, wi, br, bi, lam)


def _lru_bwd(pa, dy, cw, cb, wr, wi, br, bi, lam, name):
    t, w = pa.shape[0], pa.shape[1] // 2
    n_tiles, width = w // LANES, cw.shape[0]

    def body(x_ref, g_ref, cw_ref, cb_ref, wr_ref, wi_ref, br_ref, bi_ref, lam_ref, dy_ref,
             dx_ref, dg_ref, dcw_ref, dcb_ref, dwr_ref, dwi_ref, dbr_ref, dbi_ref, dlam_ref, sx_ref, sg_ref):
        x = x_ref[...]
        xa = _conv_fwd(x, cw_ref, cb_ref[...], width)
        (a, u), gates_vjp = jax.vjp(_lru_gates, xa, wr_ref[0], wi_ref[0], br_ref[...], bi_ref[...], lam_ref[...])
        h = _scan(a, u, _shift_down)
        _, out_vjp = jax.vjp(lambda h_, g_: h_ * jax.nn.gelu(g_), h, g_ref[...])
        dh, dgate = out_vjp(dy_ref[...])
        adj = _scan(_shift_up(a, 1), dh, _shift_up)
        dxa, dwr, dwi, dbr, dbi, dlam = gates_vjp((adj * _shift_down(h, 1), adj))
        dx = _conv_bwd(x, dxa, cw_ref, dcw_ref, width)
        dcb_ref[...] = jnp.sum(dxa, axis=0, keepdims=True)
        dx_ref[...] = dx.astype(dx_ref.dtype)
        dg_ref[...] = dgate.astype(dg_ref.dtype)
        sx_ref[...] = jnp.sum(dx, axis=0, keepdims=True)
        sg_ref[...] = jnp.sum(dgate, axis=0, keepdims=True)
        dwr_ref[0], dwi_ref[0] = dwr, dwi
        dbr_ref[...], dbi_ref[...], dlam_ref[...] = dbr, dbi, dlam

    vec = pl.BlockSpec((1, LANES), lambda c: (0, c))
    mat = pl.BlockSpec((1, LANES, LANES), lambda c: (c, 0, 0))
    vec_s, mat_s = S((1, w), F32), S((n_tiles, LANES, LANES), F32)
    return pl.pallas_call(
        body, name=name, grid=(n_tiles,), in_specs=_lru_specs(t, n_tiles, width) + [_chan_spec(t)],
        out_specs=[_chan_spec(t), _chan_spec(t), pl.BlockSpec((width, LANES), lambda c: (0, c)), vec, mat, mat, vec, vec, vec, vec, vec],
        out_shape=[S((t, w), BF16), S((t, w), BF16), S((width, w), F32), vec_s, mat_s, mat_s, vec_s, vec_s, vec_s, vec_s, vec_s],
        compiler_params=_params(("parallel",)),
    )(pa, pa, cw, cb, wr, wi, br, bi, lam, dy)


def _glu_conv_fwd(pc, cw, cb, name):
    t, c = pc.shape[0], pc.shape[1] // 2
    n_tiles, width = c // LANES, cw.shape[0]

    def body(v_ref, g_ref, cw_ref, cb_ref, y_ref):
        y_ref[...] = _conv_fwd(v_ref[...] * jax.nn.sigmoid(g_ref[...]), cw_ref, cb_ref[...], width)

    return pl.pallas_call(
        body, name=name, grid=(n_tiles,),
        in_specs=[_chan_spec(t), _chan_spec(t, n_tiles), pl.BlockSpec((width, LANES), lambda i: (0, i)), pl.BlockSpec((1, LANES), lambda i: (0, i))],
        out_specs=_chan_spec(t), out_shape=S((t, c), F32), compiler_params=_params(("parallel",)),
    )(pc, pc, cw, cb)


def _glu_conv_bwd(pc, dy, cw, name):
    t, c = pc.shape[0], pc.shape[1] // 2
    n_tiles, width = c // LANES, cw.shape[0]

    def body(v_ref, g_ref, cw_ref, dy_ref, dv_ref, dg_ref, dcw_ref, dcb_ref, sv_ref, sg_ref):
        glu = lambda v_, g_: v_ * jax.nn.sigmoid(g_)
        x, glu_vjp = jax.vjp(glu, v_ref[...], g_ref[...])
        dy_ = dy_ref[...]
        dv, dg = glu_vjp(_conv_bwd(x, dy_, cw_ref, dcw_ref, width))
        dcb_ref[...] = jnp.sum(dy_, axis=0, keepdims=True)
        dv_ref[...] = dv.astype(dv_ref.dtype)
        dg_ref[...] = dg.astype(dg_ref.dtype)
        sv_ref[...] = jnp.sum(dv, axis=0, keepdims=True)
        sg_ref[...] = jnp.sum(dg, axis=0, keepdims=True)

    vec = pl.BlockSpec((1, LANES), lambda i: (0, i))
    wspec = pl.BlockSpec((width, LANES), lambda i: (0, i))
    return pl.pallas_call(
        body, name=name, grid=(n_tiles,), in_specs=[_chan_spec(t), _chan_spec(t, n_tiles), wspec, _chan_spec(t)],
        out_specs=[_chan_spec(t), _chan_spec(t), wspec, vec, vec, vec],
        out_shape=[S((t, c), BF16), S((t, c), BF16), S((width, c), F32), S((1, c), F32), S((1, c), F32), S((1, c), F32)],
        compiler_params=_params(("parallel",)),
    )(pc, pc, cw, dy)


def _softmax_rows(q, k, causal, scale):
    s = lax.dot_general(q, k, NT, preferred_element_type=F32) * scale
    s = jnp.where(causal, s, jnp.finfo(F32).min)
    p = jnp.exp(s - jnp.max(s, axis=-1, keepdims=True))
    return p / jnp.sum(p, axis=-1, keepdims=True)


def _attn_specs(t, tq):
    return [pl.BlockSpec((tq, 2 * HEAD_PAD), lambda hp, i: (i, hp)), pl.BlockSpec((t, 2 * HEAD_PAD), lambda hp, i: (0, hp)),
            pl.BlockSpec((t, 2 * V_HEAD), lambda hp, i: (0, hp))]


def _causal(n, tq):
    row = lax.broadcasted_iota(jnp.int32, (tq, (n + 1) * tq), 0) + n * tq
    col = lax.broadcasted_iota(jnp.int32, (tq, (n + 1) * tq), 1)
    return col <= row


def _per_query_block(n_blocks, fn):
    for n in range(n_blocks):
        pl.when(pl.program_id(1) == n)(functools.partial(fn, n))


def _attn_fwd(qh, kh, v, name):
    t = qh.shape[0]
    tq = min(256, t)
    scale = (QK_NOPE + QK_ROPE) ** -0.5

    def body(q_ref, k_ref, v_ref, o_ref):
        def block(n):
            keys = (n + 1) * tq
            causal = _causal(n, tq)
            lane = lax.broadcasted_iota(jnp.int32, (keys, 2 * V_HEAD), 1)
            vv = v_ref[0:keys, :]
            acc = jnp.zeros((tq, 2 * V_HEAD), F32)
            for e in range(2):
                p = _softmax_rows(q_ref[:, e * HEAD_PAD:(e + 1) * HEAD_PAD], k_ref[0:keys, e * HEAD_PAD:(e + 1) * HEAD_PAD], causal, scale)
                ve = jnp.where((lane >= V_HEAD * e) & (lane < V_HEAD * (e + 1)), vv, jnp.zeros_like(vv))
                acc = acc + jnp.dot(p.astype(BF16), ve, preferred_element_type=F32)
            o_ref[...] = acc.astype(o_ref.dtype)

        _per_query_block(t // tq, block)

    return pl.pallas_call(
        body, name=name, grid=(MLA_HEADS // 2, t // tq), in_specs=_attn_specs(t, tq),
        out_specs=pl.BlockSpec((tq, 2 * V_HEAD), lambda hp, i: (i, hp)), out_shape=S((t, MLA_HEADS * V_HEAD), BF16),
        compiler_params=_params(("parallel", "parallel")),
    )(qh, kh, v)


def _attn_bwd(qh, kh, v, do, name):
    t = qh.shape[0]
    tq = min(256, t)
    scale = (QK_NOPE + QK_ROPE) ** -0.5

    def body(q_ref, k_ref, v_ref, do_ref, dq_ref, dk_ref, dv_ref):
        def block(n):
            keys = (n + 1) * tq
            causal = _causal(n, tq)
            lane = lax.broadcasted_iota(jnp.int32, (tq, 2 * V_HEAD), 1)
            vv, dd = v_ref[0:keys, :], do_ref[...]
            dqs, dks = [], []
            dv = jnp.zeros((keys, 2 * V_HEAD), F32)
            for e in range(2):
                q, k = q_ref[:, e * HEAD_PAD:(e + 1) * HEAD_PAD], k_ref[0:keys, e * HEAD_PAD:(e + 1) * HEAD_PAD]
                p = _softmax_rows(q, k, causal, scale)
                de = jnp.where((lane >= V_HEAD * e) & (lane < V_HEAD * (e + 1)), dd, jnp.zeros_like(dd))
                dp = lax.dot_general(de, vv, NT, preferred_element_type=F32)
                ds = (p * (dp - jnp.sum(p * dp, axis=-1, keepdims=True)) * scale).astype(BF16)
                dqs.append(jnp.dot(ds, k, preferred_element_type=F32))
                dks.append(lax.dot_general(ds, q, TN, preferred_element_type=F32))
                dv = dv + lax.dot_general(p.astype(BF16), de, TN, preferred_element_type=F32)
            dq_ref[...] = jnp.concatenate(dqs, axis=-1)
            dk = jnp.concatenate(dks, axis=-1)
            if n == 0:
                dk_ref[0:keys, :], dv_ref[0:keys, :] = dk, dv
                if keys < t:
                    dk_ref[keys:t, :] = jnp.zeros((t - keys, 2 * HEAD_PAD), F32)
                    dv_ref[keys:t, :] = jnp.zeros((t - keys, 2 * V_HEAD), F32)
            else:
                dk_ref[0:keys, :] += dk
                dv_ref[0:keys, :] += dv

        _per_query_block(t // tq, block)

    return pl.pallas_call(
        body, name=name, grid=(MLA_HEADS // 2, t // tq),
        in_specs=_attn_specs(t, tq) + [pl.BlockSpec((tq, 2 * V_HEAD), lambda hp, i: (i, hp))],
        out_specs=[pl.BlockSpec((tq, 2 * HEAD_PAD), lambda hp, i: (i, hp)), pl.BlockSpec((t, 2 * HEAD_PAD), lambda hp, i: (0, hp)),
                   pl.BlockSpec((t, 2 * V_HEAD), lambda hp, i: (0, hp))],
        out_shape=[S((t, MLA_HEADS * HEAD_PAD), F32), S((t, MLA_HEADS * HEAD_PAD), F32), S((t, MLA_HEADS * V_HEAD), F32)],
        compiler_params=_params(("parallel", "arbitrary")),
    )(qh, kh, v, do)


def _swiglu(gate, up):
    return jax.nn.silu(gate) * up


def _ffn_up(h, w1, name):
    t, d = h.shape
    chips, _, c = w1.shape
    f = chips * c // 2
    tm, tn = _tile(t, 512), _tile(c, 1536)
    per = c // tn

    def body(h_ref, wg_ref, wu_ref, gu_ref, a_ref):
        hb = h_ref[...]
        gate = jnp.dot(hb, wg_ref[...], preferred_element_type=F32).astype(BF16)
        up = jnp.dot(hb, wu_ref[...], preferred_element_type=F32).astype(BF16)
        gu_ref[0], gu_ref[1] = gate, up
        a_ref[...] = _swiglu(gate.astype(F32), up.astype(F32)).astype(a_ref.dtype)

    w_spec = lambda first: pl.BlockSpec((None, d, tn), lambda i, j: (first + j // per, 0, j % per))
    return pl.pallas_call(
        body, name=name, grid=(t // tm, f // tn), in_specs=[pl.BlockSpec((tm, d), lambda i, j: (i, 0)), w_spec(0), w_spec(chips // 2)],
        out_specs=[pl.BlockSpec((2, tm, tn), lambda i, j: (0, i, j)), pl.BlockSpec((tm, tn), lambda i, j: (i, j))],
        out_shape=[S((2, t, f), BF16), S((t, f), BF16)], compiler_params=_params(("parallel", "parallel")),
    )(h, w1, w1)


def _ffn_dgu(dy, w2, gu, bias, name):
    t, d = dy.shape
    f = w2.shape[0]
    tm, tn = _tile(t, 512), _tile(f, 1536)

    def body(*refs):
        dy_ref, w2_ref, gu_ref, o_ref = refs[0], refs[1], refs[2], refs[-1]
        da = 0.5 * lax.dot_general(dy_ref[...].astype(BF16), w2_ref[...], NT, preferred_element_type=F32)
        if bias is not None:
            da = da + refs[3][...]
        _, vjp = jax.vjp(_swiglu, gu_ref[0].astype(F32), gu_ref[1].astype(F32))
        dgate, dup = vjp(da)
        o_ref[0], o_ref[1] = dgate.astype(o_ref.dtype), dup.astype(o_ref.dtype)

    blk = pl.BlockSpec((2, tm, tn), lambda i, j: (0, i, j))
    in_specs = [pl.BlockSpec((tm, d), lambda i, j: (i, 0)), pl.BlockSpec((tn, d), lambda i, j: (j, 0)), blk]
    if bias is not None:
        in_specs.append(pl.BlockSpec((1, tn), lambda i, j: (0, j)))
    return pl.pallas_call(body, name=name, grid=(t // tm, f // tn), in_specs=in_specs, out_specs=blk, out_shape=S((2, t, f), BF16),
                          compiler_params=_params(("parallel", "parallel")))(dy, w2, gu, *([] if bias is None else [bias]))


def _ffn_fwd(x, g, w1, w2, tag):
    h, = _rowwise(lambda x_, g_: ([_rms(x_, g_)], []), [x], [g], outs=[(x.shape[1], BF16)], name=tag + "_rms")
    gu, a = _ffn_up(h, w1, tag + "_up")
    return _mm(a, w2, res=x, alpha=0.5, name=tag + "_down"), (x, h, gu, a)


def _after(dep, n):
    return None if dep is None else jnp.zeros((1, n), F32) + dep


def _ffn_bwd(dy, saved, g, w1, w2, tag, dep=None, mid=None):
    x, h, gu, a = saved
    d, f = x.shape[1], w2.shape[0]
    dgu = _ffn_dgu(dy, w2, gu, _after(dep, f), tag + "_dgu")
    dep = None if mid is None else mid(dgu)
    dw2 = _mm(a, dy, "tn", GRAD_DTYPE, alpha=0.5, bias=_after(dep, d), name=tag + "_dw2")
    dw1 = _mm(h, dgu, "tn", GRAD_DTYPE, bias=_after(dep, 2 * f), b_chips=True, out_chips=True, name=tag + "_dw1")
    dx, dg = _ffn_dx(dgu, w1, x, g, dy, _after(dep, d), tag + "_dx")
    return dx, (dg, dw1, dw2)


def _ffn_dx(dgu, w1, x, g, dres, bias, name):
    blocks, t, f = dgu.shape
    chips, d, c = w1.shape
    tm, tk = _tile(t, 512), _tile(min(c, f), 3072)
    nk = blocks * f // tk

    def body(*refs):
        dgu_ref, w_ref, x_ref, g_ref, dres_ref = refs[:5]
        dx_ref, dg_ref, acc_ref = refs[-3:]
        i, kk = pl.program_id(0), pl.program_id(1)
        part = lax.dot_general(dgu_ref[...], w_ref[...], NT, preferred_element_type=F32)

        @pl.when(kk == 0)
        def _():
            acc_ref[...] = part

        @pl.when(kk > 0)
        def _():
            acc_ref[...] += part

        @pl.when(kk == nk - 1)
        def _():
            dh = acc_ref[...] if bias is None else acc_ref[...] + refs[5][...]
            _, vjp = jax.vjp(_rms, x_ref[...], g_ref[...])
            dx, dg = vjp(dh)
            dx_ref[...] = dx + dres_ref[...]

            @pl.when(i == 0)
            def _():
                dg_ref[...] = dg

            @pl.when(i > 0)
            def _():
                dg_ref[...] += dg

    row = pl.BlockSpec((tm, d), lambda i, kk: (i, 0))
    vec = pl.BlockSpec((1, d), lambda i, kk: (0, 0))
    in_specs = [pl.BlockSpec((None, tm, tk), functools.partial(lambda i, kk, per: (kk // per, i, kk % per), per=f // tk)),
                pl.BlockSpec((None, d, tk), functools.partial(lambda i, kk, per: (kk // per, 0, kk % per), per=c // tk)), row, vec, row]
    operands = [dgu, w1, x, g, dres]
    if bias is not None:
        in_specs.append(vec)
        operands.append(bias)
    return pl.pallas_call(body, name=name, grid=(t // tm, nk), in_specs=in_specs, out_specs=[row, vec],
                          out_shape=[S((t, d), F32), S((1, d), F32)], scratch_shapes=[pltpu.VMEM((tm, d), F32)],
                          compiler_params=_params(("arbitrary", "arbitrary")))(*operands)


def _rms_bwd(x, dh, dres, g, name):
    def fn(x_, dh_, dres_, g_):
        _, vjp = jax.vjp(_rms, x_, g_)
        dx, dg = vjp(dh_)
        return [dx + dres_], [dg]

    return _rowwise(fn, [x, dh, dres], [g], outs=[(x.shape[1], F32)], accs=[((1, x.shape[1]), F32)], name=name)


def _rope_tables(positions, name):
    half = QK_ROPE // 2
    inv = ROPE_THETA ** (-jnp.arange(0, QK_ROPE, 2, dtype=F32) / QK_ROPE)
    inv_lanes = jnp.zeros((1, HEAD_PAD), F32).at[0, QK_NOPE:QK_NOPE + QK_ROPE].set(jnp.tile(inv, 2))

    def fn(pos, inv_):
        ang = pos.astype(F32) * inv_
        lane = lax.broadcasted_iota(jnp.int32, ang.shape, 1)
        cos, sin = jnp.cos(ang), jnp.sin(ang)
        c = jnp.where(lane < QK_NOPE, 1.0, jnp.where(lane < QK_NOPE + QK_ROPE, cos, 0.0))
        s1 = jnp.where((lane >= QK_NOPE) & (lane < QK_NOPE + half), -sin, 0.0)
        s2 = jnp.where((lane >= QK_NOPE + half) & (lane < QK_NOPE + QK_ROPE), sin, 0.0)
        return [c, s1, s2], []

    return _rowwise(fn, [positions.reshape(-1, 1)], [inv_lanes], outs=[(HEAD_PAD, F32)] * 3, name=name)


def _chip_cols(g, lo, hi):
    c = g.shape[-1]
    parts = [g[j, :, max(lo, j * c) - j * c:min(hi, (j + 1) * c) - j * c] for j in range(g.shape[0]) if max(lo, j * c) < min(hi, (j + 1) * c)]
    return parts[0] if len(parts) == 1 else jnp.concatenate(parts, axis=-1)


def _cols_by_chip(segments, c):
    out, start = [[] for _ in range(N_CHIPS)], 0
    for arr, first, width in segments:
        for j in range(N_CHIPS):
            lo, hi = max(start, j * c), min(start + width, (j + 1) * c)
            if lo < hi:
                out[j].append(arr[:, first + lo - start:first + hi - start])
        start += width
    assert start == N_CHIPS * c
    return jnp.stack([jnp.concatenate(parts, axis=-1) for parts in out])


def _whole_cols(g):
    return g.transpose(1, 0, 2).reshape(g.shape[1], -1)


def _mix_offsets(lw):
    wl, ql, kvl = lw["lru_lambda"].shape[-1], lw["q_norm"].shape[-1], lw["kv_norm"].shape[-1]
    o1 = 2 * wl
    o2 = o1 + ql + kvl + QK_ROPE
    return wl, ql, kvl, o1, o2, o2 + 2 * lw["conv_ln_g"].shape[-1]


def _mix_weights(lw):
    wl, ql, kvl, o1, o2, o3 = _mix_offsets(lw)
    w_in, b_in = lw["w_in"], lw["b_in"][None, :]
    d_in = b_in.shape[1]
    z = lambda m, n: jnp.zeros(m.shape[:-1] + (n,), m.dtype)
    w_b = jnp.concatenate([_chip_cols(w_in, o1 + ql, o1 + ql + kvl), z(w_in[0], QK_NOPE), _chip_cols(w_in, o1 + ql + kvl, o2),
                           z(w_in[0], HEAD_PAD - QK_NOPE - QK_ROPE), _chip_cols(w_in, o1, o1 + ql)], axis=-1)
    b_b = jnp.concatenate([b_in[:, o1 + ql:o1 + ql + kvl], z(b_in, QK_NOPE), b_in[:, o1 + ql + kvl:o2],
                           z(b_in, HEAD_PAD - QK_NOPE - QK_ROPE), b_in[:, o1:o1 + ql]], axis=-1)
    hd = lw["lru_w_gate"].shape[-2]
    per = LANES // hd
    eye = jnp.eye(per, dtype=F32)
    wg = lw["lru_w_gate"].reshape(-1, per, hd, 2 * hd)
    block_diag = lambda m: jnp.einsum("cedk,ef->cedfk", m, eye).reshape(-1, LANES, LANES)
    bg = lw["lru_b_gate"]
    w_uq = _whole_cols(lw["w_uq"]).reshape(ql, MLA_HEADS, QK_NOPE + QK_ROPE)
    w_ukv = _whole_cols(lw["w_ukv"]).reshape(kvl, MLA_HEADS, QK_NOPE + V_HEAD)
    pad = lambda m, n: jnp.pad(m, ((0, 0), (0, 0), (0, n)))
    return dict(
        w_a=_chip_cols(w_in, 0, o1), w_b=w_b, w_c=_chip_cols(w_in, o2, o3), w_g=_chip_cols(w_in, o3, d_in),
        b_a=b_in[:, :o1], b_b=b_b, b_c=b_in[:, o2:o3], b_g=b_in[:, o3:],
        wr=block_diag(wg[..., :hd]), wi=block_diag(wg[..., hd:]),
        br=bg[:, :hd].reshape(1, -1), bi=bg[:, hd:].reshape(1, -1),
        w_uq=pad(w_uq, HEAD_PAD - QK_NOPE - QK_ROPE).reshape(ql, -1),
        w_k=pad(w_ukv[..., :QK_NOPE], HEAD_PAD - QK_NOPE).reshape(kvl, -1),
        w_v=w_ukv[..., QK_NOPE:].reshape(kvl, -1),
    )


def _mix_fwd(x, p, mw, rope, tag):
    d = x.shape[1]
    wl, ql, kvl = p["lru_lambda"].shape[-1], p["q_norm"].shape[-1], p["kv_norm"].shape[-1]
    row = lambda name: p[name][None, :]
    h, = _rowwise(lambda x_, g_: ([_rms(x_, g_)], []), [x], [row("mix_norm")], outs=[(d, BF16)], name=tag + "_rms")
    pa = _mm(h, mw["w_a"], bias=mw["b_a"], name=tag + "_pa")
    pb = _mm(h, mw["w_b"], bias=mw["b_b"], name=tag + "_pb")
    pc = _mm(h, mw["w_c"], bias=mw["b_c"], name=tag + "_pc")
    pg = _mm(h, mw["w_g"], bias=mw["b_g"], name=tag + "_pg")
    lru_args = (p["lru_conv_w"], row("lru_conv_b"), mw["wr"], mw["wi"], mw["br"], mw["bi"], row("lru_lambda"))
    ya_pre = _lru_fwd(pa, *lru_args, name=tag + "_lru")
    y_a = _mm(ya_pre, p["lru_w_out"], b_chips=True, name=tag + "_ya")
    mla_rows = [(pb, kvl, 0), (pb, ql, (kvl + HEAD_PAD) // ql)]
    assert (kvl + HEAD_PAD) % ql == 0 and kvl % HEAD_PAD == 0
    ckvn, cqn = _rowwise(lambda kv_, q_, gk, gq: ([_rms(kv_, gk), _rms(q_, gq)], []), mla_rows, [row("kv_norm"), row("q_norm")],
                         outs=[(kvl, BF16), (ql, BF16)], name=tag + "_lat_rms")
    q0 = _mm(cqn, mw["w_uq"], name=tag + "_q")
    k0 = _mm(ckvn, mw["w_k"], name=tag + "_k")
    v = _mm(ckvn, mw["w_v"], out_dtype=BF16, name=tag + "_v")

    def rope_fwd(q_, k_, kpe, c, s1, s2):
        kr = _rope(kpe, c, s1, s2)
        return [jnp.concatenate([_rope(z, c, s1, s2) for z in _heads(q_)], axis=-1),
                jnp.concatenate([z + kr for z in _heads(k_)], axis=-1)], []

    qh, kh = _rowwise(rope_fwd, [q0, k0, (pb, HEAD_PAD, kvl // HEAD_PAD), *rope],
                      outs=[(q0.shape[1], BF16), (k0.shape[1], BF16)], name=tag + "_rope")
    o = _attn_fwd(qh, kh, v, tag + "_attn")
    y_b = _mm(o, p["mla_w_o"], b_chips=True, name=tag + "_yb")
    c2 = _glu_conv_fwd(pc, p["conv_dw_w"], row("conv_dw_b"), tag + "_conv")
    c3, = _rowwise(lambda c_, g_, b_: ([_layer_norm_silu(c_, g_, b_)], []), [c2], [row("conv_ln_g"), row("conv_ln_b")],
                   outs=[(c2.shape[1], BF16)], name=tag + "_ln")
    y_c = _mm(c3, p["conv_w_out"], bias=row("conv_b_out"), b_chips=True, name=tag + "_yc")
    merged, = _rowwise(_merge, [y_a, y_b, y_c, (pg, d, 0), (pg, d, 1), (pg, d, 2)], outs=[(d, BF16)], name=tag + "_merge")
    out = _mm(merged, p["w_out"], res=x, name=tag + "_out")
    saved = dict(x=x, h=h, pa=pa, pb=pb, pc=pc, pg=pg, ya_pre=ya_pre, y_a=y_a, y_b=y_b, y_c=y_c, ckvn=ckvn, cqn=cqn,
                 qh=qh, kh=kh, v=v, o=o, c2=c2, c3=c3, merged=merged, lru_args=lru_args)
    return out, saved


def _merge(ya, yb, yc, g0, g1, g2):
    return [jax.nn.sigmoid(g0) * ya + jax.nn.sigmoid(g1) * yb + jax.nn.sigmoid(g2) * yc], []


def _mix_bwd(dy, s, p, mw, rope, tag, dep=None, mid=None):
    x, h = s["x"], s["h"]
    d = x.shape[1]
    wl, ql, kvl = p["lru_lambda"].shape[-1], p["q_norm"].shape[-1], p["kv_norm"].shape[-1]
    row = lambda name: p[name][None, :]
    g = {}
    dmerged = _mm(dy, p["w_out"], "nt", bias=_after(dep, d), name=tag + "_dmerged")
    g["w_out"] = _mm(s["merged"], dy, "tn", GRAD_DTYPE, name=tag + "_dw_out")

    def merge_bwd(ya, yb, yc, g0, g1, g2, dm):
        _, vjp = jax.vjp(lambda *a: _merge(*a)[0][0], ya, yb, yc, g0, g1, g2)
        dya, dyb, dyc, d0, d1, d2 = vjp(dm)
        dpg = jnp.concatenate([d0, d1, d2], axis=-1)
        return [dya, dyb, dyc, dpg], [jnp.sum(dyc, axis=0, keepdims=True), jnp.sum(dpg, axis=0, keepdims=True)]

    pg = s["pg"]
    dya, dyb, dyc, dpg, g["conv_b_out"], db_g = _rowwise(
        merge_bwd, [s["y_a"], s["y_b"], s["y_c"], (pg, d, 0), (pg, d, 1), (pg, d, 2), dmerged],
        outs=[(d, BF16), (d, BF16), (d, BF16), (3 * d, BF16)], accs=[((1, d), F32), ((1, 3 * d), F32)], tt=128, name=tag + "_dmerge")
    dep = None if mid is None else mid(dya)
    g["lru_w_out"] = _mm(s["ya_pre"], dya, "tn", GRAD_DTYPE, bias=_after(dep, d), out_chips=True, name=tag + "_dw_lru_out")
    dya_pre = _mm(dya, p["lru_w_out"], "nt", bias=_after(dep, wl), b_chips=True, name=tag + "_dya_pre")
    (dpa_x, dpa_g, g["lru_conv_w"], g["lru_conv_b"], g["wr"], g["wi"], g["br"], g["bi"], g["lru_lambda"], sb_x, sb_g) = _lru_bwd(
        s["pa"], dya_pre, *s["lru_args"], name=tag + "_dlru")
    dpa = jnp.concatenate([dpa_x, dpa_g], axis=1)
    db_a = jnp.concatenate([sb_x, sb_g], axis=1)
    g["conv_w_out"] = _mm(s["c3"], dyc, "tn", GRAD_DTYPE, bias=_after(dep, d), out_chips=True, name=tag + "_dw_conv_out")
    dc3 = _mm(dyc, p["conv_w_out"], "nt", bias=_after(dep, s["c3"].shape[1]), b_chips=True, name=tag + "_dc3")

    def ln_bwd(c_, dc_, g_, b_):
        _, vjp = jax.vjp(_layer_norm_silu, c_, g_, b_)
        dc, dg_, db_ = vjp(dc_)
        return [dc], [dg_, db_]

    cc = s["c2"].shape[1]
    dc2, g["conv_ln_g"], g["conv_ln_b"] = _rowwise(ln_bwd, [s["c2"], dc3], [row("conv_ln_g"), row("conv_ln_b")], outs=[(cc, F32)],
                                                    accs=[((1, cc), F32)] * 2, name=tag + "_dln")
    dpc_v, dpc_g, g["conv_dw_w"], g["conv_dw_b"], sc_v, sc_g = _glu_conv_bwd(s["pc"], dc2, p["conv_dw_w"], tag + "_dconv")
    dpc = jnp.concatenate([dpc_v, dpc_g], axis=1)
    db_c = jnp.concatenate([sc_v, sc_g], axis=1)
    g["mla_w_o"] = _mm(s["o"], dyb, "tn", GRAD_DTYPE, bias=_after(dep, d), out_chips=True, name=tag + "_dw_o")
    do = _mm(dyb, p["mla_w_o"], "nt", out_dtype=BF16, bias=_after(dep, s["o"].shape[1]), b_chips=True, name=tag + "_do")
    dqh, dkh, dv = _attn_bwd(s["qh"], s["kh"], s["v"], do, tag + "_dattn")

    def rope_bwd(dq_, dk_, c, s1, s2):
        lane = lax.broadcasted_iota(jnp.int32, c.shape, 1)
        dkr = functools.reduce(lambda a, b: a + b, _heads(dk_))
        dkpe = jnp.where((lane >= QK_NOPE) & (lane < QK_NOPE + QK_ROPE), _rope_t(dkr, c, s1, s2), 0.0)
        return [jnp.concatenate([_rope_t(z, c, s1, s2) for z in _heads(dq_)], axis=-1), dk_, dkpe], []

    dq0, dk0, dkpe = _rowwise(rope_bwd, [dqh, dkh, *rope], outs=[(dqh.shape[1], BF16), (dkh.shape[1], BF16), (HEAD_PAD, F32)],
                              name=tag + "_drope")
    dvb = dv.astype(BF16)
    g["w_uq"] = _mm(s["cqn"], dq0, "tn", GRAD_DTYPE, name=tag + "_dw_uq")
    g["w_k"] = _mm(s["ckvn"], dk0, "tn", GRAD_DTYPE, name=tag + "_dw_k")
    g["w_v"] = _mm(s["ckvn"], dvb, "tn", GRAD_DTYPE, name=tag + "_dw_v")
    dcqn = _mm(dq0, mw["w_uq"], "nt", name=tag + "_dcqn")
    dckvn = _mm(dk0, mw["w_k"], "nt", name=tag + "_dckvn_k")
    dckvn = _mm(dvb, mw["w_v"], "nt", res=dckvn, name=tag + "_dckvn_v")

    def lat_bwd(kv_, q_, dkv_, dq_, dkpe_, gk, gq):
        _, vjp_k = jax.vjp(_rms, kv_, gk)
        _, vjp_q = jax.vjp(_rms, q_, gq)
        (dkv, dgk), (dq, dgq) = vjp_k(dkv_), vjp_q(dq_)
        dpb = jnp.concatenate([dkv, dkpe_, dq], axis=-1)
        return [dpb], [dgk, dgq, jnp.sum(dpb, axis=0, keepdims=True)]

    pb = s["pb"]
    dpb, g["kv_norm"], g["q_norm"], db_b = _rowwise(
        lat_bwd, [(pb, kvl, 0), (pb, ql, (kvl + HEAD_PAD) // ql), dckvn, dcqn, dkpe], [row("kv_norm"), row("q_norm")],
        outs=[(pb.shape[1], BF16)], accs=[((1, kvl), F32), ((1, ql), F32), ((1, pb.shape[1]), F32)], name=tag + "_dlat")
    dh = None
    for part, dpart in (("a", dpa), ("b", dpb), ("c", dpc), ("g", dpg)):
        g["w_" + part] = _mm(h, dpart, "tn", GRAD_DTYPE, name=tag + "_dw_" + part)
        dh = _mm(dpart, mw["w_" + part], "nt", res=dh, name=tag + "_dh_" + part)
    g["b_a"], g["b_b"], g["b_c"], g["b_g"] = db_a, db_b, db_c, db_g
    dx, g["mix_norm"] = _rms_bwd(x, dh, dy, row("mix_norm"), tag + "_drms")
    return dx, g


def _by_chip_cols(m):
    return m.reshape(m.shape[0], N_CHIPS, -1).transpose(1, 0, 2)


def _mix_grads_to_params(g, p):
    wl, ql, kvl, o1, o2, o3 = _mix_offsets(p)
    hd = p["lru_w_gate"].shape[-2]
    per = LANES // hd
    eye = jnp.eye(per, dtype=F32)
    diag = lambda m: jnp.einsum("cedfk,ef->cedk", m.reshape(-1, per, hd, per, hd), eye).reshape(-1, hd, hd)
    out = {k: g[k] for k in ("lru_w_out", "conv_w_out", "mla_w_o", "lru_conv_w", "conv_dw_w")}
    out["w_out"] = g["w_out"].reshape(N_CHIPS, -1, g["w_out"].shape[1])
    for k in ("mix_norm", "conv_b_out", "lru_conv_b", "lru_lambda", "conv_ln_g", "conv_ln_b", "conv_dw_b", "kv_norm", "q_norm"):
        out[k] = g[k][0]
    mla = lambda m: [(m, kvl + HEAD_PAD, ql), (m, 0, kvl), (m, kvl + QK_NOPE, QK_ROPE)]
    whole = lambda m: [(m, 0, m.shape[1])]
    out["w_in"] = _cols_by_chip(whole(g["w_a"]) + mla(g["w_b"]) + whole(g["w_c"]) + whole(g["w_g"]), p["w_in"].shape[-1])
    out["b_in"] = jnp.concatenate([g["b_a"]] + [m[:, a:a + w] for m, a, w in mla(g["b_b"])] + [g["b_c"], g["b_g"]], axis=1)[0]
    out["lru_w_gate"] = jnp.concatenate([diag(g["wr"]), diag(g["wi"])], axis=-1)
    out["lru_b_gate"] = jnp.concatenate([g["br"].reshape(-1, hd), g["bi"].reshape(-1, hd)], axis=-1)
    out["w_uq"] = _by_chip_cols(g["w_uq"].reshape(ql, MLA_HEADS, HEAD_PAD)[..., :QK_NOPE + QK_ROPE].reshape(ql, -1))
    out["w_ukv"] = _by_chip_cols(jnp.concatenate([g["w_k"].reshape(kvl, MLA_HEADS, HEAD_PAD)[..., :QK_NOPE],
                                                  g["w_v"].reshape(kvl, MLA_HEADS, V_HEAD)], axis=-1).reshape(kvl, -1))
    return out


def _loss_head(x, target, g, name):
    def fn(x_, t_, g_):
        y, vjp = jax.vjp(_rms, x_, g_)
        err = y - t_
        dx, dg = vjp(err * (1.0 / x_.shape[1]))
        loss = 0.5 * jnp.sum(jnp.mean(err * err, axis=-1, keepdims=True), axis=0, keepdims=True)
        return [dx], [dg, jnp.broadcast_to(loss, (1, LANES))]

    dx, dg, loss = _rowwise(fn, [x, target], [g], outs=[(x.shape[1], F32)], accs=[((1, x.shape[1]), F32), ((1, LANES), F32)], name=name)
    return loss[0, 0], dx, dg


def _part_fwd(part, x, p, rope, tag, dep=None):
    norm = part + "_norm"
    if dep is not None:
        p = dict(p, **{norm: p[norm] + dep})
    if part == "mix":
        mw = _mix_weights(p)
        x, s = _mix_fwd(x, p, mw, rope, tag)
        return x, (s, mw, p)
    x, s = _ffn_fwd(x, p[norm][None, :], p[part + "_w1"], p[part + "_w2"], tag)
    return x, (s, None, p)


def _part_bwd(part, dx, saved, rope, tag, dep=None, mid=None):
    s, mw, p = saved
    if part == "mix":
        dx, gm = _mix_bwd(dx, s, p, mw, rope, tag, dep, mid)
        return dx, _mix_grads_to_params(gm, p)
    dx, (dn, dw1, dw2) = _ffn_bwd(dx, s, p[part + "_norm"][None, :], p[part + "_w1"], p[part + "_w2"], tag, dep, mid)
    return dx, {part + "_norm": dn[0], part + "_w1": dw1, part + "_w2": dw2.reshape(N_CHIPS, -1, dw2.shape[1])}


ANY = pl.BlockSpec(memory_space=pl.ANY)
VMEM_WHOLE = pl.BlockSpec(memory_space=pltpu.VMEM)


def _place():
    x, y, c = (lax.axis_index(a) for a in MESH_AXES)
    return x, y, c, [(1 - x, y), (x, 1 - y), (1 - x, 1 - y)]


def _remote(src, dst, send_sem, recv_sem, device):
    return pltpu.make_async_remote_copy(src_ref=src, dst_ref=dst, send_sem=send_sem, recv_sem=recv_sem, device_id=device,
                                        device_id_type=pl.DeviceIdType.MESH)


def _half_rows(c, half):
    return pl.ds(pl.multiple_of(c * half, 16), half)


def _allreduce_all(v, name):
    r, cols = v.shape

    def body(v_ref, out_ref, buf, send_sems, recv_sems):
        x, y, c, chips = _place()
        sibling = (x, y, 1 - c)
        slot = lambda px, py, pc: buf.at[4 * px + 2 * py + pc]
        buf[4 * x + 2 * y + c] = v_ref[...]
        sent = [_remote(v_ref, slot(x, y, c), send_sems.at[0], recv_sems.at[0], sibling)]
        sent += [_remote(v_ref, slot(x, y, c), send_sems.at[1 + j], recv_sems.at[1 + j], (cx, cy, c)) for j, (cx, cy) in enumerate(chips)]
        for cp in sent:
            cp.start()
        for j, (cx, cy) in enumerate(chips):
            blk = slot(cx, cy, c)
            _remote(blk, blk, send_sems.at[1 + j], recv_sems.at[1 + j], (cx, cy, c)).wait_recv()
            passed = _remote(blk, blk, send_sems.at[4 + j], recv_sems.at[4 + j], sibling)
            passed.start()
            sent.append(passed)
        blk = slot(x, y, 1 - c)
        _remote(blk, blk, send_sems.at[0], recv_sems.at[0], sibling).wait_recv()
        for j, (cx, cy) in enumerate(chips):
            blk = slot(cx, cy, 1 - c)
            _remote(blk, blk, send_sems.at[4 + j], recv_sems.at[4 + j], sibling).wait_recv()
        for cp in sent:
            cp.wait_send()
        acc = buf[0]
        for k in range(1, 2 * N_CHIPS):
            acc = acc + buf[k]
        out_ref[...] = acc

    return pl.pallas_call(
        body, name=name, in_specs=[VMEM_WHOLE], out_specs=VMEM_WHOLE, out_shape=S((r, cols), F32),
        scratch_shapes=[pltpu.VMEM((2 * N_CHIPS, r, cols), F32), pltpu.SemaphoreType.DMA((7,)), pltpu.SemaphoreType.DMA((7,))],
        compiler_params=pltpu.CompilerParams(vmem_limit_bytes=VMEM_LIMIT),
    )(v)


def _pair_views(halves):
    def views(src, land, i, j, x, y, c, px, py):
        return src.at[pl.ds(0, N_CHIPS), _half_rows(1 - c, halves[i])], land, land
    return views


def _pair_exchange_begin(gs, name):
    lands = [lax.empty((N_CHIPS, g.shape[1] // 2, g.shape[2]), g.dtype) for g in gs]
    return _ici_begin(gs, lands, _pair_views([g.shape[1] // 2 for g in gs]), name, sibling=True)


def _pair_exchange_end(handle, after, name):
    n = (len(handle) - 3) // 2
    return _ici_end(handle, after, _pair_views([g.shape[1] // 2 for g in handle[2:2 + n]]), name, sibling=True)


def _pair_sum(g, a, name):
    n, r, cols = g.shape
    half = r // 2
    tr = _tile(half, 512, 16)
    n_blk = half // tr

    def body(c_ref, g_ref, a_ref, o_ref):
        o_ref[...] = (g_ref[...].astype(F32) + a_ref[...].astype(F32)).astype(o_ref.dtype)

    blk = pl.BlockSpec((1, tr, cols), lambda j, i, c_ref: (j, i, 0))
    return pl.pallas_call(
        body, name=name, out_shape=S((n, half, cols), BF16),
        grid_spec=pltpu.PrefetchScalarGridSpec(
            num_scalar_prefetch=1, grid=(n, n_blk),
            in_specs=[pl.BlockSpec((1, tr, cols), lambda j, i, c_ref: (j, c_ref[0] * n_blk + i, 0)), blk], out_specs=blk),
        compiler_params=_params(("parallel", "parallel")),
    )(lax.axis_index("c").reshape(1).astype(jnp.int32), g, a)


HBM = pl.BlockSpec(memory_space=pltpu.HBM)
SEM = pl.BlockSpec(memory_space=pltpu.SEMAPHORE)
SPLIT_COPY = pltpu.CompilerParams(has_side_effects=pltpu.SideEffectType.DATAFLOW_SIDE_EFFECTING)


def _peers(sibling):
    x, y, c, chips = _place()
    return x, y, c, ([(x, y, 1 - c)] if sibling else [(cx, cy, c) for cx, cy in chips])


def _ici_begin(srcs, lands, views, name, after=None, sibling=False):
    n = len(srcs)
    n_peers = 1 if sibling else N_CHIPS - 1
    extra = [] if after is None else [after]

    def body(*refs):
        s_refs, l_refs, send_sems, recv_sems, token = refs[:n], refs[n:2 * n], refs[-3 - 2 * n], refs[-2 - 2 * n], refs[-1]
        x, y, c, peers = _peers(sibling)
        for i in range(n):
            for j, peer in enumerate(peers):
                src, dst, _ = views(s_refs[i], l_refs[i], i, j, x, y, c, peer[0], peer[1])
                k = n_peers * i + j
                _remote(src, dst, send_sems.at[k], recv_sems.at[k], peer).start()
        token[...] = jnp.zeros_like(token)

    bufs = list(srcs) + list(lands)
    sems = pltpu.SemaphoreType.DMA((n * n_peers,))
    return pl.pallas_call(
        body, name=name, out_shape=(sems, sems, *[pltpu.HBM(b.shape, b.dtype) for b in bufs], S((8, LANES), F32)),
        in_specs=[HBM] * (2 * n) + [ANY] * len(extra), out_specs=(SEM, SEM, *[HBM] * (2 * n), VMEM_WHOLE),
        input_output_aliases={i: 2 + i for i in range(2 * n)}, compiler_params=SPLIT_COPY,
    )(*[pltpu.with_memory_space_constraint(b, pltpu.HBM) for b in bufs], *extra)


def _ici_end(handle, after, views, name, sibling=False):
    send_sems, recv_sems, *bufs, _ = handle
    n = len(bufs) // 2
    n_peers = 1 if sibling else N_CHIPS - 1

    def body(*refs):
        s_refs, l_refs, send_sems_, recv_sems_ = refs[:n], refs[n:2 * n], refs[2 * n], refs[2 * n + 1]
        x, y, c, peers = _peers(sibling)
        for i in range(n):
            for j, peer in enumerate(peers):
                src, _, arrival = views(s_refs[i], l_refs[i], i, j, x, y, c, peer[0], peer[1])
                k = n_peers * i + j
                cp = _remote(src, arrival, send_sems_.at[k], recv_sems_.at[k], peer)
                cp.wait_send()
                cp.wait_recv()

    out = pl.pallas_call(
        body, name=name, out_shape=[pltpu.HBM(b.shape, b.dtype) for b in bufs], in_specs=[HBM] * (2 * n) + [SEM, SEM, ANY],
        out_specs=[HBM] * (2 * n), input_output_aliases={i: i for i in range(2 * n)}, compiler_params=SPLIT_COPY,
    )(*bufs, send_sems, recv_sems, after)
    return out[:n], out[n:]


def _gather_views(halves):
    def views(src, land, i, j, x, y, c, cx, cy):
        mine = _half_rows(c, halves[i])
        return src.at[mine], land.at[2 * x + y, mine], land.at[2 * cx + cy, mine]
    return views


def _gather_begin(shards, name, after=None):
    lands = [lax.empty((N_CHIPS,) + s.shape, s.dtype) for s in shards]
    return _ici_begin(shards, lands, _gather_views([s.shape[0] // 2 for s in shards]), name, after)


def _gather_end(handle, after, name):
    n = (len(handle) - 3) // 2
    shards, lands = _ici_end(handle, after, _gather_views([s.shape[0] // 2 for s in handle[2:2 + n]]), name + "_wait")
    return _gather_finish(shards, lands, name + "_finish")


def _gather_finish(shards, lands, name):
    n = len(shards)

    def body(*refs):
        ins, l_refs, outs, send_sems, recv_sems = refs[:n], refs[n:2 * n], refs[2 * n:3 * n], refs[3 * n], refs[3 * n + 1]
        x, y, c, chips = _place()
        me, sibling = 2 * x + y, (x, y, 1 - c)
        sent = []
        for i in range(n):
            mine = _half_rows(c, shards[i].shape[0] // 2)
            for j, (cx, cy) in enumerate(chips):
                sent.append(_remote(l_refs[i].at[2 * cx + cy, mine], outs[i].at[2 * cx + cy, mine], send_sems.at[i, j], recv_sems.at[i, j], sibling))
            sent.append(_remote(ins[i], outs[i].at[me], send_sems.at[i, 3], recv_sems.at[i, 3], sibling))
        for cp in sent:
            cp.start()
        for i in range(n):
            other = _half_rows(1 - c, shards[i].shape[0] // 2)
            for j, (cx, cy) in enumerate(chips):
                rows = outs[i].at[2 * cx + cy, other]
                _remote(rows, rows, send_sems.at[i, j], recv_sems.at[i, j], sibling).wait_recv()
            own = outs[i].at[me]
            _remote(own, own, send_sems.at[i, 3], recv_sems.at[i, 3], sibling).wait_recv()
        for cp in sent:
            cp.wait_send()

    return pl.pallas_call(
        body, name=name, in_specs=[ANY] * (2 * n), out_specs=[ANY] * n, out_shape=[S(l_.shape, l_.dtype) for l_ in lands],
        input_output_aliases={n + i: i for i in range(n)},
        scratch_shapes=[pltpu.SemaphoreType.DMA((n, 4)), pltpu.SemaphoreType.DMA((n, 4))],
    )(*shards, *lands)


def _exchange_views(src, land, i, j, x, y, c, cx, cy):
    return src.at[2 * cx + cy], land.at[j], land.at[j]


def _chip_exchange_begin(ps, name):
    return _ici_begin(ps, [lax.empty((N_CHIPS - 1,) + p.shape[1:], p.dtype) for p in ps], _exchange_views, name)


def _chip_exchange_end(handle, after, name):
    return _ici_end(handle, after, _exchange_views, name)


def _quad_sum(p, q, name):
    _, h, cols = p.shape
    tr = _tile(h, 512, 16)
    n_blk = h // tr
    x, y, c, _ = _place()

    def body(s_ref, p_ref, q0_ref, q1_ref, q2_ref, o_ref):
        o_ref[...] = p_ref[0].astype(F32) + q0_ref[0].astype(F32) + q1_ref[0].astype(F32) + q2_ref[0].astype(F32)

    in_specs = [pl.BlockSpec((1, tr, cols), lambda i, s_ref: (s_ref[0], i, 0))]
    in_specs += [pl.BlockSpec((1, tr, cols), functools.partial(lambda i, s_ref, k: (k, i, 0), k=k)) for k in range(N_CHIPS - 1)]
    return pl.pallas_call(
        body, name=name, out_shape=S((2 * h, cols), F32),
        grid_spec=pltpu.PrefetchScalarGridSpec(num_scalar_prefetch=1, grid=(n_blk,), in_specs=in_specs,
                                               out_specs=pl.BlockSpec((tr, cols), lambda i, s_ref: (s_ref[1] * n_blk + i, 0))),
        compiler_params=_params(("parallel",)),
    )(jnp.stack([2 * x + y, c]).astype(jnp.int32), p, q, q, q)


def _pair_share(bufs, name):
    n = len(bufs)

    def body(*refs):
        in_refs, out_refs, send_sems, recv_sems = refs[:n], refs[n:2 * n], refs[2 * n], refs[2 * n + 1]
        x, y, c, _ = _place()
        sent = []
        for i in range(n):
            mine = _half_rows(c, bufs[i].shape[0] // 2)
            sent.append(_remote(in_refs[i].at[mine], out_refs[i].at[mine], send_sems.at[i], recv_sems.at[i], (x, y, 1 - c)))
        for cp in sent:
            cp.start()
        for i in range(n):
            other = out_refs[i].at[_half_rows(1 - c, bufs[i].shape[0] // 2)]
            _remote(other, other, send_sems.at[i], recv_sems.at[i], (x, y, 1 - c)).wait_recv()
        for cp in sent:
            cp.wait_send()

    return pl.pallas_call(body, name=name, in_specs=[ANY] * n, out_specs=[ANY] * n, out_shape=[S(b.shape, b.dtype) for b in bufs],
                          input_output_aliases={i: i for i in range(n)},
                          scratch_shapes=[pltpu.SemaphoreType.DMA((n,)), pltpu.SemaphoreType.DMA((n,))])(*bufs)


def _adamw(w, gs, m, v, name):
    depth, r, cols = w.shape
    tr, tc = _tile(r, 256, 8), cols
    if tr < 64 and r > 256 and cols % LANES == 0:
        tr, tc = r, _tile(cols, 256)

    def body(*refs):
        w_ref, m_ref, v_ref = refs[:3]
        g_refs = refs[3:3 + depth]
        go_ref, d_ref, mo_ref, vo_ref = refs[3 + depth:]
        for l in range(depth):
            @pl.when(pl.program_id(0) == l)
            def _(l=l):
                g_ = g_refs[l][...]
                m_ = ADAM_B1 * m_ref[...] + (1.0 - ADAM_B1) * g_
                v_ = ADAM_B2 * v_ref[...] + (1.0 - ADAM_B2) * jnp.square(g_)
                m_hat = m_ / (1.0 - ADAM_B1 ** ADAM_STEP)
                v_hat = v_ / (1.0 - ADAM_B2 ** ADAM_STEP)
                d_ref[...] = -ADAM_LR * (m_hat / (jnp.sqrt(v_hat) + ADAM_EPS) + ADAM_WD * w_ref[...])
                go_ref[...], mo_ref[...], vo_ref[...] = g_, m_, v_

    blk = pl.BlockSpec((None, tr, tc), lambda l, i: (l, i, 0) if tc == cols else (l, 0, i))
    g_blk = pl.BlockSpec((tr, tc), lambda l, i: (i, 0) if tc == cols else (0, i))
    return pl.pallas_call(body, name=name, grid=(depth, (r // tr) * (cols // tc)), in_specs=[blk] * 3 + [g_blk] * depth, out_specs=[blk] * 4,
                          out_shape=[S(w.shape, F32)] * 4, compiler_params=_params(("parallel", "parallel")))(w, m, v, *gs)


WEIGHTS = ("ffn1_norm", "ffn1_w1", "ffn1_w2", "mix_norm", "w_in", "b_in", "lru_conv_w", "lru_conv_b", "lru_w_gate", "lru_b_gate",
           "lru_lambda", "lru_w_out", "q_norm", "w_uq", "kv_norm", "w_ukv", "mla_w_o", "conv_dw_w", "conv_dw_b", "conv_ln_g",
           "conv_ln_b", "conv_w_out", "conv_b_out", "w_out", "ffn2_norm", "ffn2_w1", "ffn2_w2", "final_norm")
ROW_SHARDED = ("ffn1_w2", "w_out", "ffn2_w2")
COL_SHARDED = ("ffn1_w1", "w_in", "lru_w_out", "w_uq", "w_ukv", "mla_w_o", "conv_w_out", "ffn2_w1")
SMALL_SHARDED = ("lru_conv_w", "conv_dw_w")
MXU_SHARDED = tuple(n for n in WEIGHTS if n in ROW_SHARDED + COL_SHARDED)
REPLICATED = tuple(n for n in WEIGHTS if n not in MXU_SHARDED + SMALL_SHARDED)
SMALL = REPLICATED + SMALL_SHARDED
INPUTS = ("x", "positions") + WEIGHTS + ("loss_target",) + tuple("m_" + n for n in WEIGHTS) + tuple("v_" + n for n in WEIGHTS)


def _pack(arrays, dtype, cols, row_unit):
    flat = jnp.concatenate([a.astype(dtype).reshape(-1) for a in arrays])
    unit = cols * row_unit
    return jnp.pad(flat, (0, -flat.shape[0] % unit)).reshape(-1, cols)


def _unpack(flat, shapes):
    out, off = [], 0
    for shp in shapes:
        n = 1
        for s_ in shp:
            n *= s_
        out.append(flat[..., off:off + n].reshape(flat.shape[:-1] + tuple(shp)))
        off += n
    return out


def _reduce_scatter_begin(gs, tag):
    return _pair_exchange_begin(gs, tag + "_pair_exchange")


def _reduce_scatter_mid(handle, names, after, tag):
    gs, halves = _pair_exchange_end(handle, after, tag + "_pair_exchange_wait")
    pairs = [_pair_sum(g, h, f"{tag}_pair_sum_{n}") for n, g, h in zip(names, gs, halves, strict=True)]
    return _chip_exchange_begin(pairs, tag + "_chip_exchange")


def _reduce_scatter_end(handle, names, after, tag):
    pairs, others = _chip_exchange_end(handle, after, tag + "_chip_exchange_wait")
    sums = [_quad_sum(p, q, f"{tag}_chip_sum_{n}") for n, p, q in zip(names, pairs, others, strict=True)]
    return dict(zip(names, _pair_share(sums, tag + "_pair_share"), strict=True))


PARTS = (("ffn1", ("ffn1_w1", "ffn1_w2")), ("mix", ("w_in", "lru_w_out", "w_uq", "w_ukv", "mla_w_o", "conv_w_out", "w_out")),
         ("ffn2", ("ffn2_w1", "ffn2_w2")))


def _step(a):
    x, positions, target = a["x"][0], a["positions"][0], a["loss_target"][0]
    depth = a["ffn1_norm"].shape[0]
    me = 2 * lax.axis_index("x") + lax.axis_index("y")
    placed = [lax.dynamic_update_slice_in_dim(jnp.zeros(a[n].shape[:2] + (N_CHIPS,) + a[n].shape[2:], F32), 0.5 * a[n][:, :, None], me, 2)
              for n in SMALL_SHARDED]
    small_whole = _unpack(_allreduce_all(_pack(placed, F32, LANES, 8), "small_weights").reshape(-1), [p_.shape for p_ in placed])
    small_whole = {n: w.reshape(w.shape[:2] + (-1,)) for n, w in zip(SMALL_SHARDED, small_whole, strict=True)}
    base = [{n: a[n][l] for n in REPLICATED if a[n].ndim > 1} | {n: small_whole[n][l] for n in SMALL_SHARDED} for l in range(depth)]
    order = [(l, part, names) for l in range(depth) for part, names in PARTS]
    shards = lambda l, names: [a[n][l].astype(BF16) for n in names]
    rope = _rope_tables(positions, "rope_tables")
    handle = _gather_begin(shards(0, order[0][2]), "l0_ffn1_gather", small_whole[SMALL_SHARDED[0]])
    gathered = _gather_end(handle, handle[-1], "l0_ffn1_gather")
    saved = []
    for k, (l, part, names) in enumerate(order):
        p = base[l] | {n: g.reshape(-1, g.shape[-1]) if n in ROW_SHARDED else g for n, g in zip(names, gathered, strict=True)}
        dep = None
        if k + 1 < len(order):
            l2, part2, names2 = order[k + 1]
            handle = _gather_begin(shards(l2, names2), f"l{l2}_{part2}_gather", handle[-1])
            dep = handle[-1][0, 0]
        x, s = _part_fwd(part, x, p, rope, f"l{l}_{part}", dep)
        saved.append(s)
        if dep is not None:
            gathered = _gather_end(handle, x, f"l{l2}_{part2}_gather")
    loss, dx, dfinal = _loss_head(x, target, a["final_norm"][None, :], "loss_head")
    loss = lax.psum(loss, MESH_AXES)
    grads, shard_grads, pending = [{} for _ in range(depth)], [{} for _ in range(depth)], None
    for l, part, names in reversed(order):
        flying = []

        def mid(after, pending=pending, flying=flying):
            flying.append(_reduce_scatter_mid(pending[0], pending[2], after, pending[3]))
            return flying[0][-1][0, 0]

        deps = (None, None) if pending is None else (pending[0][-1][0, 0], mid)
        dx, g = _part_bwd(part, dx, saved.pop(), rope, f"l{l}_{part}", *deps)
        if pending is not None:
            shard_grads[pending[1]].update(_reduce_scatter_end(flying[0], pending[2], dx, pending[3]))
        pending = (_reduce_scatter_begin([g[n] for n in names], f"l{l}_{part}_grad"), l, names, f"l{l}_{part}_grad")
        grads[l].update({n: g[n] for n in g if n in SMALL})
    last = _reduce_scatter_mid(pending[0], pending[2], pending[0][-1], pending[3])
    shard_grads[pending[1]].update(_reduce_scatter_end(last, pending[2], last[-1], pending[3]))
    small = [jnp.stack([g[n] for g in grads]) if a[n].ndim > 1 else dfinal[0] for n in SMALL]
    g_small = _unpack(_allreduce_all(_pack(small, F32, LANES, 256), "grad_allreduce").reshape(-1), [s_.shape for s_ in small])
    g_small = [lax.dynamic_slice_in_dim(g, me * a[n].shape[-1], a[n].shape[-1], 2) if n in SMALL_SHARDED else g
               for n, g in zip(SMALL, g_small, strict=True)]
    g, delta, new_m, new_v = {}, {}, {}, {}
    for n in MXU_SHARDED:
        flip = (lambda t: jnp.swapaxes(t, -1, -2)) if (a[n].shape[-1] % LANES and not a[n].shape[-2] % LANES) else (lambda t: t)
        res = _adamw(flip(a[n]), [flip(shard_grads[l][n]) for l in range(depth)], flip(a["m_" + n]), flip(a["v_" + n]), "adamw_" + n)
        g[n], delta[n], new_m[n], new_v[n] = [flip(r) for r in res]
    shapes = [a[n].shape for n in SMALL]
    packed = [_pack([a[pre + n] for n in SMALL], F32, LANES, 256)[None] for pre in ("", "m_", "v_")]
    res = _adamw(packed[0], [_pack(g_small, F32, LANES, 256)], packed[1], packed[2], "adamw_small")
    for out, r in zip((g, delta, new_m, new_v), res, strict=True):
        out.update(zip(SMALL, _unpack(r.reshape(-1), shapes), strict=True))
    return (loss, dx[None], *[g[n] for n in WEIGHTS], *[delta[n] for n in WEIGHTS], *[new_m[n] for n in WEIGHTS], *[new_v[n] for n in WEIGHTS])


def kernel(x, positions, ffn1_norm, ffn1_w1, ffn1_w2, mix_norm, w_in, b_in, lru_conv_w, lru_conv_b, lru_w_gate, lru_b_gate, lru_lambda, lru_w_out, q_norm, w_uq, kv_norm, w_ukv, mla_w_o, conv_dw_w, conv_dw_b, conv_ln_g, conv_ln_b, conv_w_out, conv_b_out, w_out, ffn2_norm, ffn2_w1, ffn2_w2, final_norm, loss_target, m_ffn1_norm, m_ffn1_w1, m_ffn1_w2, m_mix_norm, m_w_in, m_b_in, m_lru_conv_w, m_lru_conv_b, m_lru_w_gate, m_lru_b_gate, m_lru_lambda, m_lru_w_out, m_q_norm, m_w_uq, m_kv_norm, m_w_ukv, m_mla_w_o, m_conv_dw_w, m_conv_dw_b, m_conv_ln_g, m_conv_ln_b, m_conv_w_out, m_conv_b_out, m_w_out, m_ffn2_norm, m_ffn2_w1, m_ffn2_w2, m_final_norm, v_ffn1_norm, v_ffn1_w1, v_ffn1_w2, v_mix_norm, v_w_in, v_b_in, v_lru_conv_w, v_lru_conv_b, v_lru_w_gate, v_lru_b_gate, v_lru_lambda, v_lru_w_out, v_q_norm, v_w_uq, v_kv_norm, v_w_ukv, v_mla_w_o, v_conv_dw_w, v_conv_dw_b, v_conv_ln_g, v_conv_ln_b, v_conv_w_out, v_conv_b_out, v_w_out, v_ffn2_norm, v_ffn2_w1, v_ffn2_w2, v_final_norm):
    return _step(dict(zip(INPUTS, (x, positions, ffn1_norm, ffn1_w1, ffn1_w2, mix_norm, w_in, b_in, lru_conv_w, lru_conv_b, lru_w_gate, lru_b_gate, lru_lambda, lru_w_out, q_norm, w_uq, kv_norm, w_ukv, mla_w_o, conv_dw_w, conv_dw_b, conv_ln_g, conv_ln_b, conv_w_out, conv_b_out, w_out, ffn2_norm, ffn2_w1, ffn2_w2, final_norm, loss_target, m_ffn1_norm, m_ffn1_w1, m_ffn1_w2, m_mix_norm, m_w_in, m_b_in, m_lru_conv_w, m_lru_conv_b, m_lru_w_gate, m_lru_b_gate, m_lru_lambda, m_lru_w_out, m_q_norm, m_w_uq, m_kv_norm, m_w_ukv, m_mla_w_o, m_conv_dw_w, m_conv_dw_b, m_conv_ln_g, m_conv_ln_b, m_conv_w_out, m_conv_b_out, m_w_out, m_ffn2_norm, m_ffn2_w1, m_ffn2_w2, m_final_norm, v_ffn1_norm, v_ffn1_w1, v_ffn1_w2, v_mix_norm, v_w_in, v_b_in, v_lru_conv_w, v_lru_conv_b, v_lru_w_gate, v_lru_b_gate, v_lru_lambda, v_lru_w_out, v_q_norm, v_w_uq, v_kv_norm, v_w_ukv, v_mla_w_o, v_conv_dw_w, v_conv_dw_b, v_conv_ln_g, v_conv_ln_b, v_conv_w_out, v_conv_b_out, v_w_out, v_ffn2_norm, v_ffn2_w1, v_ffn2_w2, v_final_norm), strict=True)))
```

```python
import functools

import jax
import jax.numpy as jnp
from jax import lax
from jax.experimental import pallas as pl
from jax.experimental.pallas import tpu as pltpu

F32, BF16 = jnp.float32, jnp.bfloat16
S = jax.ShapeDtypeStruct

LANES = 128
VMEM_LIMIT = 56 * 2**20
NORM_EPS = 1e-6
LRU_C = 8.0
MLA_HEADS = 8
QK_NOPE, QK_ROPE, V_HEAD = 64, 32, 64
HEAD_PAD = 128
ROPE_THETA = 10000.0
ADAM_LR, ADAM_B1, ADAM_B2, ADAM_EPS, ADAM_WD, ADAM_STEP = 0.001, 0.9, 0.999, 1e-08, 0.01, 10
MESH_AXES = ("x", "y", "c")
N_CHIPS = 4
GRAD_DTYPE = BF16
NT =(((1,), (1,)), ((), ()))
TN = (((0,), (0,)), ((), ()))
NN = (((1,), (0,)), ((), ()))


def _tile(n, cap, unit=LANES):
    best = None
    for d in range(unit, min(n, cap) + 1, unit):
        if n % d == 0:
            best = d
    return best if best is not None else n


def _params(sem):
    return pltpu.CompilerParams(dimension_semantics=sem, vmem_limit_bytes=VMEM_LIMIT)


def _mm(a, b, mode="nn", out_dtype=F32, bias=None, res=None, alpha=1.0, a_blocks=False, b_chips=False, out_chips=False, name="mm"):
    units_n, units_k = [], []
    if b_chips:
        blocks, rows, c = b.shape
        b_shape = (rows, blocks * c)
        (units_k if mode == "nt" else units_n).append(c)
    else:
        b_shape = b.shape
    if a_blocks:
        blocks_a, rows_a, c_a = a.shape
        a_shape = (rows_a, blocks_a * c_a)
        units_k.append(c_a)
    else:
        a_shape = a.shape
    if mode == "nn":
        (m, k), (k2, n), dims = a_shape, b_shape, NN
    elif mode == "nt":
        (m, k), (n, k2), dims = a_shape, b_shape, NT
    else:
        (k, m), (k2, n), dims = a_shape, b_shape, TN
    assert k == k2 and not (a_blocks and mode == "tn"), (name, a.shape, b.shape, mode)
    if out_chips:
        units_n.append(n // N_CHIPS)
    tm = _tile(m, 512)
    tn = _tile(min(units_n) if units_n else n, 1536)
    tk = k if (k <= 3072 and not units_k) else _tile(min(units_k) if units_k else k, 3072)
    nk = k // tk
    assert all(u % tn == 0 for u in units_n) and all(u % tk == 0 for u in units_k), (name, units_n, units_k, tn, tk)
    if a_blocks:
        a_spec = pl.BlockSpec((None, tm, tk), functools.partial(lambda i, j, kk, per: (kk // per, i, kk % per), per=c_a // tk))
    elif mode == "tn":
        a_spec = pl.BlockSpec((tk, tm), lambda i, j, kk: (kk, i))
    else:
        a_spec = pl.BlockSpec((tm, tk), lambda i, j, kk: (i, kk))
    if b_chips and mode != "nt":
        b_spec = pl.BlockSpec((None, tk, tn), functools.partial(lambda i, j, kk, per: (j // per, kk, j % per), per=c // tn))
    elif b_chips:
        b_spec = pl.BlockSpec((None, tn, tk), functools.partial(lambda i, j, kk, per: (kk // per, j, kk % per), per=c // tk))
    elif mode == "nt":
        b_spec = pl.BlockSpec((tn, tk), lambda i, j, kk: (j, kk))
    else:
        b_spec = pl.BlockSpec((tk, tn), lambda i, j, kk: (kk, j))
    if out_chips:
        out_spec = pl.BlockSpec((None, tm, tn), functools.partial(lambda i, j, kk, per: (j // per, i, j % per), per=n // N_CHIPS // tn))
        out_shape = S((N_CHIPS, m, n // N_CHIPS), out_dtype)
    else:
        out_spec = pl.BlockSpec((tm, tn), lambda i, j, kk: (i, j))
        out_shape = S((m, n), out_dtype)
    operands, in_specs = [a, b], [a_spec, b_spec]
    if bias is not None:
        operands.append(bias)
        in_specs.append(pl.BlockSpec((1, tn), lambda i, j, kk: (0, j)))
    if res is not None:
        operands.append(res)
        in_specs.append(pl.BlockSpec((tm, tn), lambda i, j, kk: (i, j)))

    def body(*refs):
        a_ref, b_ref = refs[0], refs[1]
        pos = 2
        bias_ref = res_ref = None
        if bias is not None:
            bias_ref, pos = refs[pos], pos + 1
        if res is not None:
            res_ref, pos = refs[pos], pos + 1
        o_ref = refs[pos]
        part = lax.dot_general(a_ref[...].astype(BF16), b_ref[...].astype(BF16), dims, preferred_element_type=F32)

        def finish(acc):
            out = acc if alpha == 1.0 else acc * alpha
            if bias_ref is not None:
                out = out + bias_ref[...]
            if res_ref is not None:
                out = out + res_ref[...]
            o_ref[...] = out.astype(o_ref.dtype)

        if nk == 1:
            finish(part)
        else:
            acc_ref = refs[pos + 1]
            kk = pl.program_id(2)

            @pl.when(kk == 0)
            def _():
                acc_ref[...] = part

            @pl.when(kk > 0)
            def _():
                acc_ref[...] += part

            @pl.when(kk == nk - 1)
            def _():
                finish(acc_ref[...])

    return pl.pallas_call(
        body, name=name, grid=(m // tm, n // tn, nk), in_specs=in_specs, out_specs=out_spec, out_shape=out_shape,
        scratch_shapes=[pltpu.VMEM((tm, tn), F32)] if nk > 1 else [],
        compiler_params=_params(("parallel", "parallel", "arbitrary")),
    )(*operands)


def _rowwise(fn, rows, params=(), outs=(), accs=(), tt=256, name="rowwise"):
    rows = [r if isinstance(r, tuple) else (r, r.shape[1], 0) for r in rows]
    t = rows[0][0].shape[0]
    tt = min(tt, t)
    n_rows, n_par, n_out = len(rows), len(params), len(outs)
    in_specs = [pl.BlockSpec((tt, w), functools.partial(lambda i, cb: (i, cb), cb=cb)) for (_, w, cb) in rows]
    in_specs += [pl.BlockSpec(p.shape, functools.partial(lambda i, nd: (0,) * nd, nd=p.ndim)) for p in params]
    out_shape = [S((t, w), dt) for (w, dt) in outs] + [S(shape, dt) for (shape, dt) in accs]
    out_specs = [pl.BlockSpec((tt, w), lambda i: (i, 0)) for (w, _) in outs]
    out_specs += [pl.BlockSpec(shape, functools.partial(lambda i, nd: (0,) * nd, nd=len(shape))) for (shape, _) in accs]

    def body(*refs):
        vals = [r[...] for r in refs[:n_rows + n_par]]
        o_vals, a_vals = fn(*vals)
        o_refs = refs[n_rows + n_par:n_rows + n_par + n_out]
        a_refs = refs[n_rows + n_par + n_out:]
        for ref, val in zip(o_refs, o_vals, strict=True):
            ref[...] = val.astype(ref.dtype)
        i = pl.program_id(0)
        for ref, val in zip(a_refs, a_vals, strict=True):
            @pl.when(i == 0)
            def _(ref=ref, val=val):
                ref[...] = val.astype(ref.dtype)

            @pl.when(i > 0)
            def _(ref=ref, val=val):
                ref[...] += val.astype(ref.dtype)

    res = pl.pallas_call(
        body, name=name, grid=(t // tt,), in_specs=in_specs, out_specs=out_specs, out_shape=out_shape,
        compiler_params=_params(("arbitrary",) if accs else ("parallel",)),
    )(*[r[0] for r in rows], *params)
    return res


def _rms(x, g):
    x = x.astype(F32)
    return x * lax.rsqrt(jnp.mean(x * x, axis=-1, keepdims=True) + NORM_EPS) * g


def _layer_norm_silu(x, g, b):
    mu = jnp.mean(x, axis=-1, keepdims=True)
    var = jnp.mean(jnp.square(x - mu), axis=-1, keepdims=True)
    return jax.nn.silu((x - mu) * lax.rsqrt(var + NORM_EPS) * g + b)


def _neg_expm1(z):
    series = -z * (1.0 + z * (0.5 + z * (1.0 / 6.0 + z * (1.0 / 24.0 + z * (1.0 / 120.0)))))
    return jnp.where(z > -0.05, series, 1.0 - jnp.exp(z))


def _shift_down(x, s, fill=0.0):
    if s == 0:
        return x
    row = lax.broadcasted_iota(jnp.int32, x.shape, 0)
    return jnp.where(row >= s, pltpu.roll(x, s, 0), fill)


def _shift_up(x, s, fill=0.0):
    if s == 0:
        return x
    t = x.shape[0]
    row = lax.broadcasted_iota(jnp.int32, x.shape, 0)
    return jnp.where(row < t - s, pltpu.roll(x, t - s, 0), fill)


def _scan(a, u, shift):
    t, d = a.shape[0], 1
    while d < t:
        u = u + a * shift(u, d, 0.0)
        if 2 * d < t:
            a = a * shift(a, d, 1.0)
        d *= 2
    return u


def _lru_gates(xa, wr, wi, br, bi, lam):
    xb = xa.astype(BF16)
    r = jax.nn.sigmoid(jnp.dot(xb, wr.astype(BF16), preferred_element_type=F32) + br)
    i = jax.nn.sigmoid(jnp.dot(xb, wi.astype(BF16), preferred_element_type=F32) + bi)
    log_a = -LRU_C * r * jax.nn.softplus(-lam)
    return jnp.exp(log_a), jnp.sqrt(_neg_expm1(2.0 * log_a)) * (i * xa)


def _conv_fwd(x, w_ref, b, width):
    y = b + w_ref[pl.ds(width - 1, 1), :] * x
    for j in range(width - 1):
        y = y + w_ref[pl.ds(j, 1), :] * _shift_down(x, width - 1 - j)
    return y


def _conv_bwd(x, dy, w_ref, dw_ref, width):
    dx = w_ref[pl.ds(width - 1, 1), :] * dy
    dw_ref[pl.ds(width - 1, 1), :] = jnp.sum(dy * x, axis=0, keepdims=True)
    for j in range(width - 1):
        s = width - 1 - j
        dx = dx + w_ref[pl.ds(j, 1), :] * _shift_up(dy, s)
        dw_ref[pl.ds(j, 1), :] = jnp.sum(dy * _shift_down(x, s), axis=0, keepdims=True)
    return dx


def _rope(z, c, s1, s2):
    return z * c + pltpu.roll(z, HEAD_PAD - QK_ROPE // 2, 1) * s1 + pltpu.roll(z, QK_ROPE // 2, 1) * s2


def _rope_t(d, c, s1, s2):
    return d * c + pltpu.roll(d * s1, QK_ROPE // 2, 1) + pltpu.roll(d * s2, HEAD_PAD - QK_ROPE // 2, 1)


def _heads(z):
    return [z[:, h * HEAD_PAD:(h + 1) * HEAD_PAD] for h in range(z.shape[1] // HEAD_PAD)]


def _chan_spec(t, c_off=0):
    return pl.BlockSpec((t, LANES), lambda c: (0, c_off + c))


def _lru_specs(t, n_tiles, width):
    vec = pl.BlockSpec((1, LANES), lambda c: (0, c))
    mat = pl.BlockSpec((1, LANES, LANES), lambda c: (c, 0, 0))
    return [_chan_spec(t), _chan_spec(t, n_tiles), pl.BlockSpec((width, LANES), lambda c: (0, c)), vec, mat, mat, vec, vec, vec]


def _lru_fwd(pa, cw, cb, wr, wi, br, bi, lam, name):
    t, w = pa.shape[0], pa.shape[1] // 2
    n_tiles, width = w // LANES, cw.shape[0]

    def body(x_ref, g_ref, cw_ref, cb_ref, wr_ref, wi_ref, br_ref, bi_ref, lam_ref, y_ref):
        xa = _conv_fwd(x_ref[...], cw_ref, cb_ref[...], width)
        a, u = _lru_gates(xa, wr_ref[0], wi_ref[0], br_ref[...], bi_ref[...], lam_ref[...])
        h = _scan(a, u, _shift_down)
        y_ref[...] = (h * jax.nn.gelu(g_ref[...])).astype(y_ref.dtype)

    return pl.pallas_call(
        body, name=name, grid=(n_tiles,), in_specs=_lru_specs(t, n_tiles, width), out_specs=_chan_spec(t),
        out_shape=S((t, w), BF16), compiler_params=_params(("parallel",)),
    )(pa, pa, cw, cb, wr, wi, br, bi, lam)


def _lru_bwd(pa, dy, cw, cb, wr, wi, br, bi, lam, name):
    t, w = pa.shape[0], pa.shape[1] // 2
    n_tiles, width = w // LANES, cw.shape[0]

    def body(x_ref, g_ref, cw_ref, cb_ref, wr_ref, wi_ref, br_ref, bi_ref, lam_ref, dy_ref,
             dx_ref, dg_ref, dcw_ref, dcb_ref, dwr_ref, dwi_ref, dbr_ref, dbi_ref, dlam_ref, sx_ref, sg_ref):
        x = x_ref[...]
        xa = _conv_fwd(x, cw_ref, cb_ref[...], width)
        (a, u), gates_vjp = jax.vjp(_lru_gates, xa, wr_ref[0], wi_ref[0], br_ref[...], bi_ref[...], lam_ref[...])
        h = _scan(a, u, _shift_down)
        _, out_vjp = jax.vjp(lambda h_, g_: h_ * jax.nn.gelu(g_), h, g_ref[...])
        dh, dgate = out_vjp(dy_ref[...])
        adj = _scan(_shift_up(a, 1), dh, _shift_up)
        dxa, dwr, dwi, dbr, dbi, dlam = gates_vjp((adj * _shift_down(h, 1), adj))
        dx = _conv_bwd(x, dxa, cw_ref, dcw_ref, width)
        dcb_ref[...] = jnp.sum(dxa, axis=0, keepdims=True)
        dx_ref[...] = dx.astype(dx_ref.dtype)
        dg_ref[...] = dgate.astype(dg_ref.dtype)
        sx_ref[...] = jnp.sum(dx, axis=0, keepdims=True)
        sg_ref[...] = jnp.sum(dgate, axis=0, keepdims=True)
        dwr_ref[0], dwi_ref[0] = dwr, dwi
        dbr_ref[...], dbi_ref[...], dlam_ref[...] = dbr, dbi, dlam

    vec = pl.BlockSpec((1, LANES), lambda c: (0, c))
    mat = pl.BlockSpec((1, LANES, LANES), lambda c: (c, 0, 0))
    vec_s, mat_s = S((1, w), F32), S((n_tiles, LANES, LANES), F32)
    return pl.pallas_call(
        body, name=name, grid=(n_tiles,), in_specs=_lru_specs(t, n_tiles, width) + [_chan_spec(t)],
        out_specs=[_chan_spec(t), _chan_spec(t), pl.BlockSpec((width, LANES), lambda c: (0, c)), vec, mat, mat, vec, vec, vec, vec, vec],
        out_shape=[S((t, w), BF16), S((t, w), BF16), S((width, w), F32), vec_s, mat_s, mat_s, vec_s, vec_s, vec_s, vec_s, vec_s],
        compiler_params=_params(("parallel",)),
    )(pa, pa, cw, cb, wr, wi, br, bi, lam, dy)


def _glu_conv_fwd(pc, cw, cb, name):
    t, c = pc.shape[0], pc.shape[1] // 2
    n_tiles, width = c // LANES, cw.shape[0]

    def body(v_ref, g_ref, cw_ref, cb_ref, y_ref):
        y_ref[...] = _conv_fwd(v_ref[...] * jax.nn.sigmoid(g_ref[...]), cw_ref, cb_ref[...], width)

    return pl.pallas_call(
        body, name=name, grid=(n_tiles,),
        in_specs=[_chan_spec(t), _chan_spec(t, n_tiles), pl.BlockSpec((width, LANES), lambda i: (0, i)), pl.BlockSpec((1, LANES), lambda i: (0, i))],
        out_specs=_chan_spec(t), out_shape=S((t, c), F32), compiler_params=_params(("parallel",)),
    )(pc, pc, cw, cb)


def _glu_conv_bwd(pc, dy, cw, name):
    t, c = pc.shape[0], pc.shape[1] // 2
    n_tiles, width = c // LANES, cw.shape[0]

    def body(v_ref, g_ref, cw_ref, dy_ref, dv_ref, dg_ref, dcw_ref, dcb_ref, sv_ref, sg_ref):
        glu = lambda v_, g_: v_ * jax.nn.sigmoid(g_)
        x, glu_vjp = jax.vjp(glu, v_ref[...], g_ref[...])
        dy_ = dy_ref[...]
        dv, dg = glu_vjp(_conv_bwd(x, dy_, cw_ref, dcw_ref, width))
        dcb_ref[...] = jnp.sum(dy_, axis=0, keepdims=True)
        dv_ref[...] = dv.astype(dv_ref.dtype)
        dg_ref[...] = dg.astype(dg_ref.dtype)
        sv_ref[...] = jnp.sum(dv, axis=0, keepdims=True)
        sg_ref[...] = jnp.sum(dg, axis=0, keepdims=True)

    vec = pl.BlockSpec((1, LANES), lambda i: (0, i))
    wspec = pl.BlockSpec((width, LANES), lambda i: (0, i))
    return pl.pallas_call(
        body, name=name, grid=(n_tiles,), in_specs=[_chan_spec(t), _chan_spec(t, n_tiles), wspec, _chan_spec(t)],
        out_specs=[_chan_spec(t), _chan_spec(t), wspec, vec, vec, vec],
        out_shape=[S((t, c), BF16), S((t, c), BF16), S((width, c), F32), S((1, c), F32), S((1, c), F32), S((1, c), F32)],
        compiler_params=_params(("parallel",)),
    )(pc, pc, cw, dy)


def _softmax_rows(q, k, causal, scale):
    s = lax.dot_general(q, k, NT, preferred_element_type=F32) * scale
    s = jnp.where(causal, s, jnp.finfo(F32).min)
    p = jnp.exp(s - jnp.max(s, axis=-1, keepdims=True))
    return p / jnp.sum(p, axis=-1, keepdims=True)


def _attn_specs(t, tq):
    return [pl.BlockSpec((tq, 2 * HEAD_PAD), lambda hp, i: (i, hp)), pl.BlockSpec((t, 2 * HEAD_PAD), lambda hp, i: (0, hp)),
            pl.BlockSpec((t, 2 * V_HEAD), lambda hp, i: (0, hp))]


def _causal(n, tq):
    row = lax.broadcasted_iota(jnp.int32, (tq, (n + 1) * tq), 0) + n * tq
    col = lax.broadcasted_iota(jnp.int32, (tq, (n + 1) * tq), 1)
    return col <= row


def _per_query_block(n_blocks, fn):
    for n in range(n_blocks):
        pl.when(pl.program_id(1) == n)(functools.partial(fn, n))


def _attn_fwd(qh, kh, v, name):
    t = qh.shape[0]
    tq = min(256, t)
    scale = (QK_NOPE + QK_ROPE) ** -0.5

    def body(q_ref, k_ref, v_ref, o_ref):
        def block(n):
            keys = (n + 1) * tq
            causal = _causal(n, tq)
            lane = lax.broadcasted_iota(jnp.int32, (keys, 2 * V_HEAD), 1)
            vv = v_ref[0:keys, :]
            acc = jnp.zeros((tq, 2 * V_HEAD), F32)
            for e in range(2):
                p = _softmax_rows(q_ref[:, e * HEAD_PAD:(e + 1) * HEAD_PAD], k_ref[0:keys, e * HEAD_PAD:(e + 1) * HEAD_PAD], causal, scale)
                ve = jnp.where((lane >= V_HEAD * e) & (lane < V_HEAD * (e + 1)), vv, jnp.zeros_like(vv))
                acc = acc + jnp.dot(p.astype(BF16), ve, preferred_element_type=F32)
            o_ref[...] = acc.astype(o_ref.dtype)

        _per_query_block(t // tq, block)

    return pl.pallas_call(
        body, name=name, grid=(MLA_HEADS // 2, t // tq), in_specs=_attn_specs(t, tq),
        out_specs=pl.BlockSpec((tq, 2 * V_HEAD), lambda hp, i: (i, hp)), out_shape=S((t, MLA_HEADS * V_HEAD), BF16),
        compiler_params=_params(("parallel", "parallel")),
    )(qh, kh, v)


def _attn_bwd(qh, kh, v, do, name):
    t = qh.shape[0]
    tq = min(256, t)
    scale = (QK_NOPE + QK_ROPE) ** -0.5

    def body(q_ref, k_ref, v_ref, do_ref, dq_ref, dk_ref, dv_ref):
        def block(n):
            keys = (n + 1) * tq
            causal = _causal(n, tq)
            lane = lax.broadcasted_iota(jnp.int32, (tq, 2 * V_HEAD), 1)
            vv, dd = v_ref[0:keys, :], do_ref[...]
            dqs, dks = [], []
            dv = jnp.zeros((keys, 2 * V_HEAD), F32)
            for e in range(2):
                q, k = q_ref[:, e * HEAD_PAD:(e + 1) * HEAD_PAD], k_ref[0:keys, e * HEAD_PAD:(e + 1) * HEAD_PAD]
                p = _softmax_rows(q, k, causal, scale)
                de = jnp.where((lane >= V_HEAD * e) & (lane < V_HEAD * (e + 1)), dd, jnp.zeros_like(dd))
                dp = lax.dot_general(de, vv, NT, preferred_element_type=F32)
                ds = (p * (dp - jnp.sum(p * dp, axis=-1, keepdims=True)) * scale).astype(BF16)
                dqs.append(jnp.dot(ds, k, preferred_element_type=F32))
                dks.append(lax.dot_general(ds, q, TN, preferred_element_type=F32))
                dv = dv + lax.dot_general(p.astype(BF16), de, TN, preferred_element_type=F32)
            dq_ref[...] = jnp.concatenate(dqs, axis=-1)
            dk = jnp.concatenate(dks, axis=-1)
            if n == 0:
                dk_ref[0:keys, :], dv_ref[0:keys, :] = dk, dv
                if keys < t:
                    dk_ref[keys:t, :] = jnp.zeros((t - keys, 2 * HEAD_PAD), F32)
                    dv_ref[keys:t, :] = jnp.zeros((t - keys, 2 * V_HEAD), F32)
            else:
                dk_ref[0:keys, :] += dk
                dv_ref[0:keys, :] += dv

        _per_query_block(t // tq, block)

    return pl.pallas_call(
        body, name=name, grid=(MLA_HEADS // 2, t // tq),
        in_specs=_attn_specs(t, tq) + [pl.BlockSpec((tq, 2 * V_HEAD), lambda hp, i: (i, hp))],
        out_specs=[pl.BlockSpec((tq, 2 * HEAD_PAD), lambda hp, i: (i, hp)), pl.BlockSpec((t, 2 * HEAD_PAD), lambda hp, i: (0, hp)),
                   pl.BlockSpec((t, 2 * V_HEAD), lambda hp, i: (0, hp))],
        out_shape=[S((t, MLA_HEADS * HEAD_PAD), F32), S((t, MLA_HEADS * HEAD_PAD), F32), S((t, MLA_HEADS * V_HEAD), F32)],
        compiler_params=_params(("parallel", "arbitrary")),
    )(qh, kh, v, do)


def _swiglu(gate, up):
    return jax.nn.silu(gate) * up


def _ffn_up(h, w1, name):
    t, d = h.shape
    chips, _, c = w1.shape
    f = chips * c // 2
    tm, tn = _tile(t, 512), _tile(c, 1536)
    per = c // tn

    def body(h_ref, wg_ref, wu_ref, gu_ref, a_ref):
        hb = h_ref[...]
        gate = jnp.dot(hb, wg_ref[...], preferred_element_type=F32).astype(BF16)
        up = jnp.dot(hb, wu_ref[...], preferred_element_type=F32).astype(BF16)
        gu_ref[0], gu_ref[1] = gate, up
        a_ref[...] = _swiglu(gate.astype(F32), up.astype(F32)).astype(a_ref.dtype)

    w_spec = lambda first: pl.BlockSpec((None, d, tn), lambda i, j: (first + j // per, 0, j % per))
    return pl.pallas_call(
        body, name=name, grid=(t // tm, f // tn), in_specs=[pl.BlockSpec((tm, d), lambda i, j: (i, 0)), w_spec(0), w_spec(chips // 2)],
        out_specs=[pl.BlockSpec((2, tm, tn), lambda i, j: (0, i, j)), pl.BlockSpec((tm, tn), lambda i, j: (i, j))],
        out_shape=[S((2, t, f), BF16), S((t, f), BF16)], compiler_params=_params(("parallel", "parallel")),
    )(h, w1, w1)


def _ffn_dgu(dy, w2, gu, bias, name):
    t, d = dy.shape
    f = w2.shape[0]
    tm, tn = _tile(t, 512), _tile(f, 1536)

    def body(*refs):
        dy_ref, w2_ref, gu_ref, o_ref = refs[0], refs[1], refs[2], refs[-1]
        da = 0.5 * lax.dot_general(dy_ref[...].astype(BF16), w2_ref[...], NT, preferred_element_type=F32)
        if bias is not None:
            da = da + refs[3][...]
        _, vjp = jax.vjp(_swiglu, gu_ref[0].astype(F32), gu_ref[1].astype(F32))
        dgate, dup = vjp(da)
        o_ref[0], o_ref[1] = dgate.astype(o_ref.dtype), dup.astype(o_ref.dtype)

    blk = pl.BlockSpec((2, tm, tn), lambda i, j: (0, i, j))
    in_specs = [pl.BlockSpec((tm, d), lambda i, j: (i, 0)), pl.BlockSpec((tn, d), lambda i, j: (j, 0)), blk]
    if bias is not None:
        in_specs.append(pl.BlockSpec((1, tn), lambda i, j: (0, j)))
    return pl.pallas_call(body, name=name, grid=(t // tm, f // tn), in_specs=in_specs, out_specs=blk, out_shape=S((2, t, f), BF16),
                          compiler_params=_params(("parallel", "parallel")))(dy, w2, gu, *([] if bias is None else [bias]))


def _ffn_fwd(x, g, w1, w2, tag):
    h, = _rowwise(lambda x_, g_: ([_rms(x_, g_)], []), [x], [g], outs=[(x.shape[1], BF16)], name=tag + "_rms")
    gu, a = _ffn_up(h, w1, tag + "_up")
    return _mm(a, w2, res=x, alpha=0.5, name=tag + "_down"), (x, h, gu, a)


def _after(dep, n):
    return None if dep is None else jnp.zeros((1, n), F32) + dep


def _ffn_bwd(dy, saved, g, w1, w2, tag, dep=None, mid=None):
    x, h, gu, a = saved
    d, f = x.shape[1], w2.shape[0]
    dgu = _ffn_dgu(dy, w2, gu, _after(dep, f), tag + "_dgu")
    dep = None if mid is None else mid(dgu)
    dw2 = _mm(a, dy, "tn", GRAD_DTYPE, alpha=0.5, bias=_after(dep, d), name=tag + "_dw2")
    dw1 = _mm(h, dgu, "tn", GRAD_DTYPE, bias=_after(dep, 2 * f), b_chips=True, out_chips=True, name=tag + "_dw1")
    dx, dg = _ffn_dx(dgu, w1, x, g, dy, _after(dep, d), tag + "_dx")
    return dx, (dg, dw1, dw2)


def _ffn_dx(dgu, w1, x, g, dres, bias, name):
    blocks, t, f = dgu.shape
    chips, d, c = w1.shape
    tm, tk = _tile(t, 512), _tile(min(c, f), 3072)
    nk = blocks * f // tk

    def body(*refs):
        dgu_ref, w_ref, x_ref, g_ref, dres_ref = refs[:5]
        dx_ref, dg_ref, acc_ref = refs[-3:]
        i, kk = pl.program_id(0), pl.program_id(1)
        part = lax.dot_general(dgu_ref[...], w_ref[...], NT, preferred_element_type=F32)

        @pl.when(kk == 0)
        def _():
            acc_ref[...] = part

        @pl.when(kk > 0)
        def _():
            acc_ref[...] += part

        @pl.when(kk == nk - 1)
        def _():
            dh = acc_ref[...] if bias is None else acc_ref[...] + refs[5][...]
            _, vjp = jax.vjp(_rms, x_ref[...], g_ref[...])
            dx, dg = vjp(dh)
            dx_ref[...] = dx + dres_ref[...]

            @pl.when(i == 0)
            def _():
                dg_ref[...] = dg

            @pl.when(i > 0)
            def _():
                dg_ref[...] += dg

    row = pl.BlockSpec((tm, d), lambda i, kk: (i, 0))
    vec = pl.BlockSpec((1, d), lambda i, kk: (0, 0))
    in_specs = [pl.BlockSpec((None, tm, tk), functools.partial(lambda i, kk, per: (kk // per, i, kk % per), per=f // tk)),
                pl.BlockSpec((None, d, tk), functools.partial(lambda i, kk, per: (kk // per, 0, kk % per), per=c // tk)), row, vec, row]
    operands = [dgu, w1, x, g, dres]
    if bias is not None:
        in_specs.append(vec)
        operands.append(bias)
    return pl.pallas_call(body, name=name, grid=(t // tm, nk), in_specs=in_specs, out_specs=[row, vec],
                          out_shape=[S((t, d), F32), S((1, d), F32)], scratch_shapes=[pltpu.VMEM((tm, d), F32)],
                          compiler_params=_params(("arbitrary", "arbitrary")))(*operands)


def _rms_bwd(x, dh, dres, g, name):
    def fn(x_, dh_, dres_, g_):
        _, vjp = jax.vjp(_rms, x_, g_)
        dx, dg = vjp(dh_)
        return [dx + dres_], [dg]

    return _rowwise(fn, [x, dh, dres], [g], outs=[(x.shape[1], F32)], accs=[((1, x.shape[1]), F32)], name=name)


def _rope_tables(positions, name):
    half = QK_ROPE // 2
    inv = ROPE_THETA ** (-jnp.arange(0, QK_ROPE, 2, dtype=F32) / QK_ROPE)
    inv_lanes = jnp.zeros((1, HEAD_PAD), F32).at[0, QK_NOPE:QK_NOPE + QK_ROPE].set(jnp.tile(inv, 2))

    def fn(pos, inv_):
        ang = pos.astype(F32) * inv_
        lane = lax.broadcasted_iota(jnp.int32, ang.shape, 1)
        cos, sin = jnp.cos(ang), jnp.sin(ang)
        c = jnp.where(lane < QK_NOPE, 1.0, jnp.where(lane < QK_NOPE + QK_ROPE, cos, 0.0))
        s1 = jnp.where((lane >= QK_NOPE) & (lane < QK_NOPE + half), -sin, 0.0)
        s2 = jnp.where((lane >= QK_NOPE + half) & (lane < QK_NOPE + QK_ROPE), sin, 0.0)
        return [c, s1, s2], []

    return _rowwise(fn, [positions.reshape(-1, 1)], [inv_lanes], outs=[(HEAD_PAD, F32)] * 3, name=name)


def _chip_cols(g, lo, hi):
    c = g.shape[-1]
    parts = [g[j, :, max(lo, j * c) - j * c:min(hi, (j + 1) * c) - j * c] for j in range(g.shape[0]) if max(lo, j * c) < min(hi, (j + 1) * c)]
    return parts[0] if len(parts) == 1 else jnp.concatenate(parts, axis=-1)


def _cols_by_chip(segments, c):
    out, start = [[] for _ in range(N_CHIPS)], 0
    for arr, first, width in segments:
        for j in range(N_CHIPS):
            lo, hi = max(start, j * c), min(start + width, (j + 1) * c)
            if lo < hi:
                out[j].append(arr[:, first + lo - start:first + hi - start])
        start += width
    assert start == N_CHIPS * c
    return jnp.stack([jnp.concatenate(parts, axis=-1) for parts in out])


def _whole_cols(g):
    return g.transpose(1, 0, 2).reshape(g.shape[1], -1)


def _mix_offsets(lw):
    wl, ql, kvl = lw["lru_lambda"].shape[-1], lw["q_norm"].shape[-1], lw["kv_norm"].shape[-1]
    o1 = 2 * wl
    o2 = o1 + ql + kvl + QK_ROPE
    return wl, ql, kvl, o1, o2, o2 + 2 * lw["conv_ln_g"].shape[-1]


def _mix_weights(lw):
    wl, ql, kvl, o1, o2, o3 = _mix_offsets(lw)
    w_in, b_in = lw["w_in"], lw["b_in"][None, :]
    d_in = b_in.shape[1]
    z = lambda m, n: jnp.zeros(m.shape[:-1] + (n,), m.dtype)
    w_b = jnp.concatenate([_chip_cols(w_in, o1 + ql, o1 + ql + kvl), z(w_in[0], QK_NOPE), _chip_cols(w_in, o1 + ql + kvl, o2),
                           z(w_in[0], HEAD_PAD - QK_NOPE - QK_ROPE), _chip_cols(w_in, o1, o1 + ql)], axis=-1)
    b_b = jnp.concatenate([b_in[:, o1 + ql:o1 + ql + kvl], z(b_in, QK_NOPE), b_in[:, o1 + ql + kvl:o2],
                           z(b_in, HEAD_PAD - QK_NOPE - QK_ROPE), b_in[:, o1:o1 + ql]], axis=-1)
    hd = lw["lru_w_gate"].shape[-2]
    per = LANES // hd
    eye = jnp.eye(per, dtype=F32)
    wg = lw["lru_w_gate"].reshape(-1, per, hd, 2 * hd)
    block_diag = lambda m: jnp.einsum("cedk,ef->cedfk", m, eye).reshape(-1, LANES, LANES)
    bg = lw["lru_b_gate"]
    w_uq = _whole_cols(lw["w_uq"]).reshape(ql, MLA_HEADS, QK_NOPE + QK_ROPE)
    w_ukv = _whole_cols(lw["w_ukv"]).reshape(kvl, MLA_HEADS, QK_NOPE + V_HEAD)
    pad = lambda m, n: jnp.pad(m, ((0, 0), (0, 0), (0, n)))
    return dict(
        w_a=_chip_cols(w_in, 0, o1), w_b=w_b, w_c=_chip_cols(w_in, o2, o3), w_g=_chip_cols(w_in, o3, d_in),
        b_a=b_in[:, :o1], b_b=b_b, b_c=b_in[:, o2:o3], b_g=b_in[:, o3:],
        wr=block_diag(wg[..., :hd]), wi=block_diag(wg[..., hd:]),
        br=bg[:, :hd].reshape(1, -1), bi=bg[:, hd:].reshape(1, -1),
        w_uq=pad(w_uq, HEAD_PAD - QK_NOPE - QK_ROPE).reshape(ql, -1),
        w_k=pad(w_ukv[..., :QK_NOPE], HEAD_PAD - QK_NOPE).reshape(kvl, -1),
        w_v=w_ukv[..., QK_NOPE:].reshape(kvl, -1),
    )


def _mix_fwd(x, p, mw, rope, tag):
    d = x.shape[1]
    wl, ql, kvl = p["lru_lambda"].shape[-1], p["q_norm"].shape[-1], p["kv_norm"].shape[-1]
    row = lambda name: p[name][None, :]
    h, = _rowwise(lambda x_, g_: ([_rms(x_, g_)], []), [x], [row("mix_norm")], outs=[(d, BF16)], name=tag + "_rms")
    pa = _mm(h, mw["w_a"], bias=mw["b_a"], name=tag + "_pa")
    pb = _mm(h, mw["w_b"], bias=mw["b_b"], name=tag + "_pb")
    pc = _mm(h, mw["w_c"], bias=mw["b_c"], name=tag + "_pc")
    pg = _mm(h, mw["w_g"], bias=mw["b_g"], name=tag + "_pg")
    lru_args = (p["lru_conv_w"], row("lru_conv_b"), mw["wr"], mw["wi"], mw["br"], mw["bi"], row("lru_lambda"))
    ya_pre = _lru_fwd(pa, *lru_args, name=tag + "_lru")
    y_a = _mm(ya_pre, p["lru_w_out"], b_chips=True, name=tag + "_ya")
    mla_rows = [(pb, kvl, 0), (pb, ql, (kvl + HEAD_PAD) // ql)]
    assert (kvl + HEAD_PAD) % ql == 0 and kvl % HEAD_PAD == 0
    ckvn, cqn = _rowwise(lambda kv_, q_, gk, gq: ([_rms(kv_, gk), _rms(q_, gq)], []), mla_rows, [row("kv_norm"), row("q_norm")],
                         outs=[(kvl, BF16), (ql, BF16)], name=tag + "_lat_rms")
    q0 = _mm(cqn, mw["w_uq"], name=tag + "_q")
    k0 = _mm(ckvn, mw["w_k"], name=tag + "_k")
    v = _mm(ckvn, mw["w_v"], out_dtype=BF16, name=tag + "_v")

    def rope_fwd(q_, k_, kpe, c, s1, s2):
        kr = _rope(kpe, c, s1, s2)
        return [jnp.concatenate([_rope(z, c, s1, s2) for z in _heads(q_)], axis=-1),
                jnp.concatenate([z + kr for z in _heads(k_)], axis=-1)], []

    qh, kh = _rowwise(rope_fwd, [q0, k0, (pb, HEAD_PAD, kvl // HEAD_PAD), *rope],
                      outs=[(q0.shape[1], BF16), (k0.shape[1], BF16)], name=tag + "_rope")
    o = _attn_fwd(qh, kh, v, tag + "_attn")
    y_b = _mm(o, p["mla_w_o"], b_chips=True, name=tag + "_yb")
    c2 = _glu_conv_fwd(pc, p["conv_dw_w"], row("conv_dw_b"), tag + "_conv")
    c3, = _rowwise(lambda c_, g_, b_: ([_layer_norm_silu(c_, g_, b_)], []), [c2], [row("conv_ln_g"), row("conv_ln_b")],
                   outs=[(c2.shape[1], BF16)], name=tag + "_ln")
    y_c = _mm(c3, p["conv_w_out"], bias=row("conv_b_out"), b_chips=True, name=tag + "_yc")
    merged, = _rowwise(_merge, [y_a, y_b, y_c, (pg, d, 0), (pg, d, 1), (pg, d, 2)], outs=[(d, BF16)], name=tag + "_merge")
    out = _mm(merged, p["w_out"], res=x, name=tag + "_out")
    saved = dict(x=x, h=h, pa=pa, pb=pb, pc=pc, pg=pg, ya_pre=ya_pre, y_a=y_a, y_b=y_b, y_c=y_c, ckvn=ckvn, cqn=cqn,
                 qh=qh, kh=kh, v=v, o=o, c2=c2, c3=c3, merged=merged, lru_args=lru_args)
    return out, saved


def _merge(ya, yb, yc, g0, g1, g2):
    return [jax.nn.sigmoid(g0) * ya + jax.nn.sigmoid(g1) * yb + jax.nn.sigmoid(g2) * yc], []


def _mix_bwd(dy, s, p, mw, rope, tag, dep=None, mid=None):
    x, h = s["x"], s["h"]
    d = x.shape[1]
    wl, ql, kvl = p["lru_lambda"].shape[-1], p["q_norm"].shape[-1], p["kv_norm"].shape[-1]
    row = lambda name: p[name][None, :]
    g = {}
    dmerged = _mm(dy, p["w_out"], "nt", bias=_after(dep, d), name=tag + "_dmerged")
    g["w_out"] = _mm(s["merged"], dy, "tn", GRAD_DTYPE, name=tag + "_dw_out")

    def merge_bwd(ya, yb, yc, g0, g1, g2, dm):
        _, vjp = jax.vjp(lambda *a: _merge(*a)[0][0], ya, yb, yc, g0, g1, g2)
        dya, dyb, dyc, d0, d1, d2 = vjp(dm)
        dpg = jnp.concatenate([d0, d1, d2], axis=-1)
        return [dya, dyb, dyc, dpg], [jnp.sum(dyc, axis=0, keepdims=True), jnp.sum(dpg, axis=0, keepdims=True)]

    pg = s["pg"]
    dya, dyb, dyc, dpg, g["conv_b_out"], db_g = _rowwise(
        merge_bwd, [s["y_a"], s["y_b"], s["y_c"], (pg, d, 0), (pg, d, 1), (pg, d, 2), dmerged],
        outs=[(d, BF16), (d, BF16), (d, BF16), (3 * d, BF16)], accs=[((1, d), F32), ((1, 3 * d), F32)], tt=128, name=tag + "_dmerge")
    dep = None if mid is None else mid(dya)
    g["lru_w_out"] = _mm(s["ya_pre"], dya, "tn", GRAD_DTYPE, bias=_after(dep, d), out_chips=True, name=tag + "_dw_lru_out")
    dya_pre = _mm(dya, p["lru_w_out"], "nt", bias=_after(dep, wl), b_chips=True, name=tag + "_dya_pre")
    (dpa_x, dpa_g, g["lru_conv_w"], g["lru_conv_b"], g["wr"], g["wi"], g["br"], g["bi"], g["lru_lambda"], sb_x, sb_g) = _lru_bwd(
        s["pa"], dya_pre, *s["lru_args"], name=tag + "_dlru")
    dpa = jnp.concatenate([dpa_x, dpa_g], axis=1)
    db_a = jnp.concatenate([sb_x, sb_g], axis=1)
    g["conv_w_out"] = _mm(s["c3"], dyc, "tn", GRAD_DTYPE, bias=_after(dep, d), out_chips=True, name=tag + "_dw_conv_out")
    dc3 = _mm(dyc, p["conv_w_out"], "nt", bias=_after(dep, s["c3"].shape[1]), b_chips=True, name=tag + "_dc3")

    def ln_bwd(c_, dc_, g_, b_):
        _, vjp = jax.vjp(_layer_norm_silu, c_, g_, b_)
        dc, dg_, db_ = vjp(dc_)
        return [dc], [dg_, db_]

    cc = s["c2"].shape[1]
    dc2, g["conv_ln_g"], g["conv_ln_b"] = _rowwise(ln_bwd, [s["c2"], dc3], [row("conv_ln_g"), row("conv_ln_b")], outs=[(cc, F32)],
                                                    accs=[((1, cc), F32)] * 2, name=tag + "_dln")
    dpc_v, dpc_g, g["conv_dw_w"], g["conv_dw_b"], sc_v, sc_g = _glu_conv_bwd(s["pc"], dc2, p["conv_dw_w"], tag + "_dconv")
    dpc = jnp.concatenate([dpc_v, dpc_g], axis=1)
    db_c = jnp.concatenate([sc_v, sc_g], axis=1)
    g["mla_w_o"] = _mm(s["o"], dyb, "tn", GRAD_DTYPE, bias=_after(dep, d), out_chips=True, name=tag + "_dw_o")
    do = _mm(dyb, p["mla_w_o"], "nt", out_dtype=BF16, bias=_after(dep, s["o"].shape[1]), b_chips=True, name=tag + "_do")
    dqh, dkh, dv = _attn_bwd(s["qh"], s["kh"], s["v"], do, tag + "_dattn")

    def rope_bwd(dq_, dk_, c, s1, s2):
        lane = lax.broadcasted_iota(jnp.int32, c.shape, 1)
        dkr = functools.reduce(lambda a, b: a + b, _heads(dk_))
        dkpe = jnp.where((lane >= QK_NOPE) & (lane < QK_NOPE + QK_ROPE), _rope_t(dkr, c, s1, s2), 0.0)
        return [jnp.concatenate([_rope_t(z, c, s1, s2) for z in _heads(dq_)], axis=-1), dk_, dkpe], []

    dq0, dk0, dkpe = _rowwise(rope_bwd, [dqh, dkh, *rope], outs=[(dqh.shape[1], BF16), (dkh.shape[1], BF16), (HEAD_PAD, F32)],
                              name=tag + "_drope")
    dvb = dv.astype(BF16)
    g["w_uq"] = _mm(s["cqn"], dq0, "tn", GRAD_DTYPE, name=tag + "_dw_uq")
    g["w_k"] = _mm(s["ckvn"], dk0, "tn", GRAD_DTYPE, name=tag + "_dw_k")
    g["w_v"] = _mm(s["ckvn"], dvb, "tn", GRAD_DTYPE, name=tag + "_dw_v")
    dcqn = _mm(dq0, mw["w_uq"], "nt", name=tag + "_dcqn")
    dckvn = _mm(dk0, mw["w_k"], "nt", name=tag + "_dckvn_k")
    dckvn = _mm(dvb, mw["w_v"], "nt", res=dckvn, name=tag + "_dckvn_v")

    def lat_bwd(kv_, q_, dkv_, dq_, dkpe_, gk, gq):
        _, vjp_k = jax.vjp(_rms, kv_, gk)
        _, vjp_q = jax.vjp(_rms, q_, gq)
        (dkv, dgk), (dq, dgq) = vjp_k(dkv_), vjp_q(dq_)
        dpb = jnp.concatenate([dkv, dkpe_, dq], axis=-1)
        return [dpb], [dgk, dgq, jnp.sum(dpb, axis=0, keepdims=True)]

    pb = s["pb"]
    dpb, g["kv_norm"], g["q_norm"], db_b = _rowwise(
        lat_bwd, [(pb, kvl, 0), (pb, ql, (kvl + HEAD_PAD) // ql), dckvn, dcqn, dkpe], [row("kv_norm"), row("q_norm")],
        outs=[(pb.shape[1], BF16)], accs=[((1, kvl), F32), ((1, ql), F32), ((1, pb.shape[1]), F32)], name=tag + "_dlat")
    dh = None
    for part, dpart in (("a", dpa), ("b", dpb), ("c", dpc), ("g", dpg)):
        g["w_" + part] = _mm(h, dpart, "tn", GRAD_DTYPE, name=tag + "_dw_" + part)
        dh = _mm(dpart, mw["w_" + part], "nt", res=dh, name=tag + "_dh_" + part)
    g["b_a"], g["b_b"], g["b_c"], g["b_g"] = db_a, db_b, db_c, db_g
    dx, g["mix_norm"] = _rms_bwd(x, dh, dy, row("mix_norm"), tag + "_drms")
    return dx, g


def _by_chip_cols(m):
    return m.reshape(m.shape[0], N_CHIPS, -1).transpose(1, 0, 2)


def _mix_grads_to_params(g, p):
    wl, ql, kvl, o1, o2, o3 = _mix_offsets(p)
    hd = p["lru_w_gate"].shape[-2]
    per = LANES // hd
    eye = jnp.eye(per, dtype=F32)
    diag = lambda m: jnp.einsum("cedfk,ef->cedk", m.reshape(-1, per, hd, per, hd), eye).reshape(-1, hd, hd)
    out = {k: g[k] for k in ("lru_w_out", "conv_w_out", "mla_w_o", "lru_conv_w", "conv_dw_w")}
    out["w_out"] = g["w_out"].reshape(N_CHIPS, -1, g["w_out"].shape[1])
    for k in ("mix_norm", "conv_b_out", "lru_conv_b", "lru_lambda", "conv_ln_g", "conv_ln_b", "conv_dw_b", "kv_norm", "q_norm"):
        out[k] = g[k][0]
    mla = lambda m: [(m, kvl + HEAD_PAD, ql), (m, 0, kvl), (m, kvl + QK_NOPE, QK_ROPE)]
    whole = lambda m: [(m, 0, m.shape[1])]
    out["w_in"] = _cols_by_chip(whole(g["w_a"]) + mla(g["w_b"]) + whole(g["w_c"]) + whole(g["w_g"]), p["w_in"].shape[-1])
    out["b_in"] = jnp.concatenate([g["b_a"]] + [m[:, a:a + w] for m, a, w in mla(g["b_b"])] + [g["b_c"], g["b_g"]], axis=1)[0]
    out["lru_w_gate"] = jnp.concatenate([diag(g["wr"]), diag(g["wi"])], axis=-1)
    out["lru_b_gate"] = jnp.concatenate([g["br"].reshape(-1, hd), g["bi"].reshape(-1, hd)], axis=-1)
    out["w_uq"] = _by_chip_cols(g["w_uq"].reshape(ql, MLA_HEADS, HEAD_PAD)[..., :QK_NOPE + QK_ROPE].reshape(ql, -1))
    out["w_ukv"] = _by_chip_cols(jnp.concatenate([g["w_k"].reshape(kvl, MLA_HEADS, HEAD_PAD)[..., :QK_NOPE],
                                                  g["w_v"].reshape(kvl, MLA_HEADS, V_HEAD)], axis=-1).reshape(kvl, -1))
    return out


def _loss_head(x, target, g, name):
    def fn(x_, t_, g_):
        y, vjp = jax.vjp(_rms, x_, g_)
        err = y - t_
        dx, dg = vjp(err * (1.0 / x_.shape[1]))
        loss = 0.5 * jnp.sum(jnp.mean(err * err, axis=-1, keepdims=True), axis=0, keepdims=True)
        return [dx], [dg, jnp.broadcast_to(loss, (1, LANES))]

    dx, dg, loss = _rowwise(fn, [x, target], [g], outs=[(x.shape[1], F32)], accs=[((1, x.shape[1]), F32), ((1, LANES), F32)], name=name)
    return loss[0, 0], dx, dg


def _part_fwd(part, x, p, rope, tag, dep=None):
    norm = part + "_norm"
    if dep is not None:
        p = dict(p, **{norm: p[norm] + dep})
    if part == "mix":
        mw = _mix_weights(p)
        x, s = _mix_fwd(x, p, mw, rope, tag)
        return x, (s, mw, p)
    x, s = _ffn_fwd(x, p[norm][None, :], p[part + "_w1"], p[part + "_w2"], tag)
    return x, (s, None, p)


def _part_bwd(part, dx, saved, rope, tag, dep=None, mid=None):
    s, mw, p = saved
    if part == "mix":
        dx, gm = _mix_bwd(dx, s, p, mw, rope, tag, dep, mid)
        return dx, _mix_grads_to_params(gm, p)
    dx, (dn, dw1, dw2) = _ffn_bwd(dx, s, p[part + "_norm"][None, :], p[part + "_w1"], p[part + "_w2"], tag, dep, mid)
    return dx, {part + "_norm": dn[0], part + "_w1": dw1, part + "_w2": dw2.reshape(N_CHIPS, -1, dw2.shape[1])}


ANY = pl.BlockSpec(memory_space=pl.ANY)
VMEM_WHOLE = pl.BlockSpec(memory_space=pltpu.VMEM)


def _place():
    x, y, c = (lax.axis_index(a) for a in MESH_AXES)
    return x, y, c, [(1 - x, y), (x, 1 - y), (1 - x, 1 - y)]


def _remote(src, dst, send_sem, recv_sem, device):
    return pltpu.make_async_remote_copy(src_ref=src, dst_ref=dst, send_sem=send_sem, recv_sem=recv_sem, device_id=device,
                                        device_id_type=pl.DeviceIdType.MESH)


def _half_rows(c, half):
    return pl.ds(pl.multiple_of(c * half, 16), half)


def _allreduce_all(v, name):
    r, cols = v.shape

    def body(v_ref, out_ref, buf, send_sems, recv_sems):
        x, y, c, chips = _place()
        sibling = (x, y, 1 - c)
        slot = lambda px, py, pc: buf.at[4 * px + 2 * py + pc]
        buf[4 * x + 2 * y + c] = v_ref[...]
        sent = [_remote(v_ref, slot(x, y, c), send_sems.at[0], recv_sems.at[0], sibling)]
        sent += [_remote(v_ref, slot(x, y, c), send_sems.at[1 + j], recv_sems.at[1 + j], (cx, cy, c)) for j, (cx, cy) in enumerate(chips)]
        for cp in sent:
            cp.start()
        for j, (cx, cy) in enumerate(chips):
            blk = slot(cx, cy, c)
            _remote(blk, blk, send_sems.at[1 + j], recv_sems.at[1 + j], (cx, cy, c)).wait_recv()
            passed = _remote(blk, blk, send_sems.at[4 + j], recv_sems.at[4 + j], sibling)
            passed.start()
            sent.append(passed)
        blk = slot(x, y, 1 - c)
        _remote(blk, blk, send_sems.at[0], recv_sems.at[0], sibling).wait_recv()
        for j, (cx, cy) in enumerate(chips):
            blk = slot(cx, cy, 1 - c)
            _remote(blk, blk, send_sems.at[4 + j], recv_sems.at[4 + j], sibling).wait_recv()
        for cp in sent:
            cp.wait_send()
        acc = buf[0]
        for k in range(1, 2 * N_CHIPS):
            acc = acc + buf[k]
        out_ref[...] = acc

    return pl.pallas_call(
        body, name=name, in_specs=[VMEM_WHOLE], out_specs=VMEM_WHOLE, out_shape=S((r, cols), F32),
        scratch_shapes=[pltpu.VMEM((2 * N_CHIPS, r, cols), F32), pltpu.SemaphoreType.DMA((7,)), pltpu.SemaphoreType.DMA((7,))],
        compiler_params=pltpu.CompilerParams(vmem_limit_bytes=VMEM_LIMIT),
    )(v)


def _pair_views(halves):
    def views(src, land, i, j, x, y, c, px, py):
        return src.at[pl.ds(0, N_CHIPS), _half_rows(1 - c, halves[i])], land, land
    return views


def _pair_exchange_begin(gs, name):
    lands = [lax.empty((N_CHIPS, g.shape[1] // 2, g.shape[2]), g.dtype) for g in gs]
    return _ici_begin(gs, lands, _pair_views([g.shape[1] // 2 for g in gs]), name, sibling=True)


def _pair_exchange_end(handle, after, name):
    n = (len(handle) - 3) // 2
    return _ici_end(handle, after, _pair_views([g.shape[1] // 2 for g in handle[2:2 + n]]), name, sibling=True)


def _pair_sum(g, a, name):
    n, r, cols = g.shape
    half = r // 2
    tr = _tile(half, 512, 16)
    n_blk = half // tr

    def body(c_ref, g_ref, a_ref, o_ref):
        o_ref[...] = (g_ref[...].astype(F32) + a_ref[...].astype(F32)).astype(o_ref.dtype)

    blk = pl.BlockSpec((1, tr, cols), lambda j, i, c_ref: (j, i, 0))
    return pl.pallas_call(
        body, name=name, out_shape=S((n, half, cols), BF16),
        grid_spec=pltpu.PrefetchScalarGridSpec(
            num_scalar_prefetch=1, grid=(n, n_blk),
            in_specs=[pl.BlockSpec((1, tr, cols), lambda j, i, c_ref: (j, c_ref[0] * n_blk + i, 0)), blk], out_specs=blk),
        compiler_params=_params(("parallel", "parallel")),
    )(lax.axis_index("c").reshape(1).astype(jnp.int32), g, a)


HBM = pl.BlockSpec(memory_space=pltpu.HBM)
SEM = pl.BlockSpec(memory_space=pltpu.SEMAPHORE)
SPLIT_COPY = pltpu.CompilerParams(has_side_effects=pltpu.SideEffectType.DATAFLOW_SIDE_EFFECTING)


def _peers(sibling):
    x, y, c, chips = _place()
    return x, y, c, ([(x, y, 1 - c)] if sibling else [(cx, cy, c) for cx, cy in chips])


def _ici_begin(srcs, lands, views, name, after=None, sibling=False):
    n = len(srcs)
    n_peers = 1 if sibling else N_CHIPS - 1
    extra = [] if after is None else [after]

    def body(*refs):
        s_refs, l_refs, send_sems, recv_sems, token = refs[:n], refs[n:2 * n], refs[-3 - 2 * n], refs[-2 - 2 * n], refs[-1]
        x, y, c, peers = _peers(sibling)
        for i in range(n):
            for j, peer in enumerate(peers):
                src, dst, _ = views(s_refs[i], l_refs[i], i, j, x, y, c, peer[0], peer[1])
                k = n_peers * i + j
                _remote(src, dst, send_sems.at[k], recv_sems.at[k], peer).start()
        token[...] = jnp.zeros_like(token)

    bufs = list(srcs) + list(lands)
    sems = pltpu.SemaphoreType.DMA((n * n_peers,))
    return pl.pallas_call(
        body, name=name, out_shape=(sems, sems, *[pltpu.HBM(b.shape, b.dtype) for b in bufs], S((8, LANES), F32)),
        in_specs=[HBM] * (2 * n) + [ANY] * len(extra), out_specs=(SEM, SEM, *[HBM] * (2 * n), VMEM_WHOLE),
        input_output_aliases={i: 2 + i for i in range(2 * n)}, compiler_params=SPLIT_COPY,
    )(*[pltpu.with_memory_space_constraint(b, pltpu.HBM) for b in bufs], *extra)


def _ici_end(handle, after, views, name, sibling=False):
    send_sems, recv_sems, *bufs, _ = handle
    n = len(bufs) // 2
    n_peers = 1 if sibling else N_CHIPS - 1

    def body(*refs):
        s_refs, l_refs, send_sems_, recv_sems_ = refs[:n], refs[n:2 * n], refs[2 * n], refs[2 * n + 1]
        x, y, c, peers = _peers(sibling)
        for i in range(n):
            for j, peer in enumerate(peers):
                src, _, arrival = views(s_refs[i], l_refs[i], i, j, x, y, c, peer[0], peer[1])
                k = n_peers * i + j
                cp = _remote(src, arrival, send_sems_.at[k], recv_sems_.at[k], peer)
                cp.wait_send()
                cp.wait_recv()

    out = pl.pallas_call(
        body, name=name, out_shape=[pltpu.HBM(b.shape, b.dtype) for b in bufs], in_specs=[HBM] * (2 * n) + [SEM, SEM, ANY],
        out_specs=[HBM] * (2 * n), input_output_aliases={i: i for i in range(2 * n)}, compiler_params=SPLIT_COPY,
    )(*bufs, send_sems, recv_sems, after)
    return out[:n], out[n:]


def _gather_views(halves):
    def views(src, land, i, j, x, y, c, cx, cy):
        mine = _half_rows(c, halves[i])
        return src.at[mine], land.at[2 * x + y, mine], land.at[2 * cx + cy, mine]
    return views


def _gather_begin(shards, name, after=None):
    lands = [lax.empty((N_CHIPS,) + s.shape, s.dtype) for s in shards]
    return _ici_begin(shards, lands, _gather_views([s.shape[0] // 2 for s in shards]), name, after)


def _gather_end(handle, after, name):
    n = (len(handle) - 3) // 2
    shards, lands = _ici_end(handle, after, _gather_views([s.shape[0] // 2 for s in handle[2:2 + n]]), name + "_wait")
    return _gather_finish(shards, lands, name + "_finish")


def _gather_finish(shards, lands, name):
    n = len(shards)

    def body(*refs):
        ins, l_refs, outs, send_sems, recv_sems = refs[:n], refs[n:2 * n], refs[2 * n:3 * n], refs[3 * n], refs[3 * n + 1]
        x, y, c, chips = _place()
        me, sibling = 2 * x + y, (x, y, 1 - c)
        sent = []
        for i in range(n):
            mine = _half_rows(c, shards[i].shape[0] // 2)
            for j, (cx, cy) in enumerate(chips):
                sent.append(_remote(l_refs[i].at[2 * cx + cy, mine], outs[i].at[2 * cx + cy, mine], send_sems.at[i, j], recv_sems.at[i, j], sibling))
            sent.append(_remote(ins[i], outs[i].at[me], send_sems.at[i, 3], recv_sems.at[i, 3], sibling))
        for cp in sent:
            cp.start()
        for i in range(n):
            other = _half_rows(1 - c, shards[i].shape[0] // 2)
            for j, (cx, cy) in enumerate(chips):
                rows = outs[i].at[2 * cx + cy, other]
                _remote(rows, rows, send_sems.at[i, j], recv_sems.at[i, j], sibling).wait_recv()
            own = outs[i].at[me]
            _remote(own, own, send_sems.at[i, 3], recv_sems.at[i, 3], sibling).wait_recv()
        for cp in sent:
            cp.wait_send()

    return pl.pallas_call(
        body, name=name, in_specs=[ANY] * (2 * n), out_specs=[ANY] * n, out_shape=[S(l_.shape, l_.dtype) for l_ in lands],
        input_output_aliases={n + i: i for i in range(n)},
        scratch_shapes=[pltpu.SemaphoreType.DMA((n, 4)), pltpu.SemaphoreType.DMA((n, 4))],
    )(*shards, *lands)


def _exchange_views(src, land, i, j, x, y, c, cx, cy):
    return src.at[2 * cx + cy], land.at[j], land.at[j]


def _chip_exchange_begin(ps, name):
    return _ici_begin(ps, [lax.empty((N_CHIPS - 1,) + p.shape[1:], p.dtype) for p in ps], _exchange_views, name)


def _chip_exchange_end(handle, after, name):
    return _ici_end(handle, after, _exchange_views, name)


def _quad_sum(p, q, name):
    _, h, cols = p.shape
    tr = _tile(h, 512, 16)
    n_blk = h // tr
    x, y, c, _ = _place()

    def body(s_ref, p_ref, q0_ref, q1_ref, q2_ref, o_ref):
        o_ref[...] = p_ref[0].astype(F32) + q0_ref[0].astype(F32) + q1_ref[0].astype(F32) + q2_ref[0].astype(F32)

    in_specs = [pl.BlockSpec((1, tr, cols), lambda i, s_ref: (s_ref[0], i, 0))]
    in_specs += [pl.BlockSpec((1, tr, cols), functools.partial(lambda i, s_ref, k: (k, i, 0), k=k)) for k in range(N_CHIPS - 1)]
    return pl.pallas_call(
        body, name=name, out_shape=S((2 * h, cols), F32),
        grid_spec=pltpu.PrefetchScalarGridSpec(num_scalar_prefetch=1, grid=(n_blk,), in_specs=in_specs,
                                               out_specs=pl.BlockSpec((tr, cols), lambda i, s_ref: (s_ref[1] * n_blk + i, 0))),
        compiler_params=_params(("parallel",)),
    )(jnp.stack([2 * x + y, c]).astype(jnp.int32), p, q, q, q)


def _pair_share(bufs, name):
    n = len(bufs)

    def body(*refs):
        in_refs, out_refs, send_sems, recv_sems = refs[:n], refs[n:2 * n], refs[2 * n], refs[2 * n + 1]
        x, y, c, _ = _place()
        sent = []
        for i in range(n):
            mine = _half_rows(c, bufs[i].shape[0] // 2)
            sent.append(_remote(in_refs[i].at[mine], out_refs[i].at[mine], send_sems.at[i], recv_sems.at[i], (x, y, 1 - c)))
        for cp in sent:
            cp.start()
        for i in range(n):
            other = out_refs[i].at[_half_rows(1 - c, bufs[i].shape[0] // 2)]
            _remote(other, other, send_sems.at[i], recv_sems.at[i], (x, y, 1 - c)).wait_recv()
        for cp in sent:
            cp.wait_send()

    return pl.pallas_call(body, name=name, in_specs=[ANY] * n, out_specs=[ANY] * n, out_shape=[S(b.shape, b.dtype) for b in bufs],
                          input_output_aliases={i: i for i in range(n)},
                          scratch_shapes=[pltpu.SemaphoreType.DMA((n,)), pltpu.SemaphoreType.DMA((n,))])(*bufs)


def _adamw(w, gs, m, v, name, after=None):
    depth, r, cols = w.shape
    extra = [] if after is None else [after]
    tr, tc = _tile(r, 256, 8), cols
    if tr < 64 and r > 256 and cols % LANES == 0:
        tr, tc = r, _tile(cols, 256)

    def body(*refs):
        w_ref, m_ref, v_ref = refs[:3]
        g_refs = refs[3:3 + depth]
        go_ref, d_ref, mo_ref, vo_ref = refs[-4:]
        for l in range(depth):
            @pl.when(pl.program_id(0) == l)
            def _(l=l):
                g_ = g_refs[l][...]
                m_ = ADAM_B1 * m_ref[...] + (1.0 - ADAM_B1) * g_
                v_ = ADAM_B2 * v_ref[...] + (1.0 - ADAM_B2) * jnp.square(g_)
                m_hat = m_ / (1.0 - ADAM_B1 ** ADAM_STEP)
                v_hat = v_ / (1.0 - ADAM_B2 ** ADAM_STEP)
                d_ref[...] = -ADAM_LR * (m_hat / (jnp.sqrt(v_hat) + ADAM_EPS) + ADAM_WD * w_ref[...])
                go_ref[...], mo_ref[...], vo_ref[...] = g_, m_, v_

    blk = pl.BlockSpec((None, tr, tc), lambda l, i: (l, i, 0) if tc == cols else (l, 0, i))
    g_blk = pl.BlockSpec((tr, tc), lambda l, i: (i, 0) if tc == cols else (0, i))
    in_specs = [blk] * 3 + [g_blk] * depth + [pl.BlockSpec((8, LANES), lambda l, i: (0, 0))] * len(extra)
    return pl.pallas_call(body, name=name, grid=(depth, (r // tr) * (cols // tc)), in_specs=in_specs, out_specs=[blk] * 4,
                          out_shape=[S(w.shape, F32)] * 4, compiler_params=_params(("parallel", "parallel")))(w, m, v, *gs, *extra)


WEIGHTS = ("ffn1_norm", "ffn1_w1", "ffn1_w2", "mix_norm", "w_in", "b_in", "lru_conv_w", "lru_conv_b", "lru_w_gate", "lru_b_gate",
           "lru_lambda", "lru_w_out", "q_norm", "w_uq", "kv_norm", "w_ukv", "mla_w_o", "conv_dw_w", "conv_dw_b", "conv_ln_g",
           "conv_ln_b", "conv_w_out", "conv_b_out", "w_out", "ffn2_norm", "ffn2_w1", "ffn2_w2", "final_norm")
ROW_SHARDED = ("ffn1_w2", "w_out", "ffn2_w2")
COL_SHARDED = ("ffn1_w1", "w_in", "lru_w_out", "w_uq", "w_ukv", "mla_w_o", "conv_w_out", "ffn2_w1")
SMALL_SHARDED = ("lru_conv_w", "conv_dw_w")
MXU_SHARDED = tuple(n for n in WEIGHTS if n in ROW_SHARDED + COL_SHARDED)
REPLICATED = tuple(n for n in WEIGHTS if n not in MXU_SHARDED + SMALL_SHARDED)
SMALL = REPLICATED + SMALL_SHARDED
INPUTS = ("x", "positions") + WEIGHTS + ("loss_target",) + tuple("m_" + n for n in WEIGHTS) + tuple("v_" + n for n in WEIGHTS)


def _pack(arrays, dtype, cols, row_unit):
    flat = jnp.concatenate([a.astype(dtype).reshape(-1) for a in arrays])
    unit = cols * row_unit
    return jnp.pad(flat, (0, -flat.shape[0] % unit)).reshape(-1, cols)


def _unpack(flat, shapes):
    out, off = [], 0
    for shp in shapes:
        n = 1
        for s_ in shp:
            n *= s_
        out.append(flat[..., off:off + n].reshape(flat.shape[:-1] + tuple(shp)))
        off += n
    return out


def _reduce_scatter_begin(gs, tag):
    return _pair_exchange_begin(gs, tag + "_pair_exchange")


def _reduce_scatter_mid(handle, names, after, tag):
    gs, halves = _pair_exchange_end(handle, after, tag + "_pair_exchange_wait")
    pairs = [_pair_sum(g, h, f"{tag}_pair_sum_{n}") for n, g, h in zip(names, gs, halves, strict=True)]
    return _chip_exchange_begin(pairs, tag + "_chip_exchange")


def _reduce_scatter_end(handle, names, after, tag):
    pairs, others = _chip_exchange_end(handle, after, tag + "_chip_exchange_wait")
    sums = [_quad_sum(p, q, f"{tag}_chip_sum_{n}") for n, p, q in zip(names, pairs, others, strict=True)]
    return dict(zip(names, _pair_share(sums, tag + "_pair_share"), strict=True))


PARTS = (("ffn1", ("ffn1_w1", "ffn1_w2")), ("mix", ("w_in", "lru_w_out", "w_uq", "w_ukv", "mla_w_o", "conv_w_out", "w_out")),
         ("ffn2", ("ffn2_w1", "ffn2_w2")))


def _step(a):
    x, positions, target = a["x"][0], a["positions"][0], a["loss_target"][0]
    depth = a["ffn1_norm"].shape[0]
    me = 2 * lax.axis_index("x") + lax.axis_index("y")
    placed = [lax.dynamic_update_slice_in_dim(jnp.zeros(a[n].shape[:2] + (N_CHIPS,) + a[n].shape[2:], F32), 0.5 * a[n][:, :, None], me, 2)
              for n in SMALL_SHARDED]
    small_whole = _unpack(_allreduce_all(_pack(placed, F32, LANES, 8), "small_weights").reshape(-1), [p_.shape for p_ in placed])
    small_whole = {n: w.reshape(w.shape[:2] + (-1,)) for n, w in zip(SMALL_SHARDED, small_whole, strict=True)}
    base = [{n: a[n][l] for n in REPLICATED if a[n].ndim > 1} | {n: small_whole[n][l] for n in SMALL_SHARDED} for l in range(depth)]
    order = [(l, part, names) for l in range(depth) for part, names in PARTS]
    shards = lambda l, names: [a[n][l].astype(BF16) for n in names]
    rope = _rope_tables(positions, "rope_tables")
    handle = _gather_begin(shards(0, order[0][2]), "l0_ffn1_gather", small_whole[SMALL_SHARDED[0]])
    gathered = _gather_end(handle, handle[-1], "l0_ffn1_gather")
    saved = []
    for k, (l, part, names) in enumerate(order):
        p = base[l] | {n: g.reshape(-1, g.shape[-1]) if n in ROW_SHARDED else g for n, g in zip(names, gathered, strict=True)}
        dep = None
        if k + 1 < len(order):
            l2, part2, names2 = order[k + 1]
            handle = _gather_begin(shards(l2, names2), f"l{l2}_{part2}_gather", handle[-1])
            dep = handle[-1][0, 0]
        x, s = _part_fwd(part, x, p, rope, f"l{l}_{part}", dep)
        saved.append(s)
        if dep is not None:
            gathered = _gather_end(handle, x, f"l{l2}_{part2}_gather")
    loss, dx, dfinal = _loss_head(x, target, a["final_norm"][None, :], "loss_head")
    loss = lax.psum(loss, MESH_AXES)
    grads, shard_grads, pending = [{} for _ in range(depth)], [{} for _ in range(depth)], None
    for l, part, names in reversed(order):
        flying = []

        def mid(after, pending=pending, flying=flying):
            flying.append(_reduce_scatter_mid(pending[0], pending[2], after, pending[3]))
            return flying[0][-1][0, 0]

        deps = (None, None) if pending is None else (pending[0][-1][0, 0], mid)
        dx, g = _part_bwd(part, dx, saved.pop(), rope, f"l{l}_{part}", *deps)
        if pending is not None:
            shard_grads[pending[1]].update(_reduce_scatter_end(flying[0], pending[2], dx, pending[3]))
        pending = (_reduce_scatter_begin([g[n] for n in names], f"l{l}_{part}_grad"), l, names, f"l{l}_{part}_grad")
        grads[l].update({n: g[n] for n in g if n in SMALL})
    last = _reduce_scatter_mid(pending[0], pending[2], pending[0][-1], pending[3])
    g, delta, new_m, new_v = {}, {}, {}, {}

    def update(n, after=None):
        flip = (lambda t: jnp.swapaxes(t, -1, -2)) if (a[n].shape[-1] % LANES and not a[n].shape[-2] % LANES) else (lambda t: t)
        res = _adamw(flip(a[n]), [flip(shard_grads[l][n]) for l in range(depth)], flip(a["m_" + n]), flip(a["v_" + n]), "adamw_" + n, after)
        g[n], delta[n], new_m[n], new_v[n] = [flip(r) for r in res]

    early = [n for n in MXU_SHARDED if n not in pending[2]]
    for n in early:
        update(n, last[-1])
    shard_grads[pending[1]].update(_reduce_scatter_end(last, pending[2], delta[early[-1]], pending[3]))
    small = [jnp.stack([g_[n] for g_ in grads]) if a[n].ndim > 1 else dfinal[0] for n in SMALL]
    g_small = _unpack(_allreduce_all(_pack(small, F32, LANES, 256), "grad_allreduce").reshape(-1), [s_.shape for s_ in small])
    g_small = [lax.dynamic_slice_in_dim(g, me * a[n].shape[-1], a[n].shape[-1], 2) if n in SMALL_SHARDED else g
               for n, g in zip(SMALL, g_small, strict=True)]
    for n in pending[2]:
        update(n)
    shapes = [a[n].shape for n in SMALL]
    packed = [_pack([a[pre + n] for n in SMALL], F32, LANES, 256)[None] for pre in ("", "m_", "v_")]
    res = _adamw(packed[0], [_pack(g_small, F32, LANES, 256)], packed[1], packed[2], "adamw_small")
    for out, r in zip((g, delta, new_m, new_v), res, strict=True):
        out.update(zip(SMALL, _unpack(r.reshape(-1), shapes), strict=True))
    return (loss, dx[None], *[g[n] for n in WEIGHTS], *[delta[n] for n in WEIGHTS], *[new_m[n] for n in WEIGHTS], *[new_v[n] for n in WEIGHTS])


def kernel(x, positions, ffn1_norm, ffn1_w1, ffn1_w2, mix_norm, w_in, b_in, lru_conv_w, lru_conv_b, lru_w_gate, lru_b_gate, lru_lambda, lru_w_out, q_norm, w_uq, kv_norm, w_ukv, mla_w_o, conv_dw_w, conv_dw_b, conv_ln_g, conv_ln_b, conv_w_out, conv_b_out, w_out, ffn2_norm, ffn2_w1, ffn2_w2, final_norm, loss_target, m_ffn1_norm, m_ffn1_w1, m_ffn1_w2, m_mix_norm, m_w_in, m_b_in, m_lru_conv_w, m_lru_conv_b, m_lru_w_gate, m_lru_b_gate, m_lru_lambda, m_lru_w_out, m_q_norm, m_w_uq, m_kv_norm, m_w_ukv, m_mla_w_o, m_conv_dw_w, m_conv_dw_b, m_conv_ln_g, m_conv_ln_b, m_conv_w_out, m_conv_b_out, m_w_out, m_ffn2_norm, m_ffn2_w1, m_ffn2_w2, m_final_norm, v_ffn1_norm, v_ffn1_w1, v_ffn1_w2, v_mix_norm, v_w_in, v_b_in, v_lru_conv_w, v_lru_conv_b, v_lru_w_gate, v_lru_b_gate, v_lru_lambda, v_lru_w_out, v_q_norm, v_w_uq, v_kv_norm, v_w_ukv, v_mla_w_o, v_conv_dw_w, v_conv_dw_b, v_conv_ln_g, v_conv_ln_b, v_conv_w_out, v_conv_b_out, v_w_out, v_ffn2_norm, v_ffn2_w1, v_ffn2_w2, v_final_norm):
    return _step(dict(zip(INPUTS, (x, positions, ffn1_norm, ffn1_w1, ffn1_w2, mix_norm, w_in, b_in, lru_conv_w, lru_conv_b, lru_w_gate, lru_b_gate, lru_lambda, lru_w_out, q_norm, w_uq, kv_norm, w_ukv, mla_w_o, conv_dw_w, conv_dw_b, conv_ln_g, conv_ln_b, conv_w_out, conv_b_out, w_out, ffn2_norm, ffn2_w1, ffn2_w2, final_norm, loss_target, m_ffn1_norm, m_ffn1_w1, m_ffn1_w2, m_mix_norm, m_w_in, m_b_in, m_lru_conv_w, m_lru_conv_b, m_lru_w_gate, m_lru_b_gate, m_lru_lambda, m_lru_w_out, m_q_norm, m_w_uq, m_kv_norm, m_w_ukv, m_mla_w_o, m_conv_dw_w, m_conv_dw_b, m_conv_ln_g, m_conv_ln_b, m_conv_w_out, m_conv_b_out, m_w_out, m_ffn2_norm, m_ffn2_w1, m_ffn2_w2, m_final_norm, v_ffn1_norm, v_ffn1_w1, v_ffn1_w2, v_mix_norm, v_w_in, v_b_in, v_lru_conv_w, v_lru_conv_b, v_lru_w_gate, v_lru_b_gate, v_lru_lambda, v_lru_w_out, v_q_norm, v_w_uq, v_kv_norm, v_w_ukv, v_mla_w_o, v_conv_dw_w, v_conv_dw_b, v_conv_ln_g, v_conv_ln_b, v_conv_w_out, v_conv_b_out, v_w_out, v_ffn2_norm, v_ffn2_w1, v_ffn2_w2, v_final_norm), strict=True)))
```
